```python
import math
import jax, jax.numpy as jnp
from jax import lax
import numpy as np

D_MODEL = 1024
BATCH = 2
SEQ = 8192
DEPTH = 2

CTX_LEN = 256
GRID_W = 64
EPS = 1e-6
ROPE_BASE = 10000.0
BLOCK = 128
N_BRANCH = 4
BRANCH_W = D_MODEL // N_BRANCH
DA_HEADS = 4
DA_HALF = BRANCH_W // DA_HEADS // 2
DA_VDIM = 2 * DA_HALF
CONV_W = 31
CONV_CH = BRANCH_W
SW_HEADS = 4
SW_KV = 2
SW_DH = BRANCH_W // SW_HEADS
WINDOW = 128
RW_HEADS = 4
RW_DH = BRANCH_W // RW_HEADS
RW_DECAY_RANK = 64
RW_A_RANK = 64
RW_G_RANK = 128
RW_DECAY_SCALE = math.exp(-0.5)
RW_GN_EPS = 64e-5
RW_SHIFT_W = 3 * BRANCH_W + RW_DECAY_RANK + RW_A_RANK
DA_COLS = 3 * BRANCH_W
CV_COLS = 2 * BRANCH_W
SW_COLS = (SW_HEADS + 2 * SW_KV) * SW_DH
RW_COLS = RW_SHIFT_W + RW_G_RANK
GATE_COLS = N_BRANCH * D_MODEL
SPLITS = (DA_COLS, DA_COLS + CV_COLS, DA_COLS + CV_COLS + SW_COLS, DA_COLS + CV_COLS + SW_COLS + RW_COLS)
IN_COLS = DA_COLS + CV_COLS + SW_COLS + RW_COLS + GATE_COLS
N_EXPERTS = 64
TOP_K = 6
N_GROUPS = 8
TOPK_GROUPS = 4
D_EXPERT = D_MODEL // 4
D_SHARED = D_MODEL // 4
ROUTE_SCALE = 2.5
MOE_BLOCK = 256

kernel_name = 'hybrid_parallel_diffusion_block'

F32 = jnp.float32


def rms_norm(x, g):
    xf = x.astype(F32)
    y = xf * lax.rsqrt(jnp.mean(xf * xf, axis=-1, keepdims=True) + EPS)
    return (y * g.astype(F32)).astype(x.dtype)


def layer_norm(x, g, b, eps):
    xf = x.astype(F32)
    xc = xf - jnp.mean(xf, axis=-1, keepdims=True)
    y = xc * lax.rsqrt(jnp.mean(xc * xc, axis=-1, keepdims=True) + eps)
    return (y * g.astype(F32) + b.astype(F32)).astype(x.dtype)


def modulate(x, shift, scale):
    return x * (1 + scale) + shift


def grid_angles(rows, dim):
    q = dim // 4
    row = jnp.repeat(jnp.arange(rows, dtype=F32), GRID_W)
    col = jnp.tile(jnp.arange(GRID_W, dtype=F32), rows)
    freqs = ROPE_BASE ** (-jnp.arange(q, dtype=F32) / q)
    return jnp.stack([row[:, None] * freqs, col[:, None] * freqs], axis=1)


def axial_rope(x, ang):
    b, t, h, dim = x.shape
    q = dim // 4
    cos = jnp.cos(ang)[None, :, None]
    sin = jnp.sin(ang)[None, :, None]
    xr = x.astype(F32).reshape(b, t, h, 2, 2, q)
    x1, x2 = xr[..., 0, :], xr[..., 1, :]
    out = jnp.stack([x1 * cos - x2 * sin, x2 * cos + x1 * sin], axis=-2)
    return out.reshape(x.shape).astype(x.dtype)


def diff_softmax(q1, q2, k1, k2, v, lam):
    scale = DA_HALF ** -0.5
    s1 = jnp.einsum('bqhd,bkhd->bhqk', q1, k1).astype(F32) * scale
    s2 = jnp.einsum('bqhd,bkhd->bhqk', q2, k2).astype(F32) * scale
    p = jax.nn.softmax(s1, axis=-1) - lam * jax.nn.softmax(s2, axis=-1)
    return jnp.einsum('bhqk,bkhd->bqhd', p.astype(v.dtype), v)


def diff_attention(ua, uac, ang, lam_vec, subln_g, lam_init, need_ctx):
    def heads(u):
        b, t = u.shape[:2]
        q, k, v = jnp.split(u, 3, axis=-1)
        return (q.reshape(b, t, DA_HEADS, 2, DA_HALF), k.reshape(b, t, DA_HEADS, 2, DA_HALF),
                v.reshape(b, t, DA_HEADS, DA_VDIM))

    q, k, v = heads(ua)
    qc, kc, vc = heads(uac)
    b, t = ua.shape[:2]
    q = axial_rope(q.reshape(b, t, 2 * DA_HEADS, DA_HALF), ang).reshape(q.shape)
    k = axial_rope(k.reshape(b, t, 2 * DA_HEADS, DA_HALF), ang).reshape(k.shape)
    lv = lam_vec.astype(F32)
    lam = jnp.exp(jnp.sum(lv[0] * lv[1])) - jnp.exp(jnp.sum(lv[2] * lv[3])) + lam_init
    k1 = jnp.concatenate([k[..., 0, :], kc[..., 0, :]], axis=1)
    k2 = jnp.concatenate([k[..., 1, :], kc[..., 1, :]], axis=1)
    vall = jnp.concatenate([v, vc], axis=1)
    nb = t // BLOCK

    def to_blocks(z):
        return z.reshape(b, nb, BLOCK, DA_HEADS, DA_HALF).swapaxes(0, 1)

    o = lax.map(lambda qs: diff_softmax(qs[0], qs[1], k1, k2, vall, lam),
                (to_blocks(q[..., 0, :]), to_blocks(q[..., 1, :])))
    o = o.swapaxes(0, 1).reshape(b, t, DA_HEADS, DA_VDIM)

    def finish(z):
        return (rms_norm(z, subln_g) * (1.0 - lam_init)).reshape(z.shape[0], z.shape[1], BRANCH_W)

    if not need_ctx:
        return finish(o), None
    oc = diff_softmax(qc[..., 0, :], qc[..., 1, :], kc[..., 0, :], kc[..., 1, :], vc, lam)
    return finish(o), finish(oc)


def conformer_conv(ub, w, bias, ln):
    a, g = jnp.split(ub, 2, axis=-1)
    z = a * jax.nn.sigmoid(g)
    z = lax.conv_general_dilated(z, w[:, None, :].astype(z.dtype), window_strides=(1,), padding='SAME',
                                 dimension_numbers=('NWC', 'WIO', 'NWC'),
                                 feature_group_count=CONV_CH) + bias
    z = layer_norm(z, ln[0], ln[1], EPS)
    return jax.nn.silu(z)


def window_attention(us, usc, ang, sink, need_ctx):
    rep = SW_HEADS // SW_KV
    scale = SW_DH ** -0.5

    def heads(u):
        b, t = u.shape[:2]
        q, k, v = jnp.split(u, [SW_HEADS * SW_DH, (SW_HEADS + SW_KV) * SW_DH], axis=-1)
        return (q.reshape(b, t, SW_HEADS, SW_DH), k.reshape(b, t, SW_KV, SW_DH), v.reshape(b, t, SW_KV, SW_DH))

    q, k, v = heads(us)
    qc, kc, vc = heads(usc)
    b, t = us.shape[:2]
    n_ctx = kc.shape[1]
    q = axial_rope(q, ang)
    k = axial_rope(k, ang)
    nb = t // BLOCK
    qb = q.reshape(b, nb, BLOCK, SW_KV, rep, SW_DH)

    def band(z):
        zp = jnp.pad(z, ((0, 0), (BLOCK, BLOCK), (0, 0), (0, 0))).reshape(b, nb + 2, BLOCK, SW_KV, SW_DH)
        return jnp.concatenate([zp[:, :-2], zp[:, 1:-1], zp[:, 2:]], axis=2)

    kb, vb = band(k), band(v)
    qpos = jnp.arange(t).reshape(nb, BLOCK)
    kpos = (jnp.arange(nb)[:, None] - 1) * BLOCK + jnp.arange(3 * BLOCK)
    valid = ((jnp.abs(kpos[:, None, :] - qpos[:, :, None]) <= WINDOW)
             & (kpos >= 0)[:, None, :] & (kpos < t)[:, None, :])
    s_loc = jnp.einsum('bnqgrd,bnkgd->bngrqk', qb, kb).astype(F32) * scale
    s_loc = jnp.where(valid[None, :, None, None], s_loc, -jnp.inf)
    s_ctx = jnp.einsum('bnqgrd,bcgd->bngrqc', qb, kc).astype(F32) * scale
    sk = sink.astype(F32).reshape(SW_KV, rep)
    s_sink = jnp.broadcast_to(sk[None, None, :, :, None, None], s_loc.shape[:-1] + (1,))
    p = jax.nn.softmax(jnp.concatenate([s_loc, s_ctx, s_sink], axis=-1), axis=-1).astype(v.dtype)
    n_loc = 3 * BLOCK
    o = (jnp.einsum('bngrqk,bnkgd->bnqgrd', p[..., :n_loc], vb)
         + jnp.einsum('bngrqc,bcgd->bnqgrd', p[..., n_loc:n_loc + n_ctx], vc))
    out = o.reshape(b, t, BRANCH_W)
    if not need_ctx:
        return out, None
    qcr = qc.reshape(b, n_ctx, SW_KV, rep, SW_DH)
    sc = jnp.einsum('bqgrd,bkgd->bgrqk', qcr, kc).astype(F32) * scale
    sc_sink = jnp.broadcast_to(sk[None, :, :, None, None], sc.shape[:-1] + (1,))
    pc = jax.nn.softmax(jnp.concatenate([sc, sc_sink], axis=-1), axis=-1).astype(v.dtype)
    oc = jnp.einsum('bgrqk,bkgd->bqgrd', pc[..., :n_ctx], vc)
    return out, oc.reshape(b, n_ctx, BRANCH_W)


def token_shift(z, direction):
    if direction == 0:
        return jnp.pad(z[:, :-1], ((0, 0), (1, 0), (0, 0)))
    return jnp.pad(z[:, 1:], ((0, 0), (0, 1), (0, 0)))


def head_param(p):
    return p.astype(F32).reshape(RW_HEADS, RW_DH)


def rwkv_prepare(z, d, lp):
    zs = z + (token_shift(z, d) - z) * lp['rw_mu'][d]
    r, k, v, wd, ad = jnp.split(zs, [BRANCH_W, 2 * BRANCH_W, 3 * BRANCH_W, 3 * BRANCH_W + RW_DECAY_RANK], axis=-1)
    w = jnp.exp(-RW_DECAY_SCALE * jax.nn.sigmoid((lp['rw_w0'][d] + jnp.tanh(wd) @ lp['rw_w_up'][d]).astype(F32)))
    a = jax.nn.sigmoid((lp['rw_a0'][d] + ad @ lp['rw_a_up'][d]).astype(F32))

    def hd(u):
        return u.astype(F32).reshape(u.shape[0], u.shape[1], RW_HEADS, RW_DH)

    r, k, v, w, a = hd(r), hd(k), hd(v), hd(w), hd(a)
    kappa = k * head_param(lp['rw_kk'][d])
    kh = kappa * lax.rsqrt(jnp.maximum(jnp.sum(kappa * kappa, axis=-1, keepdims=True), 1e-12))
    kt = k * (1 + (a - 1) * head_param(lp['rw_ka'][d]))
    return (w, kh, a * kh, v, kt, r)


def rwkv_bonus(inp, rk):
    w, kh, akh, v, kt, r = inp
    return jnp.sum(r * kt * rk.astype(F32), axis=-1, keepdims=True) * v


def wkv_scan(inp, s0, reverse):
    xs = tuple(jnp.swapaxes(u, 0, 1) for u in inp)

    def step(s, xt):
        w, kh, akh, v, kt, r = xt
        skh = jnp.einsum('bhvk,bhk->bhv', s, kh)
        s = s * w[:, :, None, :] - skh[..., None] * akh[:, :, None, :] + v[..., None] * kt[:, :, None, :]
        return s, jnp.einsum('bhvk,bhk->bhv', s, r)

    s, ys = lax.scan(step, s0, xs, reverse=reverse)
    return s, jnp.swapaxes(ys, 0, 1)


def rwkv_readout(y, bonus, gd, lp):
    gn = lp['rw_gn']
    yn = layer_norm(y, head_param(gn[0]), head_param(gn[1]), RW_GN_EPS) + bonus
    gate = jax.nn.sigmoid(gd) @ lp['rw_g_up']
    return yn.reshape(gate.shape).astype(gate.dtype) * gate


def rwkv_mixer(ud, udc, lp, need_ctx):
    z, gd = ud[..., :RW_SHIFT_W], ud[..., RW_SHIFT_W:]
    zc, gdc = udc[..., :RW_SHIFT_W], udc[..., RW_SHIFT_W:]
    s0 = jnp.zeros((ud.shape[0], RW_HEADS, RW_DH, RW_DH), F32)
    inp_l, inp_c, y_l, y_c = [], [], [], []
    for d in (0, 1):
        ic = rwkv_prepare(zc, d, lp)
        s_ctx, yc = wkv_scan(ic, s0, d == 1)
        il = rwkv_prepare(z, d, lp)
        _, yl = wkv_scan(il, s_ctx, d == 1)
        inp_l.append(il); inp_c.append(ic); y_l.append(yl); y_c.append(yc)
    rk = lp['rw_rk']
    out = rwkv_readout(y_l[0] + y_l[1], rwkv_bonus(inp_l[0], rk) + rwkv_bonus(inp_l[1], rk), gd, lp)
    if not need_ctx:
        return out, None
    out_c = rwkv_readout(y_c[0] + y_c[1], rwkv_bonus(inp_c[0], rk) + rwkv_bonus(inp_c[1], rk), gdc, lp)
    return out, out_c


def merge_branches(outs, gate_logits, w_branch, w_out):
    gates = jax.nn.sigmoid(gate_logits)
    m = gates[..., :D_MODEL] * (outs[0] @ w_branch[0])
    for i in range(1, N_BRANCH):
        m = m + gates[..., i * D_MODEL:(i + 1) * D_MODEL] * (outs[i] @ w_branch[i])
    return m @ w_out


def token_mixer(h, hc, lp, ang32, ang64, lam_init, need_ctx):
    pa, pb, pc, pd, pg = jnp.split(h @ lp['w_in'], SPLITS, axis=-1)
    qa, qb, qc, qd, qg = jnp.split(hc @ lp['w_in'], SPLITS, axis=-1)
    oa, oa_c = diff_attention(pa, qa, ang32, lp['da_lambda'], lp['da_subln'], lam_init, need_ctx)
    ob = conformer_conv(pb, lp['cv_w'], lp['cv_b'], lp['cv_ln'])
    oc, oc_c = window_attention(pc, qc, ang64, lp['sw_sink'], need_ctx)
    od, od_c = rwkv_mixer(pd, qd, lp, need_ctx)
    y = merge_branches((oa, ob, oc, od), pg, lp['w_branch'], lp['w_out'])
    if not need_ctx:
        return y, None
    ob_c = conformer_conv(qb, lp['cv_w'], lp['cv_b'], lp['cv_ln'])
    y_c = merge_branches((oa_c, ob_c, oc_c, od_c), qg, lp['w_branch'], lp['w_out'])
    return y, y_c


def moe_ffn(h, router_w, router_b, ex_w_gu, ex_w_down, sh_w_gu, sh_w_down):
    n, d = h.shape
    scores = jax.nn.sigmoid((h @ router_w).astype(F32))
    biased = scores + router_b.astype(F32)
    grp = biased.reshape(n, N_GROUPS, N_EXPERTS // N_GROUPS)
    grp_score = lax.top_k(grp, 2)[0].sum(-1)
    _, top_grp = lax.top_k(grp_score, TOPK_GROUPS)
    grp_mask = jax.nn.one_hot(top_grp, N_GROUPS, dtype=F32).sum(1)
    exp_mask = jnp.repeat(grp_mask, N_EXPERTS // N_GROUPS, axis=1) > 0
    _, idx = lax.top_k(jnp.where(exp_mask, biased, -jnp.inf), TOP_K)
    wts = jnp.take_along_axis(scores, idx, axis=1)
    wts = wts / jnp.sum(wts, axis=-1, keepdims=True) * ROUTE_SCALE
    nk = n * TOP_K
    flat_e = idx.reshape(-1)
    flat_t = jnp.repeat(jnp.arange(n, dtype=jnp.int32), TOP_K)
    order = jnp.argsort(flat_e)
    e_s, t_s, w_s = flat_e[order], flat_t[order], wts.reshape(-1)[order]
    counts = jnp.bincount(flat_e, length=N_EXPERTS)
    starts = jnp.cumsum(counts) - counts
    padded = (counts + MOE_BLOCK - 1) // MOE_BLOCK * MOE_BLOCK
    pends = jnp.cumsum(padded)
    pstarts = pends - padded
    dest = pstarts[e_s] + jnp.arange(nk) - starts[e_s]
    n_blocks = (nk + N_EXPERTS * (MOE_BLOCK - 1) + MOE_BLOCK - 1) // MOE_BLOCK
    n_rows = n_blocks * MOE_BLOCK
    tok = jnp.full((n_rows,), n, jnp.int32).at[dest].set(t_s)
    gw = jnp.zeros((n_rows,), F32).at[dest].set(w_s)
    blk_e = jnp.minimum(jnp.searchsorted(pends, jnp.arange(n_blocks) * MOE_BLOCK, side='right'), N_EXPERTS - 1)
    h_ext = jnp.concatenate([h, jnp.zeros((1, d), h.dtype)], axis=0)

    def body(acc, blk):
        t_b, w_b, e = blk
        g, u = jnp.split(h_ext[t_b] @ ex_w_gu[e], 2, axis=-1)
        y_b = (jax.nn.silu(g) * u) @ ex_w_down[e]
        return acc.at[t_b].add(y_b * w_b[:, None].astype(y_b.dtype)), None

    acc, _ = lax.scan(body, jnp.zeros((n + 1, d), h.dtype),
                      (tok.reshape(n_blocks, MOE_BLOCK), gw.reshape(n_blocks, MOE_BLOCK), blk_e))
    gs, us = jnp.split(h @ sh_w_gu, 2, axis=-1)
    return acc[:n] + (jax.nn.silu(gs) * us) @ sh_w_down


def setup_inputs(seed: int = 0) -> dict:
    key = jax.random.key(seed)
    ks = iter(jax.random.split(key, 40))
    L, D, BW = DEPTH, D_MODEL, BRANCH_W

    def nrm(shape, s):
        return jax.random.normal(next(ks), shape, F32) * s

    gain_bias = jnp.array([1.0, 0.0], F32)[None, :, None]
    return {
        'x': nrm((BATCH, SEQ, D), 1.0),
        'c': nrm((BATCH, D), 1.0),
        'ctx': nrm((BATCH, CTX_LEN, D), 1.0),
        'c_ctx': nrm((D,), 1.0),
        'ada_w': nrm((L, D, 6 * D), 0.5 * D ** -0.5),
        'ada_b': nrm((L, 6 * D), 0.02),
        'norm_g': 1.0 + nrm((L, 4, D), 0.05),
        'w_in': nrm((L, D, IN_COLS), D ** -0.5),
        'w_branch': nrm((L, N_BRANCH, BW, D), BW ** -0.5),
        'w_out': nrm((L, D, D), D ** -0.5),
        'da_lambda': nrm((L, 4, DA_HALF), 0.1),
        'da_subln': 1.0 + nrm((L, DA_VDIM), 0.05),
        'cv_w': nrm((L, CONV_W, CONV_CH), CONV_W ** -0.5),
        'cv_b': nrm((L, CONV_CH), 0.02),
        'cv_ln': gain_bias + nrm((L, 2, CONV_CH), 0.05),
        'sw_sink': nrm((L, SW_HEADS), 1.0),
        'rw_mu': jax.random.uniform(next(ks), (L, 2, RW_SHIFT_W), F32),
        'rw_w0': nrm((L, 2, BW), 1.0),
        'rw_w_up': nrm((L, 2, RW_DECAY_RANK, BW), 0.5 * RW_DECAY_RANK ** -0.5),
        'rw_a0': nrm((L, 2, BW), 0.5),
        'rw_a_up': nrm((L, 2, RW_A_RANK, BW), 0.5 * RW_A_RANK ** -0.5),
        'rw_kk': 0.85 + nrm((L, 2, BW), 0.05),
        'rw_ka': 1.0 + nrm((L, 2, BW), 0.05),
        'rw_g_up': nrm((L, RW_G_RANK, BW), RW_G_RANK ** -0.5),
        'rw_rk': nrm((L, RW_HEADS, RW_DH), 0.1),
        'rw_gn': gain_bias + nrm((L, 2, BW), 0.05),
        'router_w': nrm((L, D, N_EXPERTS), D ** -0.5),
        'router_b': nrm((L, N_EXPERTS), 0.01),
        'ex_w_gu': nrm((L, N_EXPERTS, D, 2 * D_EXPERT), D ** -0.5),
        'ex_w_down': nrm((L, N_EXPERTS, D_EXPERT, D), D_EXPERT ** -0.5),
        'sh_w_gu': nrm((L, D, 2 * D_SHARED), D ** -0.5),
        'sh_w_down': nrm((L, D_SHARED, D), D_SHARED ** -0.5),
    }


def reference(x, c, ctx, c_ctx, ada_w, ada_b, norm_g, w_in, w_branch, w_out, da_lambda, da_subln,
              cv_w, cv_b, cv_ln, sw_sink, rw_mu, rw_w0, rw_w_up, rw_a0, rw_a_up, rw_kk, rw_ka,
              rw_g_up, rw_rk, rw_gn, router_w, router_b, ex_w_gu, ex_w_down, sh_w_gu, sh_w_down):
    rows = x.shape[1] // GRID_W
    ang32 = grid_angles(rows, DA_HALF)
    ang64 = grid_angles(rows, SW_DH)
    sc = jax.nn.silu(c)
    scc = jax.nn.silu(c_ctx)
    xl, xc = x, ctx
    for l in range(DEPTH):
        need_ctx = l < DEPTH - 1
        lp = {'w_in': w_in[l], 'w_branch': w_branch[l], 'w_out': w_out[l],
              'da_lambda': da_lambda[l], 'da_subln': da_subln[l],
              'cv_w': cv_w[l], 'cv_b': cv_b[l], 'cv_ln': cv_ln[l], 'sw_sink': sw_sink[l],
              'rw_mu': rw_mu[l], 'rw_w0': rw_w0[l], 'rw_w_up': rw_w_up[l], 'rw_a0': rw_a0[l],
              'rw_a_up': rw_a_up[l], 'rw_kk': rw_kk[l], 'rw_ka': rw_ka[l], 'rw_g_up': rw_g_up[l],
              'rw_rk': rw_rk[l], 'rw_gn': rw_gn[l]}
        mod = jnp.split((sc @ ada_w[l] + ada_b[l])[:, None, :], 6, axis=-1)
        mod_c = jnp.split((scc @ ada_w[l] + ada_b[l])[None, None, :], 6, axis=-1)
        ng = norm_g[l]
        h = modulate(rms_norm(xl, ng[0]), mod[0], mod[1])
        hc = modulate(rms_norm(xc, ng[0]), mod_c[0], mod_c[1])
        y, y_c = token_mixer(h, hc, lp, ang32, ang64, 0.8 - 0.6 * math.exp(-0.3 * l), need_ctx)
        xl = xl + mod[2] * rms_norm(y, ng[1])
        tokens = modulate(rms_norm(xl, ng[2]), mod[3], mod[4]).reshape(-1, D_MODEL)
        if need_ctx:
            xc = xc + mod_c[2] * rms_norm(y_c, ng[1])
            hc2 = modulate(rms_norm(xc, ng[2]), mod_c[3], mod_c[4]).reshape(-1, D_MODEL)
            tokens = jnp.concatenate([tokens, hc2], axis=0)
        f = moe_ffn(tokens, router_w[l], router_b[l], ex_w_gu[l], ex_w_down[l], sh_w_gu[l], sh_w_down[l])
        n_lat = xl.shape[0] * xl.shape[1]
        xl = xl + mod[5] * rms_norm(f[:n_lat].reshape(xl.shape), ng[3])
        if need_ctx:
            xc = xc + mod_c[5] * rms_norm(f[n_lat:].reshape(xc.shape), ng[3])
    return xl
```

```python
import functools
import math

import numpy as np
import jax
import jax.numpy as jnp
from jax import lax
from jax.experimental import pallas as pl
from jax.experimental.pallas import tpu as pltpu

F32 = jnp.float32
BF16 = jnp.bfloat16

GRID_W = 64
EPS = 1e-6
ROPE_BASE = 10000.0
N_BRANCH = 4
BW = 256
DA_HEADS = 4
DA_HALF = 32
SW_HEADS = 4
SW_KV = 2
SW_DH = 64
WINDOW = 128
CONV_W = 31
RW_HEADS = 4
RW_DH = 64
RW_DECAY_RANK = 64
RW_A_RANK = 64
RW_G_RANK = 128
RW_DECAY_SCALE = math.exp(-0.5)
RW_GN_EPS = 64e-5
RW_SHIFT_W = 3 * BW + RW_DECAY_RANK + RW_A_RANK
N_EXPERTS = 64
TOP_K = 6
N_GROUPS = 8
TOPK_GROUPS = 4
D_EXPERT = 256
ROUTE_SCALE = 2.5

TILE = 256
CHUNK = 128
NEG = -1e30

C_ROPE = 0
C_SWAP = 1024
C_DAV = 2048
C_SWV = 2304
C_CV = 2560
C_RW = 3072
C_GATE = 4096
C_TOTAL = 8192

VMEM_LIMIT = 48 * 1024 * 1024

_NT = (((1,), (1,)), ((), ()))
_NN = (((1,), (0,)), ((), ()))


def _cparams(sem):
    return pltpu.CompilerParams(dimension_semantics=sem, vmem_limit_bytes=VMEM_LIMIT)


def _dot(a, b, dims=_NN):
    return lax.dot_general(a.astype(BF16), b.astype(BF16), dims, preferred_element_type=F32)


def _split2(x):
    hi = x.astype(BF16)
    lo = (x - hi.astype(F32)).astype(BF16)
    return hi, lo


def _dot3(a, b, dims=_NN):
    ah, al = _split2(a)
    bh, bl = _split2(b)
    dg = lambda x, y: lax.dot_general(x, y, dims, preferred_element_type=F32)
    return dg(ah, bh) + (dg(ah, bl) + dg(al, bh))


def _dot_sel(x, sel, dims=_NN):
    h0 = x.astype(BF16)
    r1 = x - h0.astype(F32)
    h1 = r1.astype(BF16)
    h2 = (r1 - h1.astype(F32)).astype(BF16)
    dg = lambda y: lax.dot_general(y, sel, dims, preferred_element_type=F32)
    return dg(h0) + (dg(h1) + dg(h2))


def _sel_dot(sel, x):
    h0 = x.astype(BF16)
    r1 = x - h0.astype(F32)
    h1 = r1.astype(BF16)
    h2 = (r1 - h1.astype(F32)).astype(BF16)
    dg = lambda y: lax.dot_general(sel, y, _NN, preferred_element_type=F32)
    return dg(h0) + (dg(h1) + dg(h2))


def _sigmoid(x):
    return jax.nn.sigmoid(x)


def _silu(x):
    return x * jax.nn.sigmoid(x)


def _head_of_lane(shape, width):
    return lax.broadcasted_iota(jnp.int32, shape, len(shape) - 1) // width


def _ada_kernel(s_ref, w_ref, b_ref, o_ref):
    s = _silu(s_ref[...])
    o_ref[0] = _dot(s, w_ref[0]) + b_ref[0]


def _ada_call(cond, ada_w, ada_b):
    depth, d, cols = ada_w.shape
    tn = 1536
    return pl.pallas_call(
        _ada_kernel,
        grid=(depth, cols // tn),
        in_specs=[pl.BlockSpec((8, d), lambda l, j: (0, 0)),
                  pl.BlockSpec((1, d, tn), lambda l, j: (l, 0, j)),
                  pl.BlockSpec((1, 1, tn), lambda l, j: (l, 0, j))],
        out_specs=pl.BlockSpec((1, 8, tn), lambda l, j: (l, 0, j)),
        out_shape=jax.ShapeDtypeStruct((depth, 8, cols), F32),
        compiler_params=_cparams(("arbitrary", "arbitrary")),
        name="ada_mod",
    )(cond, ada_w, ada_b.reshape(depth, 1, cols))


def _norm_mod_kernel(x_ref, g_ref, sh_ref, sc_ref, o_ref, *o32):
    x = x_ref[...]
    y = x * lax.rsqrt(jnp.mean(x * x, axis=-1, keepdims=True) + EPS) * g_ref[...]
    h = y * (1.0 + sc_ref[0]) + sh_ref[0]
    o_ref[...] = h.astype(BF16)
    if o32:
        o32[0][...] = h


def _norm_mod_call(x, g, shift, scale, tpb, n_batch, want_f32):
    n, d = x.shape
    kind = lambda i: (jnp.where(i % tpb == 0, n_batch, i // tpb), 0, 0)
    row = pl.BlockSpec((TILE, d), lambda i: (i, 0))
    out_shape = [jax.ShapeDtypeStruct((n, d), BF16)]
    out_specs = [row]
    if want_f32:
        out_shape.append(jax.ShapeDtypeStruct((n, d), F32))
        out_specs.append(row)
    return pl.pallas_call(
        _norm_mod_kernel,
        grid=(n // TILE,),
        in_specs=[row, pl.BlockSpec((1, d), lambda i: (0, 0)),
                  pl.BlockSpec((1, 1, d), kind), pl.BlockSpec((1, 1, d), kind)],
        out_specs=out_specs,
        out_shape=out_shape,
        compiler_params=_cparams(("arbitrary",)),
        name="norm_mod",
    )(x, g.reshape(1, d), shift, scale)


def _mm_kernel(a_ref, w_ref, o_ref):
    o_ref[...] = jnp.dot(a_ref[...], w_ref[...], preferred_element_type=F32)


def _proj_call(h, w):
    n, d = h.shape
    cols = w.shape[1]
    tm, tn = 2 * TILE, 1024
    return pl.pallas_call(
        _mm_kernel,
        grid=(cols // tn, n // tm),
        in_specs=[pl.BlockSpec((tm, d), lambda j, i: (i, 0)),
                  pl.BlockSpec((d, tn), lambda j, i: (0, j))],
        out_specs=pl.BlockSpec((tm, tn), lambda j, i: (i, j)),
        out_shape=jax.ShapeDtypeStruct((n, cols), F32),
        compiler_params=_cparams(("arbitrary", "arbitrary")),
        name="in_proj",
    )(h, w)


def _rope_kernel(um_ref, us_ref, c_ref, s_ref, o_ref):
    o_ref[...] = (um_ref[...] * c_ref[...] + us_ref[...] * s_ref[...]).astype(BF16)


def _rope_call(u, cos_t, sin_t, tpb):
    n = u.shape[0]
    w = cos_t.shape[1]
    tab = pl.BlockSpec((TILE, w), lambda i: (i % tpb, 0))
    return pl.pallas_call(
        _rope_kernel,
        grid=(n // TILE,),
        in_specs=[pl.BlockSpec((TILE, w), lambda i: (i, C_ROPE // w)),
                  pl.BlockSpec((TILE, w), lambda i: (i, C_SWAP // w)), tab, tab],
        out_specs=pl.BlockSpec((TILE, w), lambda i: (i, 0)),
        out_shape=jax.ShapeDtypeStruct((n, w), BF16),
        compiler_params=_cparams(("arbitrary",)),
        name="rope",
    )(u, u, cos_t, sin_t)


def _da_kernel(lam_ref, q_ref, k_ref, v_ref, g_ref, bd_ref, o_ref,
               qs_ref, m_ref, l_ref, acc_ref, *, nkt, lam_init):
    i = pl.program_id(1)
    tq = q_ref.shape[0]
    scale = DA_HALF ** -0.5
    q = q_ref[...]
    qmap = _head_of_lane((tq, BW), DA_HALF)
    for g in range(2 * DA_HEADS):
        qs_ref[g] = jnp.where(qmap == g, q, jnp.zeros_like(q))
    m_ref[...] = jnp.full(m_ref.shape, NEG, F32)
    l_ref[...] = jnp.zeros(l_ref.shape, F32)
    acc_ref[...] = jnp.zeros(acc_ref.shape, F32)
    head_col = _head_of_lane((tq, BW), 2 * DA_HALF)

    def per_head(vals):
        out = vals[DA_HEADS - 1]
        for h in range(DA_HEADS - 2, -1, -1):
            out = jnp.where(head_col == h, vals[h], out)
        return out

    def body(j, carry):
        off = pl.multiple_of(j * TILE, TILE)
        kt = k_ref[pl.ds(off, TILE), :]
        vt = v_ref[pl.ds(off, TILE), :].astype(BF16)
        vhead = _head_of_lane((TILE, BW), 2 * DA_HALF)
        vstack = jnp.concatenate(
            [jnp.where(vhead == h, vt, jnp.zeros_like(vt)) for h in range(DA_HEADS)], axis=0)
        for mm in range(2):
            ps, alphas = [], []
            for h in range(DA_HEADS):
                g = 2 * h + mm
                s = lax.dot_general(qs_ref[g], kt, _NT, preferred_element_type=F32) * scale
                m_old = m_ref[g]
                m_new = jnp.maximum(m_old, jnp.max(s, axis=1, keepdims=True))
                alpha = jnp.exp(m_old - m_new)
                p = jnp.exp(s - m_new)
                l_ref[g] = l_ref[g] * alpha + jnp.sum(p, axis=1, keepdims=True)
                m_ref[g] = m_new
                ps.append(p.astype(BF16))
                alphas.append(alpha)
            pv = jnp.dot(jnp.concatenate(ps, axis=1), vstack, preferred_element_type=F32)
            acc_ref[mm] = acc_ref[mm] * per_head(alphas) + pv
        return carry

    lax.fori_loop(0, jnp.where(i == 0, 1, nkt), body, 0)

    lam = lam_ref[0]
    l1 = per_head([l_ref[2 * h] for h in range(DA_HEADS)])
    l2 = per_head([l_ref[2 * h + 1] for h in range(DA_HEADS)])
    o = acc_ref[0] / l1 - lam * (acc_ref[1] / l2)
    ms = _dot_sel(o * o, bd_ref[...]) * (1.0 / (2 * DA_HALF))
    y = o * lax.rsqrt(ms + EPS) * g_ref[...]
    o_ref[...] = y * (1.0 - lam_init)


def _da_call(lam, qk, u, subln, bd, n_batch, tpb, lam_init):
    n = qk.shape[0]
    rows_b = tpb * TILE
    kern = functools.partial(_da_kernel, nkt=tpb, lam_init=lam_init)
    return pl.pallas_call(
        kern,
        grid=(n_batch, tpb),
        in_specs=[pl.BlockSpec(memory_space=pltpu.SMEM),
                  pl.BlockSpec((TILE, BW), lambda b, i: (b * tpb + i, 0)),
                  pl.BlockSpec((rows_b, BW), lambda b, i: (b, 1)),
                  pl.BlockSpec((rows_b, BW), lambda b, i: (b, C_DAV // BW)),
                  pl.BlockSpec((1, BW), lambda b, i: (0, 0)),
                  pl.BlockSpec((BW, BW), lambda b, i: (0, 0))],
        out_specs=pl.BlockSpec((TILE, BW), lambda b, i: (b * tpb + i, 0)),
        out_shape=jax.ShapeDtypeStruct((n, BW), F32),
        scratch_shapes=[pltpu.VMEM((2 * DA_HEADS, TILE, BW), BF16),
                        pltpu.VMEM((2 * DA_HEADS, TILE, 1), F32),
                        pltpu.VMEM((2 * DA_HEADS, TILE, 1), F32),
                        pltpu.VMEM((2, TILE, BW), F32)],
        compiler_params=_cparams(("arbitrary", "arbitrary")),
        name="diff_attn",
    )(lam, qk, qk, u, subln, bd)


def _sw_kernel(sink_ref, q_ref, kp_ref, ko_ref, kn_ref, kc_ref,
               vp_ref, vo_ref, vn_ref, vc_ref, o_ref, *, cpb):
    j = pl.program_id(1)
    tq = q_ref.shape[0]
    ctx_chunks = TILE // CHUNK
    q = q_ref[...]
    kk = jnp.concatenate([kp_ref[...], ko_ref[...], kn_ref[...], kc_ref[...]], axis=0)
    vv = jnp.concatenate([vp_ref[...], vo_ref[...], vn_ref[...], vc_ref[...]], axis=0).astype(BF16)
    nk = kk.shape[0]
    r = lax.broadcasted_iota(jnp.int32, (tq, nk), 0)
    c = lax.broadcasted_iota(jnp.int32, (tq, nk), 1)
    is_lat = j >= ctx_chunks
    far = 4 * nk
    lo_prev = jnp.where(is_lat & (j - 1 >= ctx_chunks), 0, far)
    hi_own = jnp.where(is_lat, 2 * CHUNK, 0)
    hi_next = jnp.where(is_lat & (j + 1 < cpb), 2 * CHUNK, -far)
    valid = ((c >= 3 * CHUNK)
             | ((c < CHUNK) & (c >= r + lo_prev))
             | ((c >= CHUNK) & (c < hi_own))
             | ((c >= 2 * CHUNK) & (c < 3 * CHUNK) & (c <= r + hi_next)))
    qhead = _head_of_lane((tq, BW), SW_DH)
    vhead = _head_of_lane((nk, BW), SW_DH)
    ps, vs = [], []
    for h in range(SW_HEADS):
        qm = jnp.where(qhead == h, q, jnp.zeros_like(q))
        s = lax.dot_general(qm, kk, _NT, preferred_element_type=F32) * (SW_DH ** -0.5)
        s = jnp.where(valid, s, NEG)
        sk = sink_ref[h]
        m = jnp.maximum(jnp.max(s, axis=1, keepdims=True), sk)
        p = jnp.exp(s - m)
        den = jnp.sum(p, axis=1, keepdims=True) + jnp.exp(sk - m)
        ps.append((p / den).astype(BF16))
        vs.append(jnp.where(vhead == h, vv, jnp.zeros_like(vv)))
    o_ref[...] = jnp.dot(jnp.concatenate(ps, axis=1), jnp.concatenate(vs, axis=0),
                         preferred_element_type=F32)


def _sw_call(sink, qk, u, n_batch, tpb):
    n = qk.shape[0]
    cpb = tpb * (TILE // CHUNK)
    cur = lambda b, j: b * cpb + j
    prv = lambda b, j: b * cpb + jnp.maximum(j - 1, 0)
    nxt = lambda b, j: b * cpb + jnp.minimum(j + 1, cpb - 1)
    kcol, vcol = 3, C_SWV // BW
    blk = lambda f, col: pl.BlockSpec((CHUNK, BW), lambda b, j: (f(b, j), col))
    ctx = lambda col: pl.BlockSpec((TILE, BW), lambda b, j: (b * tpb, col))
    return pl.pallas_call(
        functools.partial(_sw_kernel, cpb=cpb),
        grid=(n_batch, cpb),
        in_specs=[pl.BlockSpec(memory_space=pltpu.SMEM),
                  blk(cur, 2), blk(prv, kcol), blk(cur, kcol), blk(nxt, kcol), ctx(kcol),
                  blk(prv, vcol), blk(cur, vcol), blk(nxt, vcol), ctx(vcol)],
        out_specs=pl.BlockSpec((CHUNK, BW), lambda b, j: (cur(b, j), 0)),
        out_shape=jax.ShapeDtypeStruct((n, BW), F32),
        compiler_params=_cparams(("arbitrary", "arbitrary")),
        name="window_attn",
    )(sink, qk, qk, qk, qk, qk, u, u, u, u)


def _conv_kernel(prev_ref, cur_ref, next_ref, w_ref, b_ref, ln_ref, o_ref, z_ref, *, tpb):
    i = pl.program_id(0)
    pos = i % tpb
    has_prev = pos >= 2
    has_next = (pos >= 1) & (pos < tpb - 1)
    halo = prev_ref.shape[0]

    def glu(x):
        return x[:, :BW] * _sigmoid(x[:, BW:])

    zp = glu(prev_ref[...])
    zn = glu(next_ref[...])
    z_ref[0:halo, :] = jnp.where(has_prev, zp, 0.0)
    z_ref[halo:halo + TILE, :] = glu(cur_ref[...])
    z_ref[halo + TILE:2 * halo + TILE, :] = jnp.where(has_next, zn, 0.0)
    acc = jnp.zeros((TILE, BW), F32) + b_ref[...]
    pad = CONV_W // 2
    for t in range(CONV_W):
        acc = acc + z_ref[halo - pad + t:halo - pad + t + TILE, :] * w_ref[t:t + 1, :]
    mu = jnp.mean(acc, axis=-1, keepdims=True)
    xc = acc - mu
    y = xc * lax.rsqrt(jnp.mean(xc * xc, axis=-1, keepdims=True) + EPS)
    y = y * ln_ref[0:1, :] + ln_ref[1:2, :]
    o_ref[...] = _silu(y)


def _conv_call(u, w, b, ln, tpb):
    n = u.shape[0]
    halo = 16
    per = TILE // halo
    nh = n // halo
    wcv = 2 * BW
    col = C_CV // wcv
    return pl.pallas_call(
        functools.partial(_conv_kernel, tpb=tpb),
        grid=(n // TILE,),
        in_specs=[pl.BlockSpec((halo, wcv), lambda i: (jnp.maximum(i * per - 1, 0), col)),
                  pl.BlockSpec((TILE, wcv), lambda i: (i, col)),
                  pl.BlockSpec((halo, wcv), lambda i: (jnp.minimum((i + 1) * per, nh - 1), col)),
                  pl.BlockSpec((CONV_W, BW), lambda i: (0, 0)),
                  pl.BlockSpec((1, BW), lambda i: (0, 0)),
                  pl.BlockSpec((2, BW), lambda i: (0, 0))],
        out_specs=pl.BlockSpec((TILE, BW), lambda i: (i, 0)),
        out_shape=jax.ShapeDtypeStruct((n, BW), F32),
        scratch_shapes=[pltpu.VMEM((TILE + 2 * halo, BW), F32)],
        compiler_params=_cparams(("arbitrary",)),
        name="conformer_conv",
    )(u, u, u, w, b.reshape(1, BW), ln)


def _rw_chunk_kernel(prev_ref, cur_ref, next_ref, mu_ref, w0_ref, a0_ref, kk_ref, ka_ref,
                     wup_ref, aup_ref, rk_ref, bd_ref,
                     m_ref, n_ref, q_ref, y0_ref, bonus_ref, z_ref, *, cpb):
    i = pl.program_id(0)
    d = pl.program_id(1)
    jj = i % cpb
    ctx_chunks = TILE // CHUNK
    has_prev = (jj != 0) & (jj != ctx_chunks)
    has_next = (jj != ctx_chunks - 1) & (jj != cpb - 1)
    c = CHUNK
    zc = cur_ref[:, 0:RW_SHIFT_W]
    z_ref[0:8, :] = jnp.where(has_prev, prev_ref[:, 0:RW_SHIFT_W], 0.0)
    z_ref[8:8 + c, :] = zc
    z_ref[8 + c:16 + c, :] = jnp.where(has_next, next_ref[:, 0:RW_SHIFT_W], 0.0)
    zsh = jnp.where(d == 0, z_ref[7:7 + c, :], z_ref[9:9 + c, :])
    zs = zc + (zsh - zc) * mu_ref[0]
    r = zs[:, 0:BW]
    k = zs[:, BW:2 * BW]
    v = zs[:, 2 * BW:3 * BW]
    wa = zs[:, 3 * BW:RW_SHIFT_W]
    bd = bd_ref[...]
    lw = -RW_DECAY_SCALE * _sigmoid(w0_ref[0] + _dot(jnp.tanh(wa), wup_ref[0]))
    a = _sigmoid(a0_ref[0] + _dot(wa, aup_ref[0]))
    kappa = k * kk_ref[0]
    kh = kappa * lax.rsqrt(jnp.maximum(_dot_sel(kappa * kappa, bd), 1e-12))
    kt = k * (1.0 + (a - 1.0) * ka_ref[0])
    akh = a * kh
    bonus_ref[0] = _dot_sel(r * kt * rk_ref[...], bd) * v

    t_io = lax.broadcasted_iota(jnp.int32, (c, c), 0)
    s_io = lax.broadcasted_iota(jnp.int32, (c, c), 1)
    ahead = (t_io - s_io) * jnp.where(d == 0, 1, -1)
    earlier = ahead > 0
    upto = ahead >= 0
    tri = jnp.where(upto, 1.0, 0.0).astype(BF16)
    cl = _sel_dot(tri, lw)
    tot = jnp.sum(lw, axis=0, keepdims=True)
    rho = 0.5 * tot
    cle = cl - lw
    a_true = -kh * jnp.exp(cle)
    r_true = r * jnp.exp(cl)
    a_c = -kh * jnp.exp(cle - rho)
    r_c = r * jnp.exp(cl - rho)
    b_c = akh * jnp.exp(rho - cl)
    k_c = kt * jnp.exp(rho - cl)
    b_end = akh * jnp.exp(tot - cl)
    k_end = kt * jnp.exp(tot - cl)

    lane_head = _head_of_lane((c, BW), RW_DH)

    def stack(x):
        return jnp.concatenate(
            [jnp.where(lane_head == h, x, 0.0) for h in range(RW_HEADS)], axis=0)

    pair = _dot3(jnp.concatenate([stack(a_c), stack(r_c)], axis=0),
                 jnp.concatenate([b_c, k_c], axis=0), _NT)
    eye = jnp.where(t_io == s_io, 1.0, 0.0)
    t_inv, l_ak, a_rb, a_rk = [], [], [], []
    for h in range(RW_HEADS):
        blk_a = pair[h * c:(h + 1) * c]
        blk_r = pair[(RW_HEADS + h) * c:(RW_HEADS + h + 1) * c]
        l_ab = jnp.where(earlier, blk_a[:, 0:c], 0.0)
        l_ak.append(jnp.where(earlier, blk_a[:, c:2 * c], 0.0))
        a_rb.append(jnp.where(upto, blk_r[:, 0:c], 0.0))
        a_rk.append(jnp.where(upto, blk_r[:, c:2 * c], 0.0))
        t = eye + l_ab
        lp = l_ab
        step = 1
        while step < c // 2:
            lp = _dot3(lp, lp)
            t = t + _dot3(t, lp)
            step *= 2
        t_inv.append(t)
    cat = lambda xs: jnp.concatenate(xs, axis=1)
    lv = _dot3(cat(l_ak), stack(v))
    w12 = _dot3(cat(t_inv), jnp.concatenate([stack(a_true), stack(lv)], axis=1))
    w1 = w12[:, 0:BW]
    w2 = w12[:, BW:2 * BW]
    arb = cat(a_rb)
    q_ref[0, 0] = r_true + _dot3(arb, stack(w1))
    y0_ref[0, 0] = _dot3(arb, stack(w2)) + _dot3(cat(a_rk), stack(v))
    row = lax.broadcasted_iota(jnp.int32, (BW, BW), 0)
    colm = lax.broadcasted_iota(jnp.int32, (BW, BW), 1)
    same_head = (row // RW_DH) == (colm // RW_DH)
    b_end_t = b_end.T
    decay = jnp.where(row == colm, jnp.broadcast_to(jnp.exp(tot), (BW, BW)), 0.0)
    m_ref[0, 0] = jnp.where(same_head, _dot3(b_end_t, w1), 0.0) + decay
    n_ref[0, 0] = jnp.where(same_head, _dot3(b_end_t, w2) + _dot3(k_end.T, v), 0.0)


def _rw_chunk_call(u, lp, bd, tpb):
    n = u.shape[0]
    cpb = tpb * (TILE // CHUNK)
    nch = n // CHUNK
    per = CHUNK // 8
    n8 = n // 8
    wrw = 1024
    col = C_RW // wrw
    pvec = lambda w: pl.BlockSpec((1, 1, w), lambda i, d: (d, 0, 0))
    pad = jnp.zeros((2, RW_DECAY_RANK, BW), F32)
    wup = jnp.concatenate([lp['rw_w_up'], pad], axis=1).astype(BF16)
    aup = jnp.concatenate([pad, lp['rw_a_up']], axis=1).astype(BF16)
    mat = lambda rows: pl.BlockSpec((1, 1, rows, BW), lambda i, d: (d, i, 0, 0))
    return pl.pallas_call(
        functools.partial(_rw_chunk_kernel, cpb=cpb),
        grid=(nch, 2),
        in_specs=[pl.BlockSpec((8, wrw), lambda i, d: (jnp.maximum(i * per - 1, 0), col)),
                  pl.BlockSpec((CHUNK, wrw), lambda i, d: (i, col)),
                  pl.BlockSpec((8, wrw), lambda i, d: (jnp.minimum((i + 1) * per, n8 - 1), col)),
                  pvec(RW_SHIFT_W), pvec(BW), pvec(BW), pvec(BW), pvec(BW),
                  pl.BlockSpec((1, 2 * RW_DECAY_RANK, BW), lambda i, d: (d, 0, 0)),
                  pl.BlockSpec((1, 2 * RW_A_RANK, BW), lambda i, d: (d, 0, 0)),
                  pl.BlockSpec((1, BW), lambda i, d: (0, 0)),
                  pl.BlockSpec((BW, BW), lambda i, d: (0, 0))],
        out_specs=[mat(BW), mat(BW), mat(CHUNK), mat(CHUNK),
                   pl.BlockSpec((1, CHUNK, BW), lambda i, d: (d, i, 0))],
        out_shape=[jax.ShapeDtypeStruct((2, nch, BW, BW), F32),
                   jax.ShapeDtypeStruct((2, nch, BW, BW), F32),
                   jax.ShapeDtypeStruct((2, nch, CHUNK, BW), F32),
                   jax.ShapeDtypeStruct((2, nch, CHUNK, BW), F32),
                   jax.ShapeDtypeStruct((2, n, BW), F32)],
        scratch_shapes=[pltpu.VMEM((CHUNK + 16, RW_SHIFT_W), F32)],
        compiler_params=_cparams(("arbitrary", "arbitrary")),
        name="rwkv_chunk",
    )(u, u, u, lp['rw_mu'].reshape(2, 1, RW_SHIFT_W), lp['rw_w0'].reshape(2, 1, BW),
      lp['rw_a0'].reshape(2, 1, BW), lp['rw_kk'].reshape(2, 1, BW), lp['rw_ka'].reshape(2, 1, BW),
      wup, aup, lp['rw_rk'].reshape(1, BW), bd)


def _rw_scan_kernel(m_ref, n_ref, q_ref, y0_ref, y_ref, x_ref):
    @pl.when(pl.program_id(2) == 0)
    def _():
        x_ref[...] = jnp.zeros(x_ref.shape, F32)

    x = x_ref[...]
    y_ref[0] = _dot3(q_ref[0, 0], x) + y0_ref[0, 0]
    x_ref[...] = _dot3(m_ref[0, 0], x) + n_ref[0, 0]


def _rw_scan_call(m, nn, q, y0, n_batch, tpb):
    nch = m.shape[1]
    cpb = tpb * (TILE // CHUNK)
    ctx_chunks = TILE // CHUNK

    def chunk(b, d, i):
        rev = jnp.where(i < ctx_chunks, ctx_chunks - 1 - i, cpb + ctx_chunks - 1 - i)
        return b * cpb + jnp.where(d == 0, i, rev)

    mat = lambda rows: pl.BlockSpec((1, 1, rows, BW), lambda b, d, i: (d, chunk(b, d, i), 0, 0))
    return pl.pallas_call(
        _rw_scan_kernel,
        grid=(n_batch, 2, cpb),
        in_specs=[mat(BW), mat(BW), mat(CHUNK), mat(CHUNK)],
        out_specs=pl.BlockSpec((1, CHUNK, BW), lambda b, d, i: (d, chunk(b, d, i), 0)),
        out_shape=jax.ShapeDtypeStruct((2, nch * CHUNK, BW), F32),
        scratch_shapes=[pltpu.VMEM((BW, BW), F32)],
        compiler_params=_cparams(("arbitrary", "arbitrary", "arbitrary")),
        name="rwkv_scan",
    )(m, nn, q, y0)


def _rw_out_kernel(y_ref, bonus_ref, u_ref, gup_ref, gn_ref, bd_ref, o_ref):
    bd = bd_ref[...]
    y = y_ref[0] + y_ref[1]
    mean = _dot_sel(y, bd) * (1.0 / RW_DH)
    yc = y - mean
    var = _dot_sel(yc * yc, bd) * (1.0 / RW_DH)
    yn = yc * lax.rsqrt(var + RW_GN_EPS) * gn_ref[0:1, :] + gn_ref[1:2, :]
    yn = yn + (bonus_ref[0] + bonus_ref[1])
    gd = u_ref[:, RW_SHIFT_W:RW_SHIFT_W + RW_G_RANK]
    o_ref[...] = yn * _dot(_sigmoid(gd), gup_ref[...])


def _rw_out_call(y, bonus, u, gup, gn, bd):
    n = u.shape[0]
    wrw = 1024
    both = pl.BlockSpec((2, TILE, BW), lambda i: (0, i, 0))
    return pl.pallas_call(
        _rw_out_kernel,
        grid=(n // TILE,),
        in_specs=[both, both, pl.BlockSpec((TILE, wrw), lambda i: (i, C_RW // wrw)),
                  pl.BlockSpec((RW_G_RANK, BW), lambda i: (0, 0)),
                  pl.BlockSpec((2, BW), lambda i: (0, 0)),
                  pl.BlockSpec((BW, BW), lambda i: (0, 0))],
        out_specs=pl.BlockSpec((TILE, BW), lambda i: (i, 0)),
        out_shape=jax.ShapeDtypeStruct((n, BW), F32),
        compiler_params=_cparams(("arbitrary",)),
        name="rwkv_readout",
    )(y, bonus, u, gup, gn, bd)


def _merge_kernel(oa_ref, ob_ref, oc_ref, od_ref, gl_ref, wb_ref, wo_ref, x_ref, g_ref, mod_ref,
                  o_ref):
    d = x_ref.shape[1]
    m = None
    for i, o in enumerate((oa_ref, ob_ref, oc_ref, od_ref)):
        t = _sigmoid(gl_ref[:, i * d:(i + 1) * d]) * _dot(o[...], wb_ref[i])
        m = t if m is None else m + t
    y = _dot(m, wo_ref[...])
    y = y * lax.rsqrt(jnp.mean(y * y, axis=-1, keepdims=True) + EPS) * g_ref[...]
    o_ref[...] = x_ref[...] + mod_ref[0] * y


def _merge_call(outs, u, wb, wo, x, g, mod, tpb, n_batch):
    n, d = x.shape
    kind = lambda i: (jnp.where(i % tpb == 0, n_batch, i // tpb), 0, 0)
    br = pl.BlockSpec((TILE, BW), lambda i: (i, 0))
    wg = N_BRANCH * d
    return pl.pallas_call(
        _merge_kernel,
        grid=(n // TILE,),
        in_specs=[br, br, br, br,
                  pl.BlockSpec((TILE, wg), lambda i: (i, C_GATE // wg)),
                  pl.BlockSpec((N_BRANCH, BW, d), lambda i: (0, 0, 0)),
                  pl.BlockSpec((d, d), lambda i: (0, 0)),
                  pl.BlockSpec((TILE, d), lambda i: (i, 0)),
                  pl.BlockSpec((1, d), lambda i: (0, 0)),
                  pl.BlockSpec((1, 1, d), kind)],
        out_specs=pl.BlockSpec((TILE, d), lambda i: (i, 0)),
        out_shape=jax.ShapeDtypeStruct((n, d), F32),
        compiler_params=_cparams(("arbitrary",)),
        name="merge",
    )(*outs, u, wb, wo, x, g.reshape(1, d), mod)


def _router_kernel(t_ref, w_ref, b_ref, o_ref):
    tm = t_ref.shape[0]
    gsz = N_EXPERTS // N_GROUPS
    logits = _dot3(w_ref[...], t_ref[...], _NT)
    sc = _sigmoid(logits).reshape(N_GROUPS, gsz, tm)
    bi = sc + b_ref[...].reshape(N_GROUPS, gsz, 1)
    shape = (N_GROUPS, gsz, tm)
    g_io = lax.broadcasted_iota(jnp.int32, shape, 0)
    j_io = lax.broadcasted_iota(jnp.int32, shape, 1)
    e_io = g_io * gsz + j_io
    ninf = -jnp.inf
    m1 = jnp.max(bi, axis=1, keepdims=True)
    i1 = jnp.min(jnp.where(bi == m1, j_io, gsz), axis=1, keepdims=True)
    m2 = jnp.max(jnp.where(j_io == i1, ninf, bi), axis=1, keepdims=True)
    cur = jnp.broadcast_to(m1 + m2, shape)
    gsel = jnp.zeros(shape, F32)
    for _ in range(TOPK_GROUPS):
        mx = jnp.max(cur, axis=0, keepdims=True)
        ix = jnp.min(jnp.where(cur == mx, g_io, N_GROUPS), axis=0, keepdims=True)
        hit = g_io == ix
        gsel = jnp.where(hit, 1.0, gsel)
        cur = jnp.where(hit, ninf, cur)
    cur = jnp.where(gsel > 0.0, bi, ninf)
    esel = jnp.zeros(shape, F32)
    for _ in range(TOP_K):
        mx = jnp.max(jnp.max(cur, axis=0, keepdims=True), axis=1, keepdims=True)
        ix = jnp.min(jnp.min(jnp.where(cur == mx, e_io, N_EXPERTS), axis=0, keepdims=True),
                     axis=1, keepdims=True)
        hit = e_io == ix
        esel = jnp.where(hit, 1.0, esel)
        cur = jnp.where(hit, ninf, cur)
    wsel = sc * esel
    den = jnp.sum(jnp.sum(wsel, axis=0, keepdims=True), axis=1, keepdims=True)
    o_ref[...] = (wsel / den * ROUTE_SCALE).reshape(N_EXPERTS, tm)


def _router_call(tokens, rw_t, rb):
    n, d = tokens.shape
    return pl.pallas_call(
        _router_kernel,
        grid=(n // TILE,),
        in_specs=[pl.BlockSpec((TILE, d), lambda i: (i, 0)),
                  pl.BlockSpec((N_EXPERTS, d), lambda i: (0, 0)),
                  pl.BlockSpec((N_EXPERTS, 1), lambda i: (0, 0))],
        out_specs=pl.BlockSpec((N_EXPERTS, TILE), lambda i: (0, i)),
        out_shape=jax.ShapeDtypeStruct((N_EXPERTS, n), F32),
        compiler_params=_cparams(("arbitrary",)),
        name="router",
    )(tokens, rw_t, rb.reshape(N_EXPERTS, 1))


def _moe_kernel(x_ref, g_ref, wgu_ref, wd_ref, o_ref, acc_ref):
    e = pl.program_id(1)

    @pl.when(e == 0)
    def _():
        acc_ref[...] = jnp.zeros(acc_ref.shape, F32)

    hgu = jnp.dot(x_ref[...], wgu_ref[0], preferred_element_type=F32)
    act = _silu(hgu[:, :D_EXPERT]) * hgu[:, D_EXPERT:]
    gates = g_ref[...]
    lane = lax.broadcasted_iota(jnp.int32, gates.shape, 1)
    gcol = jnp.sum(jnp.where(lane == e, gates, 0.0), axis=1, keepdims=True)
    gcol = jnp.where(e == N_EXPERTS, 1.0, gcol)
    acc_ref[...] += _dot(act, wd_ref[0]) * gcol

    @pl.when(e == N_EXPERTS)
    def _():
        o_ref[...] = acc_ref[...]


def _moe_call(tok, gates, wgu, wd, tm):
    n, d = tok.shape
    ne = wgu.shape[0]
    return pl.pallas_call(
        _moe_kernel,
        grid=(n // tm, ne),
        in_specs=[pl.BlockSpec((tm, d), lambda i, e: (i, 0)),
                  pl.BlockSpec((tm, N_EXPERTS), lambda i, e: (i, 0)),
                  pl.BlockSpec((1, d, 2 * D_EXPERT), lambda i, e: (e, 0, 0)),
                  pl.BlockSpec((1, D_EXPERT, d), lambda i, e: (e, 0, 0))],
        out_specs=pl.BlockSpec((tm, d), lambda i, e: (i, 0)),
        out_shape=jax.ShapeDtypeStruct((n, d), F32),
        scratch_shapes=[pltpu.VMEM((tm, d), F32)],
        compiler_params=_cparams(("arbitrary", "arbitrary")),
        name="moe_experts",
    )(tok, gates, wgu, wd)


def _resid_kernel(x_ref, f_ref, g_ref, mod_ref, o_ref):
    f = f_ref[...]
    y = f * lax.rsqrt(jnp.mean(f * f, axis=-1, keepdims=True) + EPS) * g_ref[...]
    o_ref[...] = x_ref[...] + mod_ref[0] * y


def _resid_call(x, f, g, mod, tpb, n_batch):
    n, d = x.shape
    kind = lambda i: (jnp.where(i % tpb == 0, n_batch, i // tpb), 0, 0)
    row = pl.BlockSpec((TILE, d), lambda i: (i, 0))
    return pl.pallas_call(
        _resid_kernel,
        grid=(n // TILE,),
        in_specs=[row, row, pl.BlockSpec((1, d), lambda i: (0, 0)), pl.BlockSpec((1, 1, d), kind)],
        out_specs=row,
        out_shape=jax.ShapeDtypeStruct((n, d), F32),
        compiler_params=_cparams(("arbitrary",)),
        name="moe_residual",
    )(x, f, g.reshape(1, d), mod)


def _partner(dim, nblocks):
    q = dim // 4
    base = np.concatenate([np.arange(q) + q, np.arange(q), np.arange(q) + 3 * q, np.arange(q) + 2 * q])
    return (np.arange(nblocks)[:, None] * dim + base[None, :]).reshape(-1)


def _proj_columns():
    da, cv, sw = 0, 3 * BW, 5 * BW
    rw = sw + (SW_HEADS + 2 * SW_KV) * SW_DH
    gate = rw + RW_SHIFT_W + RW_G_RANK
    rep = np.repeat(np.arange(SW_KV), SW_HEADS // SW_KV)
    kv_rep = (rep[:, None] * SW_DH + np.arange(SW_DH)[None, :]).reshape(-1)
    da_q = da + np.arange(BW)
    da_k = da + BW + np.arange(BW)
    da_v = da + 2 * BW + np.arange(BW)
    sw_q = sw + np.arange(BW)
    sw_k = sw + BW + kv_rep
    sw_v = sw + BW + SW_KV * SW_DH + kv_rep
    p32 = _partner(DA_HALF, BW // DA_HALF)
    p64 = _partner(SW_DH, BW // SW_DH)
    cols = np.concatenate([
        da_q, da_k, sw_q, sw_k,
        da_q[p32], da_k[p32], sw_q[p64], sw_k[p64],
        da_v, sw_v,
        cv + np.arange(2 * BW),
        rw + np.arange(RW_SHIFT_W + RW_G_RANK),
        gate + np.arange(N_BRANCH * 4 * BW)])
    assert cols.shape[0] == C_TOTAL
    return cols


def _rope_tables(seq):
    rows = seq // GRID_W
    row = jnp.repeat(jnp.arange(rows, dtype=F32), GRID_W)
    colp = jnp.tile(jnp.arange(GRID_W, dtype=F32), rows)

    def tables(dim, reps):
        q = dim // 4
        freqs = ROPE_BASE ** (-jnp.arange(q, dtype=F32) / q)
        ar, ac = row[:, None] * freqs, colp[:, None] * freqs
        cos = jnp.concatenate([jnp.cos(ar), jnp.cos(ar), jnp.cos(ac), jnp.cos(ac)], axis=1)
        sin = jnp.concatenate([-jnp.sin(ar), jnp.sin(ar), -jnp.sin(ac), jnp.sin(ac)], axis=1)
        return jnp.tile(cos, (1, reps)), jnp.tile(sin, (1, reps))

    c32, s32 = tables(DA_HALF, 2 * BW // DA_HALF)
    c64, s64 = tables(SW_DH, 2 * BW // SW_DH)
    cos = jnp.concatenate([c32, c64], axis=1)
    sin = jnp.concatenate([s32, s64], axis=1)
    w = cos.shape[1]
    cos = jnp.concatenate([jnp.ones((TILE, w), F32), cos], axis=0)
    sin = jnp.concatenate([jnp.zeros((TILE, w), F32), sin], axis=0)
    return cos, sin


def _moe_tile(rows_b):
    best = TILE
    for t in range(TILE, 1200, 8):
        if rows_b % t == 0:
            best = t
    return best


def kernel(x, c, ctx, c_ctx, ada_w, ada_b, norm_g, w_in, w_branch, w_out, da_lambda, da_subln,
           cv_w, cv_b, cv_ln, sw_sink, rw_mu, rw_w0, rw_w_up, rw_a0, rw_a_up, rw_kk, rw_ka,
           rw_g_up, rw_rk, rw_gn, router_w, router_b, ex_w_gu, ex_w_down, sh_w_gu, sh_w_down):
    n_batch, seq, d = x.shape
    ctx_len = ctx.shape[1]
    depth = w_in.shape[0]
    assert ctx_len == TILE and seq % TILE == 0 and seq % GRID_W == 0
    assert n_batch + 1 <= 8
    rows_b = ctx_len + seq
    tpb = rows_b // TILE
    n = n_batch * rows_b

    xs = jnp.concatenate([ctx, x], axis=1).reshape(n, d)
    cond = jnp.zeros((8, d), F32).at[:n_batch].set(c).at[n_batch].set(c_ctx)
    mods = _ada_call(cond, ada_w, ada_b)[:, :n_batch + 1]
    cos_t, sin_t = _rope_tables(seq)
    cols = _proj_columns()
    hio = np.arange(BW) // RW_DH
    bd = jnp.asarray(hio[:, None] == hio[None, :], BF16)

    for l in range(depth):
        mod = [mods[l, :, i * d:(i + 1) * d].reshape(n_batch + 1, 1, d) for i in range(6)]
        ng = norm_g[l]
        lam_init = 0.8 - 0.6 * math.exp(-0.3 * l)
        lv = da_lambda[l]
        lam = (jnp.exp(jnp.sum(lv[0] * lv[1])) - jnp.exp(jnp.sum(lv[2] * lv[3])) + lam_init)
        lp = {'rw_mu': rw_mu[l], 'rw_w0': rw_w0[l], 'rw_w_up': rw_w_up[l], 'rw_a0': rw_a0[l],
              'rw_a_up': rw_a_up[l], 'rw_kk': rw_kk[l], 'rw_ka': rw_ka[l], 'rw_rk': rw_rk[l]}

        (h,) = _norm_mod_call(xs, ng[0], mod[0], mod[1], tpb, n_batch, False)
        u = _proj_call(h, w_in[l][:, cols].astype(BF16))
        qk = _rope_call(u, cos_t, sin_t, tpb)
        oa = _da_call(lam.reshape(1), qk, u, jnp.tile(da_subln[l], DA_HEADS).reshape(1, BW), bd,
                      n_batch, tpb, lam_init)
        ob = _conv_call(u, cv_w[l], cv_b[l], cv_ln[l], tpb)
        oc = _sw_call(sw_sink[l], qk, u, n_batch, tpb)
        cm, cn, cq, cy0, bonus = _rw_chunk_call(u, lp, bd, tpb)
        yscan = _rw_scan_call(cm, cn, cq, cy0, n_batch, tpb)
        od = _rw_out_call(yscan, bonus, u, rw_g_up[l].astype(BF16), rw_gn[l], bd)
        xs = _merge_call((oa, ob, oc, od), u, w_branch[l].astype(BF16), w_out[l].astype(BF16),
                         xs, ng[1], mod[2], tpb, n_batch)

        tok, tok32 = _norm_mod_call(xs, ng[2], mod[3], mod[4], tpb, n_batch, True)
        gates = _router_call(tok32, router_w[l].T, router_b[l]).T
        wgu = jnp.concatenate([ex_w_gu[l], sh_w_gu[l][None]], axis=0).astype(BF16)
        wd = jnp.concatenate([ex_w_down[l], sh_w_down[l][None]], axis=0).astype(BF16)
        f = _moe_call(tok, gates, wgu, wd, _moe_tile(rows_b))
        xs = _resid_call(xs, f, ng[3], mod[5], tpb, n_batch)

    return xs.reshape(n_batch, rows_b, d)[:, ctx_len:]
```

```python
import functools
import math

import numpy as np
import jax
import jax.numpy as jnp
from jax import lax
from jax.experimental import pallas as pl
from jax.experimental.pallas import tpu as pltpu

F32 = jnp.float32
BF16 = jnp.bfloat16

GRID_W = 64
EPS = 1e-6
ROPE_BASE = 10000.0
N_BRANCH = 4
BW = 256
DA_HEADS = 4
DA_HALF = 32
SW_HEADS = 4
SW_KV = 2
SW_DH = 64
WINDOW = 128
CONV_W = 31
RW_HEADS = 4
RW_DH = 64
RW_DECAY_RANK = 64
RW_A_RANK = 64
RW_G_RANK = 128
RW_DECAY_SCALE = math.exp(-0.5)
RW_GN_EPS = 64e-5
RW_SHIFT_W = 3 * BW + RW_DECAY_RANK + RW_A_RANK
N_EXPERTS = 64
TOP_K = 6
N_GROUPS = 8
TOPK_GROUPS = 4
D_EXPERT = 256
ROUTE_SCALE = 2.5

TILE = 256
CHUNK = 128
NEG = -1e30

C_ROPE = 0
C_SWAP = 1024
C_DAV = 2048
C_SWV = 2304
C_CV = 2560
C_RW = 3072
C_GATE = 4096
C_TOTAL = 8192

VMEM_LIMIT = 48 * 1024 * 1024

_NT = (((1,), (1,)), ((), ()))
_NN = (((1,), (0,)), ((), ()))


def _cparams(sem):
    return pltpu.CompilerParams(dimension_semantics=sem, vmem_limit_bytes=VMEM_LIMIT)


def _dot(a, b, dims=_NN):
    return lax.dot_general(a.astype(BF16), b.astype(BF16), dims, preferred_element_type=F32)


def _split2(x):
    hi = x.astype(BF16)
    lo = (x - hi.astype(F32)).astype(BF16)
    return hi, lo


def _dot3(a, b, dims=_NN):
    ah, al = _split2(a)
    bh, bl = _split2(b)
    dg = lambda x, y: lax.dot_general(x, y, dims, preferred_element_type=F32)
    return dg(ah, bh) + (dg(ah, bl) + dg(al, bh))


def _dot_sel(x, sel, dims=_NN):
    h0 = x.astype(BF16)
    r1 = x - h0.astype(F32)
    h1 = r1.astype(BF16)
    h2 = (r1 - h1.astype(F32)).astype(BF16)
    dg = lambda y: lax.dot_general(y, sel, dims, preferred_element_type=F32)
    return dg(h0) + (dg(h1) + dg(h2))


def _sel_dot(sel, x):
    h0 = x.astype(BF16)
    r1 = x - h0.astype(F32)
    h1 = r1.astype(BF16)
    h2 = (r1 - h1.astype(F32)).astype(BF16)
    dg = lambda y: lax.dot_general(sel, y, _NN, preferred_element_type=F32)
    return dg(h0) + (dg(h1) + dg(h2))


def _sigmoid(x):
    return jax.nn.sigmoid(x)


def _silu(x):
    return x * jax.nn.sigmoid(x)


def _head_of_lane(shape, width):
    return lax.broadcasted_iota(jnp.int32, shape, len(shape) - 1) // width


def _ada_kernel(s_ref, w_ref, b_ref, o_ref):
    s = _silu(s_ref[...])
    o_ref[0] = _dot(s, w_ref[0]) + b_ref[0]


def _ada_call(cond, ada_w, ada_b):
    depth, d, cols = ada_w.shape
    tn = 1536
    return pl.pallas_call(
        _ada_kernel,
        grid=(depth, cols // tn),
        in_specs=[pl.BlockSpec((8, d), lambda l, j: (0, 0)),
                  pl.BlockSpec((1, d, tn), lambda l, j: (l, 0, j)),
                  pl.BlockSpec((1, 1, tn), lambda l, j: (l, 0, j))],
        out_specs=pl.BlockSpec((1, 8, tn), lambda l, j: (l, 0, j)),
        out_shape=jax.ShapeDtypeStruct((depth, 8, cols), F32),
        compiler_params=_cparams(("arbitrary", "arbitrary")),
        name="ada_mod",
    )(cond, ada_w, ada_b.reshape(depth, 1, cols))


def _norm_mod_kernel(x_ref, g_ref, sh_ref, sc_ref, o_ref, *o32):
    x = x_ref[...]
    y = x * lax.rsqrt(jnp.mean(x * x, axis=-1, keepdims=True) + EPS) * g_ref[...]
    h = y * (1.0 + sc_ref[0]) + sh_ref[0]
    o_ref[...] = h.astype(BF16)
    if o32:
        o32[0][...] = h


def _norm_mod_call(x, g, shift, scale, tpb, n_batch, want_f32):
    n, d = x.shape
    kind = lambda i: (jnp.where(i % tpb == 0, n_batch, i // tpb), 0, 0)
    row = pl.BlockSpec((TILE, d), lambda i: (i, 0))
    out_shape = [jax.ShapeDtypeStruct((n, d), BF16)]
    out_specs = [row]
    if want_f32:
        out_shape.append(jax.ShapeDtypeStruct((n, d), F32))
        out_specs.append(row)
    return pl.pallas_call(
        _norm_mod_kernel,
        grid=(n // TILE,),
        in_specs=[row, pl.BlockSpec((1, d), lambda i: (0, 0)),
                  pl.BlockSpec((1, 1, d), kind), pl.BlockSpec((1, 1, d), kind)],
        out_specs=out_specs,
        out_shape=out_shape,
        compiler_params=_cparams(("arbitrary",)),
        name="norm_mod",
    )(x, g.reshape(1, d), shift, scale)


def _mm_kernel(a_ref, w_ref, o_ref):
    o_ref[...] = jnp.dot(a_ref[...], w_ref[...], preferred_element_type=F32)


def _proj_call(h, w):
    n, d = h.shape
    cols = w.shape[1]
    tm, tn = 2 * TILE, 1024
    return pl.pallas_call(
        _mm_kernel,
        grid=(cols // tn, n // tm),
        in_specs=[pl.BlockSpec((tm, d), lambda j, i: (i, 0)),
                  pl.BlockSpec((d, tn), lambda j, i: (0, j))],
        out_specs=pl.BlockSpec((tm, tn), lambda j, i: (i, j)),
        out_shape=jax.ShapeDtypeStruct((n, cols), F32),
        compiler_params=_cparams(("arbitrary", "arbitrary")),
        name="in_proj",
    )(h, w)


def _rope_kernel(um_ref, us_ref, c_ref, s_ref, o_ref):
    o_ref[...] = (um_ref[...] * c_ref[...] + us_ref[...] * s_ref[...]).astype(BF16)


def _rope_call(u, cos_t, sin_t, tpb):
    n = u.shape[0]
    w = cos_t.shape[1]
    tab = pl.BlockSpec((TILE, w), lambda i: (i % tpb, 0))
    return pl.pallas_call(
        _rope_kernel,
        grid=(n // TILE,),
        in_specs=[pl.BlockSpec((TILE, w), lambda i: (i, C_ROPE // w)),
                  pl.BlockSpec((TILE, w), lambda i: (i, C_SWAP // w)), tab, tab],
        out_specs=pl.BlockSpec((TILE, w), lambda i: (i, 0)),
        out_shape=jax.ShapeDtypeStruct((n, w), BF16),
        compiler_params=_cparams(("arbitrary",)),
        name="rope",
    )(u, u, cos_t, sin_t)


def _da_kernel(lam_ref, q_ref, k_ref, vt_ref, g_ref, o_ref,
               qs_ref, m_ref, l_ref, acc_ref, *, nkt, lam_init):
    i = pl.program_id(1)
    tq = q_ref.shape[0]
    hd = 2 * DA_HALF
    c2 = (DA_HALF ** -0.5) * math.log2(math.e)
    q = q_ref[...]
    qmap = _head_of_lane((tq, BW), DA_HALF)
    for g in range(2 * DA_HEADS):
        qs_ref[g] = jnp.where(qmap == g, q, jnp.zeros_like(q))
    m_ref[...] = jnp.full(m_ref.shape, NEG, F32)
    l_ref[...] = jnp.zeros(l_ref.shape, F32)
    acc_ref[...] = jnp.zeros(acc_ref.shape, F32)

    def body(j, carry):
        off = pl.multiple_of(j * TILE, TILE)
        kt = k_ref[pl.ds(off, TILE), :]
        vt = vt_ref[:, pl.ds(off, TILE)]
        vrow = lax.broadcasted_iota(jnp.int32, vt.shape, 0) // hd
        vcat = jnp.concatenate(
            [jnp.where(vrow == h, vt, jnp.zeros_like(vt)) for h in range(DA_HEADS)], axis=1)
        for mm in range(2):
            ps, alphas = [], []
            for h in range(DA_HEADS):
                g = 2 * h + mm
                s = lax.dot_general(kt, qs_ref[g], _NT, preferred_element_type=F32)
                m_old = m_ref[g:g + 1, :]
                m_new = jnp.maximum(m_old, jnp.max(s, axis=0, keepdims=True))
                alpha = jnp.exp2((m_old - m_new) * c2)
                p = jnp.exp2((s - m_new) * c2)
                l_ref[g:g + 1, :] = l_ref[g:g + 1, :] * alpha + jnp.sum(p, axis=0, keepdims=True)
                m_ref[g:g + 1, :] = m_new
                ps.append(p.astype(BF16))
                alphas.append(alpha)
            pv = jnp.dot(vcat, jnp.concatenate(ps, axis=0), preferred_element_type=F32)
            for h in range(DA_HEADS):
                acc_ref[mm, h * hd:(h + 1) * hd, :] = (
                    acc_ref[mm, h * hd:(h + 1) * hd, :] * alphas[h] + pv[h * hd:(h + 1) * hd, :])
        return carry

    lax.fori_loop(0, jnp.where(i == 0, 1, nkt), body, 0)

    lam = lam_ref[0]
    parts = []
    for h in range(DA_HEADS):
        rows = slice(h * hd, (h + 1) * hd)
        o_h = (acc_ref[0, rows, :] / l_ref[2 * h:2 * h + 1, :]
               - lam * (acc_ref[1, rows, :] / l_ref[2 * h + 1:2 * h + 2, :]))
        ms = jnp.mean(o_h * o_h, axis=0, keepdims=True)
        parts.append(o_h * lax.rsqrt(ms + EPS))
    y = jnp.concatenate(parts, axis=0).T * g_ref[...]
    o_ref[...] = y * (1.0 - lam_init)


def _da_call(lam, qk, vt, subln, n_batch, tpb, lam_init):
    n = qk.shape[0]
    rows_b = tpb * TILE
    kern = functools.partial(_da_kernel, nkt=tpb, lam_init=lam_init)
    return pl.pallas_call(
        kern,
        grid=(n_batch, tpb),
        in_specs=[pl.BlockSpec(memory_space=pltpu.SMEM),
                  pl.BlockSpec((TILE, BW), lambda b, i: (b * tpb + i, 0)),
                  pl.BlockSpec((rows_b, BW), lambda b, i: (b, 1)),
                  pl.BlockSpec((BW, rows_b), lambda b, i: (b, 0)),
                  pl.BlockSpec((1, BW), lambda b, i: (0, 0))],
        out_specs=pl.BlockSpec((TILE, BW), lambda b, i: (b * tpb + i, 0)),
        out_shape=jax.ShapeDtypeStruct((n, BW), F32),
        scratch_shapes=[pltpu.VMEM((2 * DA_HEADS, TILE, BW), BF16),
                        pltpu.VMEM((2 * DA_HEADS, TILE), F32),
                        pltpu.VMEM((2 * DA_HEADS, TILE), F32),
                        pltpu.VMEM((2, BW, TILE), F32)],
        compiler_params=_cparams(("arbitrary", "arbitrary")),
        name="diff_attn",
    )(lam, qk, qk, vt, subln)


def _sw_kernel(sink_ref, q_ref, kp_ref, ko_ref, kn_ref, kc_ref,
               vp_ref, vo_ref, vn_ref, vc_ref, o_ref, *, cpb):
    j = pl.program_id(1)
    tq = q_ref.shape[0]
    ctx_chunks = TILE // CHUNK
    q = q_ref[...]
    kk = jnp.concatenate([kp_ref[...], ko_ref[...], kn_ref[...], kc_ref[...]], axis=0)
    vv = jnp.concatenate([vp_ref[...], vo_ref[...], vn_ref[...], vc_ref[...]], axis=0).astype(BF16)
    nk = kk.shape[0]
    r = lax.broadcasted_iota(jnp.int32, (tq, nk), 0)
    c = lax.broadcasted_iota(jnp.int32, (tq, nk), 1)
    is_lat = j >= ctx_chunks
    far = 4 * nk
    lo_prev = jnp.where(is_lat & (j - 1 >= ctx_chunks), 0, far)
    hi_own = jnp.where(is_lat, 2 * CHUNK, 0)
    hi_next = jnp.where(is_lat & (j + 1 < cpb), 2 * CHUNK, -far)
    valid = ((c >= 3 * CHUNK)
             | ((c < CHUNK) & (c >= r + lo_prev))
             | ((c >= CHUNK) & (c < hi_own))
             | ((c >= 2 * CHUNK) & (c < 3 * CHUNK) & (c <= r + hi_next)))
    qhead = _head_of_lane((tq, BW), SW_DH)
    vhead = _head_of_lane((nk, BW), SW_DH)
    ps, vs = [], []
    for h in range(SW_HEADS):
        qm = jnp.where(qhead == h, q, jnp.zeros_like(q))
        s = lax.dot_general(qm, kk, _NT, preferred_element_type=F32) * (SW_DH ** -0.5)
        s = jnp.where(valid, s, NEG)
        sk = sink_ref[h]
        m = jnp.maximum(jnp.max(s, axis=1, keepdims=True), sk)
        p = jnp.exp(s - m)
        den = jnp.sum(p, axis=1, keepdims=True) + jnp.exp(sk - m)
        ps.append((p / den).astype(BF16))
        vs.append(jnp.where(vhead == h, vv, jnp.zeros_like(vv)))
    o_ref[...] = jnp.dot(jnp.concatenate(ps, axis=1), jnp.concatenate(vs, axis=0),
                         preferred_element_type=F32)


def _sw_call(sink, qk, u, n_batch, tpb):
    n = qk.shape[0]
    cpb = tpb * (TILE // CHUNK)
    cur = lambda b, j: b * cpb + j
    prv = lambda b, j: b * cpb + jnp.maximum(j - 1, 0)
    nxt = lambda b, j: b * cpb + jnp.minimum(j + 1, cpb - 1)
    kcol, vcol = 3, C_SWV // BW
    blk = lambda f, col: pl.BlockSpec((CHUNK, BW), lambda b, j: (f(b, j), col))
    ctx = lambda col: pl.BlockSpec((TILE, BW), lambda b, j: (b * tpb, col))
    return pl.pallas_call(
        functools.partial(_sw_kernel, cpb=cpb),
        grid=(n_batch, cpb),
        in_specs=[pl.BlockSpec(memory_space=pltpu.SMEM),
                  blk(cur, 2), blk(prv, kcol), blk(cur, kcol), blk(nxt, kcol), ctx(kcol),
                  blk(prv, vcol), blk(cur, vcol), blk(nxt, vcol), ctx(vcol)],
        out_specs=pl.BlockSpec((CHUNK, BW), lambda b, j: (cur(b, j), 0)),
        out_shape=jax.ShapeDtypeStruct((n, BW), F32),
        compiler_params=_cparams(("arbitrary", "arbitrary")),
        name="window_attn",
    )(sink, qk, qk, qk, qk, qk, u, u, u, u)


def _conv_kernel(prev_ref, cur_ref, next_ref, w_ref, b_ref, ln_ref, o_ref, z_ref, *, tpb):
    i = pl.program_id(0)
    pos = i % tpb
    has_prev = pos >= 2
    has_next = (pos >= 1) & (pos < tpb - 1)
    halo = prev_ref.shape[0]

    def glu(x):
        return x[:, :BW] * _sigmoid(x[:, BW:])

    zp = glu(prev_ref[...])
    zn = glu(next_ref[...])
    z_ref[0:halo, :] = jnp.where(has_prev, zp, 0.0)
    z_ref[halo:halo + TILE, :] = glu(cur_ref[...])
    z_ref[halo + TILE:2 * halo + TILE, :] = jnp.where(has_next, zn, 0.0)
    acc = jnp.zeros((TILE, BW), F32) + b_ref[...]
    pad = CONV_W // 2
    for t in range(CONV_W):
        acc = acc + z_ref[halo - pad + t:halo - pad + t + TILE, :] * w_ref[t:t + 1, :]
    mu = jnp.mean(acc, axis=-1, keepdims=True)
    xc = acc - mu
    y = xc * lax.rsqrt(jnp.mean(xc * xc, axis=-1, keepdims=True) + EPS)
    y = y * ln_ref[0:1, :] + ln_ref[1:2, :]
    o_ref[...] = _silu(y)


def _conv_call(u, w, b, ln, tpb):
    n = u.shape[0]
    halo = 16
    per = TILE // halo
    nh = n // halo
    wcv = 2 * BW
    col = C_CV // wcv
    return pl.pallas_call(
        functools.partial(_conv_kernel, tpb=tpb),
        grid=(n // TILE,),
        in_specs=[pl.BlockSpec((halo, wcv), lambda i: (jnp.maximum(i * per - 1, 0), col)),
                  pl.BlockSpec((TILE, wcv), lambda i: (i, col)),
                  pl.BlockSpec((halo, wcv), lambda i: (jnp.minimum((i + 1) * per, nh - 1), col)),
                  pl.BlockSpec((CONV_W, BW), lambda i: (0, 0)),
                  pl.BlockSpec((1, BW), lambda i: (0, 0)),
                  pl.BlockSpec((2, BW), lambda i: (0, 0))],
        out_specs=pl.BlockSpec((TILE, BW), lambda i: (i, 0)),
        out_shape=jax.ShapeDtypeStruct((n, BW), F32),
        scratch_shapes=[pltpu.VMEM((TILE + 2 * halo, BW), F32)],
        compiler_params=_cparams(("arbitrary",)),
        name="conformer_conv",
    )(u, u, u, w, b.reshape(1, BW), ln)


def _inv_unit_lower(l_ab, eye):
    t = eye + l_ab
    lp = l_ab
    step = 1
    while step < l_ab.shape[0] // 2:
        lp = _dot(lp, lp)
        t = t + _dot(t, lp)
        step *= 2
    return t


def _rw_chunk_kernel(prev_ref, cur_ref, next_ref, mu_ref, w0_ref, a0_ref, kk_ref, ka_ref,
                     wup_ref, aup_ref, rk_ref, bd_ref,
                     m_ref, n_ref, q_ref, y0_ref, bonus_ref, z_ref, *, cpb):
    i = pl.program_id(0)
    jj = i % cpb
    ctx_chunks = TILE // CHUNK
    has_prev = (jj != 0) & (jj != ctx_chunks)
    has_next = (jj != ctx_chunks - 1) & (jj != cpb - 1)
    c = CHUNK
    zc = cur_ref[:, 0:RW_SHIFT_W]
    z_ref[0:8, :] = jnp.where(has_prev, prev_ref[:, 0:RW_SHIFT_W], 0.0)
    z_ref[8:8 + c, :] = zc
    z_ref[8 + c:16 + c, :] = jnp.where(has_next, next_ref[:, 0:RW_SHIFT_W], 0.0)
    bd = bd_ref[...]
    t_io = lax.broadcasted_iota(jnp.int32, (c, c), 0)
    s_io = lax.broadcasted_iota(jnp.int32, (c, c), 1)
    eye = jnp.where(t_io == s_io, 1.0, 0.0)
    lane_head = _head_of_lane((c, BW), RW_DH)
    row = lax.broadcasted_iota(jnp.int32, (BW, BW), 0)
    colm = lax.broadcasted_iota(jnp.int32, (BW, BW), 1)
    same_head = (row // RW_DH) == (colm // RW_DH)

    def stack(x):
        return jnp.concatenate(
            [jnp.where(lane_head == h, x, 0.0) for h in range(RW_HEADS)], axis=0)

    cat = lambda xs: jnp.concatenate(xs, axis=1)

    for d in range(2):
        zsh = z_ref[7:7 + c, :] if d == 0 else z_ref[9:9 + c, :]
        zs = zc + (zsh - zc) * mu_ref[d]
        r = zs[:, 0:BW]
        k = zs[:, BW:2 * BW]
        v = zs[:, 2 * BW:3 * BW]
        wa = zs[:, 3 * BW:RW_SHIFT_W]
        lw = -RW_DECAY_SCALE * _sigmoid(w0_ref[d] + _dot(jnp.tanh(wa), wup_ref[d]))
        a = _sigmoid(a0_ref[d] + _dot(wa, aup_ref[d]))
        kappa = k * kk_ref[d]
        kh = kappa * lax.rsqrt(jnp.maximum(_dot_sel(kappa * kappa, bd), 1e-12))
        kt = k * (1.0 + (a - 1.0) * ka_ref[d])
        akh = a * kh
        bonus_ref[d] = _dot_sel(r * kt * rk_ref[...], bd) * v

        ahead = (t_io - s_io) if d == 0 else (s_io - t_io)
        earlier = ahead > 0
        upto = ahead >= 0
        tri = jnp.where(upto, 1.0, 0.0).astype(BF16)
        cl = _sel_dot(tri, lw)
        tot = jnp.sum(lw, axis=0, keepdims=True)
        rho = 0.5 * tot
        cle = cl - lw
        a_true = -kh * jnp.exp(cle)
        r_true = r * jnp.exp(cl)
        a_c = -kh * jnp.exp(cle - rho)
        r_c = r * jnp.exp(cl - rho)
        b_c = akh * jnp.exp(rho - cl)
        k_c = kt * jnp.exp(rho - cl)
        b_end = akh * jnp.exp(tot - cl)
        k_end = kt * jnp.exp(tot - cl)

        pair = _dot(jnp.concatenate([stack(a_c), stack(r_c)], axis=0),
                    jnp.concatenate([b_c, k_c], axis=0), _NT)
        t_inv, l_ak, a_rb, a_rk = [], [], [], []
        for h in range(RW_HEADS):
            blk_a = pair[h * c:(h + 1) * c]
            blk_r = pair[(RW_HEADS + h) * c:(RW_HEADS + h + 1) * c]
            t_inv.append(_inv_unit_lower(jnp.where(earlier, blk_a[:, 0:c], 0.0), eye))
            l_ak.append(jnp.where(earlier, blk_a[:, c:2 * c], 0.0))
            a_rb.append(jnp.where(upto, blk_r[:, 0:c], 0.0))
            a_rk.append(jnp.where(upto, blk_r[:, c:2 * c], 0.0))
        lv = _dot(cat(l_ak), stack(v))
        w12 = _dot(cat(t_inv), jnp.concatenate([stack(a_true), stack(lv)], axis=1))
        w1 = w12[:, 0:BW]
        w2 = w12[:, BW:2 * BW]
        arb = cat(a_rb)
        q_ref[d, 0] = r_true + _dot(arb, stack(w1))
        y0_ref[d, 0] = _dot(arb, stack(w2)) + _dot(cat(a_rk), stack(v))
        b_end_t = b_end.T
        decay = jnp.where(row == colm, jnp.broadcast_to(jnp.exp(tot), (BW, BW)), 0.0)
        m_ref[d, 0] = jnp.where(same_head, _dot(b_end_t, w1), 0.0) + decay
        n_ref[d, 0] = jnp.where(same_head, _dot(b_end_t, w2) + _dot(k_end.T, v), 0.0)


def _rw_chunk_call(u, lp, bd, tpb):
    n = u.shape[0]
    cpb = tpb * (TILE // CHUNK)
    nch = n // CHUNK
    per = CHUNK // 8
    n8 = n // 8
    wrw = 1024
    col = C_RW // wrw
    pvec = lambda w: pl.BlockSpec((2, 1, w), lambda i: (0, 0, 0))
    pad = jnp.zeros((2, RW_DECAY_RANK, BW), F32)
    wup = jnp.concatenate([lp['rw_w_up'], pad], axis=1).astype(BF16)
    aup = jnp.concatenate([pad, lp['rw_a_up']], axis=1).astype(BF16)
    mat = lambda rows: pl.BlockSpec((2, 1, rows, BW), lambda i: (0, i, 0, 0))
    return pl.pallas_call(
        functools.partial(_rw_chunk_kernel, cpb=cpb),
        grid=(nch,),
        in_specs=[pl.BlockSpec((8, wrw), lambda i: (jnp.maximum(i * per - 1, 0), col)),
                  pl.BlockSpec((CHUNK, wrw), lambda i: (i, col)),
                  pl.BlockSpec((8, wrw), lambda i: (jnp.minimum((i + 1) * per, n8 - 1), col)),
                  pvec(RW_SHIFT_W), pvec(BW), pvec(BW), pvec(BW), pvec(BW),
                  pl.BlockSpec((2, 2 * RW_DECAY_RANK, BW), lambda i: (0, 0, 0)),
                  pl.BlockSpec((2, 2 * RW_A_RANK, BW), lambda i: (0, 0, 0)),
                  pl.BlockSpec((1, BW), lambda i: (0, 0)),
                  pl.BlockSpec((BW, BW), lambda i: (0, 0))],
        out_specs=[mat(BW), mat(BW), mat(CHUNK), mat(CHUNK),
                   pl.BlockSpec((2, CHUNK, BW), lambda i: (0, i, 0))],
        out_shape=[jax.ShapeDtypeStruct((2, nch, BW, BW), F32),
                   jax.ShapeDtypeStruct((2, nch, BW, BW), F32),
                   jax.ShapeDtypeStruct((2, nch, CHUNK, BW), F32),
                   jax.ShapeDtypeStruct((2, nch, CHUNK, BW), F32),
                   jax.ShapeDtypeStruct((2, n, BW), F32)],
        scratch_shapes=[pltpu.VMEM((CHUNK + 16, RW_SHIFT_W), F32)],
        compiler_params=_cparams(("arbitrary",)),
        name="rwkv_chunk",
    )(u, u, u, lp['rw_mu'].reshape(2, 1, RW_SHIFT_W), lp['rw_w0'].reshape(2, 1, BW),
      lp['rw_a0'].reshape(2, 1, BW), lp['rw_kk'].reshape(2, 1, BW), lp['rw_ka'].reshape(2, 1, BW),
      wup, aup, lp['rw_rk'].reshape(1, BW), bd)


def _rw_scan_kernel(m_ref, n_ref, q_ref, y0_ref, y_ref, x_ref):
    @pl.when(pl.program_id(2) == 0)
    def _():
        x_ref[...] = jnp.zeros(x_ref.shape, F32)

    x = x_ref[...]
    y_ref[0] = _dot3(q_ref[0, 0], x) + y0_ref[0, 0]
    x_ref[...] = _dot3(m_ref[0, 0], x) + n_ref[0, 0]


def _rw_scan_call(m, nn, q, y0, n_batch, tpb):
    nch = m.shape[1]
    cpb = tpb * (TILE // CHUNK)
    ctx_chunks = TILE // CHUNK

    def chunk(b, d, i):
        rev = jnp.where(i < ctx_chunks, ctx_chunks - 1 - i, cpb + ctx_chunks - 1 - i)
        return b * cpb + jnp.where(d == 0, i, rev)

    mat = lambda rows: pl.BlockSpec((1, 1, rows, BW), lambda b, d, i: (d, chunk(b, d, i), 0, 0))
    return pl.pallas_call(
        _rw_scan_kernel,
        grid=(n_batch, 2, cpb),
        in_specs=[mat(BW), mat(BW), mat(CHUNK), mat(CHUNK)],
        out_specs=pl.BlockSpec((1, CHUNK, BW), lambda b, d, i: (d, chunk(b, d, i), 0)),
        out_shape=jax.ShapeDtypeStruct((2, nch * CHUNK, BW), F32),
        scratch_shapes=[pltpu.VMEM((BW, BW), F32)],
        compiler_params=_cparams(("arbitrary", "arbitrary", "arbitrary")),
        name="rwkv_scan",
    )(m, nn, q, y0)


def _rw_out_kernel(y_ref, bonus_ref, u_ref, gup_ref, gn_ref, bd_ref, o_ref):
    bd = bd_ref[...]
    y = y_ref[0] + y_ref[1]
    mean = _dot_sel(y, bd) * (1.0 / RW_DH)
    yc = y - mean
    var = _dot_sel(yc * yc, bd) * (1.0 / RW_DH)
    yn = yc * lax.rsqrt(var + RW_GN_EPS) * gn_ref[0:1, :] + gn_ref[1:2, :]
    yn = yn + (bonus_ref[0] + bonus_ref[1])
    gd = u_ref[:, RW_SHIFT_W:RW_SHIFT_W + RW_G_RANK]
    o_ref[...] = yn * _dot(_sigmoid(gd), gup_ref[...])


def _rw_out_call(y, bonus, u, gup, gn, bd):
    n = u.shape[0]
    wrw = 1024
    both = pl.BlockSpec((2, TILE, BW), lambda i: (0, i, 0))
    return pl.pallas_call(
        _rw_out_kernel,
        grid=(n // TILE,),
        in_specs=[both, both, pl.BlockSpec((TILE, wrw), lambda i: (i, C_RW // wrw)),
                  pl.BlockSpec((RW_G_RANK, BW), lambda i: (0, 0)),
                  pl.BlockSpec((2, BW), lambda i: (0, 0)),
                  pl.BlockSpec((BW, BW), lambda i: (0, 0))],
        out_specs=pl.BlockSpec((TILE, BW), lambda i: (i, 0)),
        out_shape=jax.ShapeDtypeStruct((n, BW), F32),
        compiler_params=_cparams(("arbitrary",)),
        name="rwkv_readout",
    )(y, bonus, u, gup, gn, bd)


def _merge_kernel(oa_ref, ob_ref, oc_ref, od_ref, gl_ref, wb_ref, wo_ref, x_ref, g_ref, mod_ref,
                  o_ref):
    d = x_ref.shape[1]
    m = None
    for i, o in enumerate((oa_ref, ob_ref, oc_ref, od_ref)):
        t = _sigmoid(gl_ref[:, i * d:(i + 1) * d]) * _dot(o[...], wb_ref[i])
        m = t if m is None else m + t
    y = _dot(m, wo_ref[...])
    y = y * lax.rsqrt(jnp.mean(y * y, axis=-1, keepdims=True) + EPS) * g_ref[...]
    o_ref[...] = x_ref[...] + mod_ref[0] * y


def _merge_call(outs, u, wb, wo, x, g, mod, tpb, n_batch):
    n, d = x.shape
    kind = lambda i: (jnp.where(i % tpb == 0, n_batch, i // tpb), 0, 0)
    br = pl.BlockSpec((TILE, BW), lambda i: (i, 0))
    wg = N_BRANCH * d
    return pl.pallas_call(
        _merge_kernel,
        grid=(n // TILE,),
        in_specs=[br, br, br, br,
                  pl.BlockSpec((TILE, wg), lambda i: (i, C_GATE // wg)),
                  pl.BlockSpec((N_BRANCH, BW, d), lambda i: (0, 0, 0)),
                  pl.BlockSpec((d, d), lambda i: (0, 0)),
                  pl.BlockSpec((TILE, d), lambda i: (i, 0)),
                  pl.BlockSpec((1, d), lambda i: (0, 0)),
                  pl.BlockSpec((1, 1, d), kind)],
        out_specs=pl.BlockSpec((TILE, d), lambda i: (i, 0)),
        out_shape=jax.ShapeDtypeStruct((n, d), F32),
        compiler_params=_cparams(("arbitrary",)),
        name="merge",
    )(*outs, u, wb, wo, x, g.reshape(1, d), mod)


def _router_kernel(t_ref, w_ref, b_ref, o_ref):
    tm = t_ref.shape[0]
    gsz = N_EXPERTS // N_GROUPS
    logits = _dot3(w_ref[...], t_ref[...], _NT)
    sc = _sigmoid(logits).reshape(N_GROUPS, gsz, tm)
    bi = sc + b_ref[...].reshape(N_GROUPS, gsz, 1)
    shape = (N_GROUPS, gsz, tm)
    g_io = lax.broadcasted_iota(jnp.int32, shape, 0)
    j_io = lax.broadcasted_iota(jnp.int32, shape, 1)
    e_io = g_io * gsz + j_io
    ninf = -jnp.inf
    m1 = jnp.max(bi, axis=1, keepdims=True)
    i1 = jnp.min(jnp.where(bi == m1, j_io, gsz), axis=1, keepdims=True)
    m2 = jnp.max(jnp.where(j_io == i1, ninf, bi), axis=1, keepdims=True)
    cur = jnp.broadcast_to(m1 + m2, shape)
    gsel = jnp.zeros(shape, F32)
    for _ in range(TOPK_GROUPS):
        mx = jnp.max(cur, axis=0, keepdims=True)
        ix = jnp.min(jnp.where(cur == mx, g_io, N_GROUPS), axis=0, keepdims=True)
        hit = g_io == ix
        gsel = jnp.where(hit, 1.0, gsel)
        cur = jnp.where(hit, ninf, cur)
    cur = jnp.where(gsel > 0.0, bi, ninf)
    esel = jnp.zeros(shape, F32)
    for _ in range(TOP_K):
        mx = jnp.max(jnp.max(cur, axis=0, keepdims=True), axis=1, keepdims=True)
        ix = jnp.min(jnp.min(jnp.where(cur == mx, e_io, N_EXPERTS), axis=0, keepdims=True),
                     axis=1, keepdims=True)
        hit = e_io == ix
        esel = jnp.where(hit, 1.0, esel)
        cur = jnp.where(hit, ninf, cur)
    wsel = sc * esel
    den = jnp.sum(jnp.sum(wsel, axis=0, keepdims=True), axis=1, keepdims=True)
    o_ref[...] = (wsel / den * ROUTE_SCALE).reshape(N_EXPERTS, tm)


def _router_call(tokens, rw_t, rb):
    n, d = tokens.shape
    return pl.pallas_call(
        _router_kernel,
        grid=(n // TILE,),
        in_specs=[pl.BlockSpec((TILE, d), lambda i: (i, 0)),
                  pl.BlockSpec((N_EXPERTS, d), lambda i: (0, 0)),
                  pl.BlockSpec((N_EXPERTS, 1), lambda i: (0, 0))],
        out_specs=pl.BlockSpec((N_EXPERTS, TILE), lambda i: (0, i)),
        out_shape=jax.ShapeDtypeStruct((N_EXPERTS, n), F32),
        compiler_params=_cparams(("arbitrary",)),
        name="router",
    )(tokens, rw_t, rb.reshape(N_EXPERTS, 1))


def _moe_kernel(x_ref, g_ref, wgu_ref, wd_ref, o_ref, acc_ref):
    e = pl.program_id(1)

    @pl.when(e == 0)
    def _():
        acc_ref[...] = jnp.zeros(acc_ref.shape, F32)

    hgu = jnp.dot(x_ref[...], wgu_ref[0], preferred_element_type=F32)
    act = _silu(hgu[:, :D_EXPERT]) * hgu[:, D_EXPERT:]
    gates = g_ref[...]
    lane = lax.broadcasted_iota(jnp.int32, gates.shape, 1)
    gcol = jnp.sum(jnp.where(lane == e, gates, 0.0), axis=1, keepdims=True)
    gcol = jnp.where(e == N_EXPERTS, 1.0, gcol)
    acc_ref[...] += _dot(act, wd_ref[0]) * gcol

    @pl.when(e == N_EXPERTS)
    def _():
        o_ref[...] = acc_ref[...]


def _moe_call(tok, gates, wgu, wd, tm):
    n, d = tok.shape
    ne = wgu.shape[0]
    return pl.pallas_call(
        _moe_kernel,
        grid=(n // tm, ne),
        in_specs=[pl.BlockSpec((tm, d), lambda i, e: (i, 0)),
                  pl.BlockSpec((tm, N_EXPERTS), lambda i, e: (i, 0)),
                  pl.BlockSpec((1, d, 2 * D_EXPERT), lambda i, e: (e, 0, 0)),
                  pl.BlockSpec((1, D_EXPERT, d), lambda i, e: (e, 0, 0))],
        out_specs=pl.BlockSpec((tm, d), lambda i, e: (i, 0)),
        out_shape=jax.ShapeDtypeStruct((n, d), F32),
        scratch_shapes=[pltpu.VMEM((tm, d), F32)],
        compiler_params=_cparams(("arbitrary", "arbitrary")),
        name="moe_experts",
    )(tok, gates, wgu, wd)


def _resid_kernel(x_ref, f_ref, g_ref, mod_ref, o_ref):
    f = f_ref[...]
    y = f * lax.rsqrt(jnp.mean(f * f, axis=-1, keepdims=True) + EPS) * g_ref[...]
    o_ref[...] = x_ref[...] + mod_ref[0] * y


def _resid_call(x, f, g, mod, tpb, n_batch):
    n, d = x.shape
    kind = lambda i: (jnp.where(i % tpb == 0, n_batch, i // tpb), 0, 0)
    row = pl.BlockSpec((TILE, d), lambda i: (i, 0))
    return pl.pallas_call(
        _resid_kernel,
        grid=(n // TILE,),
        in_specs=[row, row, pl.BlockSpec((1, d), lambda i: (0, 0)), pl.BlockSpec((1, 1, d), kind)],
        out_specs=row,
        out_shape=jax.ShapeDtypeStruct((n, d), F32),
        compiler_params=_cparams(("arbitrary",)),
        name="moe_residual",
    )(x, f, g.reshape(1, d), mod)


def _partner(dim, nblocks):
    q = dim // 4
    base = np.concatenate([np.arange(q) + q, np.arange(q), np.arange(q) + 3 * q, np.arange(q) + 2 * q])
    return (np.arange(nblocks)[:, None] * dim + base[None, :]).reshape(-1)


def _proj_columns():
    da, cv, sw = 0, 3 * BW, 5 * BW
    rw = sw + (SW_HEADS + 2 * SW_KV) * SW_DH
    gate = rw + RW_SHIFT_W + RW_G_RANK
    rep = np.repeat(np.arange(SW_KV), SW_HEADS // SW_KV)
    kv_rep = (rep[:, None] * SW_DH + np.arange(SW_DH)[None, :]).reshape(-1)
    da_q = da + np.arange(BW)
    da_k = da + BW + np.arange(BW)
    da_v = da + 2 * BW + np.arange(BW)
    sw_q = sw + np.arange(BW)
    sw_k = sw + BW + kv_rep
    sw_v = sw + BW + SW_KV * SW_DH + kv_rep
    p32 = _partner(DA_HALF, BW // DA_HALF)
    p64 = _partner(SW_DH, BW // SW_DH)
    cols = np.concatenate([
        da_q, da_k, sw_q, sw_k,
        da_q[p32], da_k[p32], sw_q[p64], sw_k[p64],
        da_v, sw_v,
        cv + np.arange(2 * BW),
        rw + np.arange(RW_SHIFT_W + RW_G_RANK),
        gate + np.arange(N_BRANCH * 4 * BW)])
    assert cols.shape[0] == C_TOTAL
    return cols


def _rope_tables(seq):
    rows = seq // GRID_W
    row = jnp.repeat(jnp.arange(rows, dtype=F32), GRID_W)
    colp = jnp.tile(jnp.arange(GRID_W, dtype=F32), rows)

    def tables(dim, reps):
        q = dim // 4
        freqs = ROPE_BASE ** (-jnp.arange(q, dtype=F32) / q)
        ar, ac = row[:, None] * freqs, colp[:, None] * freqs
        cos = jnp.concatenate([jnp.cos(ar), jnp.cos(ar), jnp.cos(ac), jnp.cos(ac)], axis=1)
        sin = jnp.concatenate([-jnp.sin(ar), jnp.sin(ar), -jnp.sin(ac), jnp.sin(ac)], axis=1)
        return jnp.tile(cos, (1, reps)), jnp.tile(sin, (1, reps))

    c32, s32 = tables(DA_HALF, 2 * BW // DA_HALF)
    c64, s64 = tables(SW_DH, 2 * BW // SW_DH)
    cos = jnp.concatenate([c32, c64], axis=1)
    sin = jnp.concatenate([s32, s64], axis=1)
    w = cos.shape[1]
    cos = jnp.concatenate([jnp.ones((TILE, w), F32), cos], axis=0)
    sin = jnp.concatenate([jnp.zeros((TILE, w), F32), sin], axis=0)
    return cos, sin


def _moe_tile(rows_b):
    best = TILE
    for t in range(TILE, 1200, 8):
        if rows_b % t == 0:
            best = t
    return best


def kernel(x, c, ctx, c_ctx, ada_w, ada_b, norm_g, w_in, w_branch, w_out, da_lambda, da_subln,
           cv_w, cv_b, cv_ln, sw_sink, rw_mu, rw_w0, rw_w_up, rw_a0, rw_a_up, rw_kk, rw_ka,
           rw_g_up, rw_rk, rw_gn, router_w, router_b, ex_w_gu, ex_w_down, sh_w_gu, sh_w_down):
    n_batch, seq, d = x.shape
    ctx_len = ctx.shape[1]
    depth = w_in.shape[0]
    assert ctx_len == TILE and seq % TILE == 0 and seq % GRID_W == 0
    assert n_batch + 1 <= 8
    rows_b = ctx_len + seq
    tpb = rows_b // TILE
    n = n_batch * rows_b

    xs = jnp.concatenate([ctx, x], axis=1).reshape(n, d)
    cond = jnp.zeros((8, d), F32).at[:n_batch].set(c).at[n_batch].set(c_ctx)
    mods = _ada_call(cond, ada_w, ada_b)[:, :n_batch + 1]
    cos_t, sin_t = _rope_tables(seq)
    cols = _proj_columns()
    hio = np.arange(BW) // RW_DH
    bd = jnp.asarray(hio[:, None] == hio[None, :], BF16)

    for l in range(depth):
        mod = [mods[l, :, i * d:(i + 1) * d].reshape(n_batch + 1, 1, d) for i in range(6)]
        ng = norm_g[l]
        lam_init = 0.8 - 0.6 * math.exp(-0.3 * l)
        lv = da_lambda[l]
        lam = (jnp.exp(jnp.sum(lv[0] * lv[1])) - jnp.exp(jnp.sum(lv[2] * lv[3])) + lam_init)
        lp = {'rw_mu': rw_mu[l], 'rw_w0': rw_w0[l], 'rw_w_up': rw_w_up[l], 'rw_a0': rw_a0[l],
              'rw_a_up': rw_a_up[l], 'rw_kk': rw_kk[l], 'rw_ka': rw_ka[l], 'rw_rk': rw_rk[l]}

        (h,) = _norm_mod_call(xs, ng[0], mod[0], mod[1], tpb, n_batch, False)
        u = _proj_call(h, w_in[l][:, cols].astype(BF16))
        qk = _rope_call(u, cos_t, sin_t, tpb)
        vt = (u[:, C_DAV:C_DAV + BW].reshape(n_batch, rows_b, BW).transpose(0, 2, 1)
              .astype(BF16).reshape(n_batch * BW, rows_b))
        oa = _da_call(lam.reshape(1), qk, vt, jnp.tile(da_subln[l], DA_HEADS).reshape(1, BW),
                      n_batch, tpb, lam_init)
        ob = _conv_call(u, cv_w[l], cv_b[l], cv_ln[l], tpb)
        oc = _sw_call(sw_sink[l], qk, u, n_batch, tpb)
        cm, cn, cq, cy0, bonus = _rw_chunk_call(u, lp, bd, tpb)
        yscan = _rw_scan_call(cm, cn, cq, cy0, n_batch, tpb)
        od = _rw_out_call(yscan, bonus, u, rw_g_up[l].astype(BF16), rw_gn[l], bd)
        xs = _merge_call((oa, ob, oc, od), u, w_branch[l].astype(BF16), w_out[l].astype(BF16),
                         xs, ng[1], mod[2], tpb, n_batch)

        tok, tok32 = _norm_mod_call(xs, ng[2], mod[3], mod[4], tpb, n_batch, True)
        gates = _router_call(tok32, router_w[l].T, router_b[l]).T
        wgu = jnp.concatenate([ex_w_gu[l], sh_w_gu[l][None]], axis=0).astype(BF16)
        wd = jnp.concatenate([ex_w_down[l], sh_w_down[l][None]], axis=0).astype(BF16)
        f = _moe_call(tok, gates, wgu, wd, _moe_tile(rows_b))
        xs = _resid_call(xs, f, ng[3], mod[5], tpb, n_batch)

    return xs.reshape(n_batch, rows_b, d)[:, ctx_len:]
```

```python
import functools
import math

import numpy as np
import jax
import jax.numpy as jnp
from jax import lax
from jax.experimental import pallas as pl
from jax.experimental.pallas import tpu as pltpu

F32 = jnp.float32
BF16 = jnp.bfloat16

GRID_W = 64
EPS = 1e-6
ROPE_BASE = 10000.0
N_BRANCH = 4
BW = 256
DA_HEADS = 4
DA_HALF = 32
SW_HEADS = 4
SW_KV = 2
SW_DH = 64
WINDOW = 128
CONV_W = 31
RW_HEADS = 4
RW_DH = 64
RW_DECAY_RANK = 64
RW_A_RANK = 64
RW_G_RANK = 128
RW_DECAY_SCALE = math.exp(-0.5)
RW_GN_EPS = 64e-5
RW_SHIFT_W = 3 * BW + RW_DECAY_RANK + RW_A_RANK
N_EXPERTS = 64
TOP_K = 6
N_GROUPS = 8
TOPK_GROUPS = 4
D_EXPERT = 256
ROUTE_SCALE = 2.5

TILE = 256
CHUNK = 128
DA_KT = 768
NEG = -1e30

C_ROPE = 0
C_SWAP = 1024
C_DAV = 2048
C_SWV = 2304
C_CV = 2560
C_RW = 3072
C_GATE = 4096
C_TOTAL = 8192

VMEM_LIMIT = 48 * 1024 * 1024

_NT = (((1,), (1,)), ((), ()))
_NN = (((1,), (0,)), ((), ()))


def _cparams(sem):
    return pltpu.CompilerParams(dimension_semantics=sem, vmem_limit_bytes=VMEM_LIMIT)


def _dot(a, b, dims=_NN):
    return lax.dot_general(a.astype(BF16), b.astype(BF16), dims, preferred_element_type=F32)


def _split2(x):
    hi = x.astype(BF16)
    lo = (x - hi.astype(F32)).astype(BF16)
    return hi, lo


def _dot3(a, b, dims=_NN):
    ah, al = _split2(a)
    bh, bl = _split2(b)
    dg = lambda x, y: lax.dot_general(x, y, dims, preferred_element_type=F32)
    return dg(ah, bh) + (dg(ah, bl) + dg(al, bh))


def _dot_sel(x, sel, dims=_NN):
    h0 = x.astype(BF16)
    r1 = x - h0.astype(F32)
    h1 = r1.astype(BF16)
    h2 = (r1 - h1.astype(F32)).astype(BF16)
    dg = lambda y: lax.dot_general(y, sel, dims, preferred_element_type=F32)
    return dg(h0) + (dg(h1) + dg(h2))


def _sel_dot(sel, x):
    h0 = x.astype(BF16)
    r1 = x - h0.astype(F32)
    h1 = r1.astype(BF16)
    h2 = (r1 - h1.astype(F32)).astype(BF16)
    dg = lambda y: lax.dot_general(sel, y, _NN, preferred_element_type=F32)
    return dg(h0) + (dg(h1) + dg(h2))


def _sigmoid(x):
    return jax.nn.sigmoid(x)


def _silu(x):
    return x * jax.nn.sigmoid(x)


def _head_of_lane(shape, width):
    return lax.broadcasted_iota(jnp.int32, shape, len(shape) - 1) // width


def _ada_kernel(s_ref, w_ref, b_ref, o_ref):
    s = _silu(s_ref[...])
    o_ref[0] = _dot(s, w_ref[0]) + b_ref[0]


def _ada_call(cond, ada_w, ada_b):
    depth, d, cols = ada_w.shape
    tn = 1536
    return pl.pallas_call(
        _ada_kernel,
        grid=(depth, cols // tn),
        in_specs=[pl.BlockSpec((8, d), lambda l, j: (0, 0)),
                  pl.BlockSpec((1, d, tn), lambda l, j: (l, 0, j)),
                  pl.BlockSpec((1, 1, tn), lambda l, j: (l, 0, j))],
        out_specs=pl.BlockSpec((1, 8, tn), lambda l, j: (l, 0, j)),
        out_shape=jax.ShapeDtypeStruct((depth, 8, cols), F32),
        compiler_params=_cparams(("arbitrary", "arbitrary")),
        name="ada_mod",
    )(cond, ada_w, ada_b.reshape(depth, 1, cols))


def _norm_mod_kernel(x_ref, g_ref, sh_ref, sc_ref, o_ref, *o32):
    x = x_ref[...]
    y = x * lax.rsqrt(jnp.mean(x * x, axis=-1, keepdims=True) + EPS) * g_ref[...]
    h = y * (1.0 + sc_ref[0]) + sh_ref[0]
    o_ref[...] = h.astype(BF16)
    if o32:
        o32[0][...] = h


def _norm_mod_call(x, g, shift, scale, tpb, n_batch, want_f32):
    n, d = x.shape
    kind = lambda i: (jnp.where(i % tpb == 0, n_batch, i // tpb), 0, 0)
    row = pl.BlockSpec((TILE, d), lambda i: (i, 0))
    out_shape = [jax.ShapeDtypeStruct((n, d), BF16)]
    out_specs = [row]
    if want_f32:
        out_shape.append(jax.ShapeDtypeStruct((n, d), F32))
        out_specs.append(row)
    return pl.pallas_call(
        _norm_mod_kernel,
        grid=(n // TILE,),
        in_specs=[row, pl.BlockSpec((1, d), lambda i: (0, 0)),
                  pl.BlockSpec((1, 1, d), kind), pl.BlockSpec((1, 1, d), kind)],
        out_specs=out_specs,
        out_shape=out_shape,
        compiler_params=_cparams(("arbitrary",)),
        name="norm_mod",
    )(x, g.reshape(1, d), shift, scale)


def _mm_kernel(a_ref, w_ref, o_ref):
    o_ref[...] = jnp.dot(a_ref[...], w_ref[...], preferred_element_type=F32)


def _proj_call(h, w):
    n, d = h.shape
    cols = w.shape[1]
    tm, tn = 2 * TILE, 1024
    return pl.pallas_call(
        _mm_kernel,
        grid=(cols // tn, n // tm),
        in_specs=[pl.BlockSpec((tm, d), lambda j, i: (i, 0)),
                  pl.BlockSpec((d, tn), lambda j, i: (0, j))],
        out_specs=pl.BlockSpec((tm, tn), lambda j, i: (i, j)),
        out_shape=jax.ShapeDtypeStruct((n, cols), F32),
        compiler_params=_cparams(("arbitrary", "arbitrary")),
        name="in_proj",
    )(h, w)


def _rope_kernel(um_ref, us_ref, c_ref, s_ref, o_ref):
    o_ref[...] = (um_ref[...] * c_ref[...] + us_ref[...] * s_ref[...]).astype(BF16)


def _rope_call(u, cos_t, sin_t, tpb):
    n = u.shape[0]
    w = cos_t.shape[1]
    tab = pl.BlockSpec((TILE, w), lambda i: (i % tpb, 0))
    return pl.pallas_call(
        _rope_kernel,
        grid=(n // TILE,),
        in_specs=[pl.BlockSpec((TILE, w), lambda i: (i, C_ROPE // w)),
                  pl.BlockSpec((TILE, w), lambda i: (i, C_SWAP // w)), tab, tab],
        out_specs=pl.BlockSpec((TILE, w), lambda i: (i, 0)),
        out_shape=jax.ShapeDtypeStruct((n, w), BF16),
        compiler_params=_cparams(("arbitrary",)),
        name="rope",
    )(u, u, cos_t, sin_t)


def _da_kernel(lam_ref, q_ref, k_ref, vt_ref, g_ref, o_ref,
               qs_ref, m_ref, l_ref, acc_ref, *, nkt, lam_init):
    i = pl.program_id(1)
    tq = q_ref.shape[0]
    hd = 2 * DA_HALF
    c2 = (DA_HALF ** -0.5) * math.log2(math.e)
    q = q_ref[...]
    qmap = _head_of_lane((tq, BW), DA_HALF)
    for g in range(2 * DA_HEADS):
        qs_ref[g] = jnp.where(qmap == g, q, jnp.zeros_like(q))
    m_ref[...] = jnp.full(m_ref.shape, NEG, F32)
    l_ref[...] = jnp.zeros(l_ref.shape, F32)
    acc_ref[...] = jnp.zeros(acc_ref.shape, F32)

    def tile(off, size):
        kt = k_ref[pl.ds(off, size), :]
        vt = vt_ref[:, pl.ds(off, size)]
        vrow = lax.broadcasted_iota(jnp.int32, vt.shape, 0) // hd
        vcat = jnp.concatenate(
            [jnp.where(vrow == h, vt, jnp.zeros_like(vt)) for h in range(DA_HEADS)], axis=1)
        for mm in range(2):
            ps, alphas = [], []
            for h in range(DA_HEADS):
                g = 2 * h + mm
                s = lax.dot_general(kt, qs_ref[g], _NT, preferred_element_type=F32)
                m_old = m_ref[g:g + 1, :]
                m_new = jnp.maximum(m_old, jnp.max(s, axis=0, keepdims=True))
                alpha = jnp.exp2((m_old - m_new) * c2)
                p = jnp.exp2((s - m_new) * c2)
                l_ref[g:g + 1, :] = l_ref[g:g + 1, :] * alpha + jnp.sum(p, axis=0, keepdims=True)
                m_ref[g:g + 1, :] = m_new
                ps.append(p.astype(BF16))
                alphas.append(alpha)
            pv = jnp.dot(vcat, jnp.concatenate(ps, axis=0), preferred_element_type=F32)
            for h in range(DA_HEADS):
                acc_ref[mm, h * hd:(h + 1) * hd, :] = (
                    acc_ref[mm, h * hd:(h + 1) * hd, :] * alphas[h] + pv[h * hd:(h + 1) * hd, :])

    @pl.when(i == 0)
    def _():
        tile(0, TILE)

    @pl.when(i > 0)
    def _():
        def body(j, carry):
            tile(pl.multiple_of(j * DA_KT, DA_KT), DA_KT)
            return carry

        lax.fori_loop(0, nkt, body, 0)

    lam = lam_ref[0]
    parts = []
    for h in range(DA_HEADS):
        rows = slice(h * hd, (h + 1) * hd)
        o_h = (acc_ref[0, rows, :] / l_ref[2 * h:2 * h + 1, :]
               - lam * (acc_ref[1, rows, :] / l_ref[2 * h + 1:2 * h + 2, :]))
        ms = jnp.mean(o_h * o_h, axis=0, keepdims=True)
        parts.append(o_h * lax.rsqrt(ms + EPS))
    y = jnp.concatenate(parts, axis=0).T * g_ref[...]
    o_ref[...] = y * (1.0 - lam_init)


def _da_call(lam, qk, vt, subln, n_batch, tpb, lam_init):
    n = qk.shape[0]
    rows_b = tpb * TILE
    assert rows_b % DA_KT == 0
    kern = functools.partial(_da_kernel, nkt=rows_b // DA_KT, lam_init=lam_init)
    return pl.pallas_call(
        kern,
        grid=(n_batch, tpb),
        in_specs=[pl.BlockSpec(memory_space=pltpu.SMEM),
                  pl.BlockSpec((TILE, BW), lambda b, i: (b * tpb + i, 0)),
                  pl.BlockSpec((rows_b, BW), lambda b, i: (b, 1)),
                  pl.BlockSpec((BW, rows_b), lambda b, i: (b, 0)),
                  pl.BlockSpec((1, BW), lambda b, i: (0, 0))],
        out_specs=pl.BlockSpec((TILE, BW), lambda b, i: (b * tpb + i, 0)),
        out_shape=jax.ShapeDtypeStruct((n, BW), F32),
        scratch_shapes=[pltpu.VMEM((2 * DA_HEADS, TILE, BW), BF16),
                        pltpu.VMEM((2 * DA_HEADS, TILE), F32),
                        pltpu.VMEM((2 * DA_HEADS, TILE), F32),
                        pltpu.VMEM((2, BW, TILE), F32)],
        compiler_params=_cparams(("arbitrary", "arbitrary")),
        name="diff_attn",
    )(lam, qk, qk, vt, subln)


def _sw_kernel(sink_ref, q_ref, kp_ref, ko_ref, kn_ref, kc_ref,
               vp_ref, vo_ref, vn_ref, vc_ref, o_ref, *, cpb):
    j = pl.program_id(1)
    tq = q_ref.shape[0]
    ctx_chunks = TILE // CHUNK
    q = q_ref[...]
    kk = jnp.concatenate([kp_ref[...], ko_ref[...], kn_ref[...], kc_ref[...]], axis=0)
    vv = jnp.concatenate([vp_ref[...], vo_ref[...], vn_ref[...], vc_ref[...]], axis=0).astype(BF16)
    nk = kk.shape[0]
    r = lax.broadcasted_iota(jnp.int32, (tq, nk), 0)
    c = lax.broadcasted_iota(jnp.int32, (tq, nk), 1)
    is_lat = j >= ctx_chunks
    far = 4 * nk
    lo_prev = jnp.where(is_lat & (j - 1 >= ctx_chunks), 0, far)
    hi_own = jnp.where(is_lat, 2 * CHUNK, 0)
    hi_next = jnp.where(is_lat & (j + 1 < cpb), 2 * CHUNK, -far)
    valid = ((c >= 3 * CHUNK)
             | ((c < CHUNK) & (c >= r + lo_prev))
             | ((c >= CHUNK) & (c < hi_own))
             | ((c >= 2 * CHUNK) & (c < 3 * CHUNK) & (c <= r + hi_next)))
    qhead = _head_of_lane((tq, BW), SW_DH)
    vhead = _head_of_lane((nk, BW), SW_DH)
    ps, vs = [], []
    for h in range(SW_HEADS):
        qm = jnp.where(qhead == h, q, jnp.zeros_like(q))
        s = lax.dot_general(qm, kk, _NT, preferred_element_type=F32) * (SW_DH ** -0.5)
        s = jnp.where(valid, s, NEG)
        sk = sink_ref[h]
        m = jnp.maximum(jnp.max(s, axis=1, keepdims=True), sk)
        p = jnp.exp(s - m)
        den = jnp.sum(p, axis=1, keepdims=True) + jnp.exp(sk - m)
        ps.append((p / den).astype(BF16))
        vs.append(jnp.where(vhead == h, vv, jnp.zeros_like(vv)))
    o_ref[...] = jnp.dot(jnp.concatenate(ps, axis=1), jnp.concatenate(vs, axis=0),
                         preferred_element_type=F32)


def _sw_call(sink, qk, u, n_batch, tpb):
    n = qk.shape[0]
    cpb = tpb * (TILE // CHUNK)
    cur = lambda b, j: b * cpb + j
    prv = lambda b, j: b * cpb + jnp.maximum(j - 1, 0)
    nxt = lambda b, j: b * cpb + jnp.minimum(j + 1, cpb - 1)
    kcol, vcol = 3, C_SWV // BW
    blk = lambda f, col: pl.BlockSpec((CHUNK, BW), lambda b, j: (f(b, j), col))
    ctx = lambda col: pl.BlockSpec((TILE, BW), lambda b, j: (b * tpb, col))
    return pl.pallas_call(
        functools.partial(_sw_kernel, cpb=cpb),
        grid=(n_batch, cpb),
        in_specs=[pl.BlockSpec(memory_space=pltpu.SMEM),
                  blk(cur, 2), blk(prv, kcol), blk(cur, kcol), blk(nxt, kcol), ctx(kcol),
                  blk(prv, vcol), blk(cur, vcol), blk(nxt, vcol), ctx(vcol)],
        out_specs=pl.BlockSpec((CHUNK, BW), lambda b, j: (cur(b, j), 0)),
        out_shape=jax.ShapeDtypeStruct((n, BW), F32),
        compiler_params=_cparams(("arbitrary", "arbitrary")),
        name="window_attn",
    )(sink, qk, qk, qk, qk, qk, u, u, u, u)


def _conv_kernel(prev_ref, cur_ref, next_ref, w_ref, b_ref, ln_ref, o_ref, z_ref, *, tpb):
    i = pl.program_id(0)
    pos = i % tpb
    has_prev = pos >= 2
    has_next = (pos >= 1) & (pos < tpb - 1)
    halo = prev_ref.shape[0]

    def glu(x):
        return x[:, :BW] * _sigmoid(x[:, BW:])

    zp = glu(prev_ref[...])
    zn = glu(next_ref[...])
    z_ref[0:halo, :] = jnp.where(has_prev, zp, 0.0)
    z_ref[halo:halo + TILE, :] = glu(cur_ref[...])
    z_ref[halo + TILE:2 * halo + TILE, :] = jnp.where(has_next, zn, 0.0)
    acc = jnp.zeros((TILE, BW), F32) + b_ref[...]
    pad = CONV_W // 2
    for t in range(CONV_W):
        acc = acc + z_ref[halo - pad + t:halo - pad + t + TILE, :] * w_ref[t:t + 1, :]
    mu = jnp.mean(acc, axis=-1, keepdims=True)
    xc = acc - mu
    y = xc * lax.rsqrt(jnp.mean(xc * xc, axis=-1, keepdims=True) + EPS)
    y = y * ln_ref[0:1, :] + ln_ref[1:2, :]
    o_ref[...] = _silu(y)


def _conv_call(u, w, b, ln, tpb):
    n = u.shape[0]
    halo = 16
    per = TILE // halo
    nh = n // halo
    wcv = 2 * BW
    col = C_CV // wcv
    return pl.pallas_call(
        functools.partial(_conv_kernel, tpb=tpb),
        grid=(n // TILE,),
        in_specs=[pl.BlockSpec((halo, wcv), lambda i: (jnp.maximum(i * per - 1, 0), col)),
                  pl.BlockSpec((TILE, wcv), lambda i: (i, col)),
                  pl.BlockSpec((halo, wcv), lambda i: (jnp.minimum((i + 1) * per, nh - 1), col)),
                  pl.BlockSpec((CONV_W, BW), lambda i: (0, 0)),
                  pl.BlockSpec((1, BW), lambda i: (0, 0)),
                  pl.BlockSpec((2, BW), lambda i: (0, 0))],
        out_specs=pl.BlockSpec((TILE, BW), lambda i: (i, 0)),
        out_shape=jax.ShapeDtypeStruct((n, BW), F32),
        scratch_shapes=[pltpu.VMEM((TILE + 2 * halo, BW), F32)],
        compiler_params=_cparams(("arbitrary",)),
        name="conformer_conv",
    )(u, u, u, w, b.reshape(1, BW), ln)


def _inv_unit_lower(ls, eye):
    ts = [eye + l for l in ls]
    lps = list(ls)
    step = 1
    while step < ls[0].shape[0] // 2:
        lps = [_dot(lp, lp) for lp in lps]
        ts = [t + _dot(t, lp) for t, lp in zip(ts, lps)]
        step *= 2
    return ts


def _rw_chunk_kernel(prev_ref, cur_ref, next_ref, mu_ref, w0_ref, a0_ref, kk_ref, ka_ref,
                     wup_ref, aup_ref, rk_ref, bd_ref,
                     m_ref, n_ref, q_ref, y0_ref, bonus_ref, z_ref, *, cpb):
    i = pl.program_id(0)
    jj = i % cpb
    ctx_chunks = TILE // CHUNK
    has_prev = (jj != 0) & (jj != ctx_chunks)
    has_next = (jj != ctx_chunks - 1) & (jj != cpb - 1)
    c = CHUNK
    zc = cur_ref[:, 0:RW_SHIFT_W]
    z_ref[0:8, :] = jnp.where(has_prev, prev_ref[:, 0:RW_SHIFT_W], 0.0)
    z_ref[8:8 + c, :] = zc
    z_ref[8 + c:16 + c, :] = jnp.where(has_next, next_ref[:, 0:RW_SHIFT_W], 0.0)
    bd = bd_ref[...]
    t_io = lax.broadcasted_iota(jnp.int32, (c, c), 0)
    s_io = lax.broadcasted_iota(jnp.int32, (c, c), 1)
    eye = jnp.where(t_io == s_io, 1.0, 0.0)
    lane_head = _head_of_lane((c, BW), RW_DH)
    row = lax.broadcasted_iota(jnp.int32, (BW, BW), 0)
    colm = lax.broadcasted_iota(jnp.int32, (BW, BW), 1)
    same_head = (row // RW_DH) == (colm // RW_DH)

    def stack(x):
        return jnp.concatenate(
            [jnp.where(lane_head == h, x, 0.0) for h in range(RW_HEADS)], axis=0)

    cat = lambda xs: jnp.concatenate(xs, axis=1)

    prep = []
    for d in range(2):
        zsh = z_ref[7:7 + c, :] if d == 0 else z_ref[9:9 + c, :]
        zs = zc + (zsh - zc) * mu_ref[d]
        r = zs[:, 0:BW]
        k = zs[:, BW:2 * BW]
        v = zs[:, 2 * BW:3 * BW]
        wa = zs[:, 3 * BW:RW_SHIFT_W]
        lw = -RW_DECAY_SCALE * _sigmoid(w0_ref[d] + _dot(jnp.tanh(wa), wup_ref[d]))
        a = _sigmoid(a0_ref[d] + _dot(wa, aup_ref[d]))
        kappa = k * kk_ref[d]
        kh = kappa * lax.rsqrt(jnp.maximum(_dot_sel(kappa * kappa, bd), 1e-12))
        kt = k * (1.0 + (a - 1.0) * ka_ref[d])
        akh = a * kh
        bonus_ref[d] = _dot_sel(r * kt * rk_ref[...], bd) * v

        ahead = (t_io - s_io) if d == 0 else (s_io - t_io)
        earlier = ahead > 0
        upto = ahead >= 0
        tri = jnp.where(upto, 1.0, 0.0).astype(BF16)
        cl = _sel_dot(tri, lw)
        tot = jnp.sum(lw, axis=0, keepdims=True)
        rho = 0.5 * tot
        cle = cl - lw
        a_true = -kh * jnp.exp(cle)
        r_true = r * jnp.exp(cl)
        a_c = -kh * jnp.exp(cle - rho)
        r_c = r * jnp.exp(cl - rho)
        b_c = akh * jnp.exp(rho - cl)
        k_c = kt * jnp.exp(rho - cl)
        b_end = akh * jnp.exp(tot - cl)
        k_end = kt * jnp.exp(tot - cl)

        pair = _dot(jnp.concatenate([stack(a_c), stack(r_c)], axis=0),
                    jnp.concatenate([b_c, k_c], axis=0), _NT)
        l_ab, l_ak, a_rb, a_rk = [], [], [], []
        for h in range(RW_HEADS):
            blk_a = pair[h * c:(h + 1) * c]
            blk_r = pair[(RW_HEADS + h) * c:(RW_HEADS + h + 1) * c]
            l_ab.append(jnp.where(earlier, blk_a[:, 0:c], 0.0))
            l_ak.append(jnp.where(earlier, blk_a[:, c:2 * c], 0.0))
            a_rb.append(jnp.where(upto, blk_r[:, 0:c], 0.0))
            a_rk.append(jnp.where(upto, blk_r[:, c:2 * c], 0.0))
        prep.append((l_ab, cat(l_ak), cat(a_rb), cat(a_rk), v, a_true, r_true, b_end, k_end, tot))

    t_all = _inv_unit_lower(prep[0][0] + prep[1][0], eye)

    for d in range(2):
        _, lak, arb, ark, v, a_true, r_true, b_end, k_end, tot = prep[d]
        t_inv = cat(t_all[d * RW_HEADS:(d + 1) * RW_HEADS])
        lv = _dot(lak, stack(v))
        w12 = _dot(t_inv, jnp.concatenate([stack(a_true), stack(lv)], axis=1))
        w1 = w12[:, 0:BW]
        w2 = w12[:, BW:2 * BW]
        q_ref[d, 0] = r_true + _dot(arb, stack(w1))
        y0_ref[d, 0] = _dot(arb, stack(w2)) + _dot(ark, stack(v))
        b_end_t = b_end.T
        decay = jnp.where(row == colm, jnp.broadcast_to(jnp.exp(tot), (BW, BW)), 0.0)
        m_ref[d, 0] = jnp.where(same_head, _dot(b_end_t, w1), 0.0) + decay
        n_ref[d, 0] = jnp.where(same_head, _dot(b_end_t, w2) + _dot(k_end.T, v), 0.0)


def _rw_chunk_call(u, lp, bd, tpb):
    n = u.shape[0]
    cpb = tpb * (TILE // CHUNK)
    nch = n // CHUNK
    per = CHUNK // 8
    n8 = n // 8
    wrw = 1024
    col = C_RW // wrw
    pvec = lambda w: pl.BlockSpec((2, 1, w), lambda i: (0, 0, 0))
    pad = jnp.zeros((2, RW_DECAY_RANK, BW), F32)
    wup = jnp.concatenate([lp['rw_w_up'], pad], axis=1).astype(BF16)
    aup = jnp.concatenate([pad, lp['rw_a_up']], axis=1).astype(BF16)
    mat = lambda rows: pl.BlockSpec((2, 1, rows, BW), lambda i: (0, i, 0, 0))
    return pl.pallas_call(
        functools.partial(_rw_chunk_kernel, cpb=cpb),
        grid=(nch,),
        in_specs=[pl.BlockSpec((8, wrw), lambda i: (jnp.maximum(i * per - 1, 0), col)),
                  pl.BlockSpec((CHUNK, wrw), lambda i: (i, col)),
                  pl.BlockSpec((8, wrw), lambda i: (jnp.minimum((i + 1) * per, n8 - 1), col)),
                  pvec(RW_SHIFT_W), pvec(BW), pvec(BW), pvec(BW), pvec(BW),
                  pl.BlockSpec((2, 2 * RW_DECAY_RANK, BW), lambda i: (0, 0, 0)),
                  pl.BlockSpec((2, 2 * RW_A_RANK, BW), lambda i: (0, 0, 0)),
                  pl.BlockSpec((1, BW), lambda i: (0, 0)),
                  pl.BlockSpec((BW, BW), lambda i: (0, 0))],
        out_specs=[mat(BW), mat(BW), mat(CHUNK), mat(CHUNK),
                   pl.BlockSpec((2, CHUNK, BW), lambda i: (0, i, 0))],
        out_shape=[jax.ShapeDtypeStruct((2, nch, BW, BW), F32),
                   jax.ShapeDtypeStruct((2, nch, BW, BW), F32),
                   jax.ShapeDtypeStruct((2, nch, CHUNK, BW), F32),
                   jax.ShapeDtypeStruct((2, nch, CHUNK, BW), F32),
                   jax.ShapeDtypeStruct((2, n, BW), F32)],
        scratch_shapes=[pltpu.VMEM((CHUNK + 16, RW_SHIFT_W), F32)],
        compiler_params=_cparams(("arbitrary",)),
        name="rwkv_chunk",
    )(u, u, u, lp['rw_mu'].reshape(2, 1, RW_SHIFT_W), lp['rw_w0'].reshape(2, 1, BW),
      lp['rw_a0'].reshape(2, 1, BW), lp['rw_kk'].reshape(2, 1, BW), lp['rw_ka'].reshape(2, 1, BW),
      wup, aup, lp['rw_rk'].reshape(1, BW), bd)


def _rw_scan_kernel(m_ref, n_ref, q_ref, y0_ref, y_ref, x_ref):
    @pl.when(pl.program_id(2) == 0)
    def _():
        x_ref[...] = jnp.zeros(x_ref.shape, F32)

    x = x_ref[...]
    y_ref[0] = _dot3(q_ref[0, 0], x) + y0_ref[0, 0]
    x_ref[...] = _dot3(m_ref[0, 0], x) + n_ref[0, 0]


def _rw_scan_call(m, nn, q, y0, n_batch, tpb):
    nch = m.shape[1]
    cpb = tpb * (TILE // CHUNK)
    ctx_chunks = TILE // CHUNK

    def chunk(b, d, i):
        rev = jnp.where(i < ctx_chunks, ctx_chunks - 1 - i, cpb + ctx_chunks - 1 - i)
        return b * cpb + jnp.where(d == 0, i, rev)

    mat = lambda rows: pl.BlockSpec((1, 1, rows, BW), lambda b, d, i: (d, chunk(b, d, i), 0, 0))
    return pl.pallas_call(
        _rw_scan_kernel,
        grid=(n_batch, 2, cpb),
        in_specs=[mat(BW), mat(BW), mat(CHUNK), mat(CHUNK)],
        out_specs=pl.BlockSpec((1, CHUNK, BW), lambda b, d, i: (d, chunk(b, d, i), 0)),
        out_shape=jax.ShapeDtypeStruct((2, nch * CHUNK, BW), F32),
        scratch_shapes=[pltpu.VMEM((BW, BW), F32)],
        compiler_params=_cparams(("arbitrary", "arbitrary", "arbitrary")),
        name="rwkv_scan",
    )(m, nn, q, y0)


def _rw_out_kernel(y_ref, bonus_ref, u_ref, gup_ref, gn_ref, bd_ref, o_ref):
    bd = bd_ref[...]
    y = y_ref[0] + y_ref[1]
    mean = _dot_sel(y, bd) * (1.0 / RW_DH)
    yc = y - mean
    var = _dot_sel(yc * yc, bd) * (1.0 / RW_DH)
    yn = yc * lax.rsqrt(var + RW_GN_EPS) * gn_ref[0:1, :] + gn_ref[1:2, :]
    yn = yn + (bonus_ref[0] + bonus_ref[1])
    gd = u_ref[:, RW_SHIFT_W:RW_SHIFT_W + RW_G_RANK]
    o_ref[...] = yn * _dot(_sigmoid(gd), gup_ref[...])


def _rw_out_call(y, bonus, u, gup, gn, bd):
    n = u.shape[0]
    wrw = 1024
    both = pl.BlockSpec((2, TILE, BW), lambda i: (0, i, 0))
    return pl.pallas_call(
        _rw_out_kernel,
        grid=(n // TILE,),
        in_specs=[both, both, pl.BlockSpec((TILE, wrw), lambda i: (i, C_RW // wrw)),
                  pl.BlockSpec((RW_G_RANK, BW), lambda i: (0, 0)),
                  pl.BlockSpec((2, BW), lambda i: (0, 0)),
                  pl.BlockSpec((BW, BW), lambda i: (0, 0))],
        out_specs=pl.BlockSpec((TILE, BW), lambda i: (i, 0)),
        out_shape=jax.ShapeDtypeStruct((n, BW), F32),
        compiler_params=_cparams(("arbitrary",)),
        name="rwkv_readout",
    )(y, bonus, u, gup, gn, bd)


def _merge_kernel(oa_ref, ob_ref, oc_ref, od_ref, gl_ref, wb_ref, wo_ref, x_ref, g_ref, mod_ref,
                  o_ref):
    d = x_ref.shape[1]
    m = None
    for i, o in enumerate((oa_ref, ob_ref, oc_ref, od_ref)):
        t = _sigmoid(gl_ref[:, i * d:(i + 1) * d]) * _dot(o[...], wb_ref[i])
        m = t if m is None else m + t
    y = _dot(m, wo_ref[...])
    y = y * lax.rsqrt(jnp.mean(y * y, axis=-1, keepdims=True) + EPS) * g_ref[...]
    o_ref[...] = x_ref[...] + mod_ref[0] * y


def _merge_call(outs, u, wb, wo, x, g, mod, tpb, n_batch):
    n, d = x.shape
    kind = lambda i: (jnp.where(i % tpb == 0, n_batch, i // tpb), 0, 0)
    br = pl.BlockSpec((TILE, BW), lambda i: (i, 0))
    wg = N_BRANCH * d
    return pl.pallas_call(
        _merge_kernel,
        grid=(n // TILE,),
        in_specs=[br, br, br, br,
                  pl.BlockSpec((TILE, wg), lambda i: (i, C_GATE // wg)),
                  pl.BlockSpec((N_BRANCH, BW, d), lambda i: (0, 0, 0)),
                  pl.BlockSpec((d, d), lambda i: (0, 0)),
                  pl.BlockSpec((TILE, d), lambda i: (i, 0)),
                  pl.BlockSpec((1, d), lambda i: (0, 0)),
                  pl.BlockSpec((1, 1, d), kind)],
        out_specs=pl.BlockSpec((TILE, d), lambda i: (i, 0)),
        out_shape=jax.ShapeDtypeStruct((n, d), F32),
        compiler_params=_cparams(("arbitrary",)),
        name="merge",
    )(*outs, u, wb, wo, x, g.reshape(1, d), mod)


def _router_kernel(t_ref, w_ref, b_ref, o_ref):
    tm = t_ref.shape[0]
    gsz = N_EXPERTS // N_GROUPS
    logits = _dot3(w_ref[...], t_ref[...], _NT)
    sc = _sigmoid(logits).reshape(N_GROUPS, gsz, tm)
    bi = sc + b_ref[...].reshape(N_GROUPS, gsz, 1)
    shape = (N_GROUPS, gsz, tm)
    g_io = lax.broadcasted_iota(jnp.int32, shape, 0)
    j_io = lax.broadcasted_iota(jnp.int32, shape, 1)
    e_io = g_io * gsz + j_io
    ninf = -jnp.inf
    m1 = jnp.max(bi, axis=1, keepdims=True)
    i1 = jnp.min(jnp.where(bi == m1, j_io, gsz), axis=1, keepdims=True)
    m2 = jnp.max(jnp.where(j_io == i1, ninf, bi), axis=1, keepdims=True)
    cur = jnp.broadcast_to(m1 + m2, shape)
    gsel = jnp.zeros(shape, F32)
    for _ in range(TOPK_GROUPS):
        mx = jnp.max(cur, axis=0, keepdims=True)
        ix = jnp.min(jnp.where(cur == mx, g_io, N_GROUPS), axis=0, keepdims=True)
        hit = g_io == ix
        gsel = jnp.where(hit, 1.0, gsel)
        cur = jnp.where(hit, ninf, cur)
    cur = jnp.where(gsel > 0.0, bi, ninf)
    esel = jnp.zeros(shape, F32)
    for _ in range(TOP_K):
        mx = jnp.max(jnp.max(cur, axis=0, keepdims=True), axis=1, keepdims=True)
        ix = jnp.min(jnp.min(jnp.where(cur == mx, e_io, N_EXPERTS), axis=0, keepdims=True),
                     axis=1, keepdims=True)
        hit = e_io == ix
        esel = jnp.where(hit, 1.0, esel)
        cur = jnp.where(hit, ninf, cur)
    wsel = sc * esel
    den = jnp.sum(jnp.sum(wsel, axis=0, keepdims=True), axis=1, keepdims=True)
    o_ref[...] = (wsel / den * ROUTE_SCALE).reshape(N_EXPERTS, tm)


def _router_call(tokens, rw_t, rb):
    n, d = tokens.shape
    return pl.pallas_call(
        _router_kernel,
        grid=(n // TILE,),
        in_specs=[pl.BlockSpec((TILE, d), lambda i: (i, 0)),
                  pl.BlockSpec((N_EXPERTS, d), lambda i: (0, 0)),
                  pl.BlockSpec((N_EXPERTS, 1), lambda i: (0, 0))],
        out_specs=pl.BlockSpec((N_EXPERTS, TILE), lambda i: (0, i)),
        out_shape=jax.ShapeDtypeStruct((N_EXPERTS, n), F32),
        compiler_params=_cparams(("arbitrary",)),
        name="router",
    )(tokens, rw_t, rb.reshape(N_EXPERTS, 1))


def _moe_kernel(x_ref, g_ref, wgu_ref, wd_ref, sgu_ref, sd_ref, o_ref, acc_ref):
    e = pl.program_id(1)

    def ffn(wgu, wd):
        hgu = _dot(x_ref[...], wgu)
        return _dot(_silu(hgu[:, :D_EXPERT]) * hgu[:, D_EXPERT:], wd)

    @pl.when(e == 0)
    def _():
        acc_ref[...] = ffn(sgu_ref[...], sd_ref[...])

    gates = g_ref[...]
    lane = lax.broadcasted_iota(jnp.int32, gates.shape, 1)
    gcol = jnp.sum(jnp.where(lane == e, gates, 0.0), axis=1, keepdims=True)
    acc_ref[...] += ffn(wgu_ref[0], wd_ref[0]) * gcol

    @pl.when(e == N_EXPERTS - 1)
    def _():
        o_ref[...] = acc_ref[...]


def _moe_call(tok, gates, wgu, wd, sgu, sd, tm):
    n, d = tok.shape
    return pl.pallas_call(
        _moe_kernel,
        grid=(n // tm, N_EXPERTS),
        in_specs=[pl.BlockSpec((tm, d), lambda i, e: (i, 0)),
                  pl.BlockSpec((tm, N_EXPERTS), lambda i, e: (i, 0)),
                  pl.BlockSpec((1, d, 2 * D_EXPERT), lambda i, e: (e, 0, 0)),
                  pl.BlockSpec((1, D_EXPERT, d), lambda i, e: (e, 0, 0)),
                  pl.BlockSpec((d, 2 * D_EXPERT), lambda i, e: (0, 0)),
                  pl.BlockSpec((D_EXPERT, d), lambda i, e: (0, 0))],
        out_specs=pl.BlockSpec((tm, d), lambda i, e: (i, 0)),
        out_shape=jax.ShapeDtypeStruct((n, d), F32),
        scratch_shapes=[pltpu.VMEM((tm, d), F32)],
        compiler_params=_cparams(("arbitrary", "arbitrary")),
        name="moe_experts",
    )(tok, gates, wgu, wd, sgu, sd)


def _resid_kernel(x_ref, f_ref, g_ref, mod_ref, o_ref):
    f = f_ref[...]
    y = f * lax.rsqrt(jnp.mean(f * f, axis=-1, keepdims=True) + EPS) * g_ref[...]
    o_ref[...] = x_ref[...] + mod_ref[0] * y


def _resid_call(x, f, g, mod, tpb, n_batch):
    n, d = x.shape
    kind = lambda i: (jnp.where(i % tpb == 0, n_batch, i // tpb), 0, 0)
    row = pl.BlockSpec((TILE, d), lambda i: (i, 0))
    return pl.pallas_call(
        _resid_kernel,
        grid=(n // TILE,),
        in_specs=[row, row, pl.BlockSpec((1, d), lambda i: (0, 0)), pl.BlockSpec((1, 1, d), kind)],
        out_specs=row,
        out_shape=jax.ShapeDtypeStruct((n, d), F32),
        compiler_params=_cparams(("arbitrary",)),
        name="moe_residual",
    )(x, f, g.reshape(1, d), mod)


def _partner(dim, nblocks):
    q = dim // 4
    base = np.concatenate([np.arange(q) + q, np.arange(q), np.arange(q) + 3 * q, np.arange(q) + 2 * q])
    return (np.arange(nblocks)[:, None] * dim + base[None, :]).reshape(-1)


def _proj_columns():
    da, cv, sw = 0, 3 * BW, 5 * BW
    rw = sw + (SW_HEADS + 2 * SW_KV) * SW_DH
    gate = rw + RW_SHIFT_W + RW_G_RANK
    rep = np.repeat(np.arange(SW_KV), SW_HEADS // SW_KV)
    kv_rep = (rep[:, None] * SW_DH + np.arange(SW_DH)[None, :]).reshape(-1)
    da_q = da + np.arange(BW)
    da_k = da + BW + np.arange(BW)
    da_v = da + 2 * BW + np.arange(BW)
    sw_q = sw + np.arange(BW)
    sw_k = sw + BW + kv_rep
    sw_v = sw + BW + SW_KV * SW_DH + kv_rep
    p32 = _partner(DA_HALF, BW // DA_HALF)
    p64 = _partner(SW_DH, BW // SW_DH)
    cols = np.concatenate([
        da_q, da_k, sw_q, sw_k,
        da_q[p32], da_k[p32], sw_q[p64], sw_k[p64],
        da_v, sw_v,
        cv + np.arange(2 * BW),
        rw + np.arange(RW_SHIFT_W + RW_G_RANK),
        gate + np.arange(N_BRANCH * 4 * BW)])
    assert cols.shape[0] == C_TOTAL
    return cols


def _rope_tables(seq):
    rows = seq // GRID_W
    row = jnp.repeat(jnp.arange(rows, dtype=F32), GRID_W)
    colp = jnp.tile(jnp.arange(GRID_W, dtype=F32), rows)

    def tables(dim, reps):
        q = dim // 4
        freqs = ROPE_BASE ** (-jnp.arange(q, dtype=F32) / q)
        ar, ac = row[:, None] * freqs, colp[:, None] * freqs
        cos = jnp.concatenate([jnp.cos(ar), jnp.cos(ar), jnp.cos(ac), jnp.cos(ac)], axis=1)
        sin = jnp.concatenate([-jnp.sin(ar), jnp.sin(ar), -jnp.sin(ac), jnp.sin(ac)], axis=1)
        return jnp.tile(cos, (1, reps)), jnp.tile(sin, (1, reps))

    c32, s32 = tables(DA_HALF, 2 * BW // DA_HALF)
    c64, s64 = tables(SW_DH, 2 * BW // SW_DH)
    cos = jnp.concatenate([c32, c64], axis=1)
    sin = jnp.concatenate([s32, s64], axis=1)
    w = cos.shape[1]
    cos = jnp.concatenate([jnp.ones((TILE, w), F32), cos], axis=0)
    sin = jnp.concatenate([jnp.zeros((TILE, w), F32), sin], axis=0)
    return cos, sin


def _moe_tile(rows_b):
    best = TILE
    for t in range(TILE, 1200, 8):
        if rows_b % t == 0:
            best = t
    return best


def kernel(x, c, ctx, c_ctx, ada_w, ada_b, norm_g, w_in, w_branch, w_out, da_lambda, da_subln,
           cv_w, cv_b, cv_ln, sw_sink, rw_mu, rw_w0, rw_w_up, rw_a0, rw_a_up, rw_kk, rw_ka,
           rw_g_up, rw_rk, rw_gn, router_w, router_b, ex_w_gu, ex_w_down, sh_w_gu, sh_w_down):
    n_batch, seq, d = x.shape
    ctx_len = ctx.shape[1]
    depth = w_in.shape[0]
    assert ctx_len == TILE and seq % TILE == 0 and seq % GRID_W == 0
    assert n_batch + 1 <= 8
    rows_b = ctx_len + seq
    tpb = rows_b // TILE
    n = n_batch * rows_b

    xs = jnp.concatenate([ctx, x], axis=1).reshape(n, d)
    cond = jnp.zeros((8, d), F32).at[:n_batch].set(c).at[n_batch].set(c_ctx)
    mods = _ada_call(cond, ada_w, ada_b)[:, :n_batch + 1]
    cos_t, sin_t = _rope_tables(seq)
    cols = _proj_columns()
    hio = np.arange(BW) // RW_DH
    bd = jnp.asarray(hio[:, None] == hio[None, :], BF16)

    for l in range(depth):
        mod = [mods[l, :, i * d:(i + 1) * d].reshape(n_batch + 1, 1, d) for i in range(6)]
        ng = norm_g[l]
        lam_init = 0.8 - 0.6 * math.exp(-0.3 * l)
        lv = da_lambda[l]
        lam = (jnp.exp(jnp.sum(lv[0] * lv[1])) - jnp.exp(jnp.sum(lv[2] * lv[3])) + lam_init)
        lp = {'rw_mu': rw_mu[l], 'rw_w0': rw_w0[l], 'rw_w_up': rw_w_up[l], 'rw_a0': rw_a0[l],
              'rw_a_up': rw_a_up[l], 'rw_kk': rw_kk[l], 'rw_ka': rw_ka[l], 'rw_rk': rw_rk[l]}

        (h,) = _norm_mod_call(xs, ng[0], mod[0], mod[1], tpb, n_batch, False)
        u = _proj_call(h, w_in[l][:, cols].astype(BF16))
        qk = _rope_call(u, cos_t, sin_t, tpb)
        vt = (u[:, C_DAV:C_DAV + BW].reshape(n_batch, rows_b, BW).transpose(0, 2, 1)
              .astype(BF16).reshape(n_batch * BW, rows_b))
        oa = _da_call(lam.reshape(1), qk, vt, jnp.tile(da_subln[l], DA_HEADS).reshape(1, BW),
                      n_batch, tpb, lam_init)
        ob = _conv_call(u, cv_w[l], cv_b[l], cv_ln[l], tpb)
        oc = _sw_call(sw_sink[l], qk, u, n_batch, tpb)
        cm, cn, cq, cy0, bonus = _rw_chunk_call(u, lp, bd, tpb)
        yscan = _rw_scan_call(cm, cn, cq, cy0, n_batch, tpb)
        od = _rw_out_call(yscan, bonus, u, rw_g_up[l].astype(BF16), rw_gn[l], bd)
        xs = _merge_call((oa, ob, oc, od), u, w_branch[l].astype(BF16), w_out[l].astype(BF16),
                         xs, ng[1], mod[2], tpb, n_batch)

        tok, tok32 = _norm_mod_call(xs, ng[2], mod[3], mod[4], tpb, n_batch, True)
        gates = _router_call(tok32, router_w[l].T, router_b[l]).T
        f = _moe_call(tok, gates, ex_w_gu[l], ex_w_down[l], sh_w_gu[l].astype(BF16),
                      sh_w_down[l].astype(BF16), _moe_tile(rows_b))
        xs = _resid_call(xs, f, ng[3], mod[5], tpb, n_batch)

    return xs.reshape(n_batch, rows_b, d)[:, ctx_len:]
```

```python
import functools
import math

import numpy as np
import jax
import jax.numpy as jnp
from jax import lax
from jax.experimental import pallas as pl
from jax.experimental.pallas import tpu as pltpu

F32 = jnp.float32
BF16 = jnp.bfloat16

GRID_W = 64
EPS = 1e-6
ROPE_BASE = 10000.0
N_BRANCH = 4
BW = 256
DA_HEADS = 4
DA_HALF = 32
SW_HEADS = 4
SW_KV = 2
SW_DH = 64
WINDOW = 128
CONV_W = 31
RW_HEADS = 4
RW_DH = 64
RW_DECAY_RANK = 64
RW_A_RANK = 64
RW_G_RANK = 128
RW_DECAY_SCALE = math.exp(-0.5)
RW_GN_EPS = 64e-5
RW_SHIFT_W = 3 * BW + RW_DECAY_RANK + RW_A_RANK
N_EXPERTS = 64
TOP_K = 6
N_GROUPS = 8
TOPK_GROUPS = 4
D_EXPERT = 256
ROUTE_SCALE = 2.5

TILE = 256
CHUNK = 128
DA_KT = 1408
NEG = -1e30

C_ROPE = 0
C_SWAP = 1024
C_DAV = 2048
C_SWV = 2304
C_CV = 2560
C_RW = 3072
C_GATE = 4096
C_TOTAL = 8192

VMEM_LIMIT = 48 * 1024 * 1024

_NT = (((1,), (1,)), ((), ()))
_NN = (((1,), (0,)), ((), ()))


def _cparams(sem):
    return pltpu.CompilerParams(dimension_semantics=sem, vmem_limit_bytes=VMEM_LIMIT)


def _dot(a, b, dims=_NN):
    return lax.dot_general(a.astype(BF16), b.astype(BF16), dims, preferred_element_type=F32)


def _split2(x):
    hi = x.astype(BF16)
    lo = (x - hi.astype(F32)).astype(BF16)
    return hi, lo


def _dot3(a, b, dims=_NN):
    ah, al = _split2(a)
    bh, bl = _split2(b)
    dg = lambda x, y: lax.dot_general(x, y, dims, preferred_element_type=F32)
    return dg(ah, bh) + (dg(ah, bl) + dg(al, bh))


def _dot_sel(x, sel, dims=_NN):
    h0 = x.astype(BF16)
    r1 = x - h0.astype(F32)
    h1 = r1.astype(BF16)
    h2 = (r1 - h1.astype(F32)).astype(BF16)
    dg = lambda y: lax.dot_general(y, sel, dims, preferred_element_type=F32)
    return dg(h0) + (dg(h1) + dg(h2))


def _sel_dot(sel, x):
    h0 = x.astype(BF16)
    r1 = x - h0.astype(F32)
    h1 = r1.astype(BF16)
    h2 = (r1 - h1.astype(F32)).astype(BF16)
    dg = lambda y: lax.dot_general(sel, y, _NN, preferred_element_type=F32)
    return dg(h0) + (dg(h1) + dg(h2))


def _sigmoid(x):
    return jax.nn.sigmoid(x)


def _silu(x):
    return x * jax.nn.sigmoid(x)


def _head_of_lane(shape, width):
    return lax.broadcasted_iota(jnp.int32, shape, len(shape) - 1) // width


def _ada_kernel(s_ref, w_ref, b_ref, o_ref):
    s = _silu(s_ref[...])
    o_ref[0] = _dot(s, w_ref[0]) + b_ref[0]


def _ada_call(cond, ada_w, ada_b):
    depth, d, cols = ada_w.shape
    tn = 1536
    return pl.pallas_call(
        _ada_kernel,
        grid=(depth, cols // tn),
        in_specs=[pl.BlockSpec((8, d), lambda l, j: (0, 0)),
                  pl.BlockSpec((1, d, tn), lambda l, j: (l, 0, j)),
                  pl.BlockSpec((1, 1, tn), lambda l, j: (l, 0, j))],
        out_specs=pl.BlockSpec((1, 8, tn), lambda l, j: (l, 0, j)),
        out_shape=jax.ShapeDtypeStruct((depth, 8, cols), F32),
        compiler_params=_cparams(("arbitrary", "arbitrary")),
        name="ada_mod",
    )(cond, ada_w, ada_b.reshape(depth, 1, cols))


def _norm_mod_kernel(x_ref, g_ref, sh_ref, sc_ref, o_ref, *o32):
    x = x_ref[...]
    y = x * lax.rsqrt(jnp.mean(x * x, axis=-1, keepdims=True) + EPS) * g_ref[...]
    h = y * (1.0 + sc_ref[0]) + sh_ref[0]
    o_ref[...] = h.astype(BF16)
    if o32:
        o32[0][...] = h


def _norm_mod_call(x, g, shift, scale, tpb, n_batch, want_f32):
    n, d = x.shape
    kind = lambda i: (jnp.where(i % tpb == 0, n_batch, i // tpb), 0, 0)
    row = pl.BlockSpec((TILE, d), lambda i: (i, 0))
    out_shape = [jax.ShapeDtypeStruct((n, d), BF16)]
    out_specs = [row]
    if want_f32:
        out_shape.append(jax.ShapeDtypeStruct((n, d), F32))
        out_specs.append(row)
    return pl.pallas_call(
        _norm_mod_kernel,
        grid=(n // TILE,),
        in_specs=[row, pl.BlockSpec((1, d), lambda i: (0, 0)),
                  pl.BlockSpec((1, 1, d), kind), pl.BlockSpec((1, 1, d), kind)],
        out_specs=out_specs,
        out_shape=out_shape,
        compiler_params=_cparams(("arbitrary",)),
        name="norm_mod",
    )(x, g.reshape(1, d), shift, scale)


def _mm_kernel(a_ref, w_ref, o_ref):
    o_ref[...] = jnp.dot(a_ref[...], w_ref[...], preferred_element_type=F32)


def _proj_call(h, w):
    n, d = h.shape
    cols = w.shape[1]
    tm, tn = 2 * TILE, 1024
    return pl.pallas_call(
        _mm_kernel,
        grid=(cols // tn, n // tm),
        in_specs=[pl.BlockSpec((tm, d), lambda j, i: (i, 0)),
                  pl.BlockSpec((d, tn), lambda j, i: (0, j))],
        out_specs=pl.BlockSpec((tm, tn), lambda j, i: (i, j)),
        out_shape=jax.ShapeDtypeStruct((n, cols), F32),
        compiler_params=_cparams(("arbitrary", "arbitrary")),
        name="in_proj",
    )(h, w)


def _rope_kernel(um_ref, us_ref, c_ref, s_ref, v_ref, o_ref, vt_ref):
    o_ref[...] = (um_ref[...] * c_ref[...] + us_ref[...] * s_ref[...]).astype(BF16)
    vt_ref[...] = v_ref[...].T.astype(BF16)


def _rope_call(u, cos_t, sin_t, tpb):
    n = u.shape[0]
    w = cos_t.shape[1]
    tab = pl.BlockSpec((TILE, w), lambda i: (i % tpb, 0))
    return pl.pallas_call(
        _rope_kernel,
        grid=(n // TILE,),
        in_specs=[pl.BlockSpec((TILE, w), lambda i: (i, C_ROPE // w)),
                  pl.BlockSpec((TILE, w), lambda i: (i, C_SWAP // w)), tab, tab,
                  pl.BlockSpec((TILE, BW), lambda i: (i, C_DAV // BW))],
        out_specs=[pl.BlockSpec((TILE, w), lambda i: (i, 0)),
                   pl.BlockSpec((BW, TILE), lambda i: (i // tpb, i % tpb))],
        out_shape=[jax.ShapeDtypeStruct((n, w), BF16),
                   jax.ShapeDtypeStruct((n // (tpb * TILE) * BW, tpb * TILE), BF16)],
        compiler_params=_cparams(("arbitrary",)),
        name="rope",
    )(u, u, cos_t, sin_t, u)


def _da_kernel(lam_ref, q_ref, k_ref, vt_ref, g_ref, o_ref,
               qs_ref, m_ref, l_ref, acc_ref, *, nkt, lam_init):
    i = pl.program_id(1)
    tq = q_ref.shape[0]
    hd = 2 * DA_HALF
    c2 = (DA_HALF ** -0.5) * math.log2(math.e)
    q = q_ref[...]
    qmap = _head_of_lane((tq, BW), DA_HALF)
    for g in range(2 * DA_HEADS):
        qs_ref[g] = jnp.where(qmap == g, q, jnp.zeros_like(q))
    m_ref[...] = jnp.full(m_ref.shape, NEG, F32)
    l_ref[...] = jnp.zeros(l_ref.shape, F32)
    acc_ref[...] = jnp.zeros(acc_ref.shape, F32)

    def tile(off, size):
        kt = k_ref[pl.ds(off, size), :]
        groups = range(2 * DA_HEADS)
        ss = [lax.dot_general(kt, qs_ref[g], _NT, preferred_element_type=F32) for g in groups]
        ps, alphas = [], []
        for g in groups:
            s = ss[g]
            m_old = m_ref[g]
            m_new = jnp.maximum(m_old, jnp.max(s, axis=0, keepdims=True))
            alpha = jnp.exp2((m_old - m_new) * c2)
            p = jnp.exp2((s - m_new[0:1, :]) * c2)
            l_ref[g] = l_ref[g] * alpha + jnp.sum(p, axis=0, keepdims=True)
            m_ref[g] = m_new
            ps.append(p.astype(BF16))
            alphas.append(alpha[0:1, :])
        for g in groups:
            h, mm = g // 2, g % 2
            vt = vt_ref[h * hd:(h + 1) * hd, pl.ds(off, size)]
            pv = jnp.dot(vt, ps[g], preferred_element_type=F32)
            acc_ref[mm, h * hd:(h + 1) * hd, :] = (
                acc_ref[mm, h * hd:(h + 1) * hd, :] * alphas[g] + pv)

    @pl.when(i == 0)
    def _():
        tile(0, TILE)

    @pl.when(i > 0)
    def _():
        def body(j, carry):
            tile(pl.multiple_of(j * DA_KT, DA_KT), DA_KT)
            return carry

        lax.fori_loop(0, nkt, body, 0)

    lam = lam_ref[0]
    parts = []
    for h in range(DA_HEADS):
        rows = slice(h * hd, (h + 1) * hd)
        o_h = (acc_ref[0, rows, :] / l_ref[2 * h, 0:1, :]
               - lam * (acc_ref[1, rows, :] / l_ref[2 * h + 1, 0:1, :]))
        ms = jnp.mean(o_h * o_h, axis=0, keepdims=True)
        parts.append(o_h * lax.rsqrt(ms + EPS))
    y = jnp.concatenate(parts, axis=0).T * g_ref[...]
    o_ref[...] = y * (1.0 - lam_init)


def _da_call(lam, qk, vt, subln, n_batch, tpb, lam_init):
    n = qk.shape[0]
    rows_b = tpb * TILE
    assert rows_b % DA_KT == 0
    kern = functools.partial(_da_kernel, nkt=rows_b // DA_KT, lam_init=lam_init)
    return pl.pallas_call(
        kern,
        grid=(n_batch, tpb),
        in_specs=[pl.BlockSpec(memory_space=pltpu.SMEM),
                  pl.BlockSpec((TILE, BW), lambda b, i: (b * tpb + i, 0)),
                  pl.BlockSpec((rows_b, BW), lambda b, i: (b, 1)),
                  pl.BlockSpec((BW, rows_b), lambda b, i: (b, 0)),
                  pl.BlockSpec((1, BW), lambda b, i: (0, 0))],
        out_specs=pl.BlockSpec((TILE, BW), lambda b, i: (b * tpb + i, 0)),
        out_shape=jax.ShapeDtypeStruct((n, BW), F32),
        scratch_shapes=[pltpu.VMEM((2 * DA_HEADS, TILE, BW), BF16),
                        pltpu.VMEM((2 * DA_HEADS, 8, TILE), F32),
                        pltpu.VMEM((2 * DA_HEADS, 8, TILE), F32),
                        pltpu.VMEM((2, BW, TILE), F32)],
        compiler_params=_cparams(("arbitrary", "arbitrary")),
        name="diff_attn",
    )(lam, qk, qk, vt, subln)


def _sw_kernel(sink_ref, q_ref, kp_ref, ko_ref, kn_ref, kc_ref,
               vp_ref, vo_ref, vn_ref, vc_ref, o_ref, *, cpb):
    j = pl.program_id(1)
    tq = q_ref.shape[0]
    ctx_chunks = TILE // CHUNK
    q = q_ref[...]
    kk = jnp.concatenate([kp_ref[...], ko_ref[...], kn_ref[...], kc_ref[...]], axis=0)
    vv = jnp.concatenate([vp_ref[...], vo_ref[...], vn_ref[...], vc_ref[...]], axis=0).astype(BF16)
    nk = kk.shape[0]
    r = lax.broadcasted_iota(jnp.int32, (tq, nk), 0)
    c = lax.broadcasted_iota(jnp.int32, (tq, nk), 1)
    is_lat = j >= ctx_chunks
    far = 4 * nk
    lo_prev = jnp.where(is_lat & (j - 1 >= ctx_chunks), 0, far)
    hi_own = jnp.where(is_lat, 2 * CHUNK, 0)
    hi_next = jnp.where(is_lat & (j + 1 < cpb), 2 * CHUNK, -far)
    valid = ((c >= 3 * CHUNK)
             | ((c < CHUNK) & (c >= r + lo_prev))
             | ((c >= CHUNK) & (c < hi_own))
             | ((c >= 2 * CHUNK) & (c < 3 * CHUNK) & (c <= r + hi_next)))
    qhead = _head_of_lane((tq, BW), SW_DH)
    vhead = _head_of_lane((nk, BW), SW_DH)
    ps, vs = [], []
    for h in range(SW_HEADS):
        qm = jnp.where(qhead == h, q, jnp.zeros_like(q))
        s = lax.dot_general(qm, kk, _NT, preferred_element_type=F32) * (SW_DH ** -0.5)
        s = jnp.where(valid, s, NEG)
        sk = sink_ref[h]
        m = jnp.maximum(jnp.max(s, axis=1, keepdims=True), sk)
        p = jnp.exp(s - m)
        den = jnp.sum(p, axis=1, keepdims=True) + jnp.exp(sk - m)
        ps.append((p / den).astype(BF16))
        vs.append(jnp.where(vhead == h, vv, jnp.zeros_like(vv)))
    o_ref[...] = jnp.dot(jnp.concatenate(ps, axis=1), jnp.concatenate(vs, axis=0),
                         preferred_element_type=F32)


def _sw_call(sink, qk, u, n_batch, tpb):
    n = qk.shape[0]
    cpb = tpb * (TILE // CHUNK)
    cur = lambda b, j: b * cpb + j
    prv = lambda b, j: b * cpb + jnp.maximum(j - 1, 0)
    nxt = lambda b, j: b * cpb + jnp.minimum(j + 1, cpb - 1)
    kcol, vcol = 3, C_SWV // BW
    blk = lambda f, col: pl.BlockSpec((CHUNK, BW), lambda b, j: (f(b, j), col))
    ctx = lambda col: pl.BlockSpec((TILE, BW), lambda b, j: (b * tpb, col))
    return pl.pallas_call(
        functools.partial(_sw_kernel, cpb=cpb),
        grid=(n_batch, cpb),
        in_specs=[pl.BlockSpec(memory_space=pltpu.SMEM),
                  blk(cur, 2), blk(prv, kcol), blk(cur, kcol), blk(nxt, kcol), ctx(kcol),
                  blk(prv, vcol), blk(cur, vcol), blk(nxt, vcol), ctx(vcol)],
        out_specs=pl.BlockSpec((CHUNK, BW), lambda b, j: (cur(b, j), 0)),
        out_shape=jax.ShapeDtypeStruct((n, BW), F32),
        compiler_params=_cparams(("arbitrary", "arbitrary")),
        name="window_attn",
    )(sink, qk, qk, qk, qk, qk, u, u, u, u)


def _conv_kernel(prev_ref, cur_ref, next_ref, w_ref, b_ref, ln_ref, o_ref, z_ref, *, tpb):
    i = pl.program_id(0)
    pos = i % tpb
    has_prev = pos >= 2
    has_next = (pos >= 1) & (pos < tpb - 1)
    halo = prev_ref.shape[0]

    def glu(x):
        return x[:, :BW] * _sigmoid(x[:, BW:])

    zp = glu(prev_ref[...])
    zn = glu(next_ref[...])
    z_ref[0:halo, :] = jnp.where(has_prev, zp, 0.0)
    z_ref[halo:halo + TILE, :] = glu(cur_ref[...])
    z_ref[halo + TILE:2 * halo + TILE, :] = jnp.where(has_next, zn, 0.0)
    acc = jnp.zeros((TILE, BW), F32) + b_ref[...]
    pad = CONV_W // 2
    for t in range(CONV_W):
        acc = acc + z_ref[halo - pad + t:halo - pad + t + TILE, :] * w_ref[t:t + 1, :]
    mu = jnp.mean(acc, axis=-1, keepdims=True)
    xc = acc - mu
    y = xc * lax.rsqrt(jnp.mean(xc * xc, axis=-1, keepdims=True) + EPS)
    y = y * ln_ref[0:1, :] + ln_ref[1:2, :]
    o_ref[...] = _silu(y)


def _conv_call(u, w, b, ln, tpb):
    n = u.shape[0]
    halo = 16
    per = TILE // halo
    nh = n // halo
    wcv = 2 * BW
    col = C_CV // wcv
    return pl.pallas_call(
        functools.partial(_conv_kernel, tpb=tpb),
        grid=(n // TILE,),
        in_specs=[pl.BlockSpec((halo, wcv), lambda i: (jnp.maximum(i * per - 1, 0), col)),
                  pl.BlockSpec((TILE, wcv), lambda i: (i, col)),
                  pl.BlockSpec((halo, wcv), lambda i: (jnp.minimum((i + 1) * per, nh - 1), col)),
                  pl.BlockSpec((CONV_W, BW), lambda i: (0, 0)),
                  pl.BlockSpec((1, BW), lambda i: (0, 0)),
                  pl.BlockSpec((2, BW), lambda i: (0, 0))],
        out_specs=pl.BlockSpec((TILE, BW), lambda i: (i, 0)),
        out_shape=jax.ShapeDtypeStruct((n, BW), F32),
        scratch_shapes=[pltpu.VMEM((TILE + 2 * halo, BW), F32)],
        compiler_params=_cparams(("arbitrary",)),
        name="conformer_conv",
    )(u, u, u, w, b.reshape(1, BW), ln)


def _inv_unit_lower(ls, eye):
    ts = [eye + l for l in ls]
    lps = list(ls)
    step = 1
    while step < ls[0].shape[0] // 2:
        lps = [_dot(lp, lp) for lp in lps]
        ts = [t + _dot(t, lp) for t, lp in zip(ts, lps)]
        step *= 2
    return ts


def _rw_chunk_kernel(prev_ref, cur_ref, next_ref, mu_ref, w0_ref, a0_ref, kk_ref, ka_ref,
                     wup_ref, aup_ref, rk_ref, bd_ref,
                     m_ref, n_ref, q_ref, y0_ref, bonus_ref, z_ref, *, cpb):
    i = pl.program_id(0)
    jj = i % cpb
    ctx_chunks = TILE // CHUNK
    has_prev = (jj != 0) & (jj != ctx_chunks)
    has_next = (jj != ctx_chunks - 1) & (jj != cpb - 1)
    c = CHUNK
    zc = cur_ref[:, 0:RW_SHIFT_W]
    z_ref[0:8, :] = jnp.where(has_prev, prev_ref[:, 0:RW_SHIFT_W], 0.0)
    z_ref[8:8 + c, :] = zc
    z_ref[8 + c:16 + c, :] = jnp.where(has_next, next_ref[:, 0:RW_SHIFT_W], 0.0)
    bd = bd_ref[...]
    t_io = lax.broadcasted_iota(jnp.int32, (c, c), 0)
    s_io = lax.broadcasted_iota(jnp.int32, (c, c), 1)
    eye = jnp.where(t_io == s_io, 1.0, 0.0)
    lane_head = _head_of_lane((c, BW), RW_DH)
    row = lax.broadcasted_iota(jnp.int32, (BW, BW), 0)
    colm = lax.broadcasted_iota(jnp.int32, (BW, BW), 1)
    same_head = (row // RW_DH) == (colm // RW_DH)

    def stack(x):
        return jnp.concatenate(
            [jnp.where(lane_head == h, x, 0.0) for h in range(RW_HEADS)], axis=0)

    cat = lambda xs: jnp.concatenate(xs, axis=1)

    prep = []
    for d in range(2):
        zsh = z_ref[7:7 + c, :] if d == 0 else z_ref[9:9 + c, :]
        zs = zc + (zsh - zc) * mu_ref[d]
        r = zs[:, 0:BW]
        k = zs[:, BW:2 * BW]
        v = zs[:, 2 * BW:3 * BW]
        wa = zs[:, 3 * BW:RW_SHIFT_W]
        lw = -RW_DECAY_SCALE * _sigmoid(w0_ref[d] + _dot(jnp.tanh(wa), wup_ref[d]))
        a = _sigmoid(a0_ref[d] + _dot(wa, aup_ref[d]))
        kappa = k * kk_ref[d]
        kh = kappa * lax.rsqrt(jnp.maximum(_dot_sel(kappa * kappa, bd), 1e-12))
        kt = k * (1.0 + (a - 1.0) * ka_ref[d])
        akh = a * kh
        bonus_ref[d] = _dot_sel(r * kt * rk_ref[...], bd) * v

        ahead = (t_io - s_io) if d == 0 else (s_io - t_io)
        earlier = ahead > 0
        upto = ahead >= 0
        tri = jnp.where(upto, 1.0, 0.0).astype(BF16)
        cl = _sel_dot(tri, lw)
        tot = jnp.sum(lw, axis=0, keepdims=True)
        rho = 0.5 * tot
        cle = cl - lw
        a_true = -kh * jnp.exp(cle)
        r_true = r * jnp.exp(cl)
        a_c = -kh * jnp.exp(cle - rho)
        r_c = r * jnp.exp(cl - rho)
        b_c = akh * jnp.exp(rho - cl)
        k_c = kt * jnp.exp(rho - cl)
        b_end = akh * jnp.exp(tot - cl)
        k_end = kt * jnp.exp(tot - cl)

        pair = _dot(jnp.concatenate([stack(a_c), stack(r_c)], axis=0),
                    jnp.concatenate([b_c, k_c], axis=0), _NT)
        l_ab, l_ak, a_rb, a_rk = [], [], [], []
        for h in range(RW_HEADS):
            blk_a = pair[h * c:(h + 1) * c]
            blk_r = pair[(RW_HEADS + h) * c:(RW_HEADS + h + 1) * c]
            l_ab.append(jnp.where(earlier, blk_a[:, 0:c], 0.0))
            l_ak.append(jnp.where(earlier, blk_a[:, c:2 * c], 0.0))
            a_rb.append(jnp.where(upto, blk_r[:, 0:c], 0.0))
            a_rk.append(jnp.where(upto, blk_r[:, c:2 * c], 0.0))
        prep.append((l_ab, cat(l_ak), cat(a_rb), cat(a_rk), v, a_true, r_true, b_end, k_end, tot))

    t_all = _inv_unit_lower(prep[0][0] + prep[1][0], eye)

    for d in range(2):
        _, lak, arb, ark, v, a_true, r_true, b_end, k_end, tot = prep[d]
        t_inv = cat(t_all[d * RW_HEADS:(d + 1) * RW_HEADS])
        lv = _dot(lak, stack(v))
        w12 = _dot(t_inv, jnp.concatenate([stack(a_true), stack(lv)], axis=1))
        w1 = w12[:, 0:BW]
        w2 = w12[:, BW:2 * BW]
        q_ref[d, 0] = r_true + _dot(arb, stack(w1))
        y0_ref[d, 0] = _dot(arb, stack(w2)) + _dot(ark, stack(v))
        b_end_t = b_end.T
        decay = jnp.where(row == colm, jnp.broadcast_to(jnp.exp(tot), (BW, BW)), 0.0)
        m_ref[d, 0] = jnp.where(same_head, _dot(b_end_t, w1), 0.0) + decay
        n_ref[d, 0] = jnp.where(same_head, _dot(b_end_t, w2) + _dot(k_end.T, v), 0.0)


def _rw_chunk_call(u, lp, bd, tpb):
    n = u.shape[0]
    cpb = tpb * (TILE // CHUNK)
    nch = n // CHUNK
    per = CHUNK // 8
    n8 = n // 8
    wrw = 1024
    col = C_RW // wrw
    pvec = lambda w: pl.BlockSpec((2, 1, w), lambda i: (0, 0, 0))
    pad = jnp.zeros((2, RW_DECAY_RANK, BW), F32)
    wup = jnp.concatenate([lp['rw_w_up'], pad], axis=1).astype(BF16)
    aup = jnp.concatenate([pad, lp['rw_a_up']], axis=1).astype(BF16)
    mat = lambda rows: pl.BlockSpec((2, 1, rows, BW), lambda i: (0, i, 0, 0))
    return pl.pallas_call(
        functools.partial(_rw_chunk_kernel, cpb=cpb),
        grid=(nch,),
        in_specs=[pl.BlockSpec((8, wrw), lambda i: (jnp.maximum(i * per - 1, 0), col)),
                  pl.BlockSpec((CHUNK, wrw), lambda i: (i, col)),
                  pl.BlockSpec((8, wrw), lambda i: (jnp.minimum((i + 1) * per, n8 - 1), col)),
                  pvec(RW_SHIFT_W), pvec(BW), pvec(BW), pvec(BW), pvec(BW),
                  pl.BlockSpec((2, 2 * RW_DECAY_RANK, BW), lambda i: (0, 0, 0)),
                  pl.BlockSpec((2, 2 * RW_A_RANK, BW), lambda i: (0, 0, 0)),
                  pl.BlockSpec((1, BW), lambda i: (0, 0)),
                  pl.BlockSpec((BW, BW), lambda i: (0, 0))],
        out_specs=[mat(BW), mat(BW), mat(CHUNK), mat(CHUNK),
                   pl.BlockSpec((2, CHUNK, BW), lambda i: (0, i, 0))],
        out_shape=[jax.ShapeDtypeStruct((2, nch, BW, BW), F32),
                   jax.ShapeDtypeStruct((2, nch, BW, BW), F32),
                   jax.ShapeDtypeStruct((2, nch, CHUNK, BW), F32),
                   jax.ShapeDtypeStruct((2, nch, CHUNK, BW), F32),
                   jax.ShapeDtypeStruct((2, n, BW), F32)],
        scratch_shapes=[pltpu.VMEM((CHUNK + 16, RW_SHIFT_W), F32)],
        compiler_params=_cparams(("arbitrary",)),
        name="rwkv_chunk",
    )(u, u, u, lp['rw_mu'].reshape(2, 1, RW_SHIFT_W), lp['rw_w0'].reshape(2, 1, BW),
      lp['rw_a0'].reshape(2, 1, BW), lp['rw_kk'].reshape(2, 1, BW), lp['rw_ka'].reshape(2, 1, BW),
      wup, aup, lp['rw_rk'].reshape(1, BW), bd)


def _rw_scan_kernel(*refs, n_batch):
    m_refs, n_refs, q_refs, y0_refs = refs[0:2], refs[2:4], refs[4:6], refs[6:8]
    y_refs, x_ref = refs[8:10], refs[10]

    @pl.when(pl.program_id(0) == 0)
    def _():
        x_ref[...] = jnp.zeros(x_ref.shape, F32)

    for d in range(2):
        for b in range(n_batch):
            x = x_ref[d, b]
            y_refs[d][b, 0] = _dot3(q_refs[d][0, b, 0], x) + y0_refs[d][0, b, 0]
            x_ref[d, b] = _dot3(m_refs[d][0, b, 0], x) + n_refs[d][0, b, 0]


def _rw_scan_call(m, nn, q, y0, n_batch, tpb):
    cpb = tpb * (TILE // CHUNK)
    ctx_chunks = TILE // CHUNK

    def chunk(d, i):
        rev = jnp.where(i < ctx_chunks, ctx_chunks - 1 - i, cpb + ctx_chunks - 1 - i)
        return i if d == 0 else rev

    def mat(rows, d):
        return pl.BlockSpec((1, n_batch, 1, rows, BW), lambda i: (d, 0, chunk(d, i), 0, 0))

    ins, specs = [], []
    for arr, rows in ((m, BW), (nn, BW), (q, CHUNK), (y0, CHUNK)):
        arr = arr.reshape(2, n_batch, cpb, rows, BW)
        for d in range(2):
            ins.append(arr)
            specs.append(mat(rows, d))
    yshape = jax.ShapeDtypeStruct((n_batch, cpb, CHUNK, BW), F32)
    ys = pl.pallas_call(
        functools.partial(_rw_scan_kernel, n_batch=n_batch),
        grid=(cpb,),
        in_specs=specs,
        out_specs=[pl.BlockSpec((n_batch, 1, CHUNK, BW), lambda i, d=d: (0, chunk(d, i), 0, 0))
                   for d in range(2)],
        out_shape=[yshape, yshape],
        scratch_shapes=[pltpu.VMEM((2, n_batch, BW, BW), F32)],
        compiler_params=_cparams(("arbitrary",)),
        name="rwkv_scan",
    )(*ins)
    return [y.reshape(n_batch * cpb * CHUNK, BW) for y in ys]


def _rw_out_kernel(yf_ref, yb_ref, bonus_ref, u_ref, gup_ref, gn_ref, bd_ref, o_ref):
    bd = bd_ref[...]
    y = yf_ref[...] + yb_ref[...]
    mean = _dot_sel(y, bd) * (1.0 / RW_DH)
    yc = y - mean
    var = _dot_sel(yc * yc, bd) * (1.0 / RW_DH)
    yn = yc * lax.rsqrt(var + RW_GN_EPS) * gn_ref[0:1, :] + gn_ref[1:2, :]
    yn = yn + (bonus_ref[0] + bonus_ref[1])
    gd = u_ref[:, RW_SHIFT_W:RW_SHIFT_W + RW_G_RANK]
    o_ref[...] = yn * _dot(_sigmoid(gd), gup_ref[...])


def _rw_out_call(ys, bonus, u, gup, gn, bd):
    n = u.shape[0]
    wrw = 1024
    both = pl.BlockSpec((2, TILE, BW), lambda i: (0, i, 0))
    one = pl.BlockSpec((TILE, BW), lambda i: (i, 0))
    return pl.pallas_call(
        _rw_out_kernel,
        grid=(n // TILE,),
        in_specs=[one, one, both, pl.BlockSpec((TILE, wrw), lambda i: (i, C_RW // wrw)),
                  pl.BlockSpec((RW_G_RANK, BW), lambda i: (0, 0)),
                  pl.BlockSpec((2, BW), lambda i: (0, 0)),
                  pl.BlockSpec((BW, BW), lambda i: (0, 0))],
        out_specs=pl.BlockSpec((TILE, BW), lambda i: (i, 0)),
        out_shape=jax.ShapeDtypeStruct((n, BW), F32),
        compiler_params=_cparams(("arbitrary",)),
        name="rwkv_readout",
    )(ys[0], ys[1], bonus, u, gup, gn, bd)


def _merge_kernel(oa_ref, ob_ref, oc_ref, od_ref, gl_ref, wb_ref, wo_ref, x_ref, g_ref, mod_ref,
                  o_ref):
    d = x_ref.shape[1]
    m = None
    for i, o in enumerate((oa_ref, ob_ref, oc_ref, od_ref)):
        t = _sigmoid(gl_ref[:, i * d:(i + 1) * d]) * _dot(o[...], wb_ref[i])
        m = t if m is None else m + t
    y = _dot(m, wo_ref[...])
    y = y * lax.rsqrt(jnp.mean(y * y, axis=-1, keepdims=True) + EPS) * g_ref[...]
    o_ref[...] = x_ref[...] + mod_ref[0] * y


def _merge_call(outs, u, wb, wo, x, g, mod, tpb, n_batch):
    n, d = x.shape
    kind = lambda i: (jnp.where(i % tpb == 0, n_batch, i // tpb), 0, 0)
    br = pl.BlockSpec((TILE, BW), lambda i: (i, 0))
    wg = N_BRANCH * d
    return pl.pallas_call(
        _merge_kernel,
        grid=(n // TILE,),
        in_specs=[br, br, br, br,
                  pl.BlockSpec((TILE, wg), lambda i: (i, C_GATE // wg)),
                  pl.BlockSpec((N_BRANCH, BW, d), lambda i: (0, 0, 0)),
                  pl.BlockSpec((d, d), lambda i: (0, 0)),
                  pl.BlockSpec((TILE, d), lambda i: (i, 0)),
                  pl.BlockSpec((1, d), lambda i: (0, 0)),
                  pl.BlockSpec((1, 1, d), kind)],
        out_specs=pl.BlockSpec((TILE, d), lambda i: (i, 0)),
        out_shape=jax.ShapeDtypeStruct((n, d), F32),
        compiler_params=_cparams(("arbitrary",)),
        name="merge",
    )(*outs, u, wb, wo, x, g.reshape(1, d), mod)


def _router_kernel(t_ref, w_ref, b_ref, o_ref):
    tm = t_ref.shape[0]
    gsz = N_EXPERTS // N_GROUPS
    logits = _dot3(w_ref[...], t_ref[...], _NT)
    sc = _sigmoid(logits).reshape(N_GROUPS, gsz, tm)
    bi = sc + b_ref[...].reshape(N_GROUPS, gsz, 1)
    shape = (N_GROUPS, gsz, tm)
    g_io = lax.broadcasted_iota(jnp.int32, shape, 0)
    j_io = lax.broadcasted_iota(jnp.int32, shape, 1)
    e_io = g_io * gsz + j_io
    ninf = -jnp.inf
    m1 = jnp.max(bi, axis=1, keepdims=True)
    i1 = jnp.min(jnp.where(bi == m1, j_io, gsz), axis=1, keepdims=True)
    m2 = jnp.max(jnp.where(j_io == i1, ninf, bi), axis=1, keepdims=True)
    cur = jnp.broadcast_to(m1 + m2, shape)
    gsel = jnp.zeros(shape, F32)
    for _ in range(TOPK_GROUPS):
        mx = jnp.max(cur, axis=0, keepdims=True)
        ix = jnp.min(jnp.where(cur == mx, g_io, N_GROUPS), axis=0, keepdims=True)
        hit = g_io == ix
        gsel = jnp.where(hit, 1.0, gsel)
        cur = jnp.where(hit, ninf, cur)
    cur = jnp.where(gsel > 0.0, bi, ninf)
    esel = jnp.zeros(shape, F32)
    for _ in range(TOP_K):
        mx = jnp.max(jnp.max(cur, axis=0, keepdims=True), axis=1, keepdims=True)
        ix = jnp.min(jnp.min(jnp.where(cur == mx, e_io, N_EXPERTS), axis=0, keepdims=True),
                     axis=1, keepdims=True)
        hit = e_io == ix
        esel = jnp.where(hit, 1.0, esel)
        cur = jnp.where(hit, ninf, cur)
    wsel = sc * esel
    den = jnp.sum(jnp.sum(wsel, axis=0, keepdims=True), axis=1, keepdims=True)
    o_ref[...] = (wsel / den * ROUTE_SCALE).reshape(N_EXPERTS, tm)


def _router_call(tokens, rw_t, rb):
    n, d = tokens.shape
    return pl.pallas_call(
        _router_kernel,
        grid=(n // TILE,),
        in_specs=[pl.BlockSpec((TILE, d), lambda i: (i, 0)),
                  pl.BlockSpec((N_EXPERTS, d), lambda i: (0, 0)),
                  pl.BlockSpec((N_EXPERTS, 1), lambda i: (0, 0))],
        out_specs=pl.BlockSpec((N_EXPERTS, TILE), lambda i: (0, i)),
        out_shape=jax.ShapeDtypeStruct((N_EXPERTS, n), F32),
        compiler_params=_cparams(("arbitrary",)),
        name="router",
    )(tokens, rw_t, rb.reshape(N_EXPERTS, 1))


def _moe_kernel(x_ref, g_ref, wgu_ref, wd_ref, sgu_ref, sd_ref, o_ref, acc_ref):
    e = pl.program_id(1)

    def ffn(wgu, wd):
        hgu = _dot(x_ref[...], wgu)
        return _dot(_silu(hgu[:, :D_EXPERT]) * hgu[:, D_EXPERT:], wd)

    @pl.when(e == 0)
    def _():
        acc_ref[...] = ffn(sgu_ref[...], sd_ref[...])

    gates = g_ref[...]
    lane = lax.broadcasted_iota(jnp.int32, gates.shape, 1)
    gcol = jnp.sum(jnp.where(lane == e, gates, 0.0), axis=1, keepdims=True)
    acc_ref[...] += ffn(wgu_ref[0], wd_ref[0]) * gcol

    @pl.when(e == N_EXPERTS - 1)
    def _():
        o_ref[...] = acc_ref[...]


def _moe_call(tok, gates, wgu, wd, sgu, sd, tm):
    n, d = tok.shape
    return pl.pallas_call(
        _moe_kernel,
        grid=(n // tm, N_EXPERTS),
        in_specs=[pl.BlockSpec((tm, d), lambda i, e: (i, 0)),
                  pl.BlockSpec((tm, N_EXPERTS), lambda i, e: (i, 0)),
                  pl.BlockSpec((1, d, 2 * D_EXPERT), lambda i, e: (e, 0, 0)),
                  pl.BlockSpec((1, D_EXPERT, d), lambda i, e: (e, 0, 0)),
                  pl.BlockSpec((d, 2 * D_EXPERT), lambda i, e: (0, 0)),
                  pl.BlockSpec((D_EXPERT, d), lambda i, e: (0, 0))],
        out_specs=pl.BlockSpec((tm, d), lambda i, e: (i, 0)),
        out_shape=jax.ShapeDtypeStruct((n, d), F32),
        scratch_shapes=[pltpu.VMEM((tm, d), F32)],
        compiler_params=_cparams(("arbitrary", "arbitrary")),
        name="moe_experts",
    )(tok, gates, wgu, wd, sgu, sd)


def _resid_kernel(x_ref, f_ref, g_ref, mod_ref, o_ref):
    f = f_ref[...]
    y = f * lax.rsqrt(jnp.mean(f * f, axis=-1, keepdims=True) + EPS) * g_ref[...]
    o_ref[...] = x_ref[...] + mod_ref[0] * y


def _resid_call(x, f, g, mod, tpb, n_batch):
    n, d = x.shape
    kind = lambda i: (jnp.where(i % tpb == 0, n_batch, i // tpb), 0, 0)
    row = pl.BlockSpec((TILE, d), lambda i: (i, 0))
    return pl.pallas_call(
        _resid_kernel,
        grid=(n // TILE,),
        in_specs=[row, row, pl.BlockSpec((1, d), lambda i: (0, 0)), pl.BlockSpec((1, 1, d), kind)],
        out_specs=row,
        out_shape=jax.ShapeDtypeStruct((n, d), F32),
        compiler_params=_cparams(("arbitrary",)),
        name="moe_residual",
    )(x, f, g.reshape(1, d), mod)


def _partner(dim, nblocks):
    q = dim // 4
    base = np.concatenate([np.arange(q) + q, np.arange(q), np.arange(q) + 3 * q, np.arange(q) + 2 * q])
    return (np.arange(nblocks)[:, None] * dim + base[None, :]).reshape(-1)


def _proj_columns():
    da, cv, sw = 0, 3 * BW, 5 * BW
    rw = sw + (SW_HEADS + 2 * SW_KV) * SW_DH
    gate = rw + RW_SHIFT_W + RW_G_RANK
    rep = np.repeat(np.arange(SW_KV), SW_HEADS // SW_KV)
    kv_rep = (rep[:, None] * SW_DH + np.arange(SW_DH)[None, :]).reshape(-1)
    da_q = da + np.arange(BW)
    da_k = da + BW + np.arange(BW)
    da_v = da + 2 * BW + np.arange(BW)
    sw_q = sw + np.arange(BW)
    sw_k = sw + BW + kv_rep
    sw_v = sw + BW + SW_KV * SW_DH + kv_rep
    p32 = _partner(DA_HALF, BW // DA_HALF)
    p64 = _partner(SW_DH, BW // SW_DH)
    cols = np.concatenate([
        da_q, da_k, sw_q, sw_k,
        da_q[p32], da_k[p32], sw_q[p64], sw_k[p64],
        da_v, sw_v,
        cv + np.arange(2 * BW),
        rw + np.arange(RW_SHIFT_W + RW_G_RANK),
        gate + np.arange(N_BRANCH * 4 * BW)])
    assert cols.shape[0] == C_TOTAL
    return cols


def _rope_tables(seq):
    rows = seq // GRID_W
    row = jnp.repeat(jnp.arange(rows, dtype=F32), GRID_W)
    colp = jnp.tile(jnp.arange(GRID_W, dtype=F32), rows)

    def tables(dim, reps):
        q = dim // 4
        freqs = ROPE_BASE ** (-jnp.arange(q, dtype=F32) / q)
        ar, ac = row[:, None] * freqs, colp[:, None] * freqs
        cos = jnp.concatenate([jnp.cos(ar), jnp.cos(ar), jnp.cos(ac), jnp.cos(ac)], axis=1)
        sin = jnp.concatenate([-jnp.sin(ar), jnp.sin(ar), -jnp.sin(ac), jnp.sin(ac)], axis=1)
        return jnp.tile(cos, (1, reps)), jnp.tile(sin, (1, reps))

    c32, s32 = tables(DA_HALF, 2 * BW // DA_HALF)
    c64, s64 = tables(SW_DH, 2 * BW // SW_DH)
    cos = jnp.concatenate([c32, c64], axis=1)
    sin = jnp.concatenate([s32, s64], axis=1)
    w = cos.shape[1]
    cos = jnp.concatenate([jnp.ones((TILE, w), F32), cos], axis=0)
    sin = jnp.concatenate([jnp.zeros((TILE, w), F32), sin], axis=0)
    return cos, sin


def _moe_tile(rows_b):
    best = TILE
    for t in range(TILE, 1200, 8):
        if rows_b % t == 0:
            best = t
    return best


def kernel(x, c, ctx, c_ctx, ada_w, ada_b, norm_g, w_in, w_branch, w_out, da_lambda, da_subln,
           cv_w, cv_b, cv_ln, sw_sink, rw_mu, rw_w0, rw_w_up, rw_a0, rw_a_up, rw_kk, rw_ka,
           rw_g_up, rw_rk, rw_gn, router_w, router_b, ex_w_gu, ex_w_down, sh_w_gu, sh_w_down):
    n_batch, seq, d = x.shape
    ctx_len = ctx.shape[1]
    depth = w_in.shape[0]
    assert ctx_len == TILE and seq % TILE == 0 and seq % GRID_W == 0
    assert n_batch + 1 <= 8
    rows_b = ctx_len + seq
    tpb = rows_b // TILE
    n = n_batch * rows_b

    xs = jnp.concatenate([ctx, x], axis=1).reshape(n, d)
    cond = jnp.zeros((8, d), F32).at[:n_batch].set(c).at[n_batch].set(c_ctx)
    mods = _ada_call(cond, ada_w, ada_b)[:, :n_batch + 1]
    cos_t, sin_t = _rope_tables(seq)
    cols = _proj_columns()
    hio = np.arange(BW) // RW_DH
    bd = jnp.asarray(hio[:, None] == hio[None, :], BF16)

    for l in range(depth):
        mod = [mods[l, :, i * d:(i + 1) * d].reshape(n_batch + 1, 1, d) for i in range(6)]
        ng = norm_g[l]
        lam_init = 0.8 - 0.6 * math.exp(-0.3 * l)
        lv = da_lambda[l]
        lam = (jnp.exp(jnp.sum(lv[0] * lv[1])) - jnp.exp(jnp.sum(lv[2] * lv[3])) + lam_init)
        lp = {'rw_mu': rw_mu[l], 'rw_w0': rw_w0[l], 'rw_w_up': rw_w_up[l], 'rw_a0': rw_a0[l],
              'rw_a_up': rw_a_up[l], 'rw_kk': rw_kk[l], 'rw_ka': rw_ka[l], 'rw_rk': rw_rk[l]}

        (h,) = _norm_mod_call(xs, ng[0], mod[0], mod[1], tpb, n_batch, False)
        u = _proj_call(h, w_in[l][:, cols].astype(BF16))
        qk, vt = _rope_call(u, cos_t, sin_t, tpb)
        oa = _da_call(lam.reshape(1), qk, vt, jnp.tile(da_subln[l], DA_HEADS).reshape(1, BW),
                      n_batch, tpb, lam_init)
        ob = _conv_call(u, cv_w[l], cv_b[l], cv_ln[l], tpb)
        oc = _sw_call(sw_sink[l], qk, u, n_batch, tpb)
        cm, cn, cq, cy0, bonus = _rw_chunk_call(u, lp, bd, tpb)
        yscan = _rw_scan_call(cm, cn, cq, cy0, n_batch, tpb)
        od = _rw_out_call(yscan, bonus, u, rw_g_up[l].astype(BF16), rw_gn[l], bd)
        xs = _merge_call((oa, ob, oc, od), u, w_branch[l].astype(BF16), w_out[l].astype(BF16),
                         xs, ng[1], mod[2], tpb, n_batch)

        tok, tok32 = _norm_mod_call(xs, ng[2], mod[3], mod[4], tpb, n_batch, True)
        gates = _router_call(tok32, router_w[l].T, router_b[l]).T
        f = _moe_call(tok, gates, ex_w_gu[l], ex_w_down[l], sh_w_gu[l].astype(BF16),
                      sh_w_down[l].astype(BF16), _moe_tile(rows_b))
        xs = _resid_call(xs, f, ng[3], mod[5], tpb, n_batch)

    return xs.reshape(n_batch, rows_b, d)[:, ctx_len:]
```

```python
import functools
import math

import numpy as np
import jax
import jax.numpy as jnp
from jax import lax
from jax.experimental import pallas as pl
from jax.experimental.pallas import tpu as pltpu

F32 = jnp.float32
BF16 = jnp.bfloat16

GRID_W = 64
EPS = 1e-6
ROPE_BASE = 10000.0
N_BRANCH = 4
BW = 256
DA_HEADS = 4
DA_HALF = 32
SW_HEADS = 4
SW_KV = 2
SW_DH = 64
WINDOW = 128
CONV_W = 31
RW_HEADS = 4
RW_DH = 64
RW_DECAY_RANK = 64
RW_A_RANK = 64
RW_G_RANK = 128
RW_DECAY_SCALE = math.exp(-0.5)
RW_GN_EPS = 64e-5
RW_SHIFT_W = 3 * BW + RW_DECAY_RANK + RW_A_RANK
N_EXPERTS = 64
TOP_K = 6
N_GROUPS = 8
TOPK_GROUPS = 4
D_EXPERT = 256
ROUTE_SCALE = 2.5

TILE = 256
CHUNK = 128
DA_KT = 1408
DA_VROWS = 80
NEG = -1e30

C_ROPE = 0
C_SWAP = 1024
C_DAV = 2048
C_SWV = 2304
C_CV = 2560
C_RW = 3072
C_GATE = 4096
C_TOTAL = 8192

VMEM_LIMIT = 48 * 1024 * 1024

_NT = (((1,), (1,)), ((), ()))
_NN = (((1,), (0,)), ((), ()))


def _cparams(sem):
    return pltpu.CompilerParams(dimension_semantics=sem, vmem_limit_bytes=VMEM_LIMIT)


def _dot(a, b, dims=_NN):
    return lax.dot_general(a.astype(BF16), b.astype(BF16), dims, preferred_element_type=F32)


def _split2(x):
    hi = x.astype(BF16)
    lo = (x - hi.astype(F32)).astype(BF16)
    return hi, lo


def _dot3(a, b, dims=_NN):
    ah, al = _split2(a)
    bh, bl = _split2(b)
    dg = lambda x, y: lax.dot_general(x, y, dims, preferred_element_type=F32)
    return dg(ah, bh) + (dg(ah, bl) + dg(al, bh))


def _dot_sel(x, sel, dims=_NN):
    h0 = x.astype(BF16)
    r1 = x - h0.astype(F32)
    h1 = r1.astype(BF16)
    h2 = (r1 - h1.astype(F32)).astype(BF16)
    dg = lambda y: lax.dot_general(y, sel, dims, preferred_element_type=F32)
    return dg(h0) + (dg(h1) + dg(h2))


def _sel_dot(sel, x):
    h0 = x.astype(BF16)
    r1 = x - h0.astype(F32)
    h1 = r1.astype(BF16)
    h2 = (r1 - h1.astype(F32)).astype(BF16)
    dg = lambda y: lax.dot_general(sel, y, _NN, preferred_element_type=F32)
    return dg(h0) + (dg(h1) + dg(h2))


def _sigmoid(x):
    return jax.nn.sigmoid(x)


def _silu(x):
    return x * jax.nn.sigmoid(x)


def _head_of_lane(shape, width):
    return lax.broadcasted_iota(jnp.int32, shape, len(shape) - 1) // width


def _ada_kernel(s_ref, w_ref, b_ref, o_ref):
    s = _silu(s_ref[...])
    o_ref[0] = _dot(s, w_ref[0]) + b_ref[0]


def _ada_call(cond, ada_w, ada_b):
    depth, d, cols = ada_w.shape
    tn = 1536
    return pl.pallas_call(
        _ada_kernel,
        grid=(depth, cols // tn),
        in_specs=[pl.BlockSpec((8, d), lambda l, j: (0, 0)),
                  pl.BlockSpec((1, d, tn), lambda l, j: (l, 0, j)),
                  pl.BlockSpec((1, 1, tn), lambda l, j: (l, 0, j))],
        out_specs=pl.BlockSpec((1, 8, tn), lambda l, j: (l, 0, j)),
        out_shape=jax.ShapeDtypeStruct((depth, 8, cols), F32),
        compiler_params=_cparams(("arbitrary", "arbitrary")),
        name="ada_mod",
    )(cond, ada_w, ada_b.reshape(depth, 1, cols))


def _norm_mod_kernel(x_ref, g_ref, sh_ref, sc_ref, o_ref, *o32):
    x = x_ref[...]
    y = x * lax.rsqrt(jnp.mean(x * x, axis=-1, keepdims=True) + EPS) * g_ref[...]
    h = y * (1.0 + sc_ref[0]) + sh_ref[0]
    o_ref[...] = h.astype(BF16)
    if o32:
        o32[0][...] = h


def _norm_mod_call(x, g, shift, scale, tpb, n_batch, want_f32):
    n, d = x.shape
    kind = lambda i: (jnp.where(i % tpb == 0, n_batch, i // tpb), 0, 0)
    row = pl.BlockSpec((TILE, d), lambda i: (i, 0))
    out_shape = [jax.ShapeDtypeStruct((n, d), BF16)]
    out_specs = [row]
    if want_f32:
        out_shape.append(jax.ShapeDtypeStruct((n, d), F32))
        out_specs.append(row)
    return pl.pallas_call(
        _norm_mod_kernel,
        grid=(n // TILE,),
        in_specs=[row, pl.BlockSpec((1, d), lambda i: (0, 0)),
                  pl.BlockSpec((1, 1, d), kind), pl.BlockSpec((1, 1, d), kind)],
        out_specs=out_specs,
        out_shape=out_shape,
        compiler_params=_cparams(("arbitrary",)),
        name="norm_mod",
    )(x, g.reshape(1, d), shift, scale)


def _mm_kernel(a_ref, w_ref, o_ref):
    o_ref[...] = jnp.dot(a_ref[...], w_ref[...], preferred_element_type=F32)


def _proj_call(h, w):
    n, d = h.shape
    cols = w.shape[1]
    tm, tn = 2 * TILE, 2048
    return pl.pallas_call(
        _mm_kernel,
        grid=(cols // tn, n // tm),
        in_specs=[pl.BlockSpec((tm, d), lambda j, i: (i, 0)),
                  pl.BlockSpec((d, tn), lambda j, i: (0, j))],
        out_specs=pl.BlockSpec((tm, tn), lambda j, i: (i, j)),
        out_shape=jax.ShapeDtypeStruct((n, cols), F32),
        compiler_params=_cparams(("arbitrary", "arbitrary")),
        name="in_proj",
    )(h, w)


def _rope_kernel(um_ref, us_ref, c_ref, s_ref, v_ref, o_ref, vt_ref):
    o_ref[...] = (um_ref[...] * c_ref[...] + us_ref[...] * s_ref[...]).astype(BF16)
    vt = v_ref[...].T.astype(BF16)
    hd = 2 * DA_HALF
    for h in range(DA_HEADS):
        vt_ref[h * DA_VROWS:h * DA_VROWS + hd, :] = vt[h * hd:(h + 1) * hd, :]
        vt_ref[h * DA_VROWS + hd:(h + 1) * DA_VROWS, :] = jnp.ones((DA_VROWS - hd, TILE), BF16)


def _rope_call(u, cos_t, sin_t, tpb):
    n = u.shape[0]
    w = cos_t.shape[1]
    tab = pl.BlockSpec((TILE, w), lambda i: (i % tpb, 0))
    return pl.pallas_call(
        _rope_kernel,
        grid=(n // TILE,),
        in_specs=[pl.BlockSpec((TILE, w), lambda i: (i, C_ROPE // w)),
                  pl.BlockSpec((TILE, w), lambda i: (i, C_SWAP // w)), tab, tab,
                  pl.BlockSpec((TILE, BW), lambda i: (i, C_DAV // BW))],
        out_specs=[pl.BlockSpec((TILE, w), lambda i: (i, 0)),
                   pl.BlockSpec((DA_HEADS * DA_VROWS, TILE), lambda i: (i // tpb, i % tpb))],
        out_shape=[jax.ShapeDtypeStruct((n, w), BF16),
                   jax.ShapeDtypeStruct((n // (tpb * TILE) * DA_HEADS * DA_VROWS, tpb * TILE),
                                        BF16)],
        compiler_params=_cparams(("arbitrary",)),
        name="rope",
    )(u, u, cos_t, sin_t, u)


def _da_kernel(lam_ref, q_ref, k_ref, vt_ref, g_ref, o_ref,
               qs_ref, m_ref, acc_ref, *, nkt, lam_init):
    i = pl.program_id(1)
    tq = q_ref.shape[0]
    hd = 2 * DA_HALF
    q = q_ref[...]
    qmap = _head_of_lane((tq, BW), DA_HALF)
    for g in range(2 * DA_HEADS):
        qs_ref[g] = jnp.where(qmap == g, q, jnp.zeros_like(q))
    m_ref[...] = jnp.full(m_ref.shape, NEG, F32)
    acc_ref[...] = jnp.zeros(acc_ref.shape, F32)

    def tile(off, size):
        kt = k_ref[pl.ds(off, size), :]
        groups = range(2 * DA_HEADS)
        ss = [lax.dot_general(kt, qs_ref[g], _NT, preferred_element_type=F32) for g in groups]
        ps, alphas = [], []
        for g in groups:
            s = ss[g]
            m_old = m_ref[g]
            m_new = jnp.maximum(m_old, jnp.max(s, axis=0, keepdims=True))
            alphas.append(jnp.exp2(m_old - m_new)[0:1, :])
            ps.append(jnp.exp2(s - m_new[0:1, :]).astype(BF16))
            m_ref[g] = m_new
        for g in groups:
            h, mm = g // 2, g % 2
            rows = slice(h * DA_VROWS, (h + 1) * DA_VROWS)
            pv = jnp.dot(vt_ref[rows, pl.ds(off, size)], ps[g], preferred_element_type=F32)
            acc_ref[mm, rows, :] = acc_ref[mm, rows, :] * alphas[g] + pv

    @pl.when(i == 0)
    def _():
        tile(0, TILE)

    @pl.when(i > 0)
    def _():
        def body(j, carry):
            tile(pl.multiple_of(j * DA_KT, DA_KT), DA_KT)
            return carry

        lax.fori_loop(0, nkt, body, 0)

    lam = lam_ref[0]
    parts = []
    for h in range(DA_HEADS):
        rows = slice(h * DA_VROWS, h * DA_VROWS + hd)
        den = slice(h * DA_VROWS + hd, h * DA_VROWS + hd + 1)
        o_h = (acc_ref[0, rows, :] / acc_ref[0, den, :]
               - lam * (acc_ref[1, rows, :] / acc_ref[1, den, :]))
        ms = jnp.mean(o_h * o_h, axis=0, keepdims=True)
        parts.append(o_h * lax.rsqrt(ms + EPS))
    y = jnp.concatenate(parts, axis=0).T * g_ref[...]
    o_ref[...] = y * (1.0 - lam_init)


def _da_call(lam, qk, vt, subln, n_batch, tpb, lam_init):
    n = qk.shape[0]
    rows_b = tpb * TILE
    assert rows_b % DA_KT == 0
    vrows = DA_HEADS * DA_VROWS
    kern = functools.partial(_da_kernel, nkt=rows_b // DA_KT, lam_init=lam_init)
    return pl.pallas_call(
        kern,
        grid=(n_batch, tpb),
        in_specs=[pl.BlockSpec(memory_space=pltpu.SMEM),
                  pl.BlockSpec((TILE, BW), lambda b, i: (b * tpb + i, 0)),
                  pl.BlockSpec((rows_b, BW), lambda b, i: (b, 1)),
                  pl.BlockSpec((vrows, rows_b), lambda b, i: (b, 0)),
                  pl.BlockSpec((1, BW), lambda b, i: (0, 0))],
        out_specs=pl.BlockSpec((TILE, BW), lambda b, i: (b * tpb + i, 0)),
        out_shape=jax.ShapeDtypeStruct((n, BW), F32),
        scratch_shapes=[pltpu.VMEM((2 * DA_HEADS, TILE, BW), BF16),
                        pltpu.VMEM((2 * DA_HEADS, 8, TILE), F32),
                        pltpu.VMEM((2, vrows, TILE), F32)],
        compiler_params=_cparams(("arbitrary", "arbitrary")),
        name="diff_attn",
    )(lam, qk, qk, vt, subln)


def _sw_kernel(sink_ref, q_ref, kp_ref, ko_ref, kn_ref, kc_ref,
               vp_ref, vo_ref, vn_ref, vc_ref, o_ref, *, cpb):
    j = pl.program_id(1)
    tq = q_ref.shape[0]
    ctx_chunks = TILE // CHUNK
    q = q_ref[...]
    kk = jnp.concatenate([kp_ref[...], ko_ref[...], kn_ref[...], kc_ref[...]], axis=0)
    vv = jnp.concatenate([vp_ref[...], vo_ref[...], vn_ref[...], vc_ref[...]], axis=0).astype(BF16)
    nk = kk.shape[0]
    r = lax.broadcasted_iota(jnp.int32, (tq, nk), 0)
    c = lax.broadcasted_iota(jnp.int32, (tq, nk), 1)
    is_lat = j >= ctx_chunks
    far = 4 * nk
    lo_prev = jnp.where(is_lat & (j - 1 >= ctx_chunks), 0, far)
    hi_own = jnp.where(is_lat, 2 * CHUNK, 0)
    hi_next = jnp.where(is_lat & (j + 1 < cpb), 2 * CHUNK, -far)
    valid = ((c >= 3 * CHUNK)
             | ((c < CHUNK) & (c >= r + lo_prev))
             | ((c >= CHUNK) & (c < hi_own))
             | ((c >= 2 * CHUNK) & (c < 3 * CHUNK) & (c <= r + hi_next)))
    qhead = _head_of_lane((tq, BW), SW_DH)
    vhead = _head_of_lane((nk, BW), SW_DH)
    ps, vs = [], []
    for h in range(SW_HEADS):
        qm = jnp.where(qhead == h, q, jnp.zeros_like(q))
        s = lax.dot_general(qm, kk, _NT, preferred_element_type=F32) * (SW_DH ** -0.5)
        s = jnp.where(valid, s, NEG)
        sk = sink_ref[h]
        m = jnp.maximum(jnp.max(s, axis=1, keepdims=True), sk)
        p = jnp.exp(s - m)
        den = jnp.sum(p, axis=1, keepdims=True) + jnp.exp(sk - m)
        ps.append((p / den).astype(BF16))
        vs.append(jnp.where(vhead == h, vv, jnp.zeros_like(vv)))
    o_ref[...] = jnp.dot(jnp.concatenate(ps, axis=1), jnp.concatenate(vs, axis=0),
                         preferred_element_type=F32)


def _sw_call(sink, qk, u, n_batch, tpb):
    n = qk.shape[0]
    cpb = tpb * (TILE // CHUNK)
    cur = lambda b, j: b * cpb + j
    prv = lambda b, j: b * cpb + jnp.maximum(j - 1, 0)
    nxt = lambda b, j: b * cpb + jnp.minimum(j + 1, cpb - 1)
    kcol, vcol = 3, C_SWV // BW
    blk = lambda f, col: pl.BlockSpec((CHUNK, BW), lambda b, j: (f(b, j), col))
    ctx = lambda col: pl.BlockSpec((TILE, BW), lambda b, j: (b * tpb, col))
    return pl.pallas_call(
        functools.partial(_sw_kernel, cpb=cpb),
        grid=(n_batch, cpb),
        in_specs=[pl.BlockSpec(memory_space=pltpu.SMEM),
                  blk(cur, 2), blk(prv, kcol), blk(cur, kcol), blk(nxt, kcol), ctx(kcol),
                  blk(prv, vcol), blk(cur, vcol), blk(nxt, vcol), ctx(vcol)],
        out_specs=pl.BlockSpec((CHUNK, BW), lambda b, j: (cur(b, j), 0)),
        out_shape=jax.ShapeDtypeStruct((n, BW), F32),
        compiler_params=_cparams(("arbitrary", "arbitrary")),
        name="window_attn",
    )(sink, qk, qk, qk, qk, qk, u, u, u, u)


def _conv_kernel(prev_ref, cur_ref, next_ref, w_ref, b_ref, ln_ref, o_ref, z_ref, *, tpb):
    i = pl.program_id(0)
    pos = i % tpb
    has_prev = pos >= 2
    has_next = (pos >= 1) & (pos < tpb - 1)
    halo = prev_ref.shape[0]

    def glu(x):
        return x[:, :BW] * _sigmoid(x[:, BW:])

    zp = glu(prev_ref[...])
    zn = glu(next_ref[...])
    z_ref[0:halo, :] = jnp.where(has_prev, zp, 0.0)
    z_ref[halo:halo + TILE, :] = glu(cur_ref[...])
    z_ref[halo + TILE:2 * halo + TILE, :] = jnp.where(has_next, zn, 0.0)
    acc = jnp.zeros((TILE, BW), F32) + b_ref[...]
    pad = CONV_W // 2
    for t in range(CONV_W):
        acc = acc + z_ref[halo - pad + t:halo - pad + t + TILE, :] * w_ref[t:t + 1, :]
    mu = jnp.mean(acc, axis=-1, keepdims=True)
    xc = acc - mu
    y = xc * lax.rsqrt(jnp.mean(xc * xc, axis=-1, keepdims=True) + EPS)
    y = y * ln_ref[0:1, :] + ln_ref[1:2, :]
    o_ref[...] = _silu(y)


def _conv_call(u, w, b, ln, tpb):
    n = u.shape[0]
    halo = 16
    per = TILE // halo
    nh = n // halo
    wcv = 2 * BW
    col = C_CV // wcv
    return pl.pallas_call(
        functools.partial(_conv_kernel, tpb=tpb),
        grid=(n // TILE,),
        in_specs=[pl.BlockSpec((halo, wcv), lambda i: (jnp.maximum(i * per - 1, 0), col)),
                  pl.BlockSpec((TILE, wcv), lambda i: (i, col)),
                  pl.BlockSpec((halo, wcv), lambda i: (jnp.minimum((i + 1) * per, nh - 1), col)),
                  pl.BlockSpec((CONV_W, BW), lambda i: (0, 0)),
                  pl.BlockSpec((1, BW), lambda i: (0, 0)),
                  pl.BlockSpec((2, BW), lambda i: (0, 0))],
        out_specs=pl.BlockSpec((TILE, BW), lambda i: (i, 0)),
        out_shape=jax.ShapeDtypeStruct((n, BW), F32),
        scratch_shapes=[pltpu.VMEM((TILE + 2 * halo, BW), F32)],
        compiler_params=_cparams(("arbitrary",)),
        name="conformer_conv",
    )(u, u, u, w, b.reshape(1, BW), ln)


def _inv_unit_lower(ls, eye):
    ts = [eye + l for l in ls]
    lps = list(ls)
    step = 1
    while step < ls[0].shape[0] // 2:
        lps = [_dot(lp, lp) for lp in lps]
        ts = [t + _dot(t, lp) for t, lp in zip(ts, lps)]
        step *= 2
    return ts


def _rw_chunk_kernel(prev_ref, cur_ref, next_ref, mu_ref, w0_ref, a0_ref, kk_ref, ka_ref,
                     wup_ref, aup_ref, rk_ref, bd_ref,
                     m_ref, n_ref, q_ref, y0_ref, bonus_ref, z_ref, *, cpb):
    i = pl.program_id(0)
    jj = i % cpb
    ctx_chunks = TILE // CHUNK
    has_prev = (jj != 0) & (jj != ctx_chunks)
    has_next = (jj != ctx_chunks - 1) & (jj != cpb - 1)
    c = CHUNK
    zc = cur_ref[:, 0:RW_SHIFT_W]
    z_ref[0:8, :] = jnp.where(has_prev, prev_ref[:, 0:RW_SHIFT_W], 0.0)
    z_ref[8:8 + c, :] = zc
    z_ref[8 + c:16 + c, :] = jnp.where(has_next, next_ref[:, 0:RW_SHIFT_W], 0.0)
    bd = bd_ref[...]
    t_io = lax.broadcasted_iota(jnp.int32, (c, c), 0)
    s_io = lax.broadcasted_iota(jnp.int32, (c, c), 1)
    eye = jnp.where(t_io == s_io, 1.0, 0.0)
    lane_head = _head_of_lane((c, BW), RW_DH)
    row = lax.broadcasted_iota(jnp.int32, (BW, BW), 0)
    colm = lax.broadcasted_iota(jnp.int32, (BW, BW), 1)
    same_head = (row // RW_DH) == (colm // RW_DH)

    def stack(x):
        return jnp.concatenate(
            [jnp.where(lane_head == h, x, 0.0) for h in range(RW_HEADS)], axis=0)

    cat = lambda xs: jnp.concatenate(xs, axis=1)

    prep = []
    for d in range(2):
        zsh = z_ref[7:7 + c, :] if d == 0 else z_ref[9:9 + c, :]
        zs = zc + (zsh - zc) * mu_ref[d]
        r = zs[:, 0:BW]
        k = zs[:, BW:2 * BW]
        v = zs[:, 2 * BW:3 * BW]
        wa = zs[:, 3 * BW:RW_SHIFT_W]
        lw = -RW_DECAY_SCALE * _sigmoid(w0_ref[d] + _dot(jnp.tanh(wa), wup_ref[d]))
        a = _sigmoid(a0_ref[d] + _dot(wa, aup_ref[d]))
        kappa = k * kk_ref[d]
        kh = kappa * lax.rsqrt(jnp.maximum(_dot_sel(kappa * kappa, bd), 1e-12))
        kt = k * (1.0 + (a - 1.0) * ka_ref[d])
        akh = a * kh
        bonus_ref[d] = _dot_sel(r * kt * rk_ref[...], bd) * v

        ahead = (t_io - s_io) if d == 0 else (s_io - t_io)
        earlier = ahead > 0
        upto = ahead >= 0
        tri = jnp.where(upto, 1.0, 0.0).astype(BF16)
        cl = _sel_dot(tri, lw)
        tot = jnp.sum(lw, axis=0, keepdims=True)
        rho = 0.5 * tot
        cle = cl - lw
        a_true = -kh * jnp.exp(cle)
        r_true = r * jnp.exp(cl)
        a_c = -kh * jnp.exp(cle - rho)
        r_c = r * jnp.exp(cl - rho)
        b_c = akh * jnp.exp(rho - cl)
        k_c = kt * jnp.exp(rho - cl)
        b_end = akh * jnp.exp(tot - cl)
        k_end = kt * jnp.exp(tot - cl)

        pair = _dot(jnp.concatenate([stack(a_c), stack(r_c)], axis=0),
                    jnp.concatenate([b_c, k_c], axis=0), _NT)
        l_ab, l_ak, a_rb, a_rk = [], [], [], []
        for h in range(RW_HEADS):
            blk_a = pair[h * c:(h + 1) * c]
            blk_r = pair[(RW_HEADS + h) * c:(RW_HEADS + h + 1) * c]
            l_ab.append(jnp.where(earlier, blk_a[:, 0:c], 0.0))
            l_ak.append(jnp.where(earlier, blk_a[:, c:2 * c], 0.0))
            a_rb.append(jnp.where(upto, blk_r[:, 0:c], 0.0))
            a_rk.append(jnp.where(upto, blk_r[:, c:2 * c], 0.0))
        prep.append((l_ab, cat(l_ak), cat(a_rb), cat(a_rk), v, a_true, r_true, b_end, k_end, tot))

    t_all = _inv_unit_lower(prep[0][0] + prep[1][0], eye)

    for d in range(2):
        _, lak, arb, ark, v, a_true, r_true, b_end, k_end, tot = prep[d]
        t_inv = cat(t_all[d * RW_HEADS:(d + 1) * RW_HEADS])
        lv = _dot(lak, stack(v))
        w12 = _dot(t_inv, jnp.concatenate([stack(a_true), stack(lv)], axis=1))
        w1 = w12[:, 0:BW]
        w2 = w12[:, BW:2 * BW]
        q_ref[d, 0] = r_true + _dot(arb, stack(w1))
        y0_ref[d, 0] = _dot(arb, stack(w2)) + _dot(ark, stack(v))
        b_end_t = b_end.T
        decay = jnp.where(row == colm, jnp.broadcast_to(jnp.exp(tot), (BW, BW)), 0.0)
        m_ref[d, 0] = jnp.where(same_head, _dot(b_end_t, w1), 0.0) + decay
        n_ref[d, 0] = jnp.where(same_head, _dot(b_end_t, w2) + _dot(k_end.T, v), 0.0)


def _rw_chunk_call(u, lp, bd, tpb):
    n = u.shape[0]
    cpb = tpb * (TILE // CHUNK)
    nch = n // CHUNK
    per = CHUNK // 8
    n8 = n // 8
    wrw = 1024
    col = C_RW // wrw
    pvec = lambda w: pl.BlockSpec((2, 1, w), lambda i: (0, 0, 0))
    pad = jnp.zeros((2, RW_DECAY_RANK, BW), F32)
    wup = jnp.concatenate([lp['rw_w_up'], pad], axis=1).astype(BF16)
    aup = jnp.concatenate([pad, lp['rw_a_up']], axis=1).astype(BF16)
    mat = lambda rows: pl.BlockSpec((2, 1, rows, BW), lambda i: (0, i, 0, 0))
    return pl.pallas_call(
        functools.partial(_rw_chunk_kernel, cpb=cpb),
        grid=(nch,),
        in_specs=[pl.BlockSpec((8, wrw), lambda i: (jnp.maximum(i * per - 1, 0), col)),
                  pl.BlockSpec((CHUNK, wrw), lambda i: (i, col)),
                  pl.BlockSpec((8, wrw), lambda i: (jnp.minimum((i + 1) * per, n8 - 1), col)),
                  pvec(RW_SHIFT_W), pvec(BW), pvec(BW), pvec(BW), pvec(BW),
                  pl.BlockSpec((2, 2 * RW_DECAY_RANK, BW), lambda i: (0, 0, 0)),
                  pl.BlockSpec((2, 2 * RW_A_RANK, BW), lambda i: (0, 0, 0)),
                  pl.BlockSpec((1, BW), lambda i: (0, 0)),
                  pl.BlockSpec((BW, BW), lambda i: (0, 0))],
        out_specs=[mat(BW), mat(BW), mat(CHUNK), mat(CHUNK),
                   pl.BlockSpec((2, CHUNK, BW), lambda i: (0, i, 0))],
        out_shape=[jax.ShapeDtypeStruct((2, nch, BW, BW), F32),
                   jax.ShapeDtypeStruct((2, nch, BW, BW), F32),
                   jax.ShapeDtypeStruct((2, nch, CHUNK, BW), F32),
                   jax.ShapeDtypeStruct((2, nch, CHUNK, BW), F32),
                   jax.ShapeDtypeStruct((2, n, BW), F32)],
        scratch_shapes=[pltpu.VMEM((CHUNK + 16, RW_SHIFT_W), F32)],
        compiler_params=_cparams(("arbitrary",)),
        name="rwkv_chunk",
    )(u, u, u, lp['rw_mu'].reshape(2, 1, RW_SHIFT_W), lp['rw_w0'].reshape(2, 1, BW),
      lp['rw_a0'].reshape(2, 1, BW), lp['rw_kk'].reshape(2, 1, BW), lp['rw_ka'].reshape(2, 1, BW),
      wup, aup, lp['rw_rk'].reshape(1, BW), bd)


def _rw_scan_kernel(*refs, n_batch):
    m_refs, n_refs, q_refs, y0_refs = refs[0:2], refs[2:4], refs[4:6], refs[6:8]
    y_refs, x_ref = refs[8:10], refs[10]

    @pl.when(pl.program_id(0) == 0)
    def _():
        x_ref[...] = jnp.zeros(x_ref.shape, F32)

    for d in range(2):
        for b in range(n_batch):
            x = x_ref[d, b]
            y_refs[d][b, 0] = _dot3(q_refs[d][0, b, 0], x) + y0_refs[d][0, b, 0]
            x_ref[d, b] = _dot3(m_refs[d][0, b, 0], x) + n_refs[d][0, b, 0]


def _rw_scan_call(m, nn, q, y0, n_batch, tpb):
    cpb = tpb * (TILE // CHUNK)
    ctx_chunks = TILE // CHUNK

    def chunk(d, i):
        rev = jnp.where(i < ctx_chunks, ctx_chunks - 1 - i, cpb + ctx_chunks - 1 - i)
        return i if d == 0 else rev

    def mat(rows, d):
        return pl.BlockSpec((1, n_batch, 1, rows, BW), lambda i: (d, 0, chunk(d, i), 0, 0))

    ins, specs = [], []
    for arr, rows in ((m, BW), (nn, BW), (q, CHUNK), (y0, CHUNK)):
        arr = arr.reshape(2, n_batch, cpb, rows, BW)
        for d in range(2):
            ins.append(arr)
            specs.append(mat(rows, d))
    yshape = jax.ShapeDtypeStruct((n_batch, cpb, CHUNK, BW), F32)
    ys = pl.pallas_call(
        functools.partial(_rw_scan_kernel, n_batch=n_batch),
        grid=(cpb,),
        in_specs=specs,
        out_specs=[pl.BlockSpec((n_batch, 1, CHUNK, BW), lambda i, d=d: (0, chunk(d, i), 0, 0))
                   for d in range(2)],
        out_shape=[yshape, yshape],
        scratch_shapes=[pltpu.VMEM((2, n_batch, BW, BW), F32)],
        compiler_params=_cparams(("arbitrary",)),
        name="rwkv_scan",
    )(*ins)
    return [y.reshape(n_batch * cpb * CHUNK, BW) for y in ys]


def _rw_out_kernel(yf_ref, yb_ref, bonus_ref, u_ref, gup_ref, gn_ref, bd_ref, o_ref):
    bd = bd_ref[...]
    y = yf_ref[...] + yb_ref[...]
    mean = _dot_sel(y, bd) * (1.0 / RW_DH)
    yc = y - mean
    var = _dot_sel(yc * yc, bd) * (1.0 / RW_DH)
    yn = yc * lax.rsqrt(var + RW_GN_EPS) * gn_ref[0:1, :] + gn_ref[1:2, :]
    yn = yn + (bonus_ref[0] + bonus_ref[1])
    gd = u_ref[:, RW_SHIFT_W:RW_SHIFT_W + RW_G_RANK]
    o_ref[...] = yn * _dot(_sigmoid(gd), gup_ref[...])


def _rw_out_call(ys, bonus, u, gup, gn, bd):
    n = u.shape[0]
    wrw = 1024
    both = pl.BlockSpec((2, TILE, BW), lambda i: (0, i, 0))
    one = pl.BlockSpec((TILE, BW), lambda i: (i, 0))
    return pl.pallas_call(
        _rw_out_kernel,
        grid=(n // TILE,),
        in_specs=[one, one, both, pl.BlockSpec((TILE, wrw), lambda i: (i, C_RW // wrw)),
                  pl.BlockSpec((RW_G_RANK, BW), lambda i: (0, 0)),
                  pl.BlockSpec((2, BW), lambda i: (0, 0)),
                  pl.BlockSpec((BW, BW), lambda i: (0, 0))],
        out_specs=pl.BlockSpec((TILE, BW), lambda i: (i, 0)),
        out_shape=jax.ShapeDtypeStruct((n, BW), F32),
        compiler_params=_cparams(("arbitrary",)),
        name="rwkv_readout",
    )(ys[0], ys[1], bonus, u, gup, gn, bd)


def _merge_kernel(oa_ref, ob_ref, oc_ref, od_ref, gl_ref, wb_ref, wo_ref, x_ref, g_ref, mod_ref,
                  o_ref):
    d = x_ref.shape[1]
    m = None
    for i, o in enumerate((oa_ref, ob_ref, oc_ref, od_ref)):
        t = _sigmoid(gl_ref[:, i * d:(i + 1) * d]) * _dot(o[...], wb_ref[i])
        m = t if m is None else m + t
    y = _dot(m, wo_ref[...])
    y = y * lax.rsqrt(jnp.mean(y * y, axis=-1, keepdims=True) + EPS) * g_ref[...]
    o_ref[...] = x_ref[...] + mod_ref[0] * y


def _merge_call(outs, u, wb, wo, x, g, mod, tpb, n_batch):
    n, d = x.shape
    kind = lambda i: (jnp.where(i % tpb == 0, n_batch, i // tpb), 0, 0)
    br = pl.BlockSpec((TILE, BW), lambda i: (i, 0))
    wg = N_BRANCH * d
    return pl.pallas_call(
        _merge_kernel,
        grid=(n // TILE,),
        in_specs=[br, br, br, br,
                  pl.BlockSpec((TILE, wg), lambda i: (i, C_GATE // wg)),
                  pl.BlockSpec((N_BRANCH, BW, d), lambda i: (0, 0, 0)),
                  pl.BlockSpec((d, d), lambda i: (0, 0)),
                  pl.BlockSpec((TILE, d), lambda i: (i, 0)),
                  pl.BlockSpec((1, d), lambda i: (0, 0)),
                  pl.BlockSpec((1, 1, d), kind)],
        out_specs=pl.BlockSpec((TILE, d), lambda i: (i, 0)),
        out_shape=jax.ShapeDtypeStruct((n, d), F32),
        compiler_params=_cparams(("arbitrary",)),
        name="merge",
    )(*outs, u, wb, wo, x, g.reshape(1, d), mod)


def _router_kernel(t_ref, w_ref, b_ref, o_ref):
    tm = t_ref.shape[0]
    gsz = N_EXPERTS // N_GROUPS
    logits = _dot3(w_ref[...], t_ref[...], _NT)
    sc = _sigmoid(logits).reshape(N_GROUPS, gsz, tm)
    bi = sc + b_ref[...].reshape(N_GROUPS, gsz, 1)
    shape = (N_GROUPS, gsz, tm)
    g_io = lax.broadcasted_iota(jnp.int32, shape, 0)
    j_io = lax.broadcasted_iota(jnp.int32, shape, 1)
    e_io = g_io * gsz + j_io
    ninf = -jnp.inf
    m1 = jnp.max(bi, axis=1, keepdims=True)
    i1 = jnp.min(jnp.where(bi == m1, j_io, gsz), axis=1, keepdims=True)
    m2 = jnp.max(jnp.where(j_io == i1, ninf, bi), axis=1, keepdims=True)
    cur = jnp.broadcast_to(m1 + m2, shape)
    gsel = jnp.zeros(shape, F32)
    for _ in range(TOPK_GROUPS):
        mx = jnp.max(cur, axis=0, keepdims=True)
        ix = jnp.min(jnp.where(cur == mx, g_io, N_GROUPS), axis=0, keepdims=True)
        hit = g_io == ix
        gsel = jnp.where(hit, 1.0, gsel)
        cur = jnp.where(hit, ninf, cur)
    cur = jnp.where(gsel > 0.0, bi, ninf)
    esel = jnp.zeros(shape, F32)
    for _ in range(TOP_K):
        mx = jnp.max(jnp.max(cur, axis=0, keepdims=True), axis=1, keepdims=True)
        ix = jnp.min(jnp.min(jnp.where(cur == mx, e_io, N_EXPERTS), axis=0, keepdims=True),
                     axis=1, keepdims=True)
        hit = e_io == ix
        esel = jnp.where(hit, 1.0, esel)
        cur = jnp.where(hit, ninf, cur)
    wsel = sc * esel
    den = jnp.sum(jnp.sum(wsel, axis=0, keepdims=True), axis=1, keepdims=True)
    o_ref[...] = (wsel / den * ROUTE_SCALE).reshape(N_EXPERTS, tm)


def _router_call(tokens, rw_t, rb):
    n, d = tokens.shape
    return pl.pallas_call(
        _router_kernel,
        grid=(n // TILE,),
        in_specs=[pl.BlockSpec((TILE, d), lambda i: (i, 0)),
                  pl.BlockSpec((N_EXPERTS, d), lambda i: (0, 0)),
                  pl.BlockSpec((N_EXPERTS, 1), lambda i: (0, 0))],
        out_specs=pl.BlockSpec((N_EXPERTS, TILE), lambda i: (0, i)),
        out_shape=jax.ShapeDtypeStruct((N_EXPERTS, n), F32),
        compiler_params=_cparams(("arbitrary",)),
        name="router",
    )(tokens, rw_t, rb.reshape(N_EXPERTS, 1))


def _moe_kernel(x_ref, g_ref, wgu_ref, wd_ref, sgu_ref, sd_ref, o_ref, acc_ref):
    e = pl.program_id(1)

    def ffn(wgu, wd):
        hgu = _dot(x_ref[...], wgu)
        return _dot(_silu(hgu[:, :D_EXPERT]) * hgu[:, D_EXPERT:], wd)

    @pl.when(e == 0)
    def _():
        acc_ref[...] = ffn(sgu_ref[...], sd_ref[...])

    gates = g_ref[...]
    lane = lax.broadcasted_iota(jnp.int32, gates.shape, 1)
    gcol = jnp.sum(jnp.where(lane == e, gates, 0.0), axis=1, keepdims=True)
    acc_ref[...] += ffn(wgu_ref[0, 0], wd_ref[0, 0]) * gcol

    @pl.when(e == N_EXPERTS - 1)
    def _():
        o_ref[...] = acc_ref[...]


def _moe_call(tok, gates, wgu, wd, layer, sgu, sd, tm):
    n, d = tok.shape
    return pl.pallas_call(
        _moe_kernel,
        grid=(n // tm, N_EXPERTS),
        in_specs=[pl.BlockSpec((tm, d), lambda i, e: (i, 0)),
                  pl.BlockSpec((tm, N_EXPERTS), lambda i, e: (i, 0)),
                  pl.BlockSpec((1, 1, d, 2 * D_EXPERT), lambda i, e: (layer, e, 0, 0)),
                  pl.BlockSpec((1, 1, D_EXPERT, d), lambda i, e: (layer, e, 0, 0)),
                  pl.BlockSpec((d, 2 * D_EXPERT), lambda i, e: (0, 0)),
                  pl.BlockSpec((D_EXPERT, d), lambda i, e: (0, 0))],
        out_specs=pl.BlockSpec((tm, d), lambda i, e: (i, 0)),
        out_shape=jax.ShapeDtypeStruct((n, d), F32),
        scratch_shapes=[pltpu.VMEM((tm, d), F32)],
        compiler_params=_cparams(("arbitrary", "arbitrary")),
        name="moe_experts",
    )(tok, gates, wgu, wd, sgu, sd)


def _resid_kernel(x_ref, f_ref, g_ref, mod_ref, o_ref):
    f = f_ref[...]
    y = f * lax.rsqrt(jnp.mean(f * f, axis=-1, keepdims=True) + EPS) * g_ref[...]
    o_ref[...] = x_ref[...] + mod_ref[0] * y


def _resid_call(x, f, g, mod, tpb, n_batch):
    n, d = x.shape
    kind = lambda i: (jnp.where(i % tpb == 0, n_batch, i // tpb), 0, 0)
    row = pl.BlockSpec((TILE, d), lambda i: (i, 0))
    return pl.pallas_call(
        _resid_kernel,
        grid=(n // TILE,),
        in_specs=[row, row, pl.BlockSpec((1, d), lambda i: (0, 0)), pl.BlockSpec((1, 1, d), kind)],
        out_specs=row,
        out_shape=jax.ShapeDtypeStruct((n, d), F32),
        compiler_params=_cparams(("arbitrary",)),
        name="moe_residual",
    )(x, f, g.reshape(1, d), mod)


def _partner(dim, nblocks):
    q = dim // 4
    base = np.concatenate([np.arange(q) + q, np.arange(q), np.arange(q) + 3 * q, np.arange(q) + 2 * q])
    return (np.arange(nblocks)[:, None] * dim + base[None, :]).reshape(-1)


def _proj_columns():
    da, cv, sw = 0, 3 * BW, 5 * BW
    rw = sw + (SW_HEADS + 2 * SW_KV) * SW_DH
    gate = rw + RW_SHIFT_W + RW_G_RANK
    rep = np.repeat(np.arange(SW_KV), SW_HEADS // SW_KV)
    kv_rep = (rep[:, None] * SW_DH + np.arange(SW_DH)[None, :]).reshape(-1)
    da_q = da + np.arange(BW)
    da_k = da + BW + np.arange(BW)
    da_v = da + 2 * BW + np.arange(BW)
    sw_q = sw + np.arange(BW)
    sw_k = sw + BW + kv_rep
    sw_v = sw + BW + SW_KV * SW_DH + kv_rep
    p32 = _partner(DA_HALF, BW // DA_HALF)
    p64 = _partner(SW_DH, BW // SW_DH)
    cols = np.concatenate([
        da_q, da_k, sw_q, sw_k,
        da_q[p32], da_k[p32], sw_q[p64], sw_k[p64],
        da_v, sw_v,
        cv + np.arange(2 * BW),
        rw + np.arange(RW_SHIFT_W + RW_G_RANK),
        gate + np.arange(N_BRANCH * 4 * BW)])
    assert cols.shape[0] == C_TOTAL
    return cols


def _rope_tables(seq):
    rows = seq // GRID_W
    row = jnp.repeat(jnp.arange(rows, dtype=F32), GRID_W)
    colp = jnp.tile(jnp.arange(GRID_W, dtype=F32), rows)

    def tables(dim, reps):
        q = dim // 4
        freqs = ROPE_BASE ** (-jnp.arange(q, dtype=F32) / q)
        ar, ac = row[:, None] * freqs, colp[:, None] * freqs
        cos = jnp.concatenate([jnp.cos(ar), jnp.cos(ar), jnp.cos(ac), jnp.cos(ac)], axis=1)
        sin = jnp.concatenate([-jnp.sin(ar), jnp.sin(ar), -jnp.sin(ac), jnp.sin(ac)], axis=1)
        return jnp.tile(cos, (1, reps)), jnp.tile(sin, (1, reps))

    c32, s32 = tables(DA_HALF, 2 * BW // DA_HALF)
    c64, s64 = tables(SW_DH, 2 * BW // SW_DH)
    cos = jnp.concatenate([c32, c64], axis=1)
    sin = jnp.concatenate([s32, s64], axis=1)
    w = cos.shape[1]
    qscale = jnp.where(jnp.arange(w) < BW, (DA_HALF ** -0.5) * math.log2(math.e), 1.0)
    cos = jnp.concatenate([jnp.ones((TILE, w), F32), cos], axis=0) * qscale
    sin = jnp.concatenate([jnp.zeros((TILE, w), F32), sin], axis=0) * qscale
    return cos, sin


def _moe_tile(rows_b):
    best = TILE
    for t in range(TILE, 1200, 8):
        if rows_b % t == 0:
            best = t
    return best


def kernel(x, c, ctx, c_ctx, ada_w, ada_b, norm_g, w_in, w_branch, w_out, da_lambda, da_subln,
           cv_w, cv_b, cv_ln, sw_sink, rw_mu, rw_w0, rw_w_up, rw_a0, rw_a_up, rw_kk, rw_ka,
           rw_g_up, rw_rk, rw_gn, router_w, router_b, ex_w_gu, ex_w_down, sh_w_gu, sh_w_down):
    n_batch, seq, d = x.shape
    ctx_len = ctx.shape[1]
    depth = w_in.shape[0]
    assert ctx_len == TILE and seq % TILE == 0 and seq % GRID_W == 0
    assert n_batch + 1 <= 8
    rows_b = ctx_len + seq
    tpb = rows_b // TILE
    n = n_batch * rows_b

    xs = jnp.concatenate([ctx, x], axis=1).reshape(n, d)
    cond = jnp.zeros((8, d), F32).at[:n_batch].set(c).at[n_batch].set(c_ctx)
    mods = _ada_call(cond, ada_w, ada_b)[:, :n_batch + 1]
    cos_t, sin_t = _rope_tables(seq)
    cols = _proj_columns()
    hio = np.arange(BW) // RW_DH
    bd = jnp.asarray(hio[:, None] == hio[None, :], BF16)

    for l in range(depth):
        mod = [mods[l, :, i * d:(i + 1) * d].reshape(n_batch + 1, 1, d) for i in range(6)]
        ng = norm_g[l]
        lam_init = 0.8 - 0.6 * math.exp(-0.3 * l)
        lv = da_lambda[l]
        lam = (jnp.exp(jnp.sum(lv[0] * lv[1])) - jnp.exp(jnp.sum(lv[2] * lv[3])) + lam_init)
        lp = {'rw_mu': rw_mu[l], 'rw_w0': rw_w0[l], 'rw_w_up': rw_w_up[l], 'rw_a0': rw_a0[l],
              'rw_a_up': rw_a_up[l], 'rw_kk': rw_kk[l], 'rw_ka': rw_ka[l], 'rw_rk': rw_rk[l]}

        (h,) = _norm_mod_call(xs, ng[0], mod[0], mod[1], tpb, n_batch, False)
        u = _proj_call(h, w_in[l][:, cols].astype(BF16))
        qk, vt = _rope_call(u, cos_t, sin_t, tpb)
        oa = _da_call(lam.reshape(1), qk, vt, jnp.tile(da_subln[l], DA_HEADS).reshape(1, BW),
                      n_batch, tpb, lam_init)
        ob = _conv_call(u, cv_w[l], cv_b[l], cv_ln[l], tpb)
        oc = _sw_call(sw_sink[l], qk, u, n_batch, tpb)
        cm, cn, cq, cy0, bonus = _rw_chunk_call(u, lp, bd, tpb)
        yscan = _rw_scan_call(cm, cn, cq, cy0, n_batch, tpb)
        od = _rw_out_call(yscan, bonus, u, rw_g_up[l].astype(BF16), rw_gn[l], bd)
        xs = _merge_call((oa, ob, oc, od), u, w_branch[l].astype(BF16), w_out[l].astype(BF16),
                         xs, ng[1], mod[2], tpb, n_batch)

        tok, tok32 = _norm_mod_call(xs, ng[2], mod[3], mod[4], tpb, n_batch, True)
        gates = _router_call(tok32, router_w[l].T, router_b[l]).T
        f = _moe_call(tok, gates, ex_w_gu, ex_w_down, l, sh_w_gu[l].astype(BF16),
                      sh_w_down[l].astype(BF16), _moe_tile(rows_b))
        xs = _resid_call(xs, f, ng[3], mod[5], tpb, n_batch)

    return xs.reshape(n_batch, rows_b, d)[:, ctx_len:]
```

```python
import functools
import math

import numpy as np
import jax
import jax.numpy as jnp
from jax import lax
from jax.experimental import pallas as pl
from jax.experimental.pallas import tpu as pltpu

F32 = jnp.float32
BF16 = jnp.bfloat16

GRID_W = 64
EPS = 1e-6
ROPE_BASE = 10000.0
N_BRANCH = 4
BW = 256
DA_HEADS = 4
DA_HALF = 32
SW_HEADS = 4
SW_KV = 2
SW_DH = 64
WINDOW = 128
CONV_W = 31
RW_HEADS = 4
RW_DH = 64
RW_DECAY_RANK = 64
RW_A_RANK = 64
RW_G_RANK = 128
RW_DECAY_SCALE = math.exp(-0.5)
RW_GN_EPS = 64e-5
RW_SHIFT_W = 3 * BW + RW_DECAY_RANK + RW_A_RANK
N_EXPERTS = 64
TOP_K = 6
N_GROUPS = 8
TOPK_GROUPS = 4
D_EXPERT = 256
ROUTE_SCALE = 2.5

TILE = 256
CHUNK = 128
DA_KT = 1408
MOE_MAX_TILE = 2112
DA_VROWS = 80
NEG = -1e30

C_ROPE = 0
C_SWAP = 1024
C_DAV = 2048
C_SWV = 2304
C_CV = 2560
C_RW = 3072
C_GATE = 4096
C_TOTAL = 8192

VMEM_LIMIT = 48 * 1024 * 1024

_NT = (((1,), (1,)), ((), ()))
_NN = (((1,), (0,)), ((), ()))


def _cparams(sem):
    return pltpu.CompilerParams(dimension_semantics=sem, vmem_limit_bytes=VMEM_LIMIT)


def _dot(a, b, dims=_NN):
    return lax.dot_general(a.astype(BF16), b.astype(BF16), dims, preferred_element_type=F32)


def _split2(x):
    hi = x.astype(BF16)
    lo = (x - hi.astype(F32)).astype(BF16)
    return hi, lo


def _dot3(a, b, dims=_NN):
    ah, al = _split2(a)
    bh, bl = _split2(b)
    dg = lambda x, y: lax.dot_general(x, y, dims, preferred_element_type=F32)
    return dg(ah, bh) + (dg(ah, bl) + dg(al, bh))


def _dot_sel(x, sel, dims=_NN):
    h0 = x.astype(BF16)
    r1 = x - h0.astype(F32)
    h1 = r1.astype(BF16)
    h2 = (r1 - h1.astype(F32)).astype(BF16)
    dg = lambda y: lax.dot_general(y, sel, dims, preferred_element_type=F32)
    return dg(h0) + (dg(h1) + dg(h2))


def _sel_dot(sel, x):
    h0 = x.astype(BF16)
    r1 = x - h0.astype(F32)
    h1 = r1.astype(BF16)
    h2 = (r1 - h1.astype(F32)).astype(BF16)
    dg = lambda y: lax.dot_general(sel, y, _NN, preferred_element_type=F32)
    return dg(h0) + (dg(h1) + dg(h2))


def _sigmoid(x):
    return jax.nn.sigmoid(x)


def _silu(x):
    return x * jax.nn.sigmoid(x)


def _head_of_lane(shape, width):
    return lax.broadcasted_iota(jnp.int32, shape, len(shape) - 1) // width


def _ada_kernel(s_ref, w_ref, b_ref, o_ref):
    s = _silu(s_ref[...])
    o_ref[0] = _dot(s, w_ref[0]) + b_ref[0]


def _ada_call(cond, ada_w, ada_b):
    depth, d, cols = ada_w.shape
    tn = 1536
    return pl.pallas_call(
        _ada_kernel,
        grid=(depth, cols // tn),
        in_specs=[pl.BlockSpec((8, d), lambda l, j: (0, 0)),
                  pl.BlockSpec((1, d, tn), lambda l, j: (l, 0, j)),
                  pl.BlockSpec((1, 1, tn), lambda l, j: (l, 0, j))],
        out_specs=pl.BlockSpec((1, 8, tn), lambda l, j: (l, 0, j)),
        out_shape=jax.ShapeDtypeStruct((depth, 8, cols), F32),
        compiler_params=_cparams(("arbitrary", "arbitrary")),
        name="ada_mod",
    )(cond, ada_w, ada_b.reshape(depth, 1, cols))


def _norm_mod_kernel(x_ref, g_ref, sh_ref, sc_ref, o_ref, *o32):
    x = x_ref[...]
    y = x * lax.rsqrt(jnp.mean(x * x, axis=-1, keepdims=True) + EPS) * g_ref[...]
    h = y * (1.0 + sc_ref[0]) + sh_ref[0]
    o_ref[...] = h.astype(BF16)
    if o32:
        o32[0][...] = h


def _norm_mod_call(x, g, shift, scale, tpb, n_batch, want_f32):
    n, d = x.shape
    kind = lambda i: (jnp.where(i % tpb == 0, n_batch, i // tpb), 0, 0)
    row = pl.BlockSpec((TILE, d), lambda i: (i, 0))
    out_shape = [jax.ShapeDtypeStruct((n, d), BF16)]
    out_specs = [row]
    if want_f32:
        out_shape.append(jax.ShapeDtypeStruct((n, d), F32))
        out_specs.append(row)
    return pl.pallas_call(
        _norm_mod_kernel,
        grid=(n // TILE,),
        in_specs=[row, pl.BlockSpec((1, d), lambda i: (0, 0)),
                  pl.BlockSpec((1, 1, d), kind), pl.BlockSpec((1, 1, d), kind)],
        out_specs=out_specs,
        out_shape=out_shape,
        compiler_params=_cparams(("arbitrary",)),
        name="norm_mod",
    )(x, g.reshape(1, d), shift, scale)


def _mm_kernel(a_ref, w_ref, o_ref):
    o_ref[...] = jnp.dot(a_ref[...], w_ref[...], preferred_element_type=F32)


def _proj_call(h, w):
    n, d = h.shape
    cols = w.shape[1]
    tm, tn = 2 * TILE, 4096
    return pl.pallas_call(
        _mm_kernel,
        grid=(cols // tn, n // tm),
        in_specs=[pl.BlockSpec((tm, d), lambda j, i: (i, 0)),
                  pl.BlockSpec((d, tn), lambda j, i: (0, j))],
        out_specs=pl.BlockSpec((tm, tn), lambda j, i: (i, j)),
        out_shape=jax.ShapeDtypeStruct((n, cols), F32),
        compiler_params=_cparams(("arbitrary", "arbitrary")),
        name="in_proj",
    )(h, w)


def _rope_kernel(um_ref, us_ref, c_ref, s_ref, v_ref, o_ref, vt_ref):
    o_ref[...] = (um_ref[...] * c_ref[...] + us_ref[...] * s_ref[...]).astype(BF16)
    vt = v_ref[...].T.astype(BF16)
    hd = 2 * DA_HALF
    for h in range(DA_HEADS):
        vt_ref[h * DA_VROWS:h * DA_VROWS + hd, :] = vt[h * hd:(h + 1) * hd, :]
        vt_ref[h * DA_VROWS + hd:(h + 1) * DA_VROWS, :] = jnp.ones((DA_VROWS - hd, TILE), BF16)


def _rope_call(u, cos_t, sin_t, tpb):
    n = u.shape[0]
    w = cos_t.shape[1]
    tab = pl.BlockSpec((TILE, w), lambda i: (i % tpb, 0))
    return pl.pallas_call(
        _rope_kernel,
        grid=(n // TILE,),
        in_specs=[pl.BlockSpec((TILE, w), lambda i: (i, C_ROPE // w)),
                  pl.BlockSpec((TILE, w), lambda i: (i, C_SWAP // w)), tab, tab,
                  pl.BlockSpec((TILE, BW), lambda i: (i, C_DAV // BW))],
        out_specs=[pl.BlockSpec((TILE, w), lambda i: (i, 0)),
                   pl.BlockSpec((DA_HEADS * DA_VROWS, TILE), lambda i: (i // tpb, i % tpb))],
        out_shape=[jax.ShapeDtypeStruct((n, w), BF16),
                   jax.ShapeDtypeStruct((n // (tpb * TILE) * DA_HEADS * DA_VROWS, tpb * TILE),
                                        BF16)],
        compiler_params=_cparams(("arbitrary",)),
        name="rope",
    )(u, u, cos_t, sin_t, u)


def _da_kernel(lam_ref, q_ref, k_ref, vt_ref, g_ref, o_ref,
               qs_ref, m_ref, acc_ref, *, nkt, lam_init):
    i = pl.program_id(1)
    tq = q_ref.shape[0]
    hd = 2 * DA_HALF
    q = q_ref[...]
    qmap = _head_of_lane((tq, BW), DA_HALF)
    for g in range(2 * DA_HEADS):
        qs_ref[g] = jnp.where(qmap == g, q, jnp.zeros_like(q))
    m_ref[...] = jnp.full(m_ref.shape, NEG, F32)
    acc_ref[...] = jnp.zeros(acc_ref.shape, F32)

    def tile(off, size):
        kt = k_ref[pl.ds(off, size), :]
        groups = range(2 * DA_HEADS)
        ss = [lax.dot_general(kt, qs_ref[g], _NT, preferred_element_type=F32) for g in groups]
        for g in groups:
            s = ss[g]
            m_old = m_ref[g]
            m_new = jnp.maximum(m_old, jnp.max(s, axis=0, keepdims=True))
            alpha = jnp.exp2(m_old - m_new)[0:1, :]
            p = jnp.exp2(s - m_new[0:1, :]).astype(BF16)
            m_ref[g] = m_new
            h, mm = g // 2, g % 2
            rows = slice(h * DA_VROWS, (h + 1) * DA_VROWS)
            pv = jnp.dot(vt_ref[rows, pl.ds(off, size)], p, preferred_element_type=F32)
            acc_ref[mm, rows, :] = acc_ref[mm, rows, :] * alpha + pv

    @pl.when(i == 0)
    def _():
        tile(0, TILE)

    @pl.when(i > 0)
    def _():
        def body(j, carry):
            tile(pl.multiple_of(j * DA_KT, DA_KT), DA_KT)
            return carry

        lax.fori_loop(0, nkt, body, 0)

    lam = lam_ref[0]
    parts = []
    for h in range(DA_HEADS):
        rows = slice(h * DA_VROWS, h * DA_VROWS + hd)
        den = slice(h * DA_VROWS + hd, h * DA_VROWS + hd + 1)
        o_h = (acc_ref[0, rows, :] / acc_ref[0, den, :]
               - lam * (acc_ref[1, rows, :] / acc_ref[1, den, :]))
        ms = jnp.mean(o_h * o_h, axis=0, keepdims=True)
        parts.append(o_h * lax.rsqrt(ms + EPS))
    y = jnp.concatenate(parts, axis=0).T * g_ref[...]
    o_ref[...] = y * (1.0 - lam_init)


def _da_call(lam, qk, vt, subln, n_batch, tpb, lam_init):
    n = qk.shape[0]
    rows_b = tpb * TILE
    assert rows_b % DA_KT == 0
    vrows = DA_HEADS * DA_VROWS
    kern = functools.partial(_da_kernel, nkt=rows_b // DA_KT, lam_init=lam_init)
    return pl.pallas_call(
        kern,
        grid=(n_batch, tpb),
        in_specs=[pl.BlockSpec(memory_space=pltpu.SMEM),
                  pl.BlockSpec((TILE, BW), lambda b, i: (b * tpb + i, 0)),
                  pl.BlockSpec((rows_b, BW), lambda b, i: (b, 1)),
                  pl.BlockSpec((vrows, rows_b), lambda b, i: (b, 0)),
                  pl.BlockSpec((1, BW), lambda b, i: (0, 0))],
        out_specs=pl.BlockSpec((TILE, BW), lambda b, i: (b * tpb + i, 0)),
        out_shape=jax.ShapeDtypeStruct((n, BW), F32),
        scratch_shapes=[pltpu.VMEM((2 * DA_HEADS, TILE, BW), BF16),
                        pltpu.VMEM((2 * DA_HEADS, 8, TILE), F32),
                        pltpu.VMEM((2, vrows, TILE), F32)],
        compiler_params=_cparams(("arbitrary", "arbitrary")),
        name="diff_attn",
    )(lam, qk, qk, vt, subln)


def _sw_kernel(sink_ref, q_ref, kp_ref, ko_ref, kn_ref, kc_ref,
               vp_ref, vo_ref, vn_ref, vc_ref, o_ref, *, cpb):
    j = pl.program_id(1)
    tq = q_ref.shape[0]
    ctx_chunks = TILE // CHUNK
    q = q_ref[...]
    kk = jnp.concatenate([kp_ref[...], ko_ref[...], kn_ref[...], kc_ref[...]], axis=0)
    vv = jnp.concatenate([vp_ref[...], vo_ref[...], vn_ref[...], vc_ref[...]], axis=0).astype(BF16)
    nk = kk.shape[0]
    r = lax.broadcasted_iota(jnp.int32, (tq, nk), 0)
    c = lax.broadcasted_iota(jnp.int32, (tq, nk), 1)
    is_lat = j >= ctx_chunks
    far = 4 * nk
    lo_prev = jnp.where(is_lat & (j - 1 >= ctx_chunks), 0, far)
    hi_own = jnp.where(is_lat, 2 * CHUNK, 0)
    hi_next = jnp.where(is_lat & (j + 1 < cpb), 2 * CHUNK, -far)
    valid = ((c >= 3 * CHUNK)
             | ((c < CHUNK) & (c >= r + lo_prev))
             | ((c >= CHUNK) & (c < hi_own))
             | ((c >= 2 * CHUNK) & (c < 3 * CHUNK) & (c <= r + hi_next)))
    qhead = _head_of_lane((tq, BW), SW_DH)
    vhead = _head_of_lane((nk, BW), SW_DH)
    ps, vs = [], []
    for h in range(SW_HEADS):
        qm = jnp.where(qhead == h, q, jnp.zeros_like(q))
        s = lax.dot_general(qm, kk, _NT, preferred_element_type=F32) * (SW_DH ** -0.5)
        s = jnp.where(valid, s, NEG)
        sk = sink_ref[h]
        m = jnp.maximum(jnp.max(s, axis=1, keepdims=True), sk)
        p = jnp.exp(s - m)
        den = jnp.sum(p, axis=1, keepdims=True) + jnp.exp(sk - m)
        ps.append((p / den).astype(BF16))
        vs.append(jnp.where(vhead == h, vv, jnp.zeros_like(vv)))
    o_ref[...] = jnp.dot(jnp.concatenate(ps, axis=1), jnp.concatenate(vs, axis=0),
                         preferred_element_type=F32)


def _sw_call(sink, qk, u, n_batch, tpb):
    n = qk.shape[0]
    cpb = tpb * (TILE // CHUNK)
    cur = lambda b, j: b * cpb + j
    prv = lambda b, j: b * cpb + jnp.maximum(j - 1, 0)
    nxt = lambda b, j: b * cpb + jnp.minimum(j + 1, cpb - 1)
    kcol, vcol = 3, C_SWV // BW
    blk = lambda f, col: pl.BlockSpec((CHUNK, BW), lambda b, j: (f(b, j), col))
    ctx = lambda col: pl.BlockSpec((TILE, BW), lambda b, j: (b * tpb, col))
    return pl.pallas_call(
        functools.partial(_sw_kernel, cpb=cpb),
        grid=(n_batch, cpb),
        in_specs=[pl.BlockSpec(memory_space=pltpu.SMEM),
                  blk(cur, 2), blk(prv, kcol), blk(cur, kcol), blk(nxt, kcol), ctx(kcol),
                  blk(prv, vcol), blk(cur, vcol), blk(nxt, vcol), ctx(vcol)],
        out_specs=pl.BlockSpec((CHUNK, BW), lambda b, j: (cur(b, j), 0)),
        out_shape=jax.ShapeDtypeStruct((n, BW), F32),
        compiler_params=_cparams(("arbitrary", "arbitrary")),
        name="window_attn",
    )(sink, qk, qk, qk, qk, qk, u, u, u, u)


def _conv_kernel(prev_ref, cur_ref, next_ref, w_ref, b_ref, ln_ref, o_ref, z_ref, *, tpb):
    i = pl.program_id(0)
    pos = i % tpb
    has_prev = pos >= 2
    has_next = (pos >= 1) & (pos < tpb - 1)
    halo = prev_ref.shape[0]

    def glu(x):
        return x[:, :BW] * _sigmoid(x[:, BW:])

    zp = glu(prev_ref[...])
    zn = glu(next_ref[...])
    z_ref[0:halo, :] = jnp.where(has_prev, zp, 0.0)
    z_ref[halo:halo + TILE, :] = glu(cur_ref[...])
    z_ref[halo + TILE:2 * halo + TILE, :] = jnp.where(has_next, zn, 0.0)
    acc = jnp.zeros((TILE, BW), F32) + b_ref[...]
    pad = CONV_W // 2
    for t in range(CONV_W):
        acc = acc + z_ref[halo - pad + t:halo - pad + t + TILE, :] * w_ref[t:t + 1, :]
    mu = jnp.mean(acc, axis=-1, keepdims=True)
    xc = acc - mu
    y = xc * lax.rsqrt(jnp.mean(xc * xc, axis=-1, keepdims=True) + EPS)
    y = y * ln_ref[0:1, :] + ln_ref[1:2, :]
    o_ref[...] = _silu(y)


def _conv_call(u, w, b, ln, tpb):
    n = u.shape[0]
    halo = 16
    per = TILE // halo
    nh = n // halo
    wcv = 2 * BW
    col = C_CV // wcv
    return pl.pallas_call(
        functools.partial(_conv_kernel, tpb=tpb),
        grid=(n // TILE,),
        in_specs=[pl.BlockSpec((halo, wcv), lambda i: (jnp.maximum(i * per - 1, 0), col)),
                  pl.BlockSpec((TILE, wcv), lambda i: (i, col)),
                  pl.BlockSpec((halo, wcv), lambda i: (jnp.minimum((i + 1) * per, nh - 1), col)),
                  pl.BlockSpec((CONV_W, BW), lambda i: (0, 0)),
                  pl.BlockSpec((1, BW), lambda i: (0, 0)),
                  pl.BlockSpec((2, BW), lambda i: (0, 0))],
        out_specs=pl.BlockSpec((TILE, BW), lambda i: (i, 0)),
        out_shape=jax.ShapeDtypeStruct((n, BW), F32),
        scratch_shapes=[pltpu.VMEM((TILE + 2 * halo, BW), F32)],
        compiler_params=_cparams(("arbitrary",)),
        name="conformer_conv",
    )(u, u, u, w, b.reshape(1, BW), ln)


def _inv_unit_lower(ls, eye):
    ts = [eye + l for l in ls]
    lps = list(ls)
    step = 1
    while step < ls[0].shape[0] // 2:
        lps = [_dot(lp, lp) for lp in lps]
        ts = [t + _dot(t, lp) for t, lp in zip(ts, lps)]
        step *= 2
    return ts


def _rw_chunk_kernel(prev_ref, cur_ref, next_ref, mu_ref, w0_ref, a0_ref, kk_ref, ka_ref,
                     wup_ref, aup_ref, rk_ref, bd_ref,
                     m_ref, n_ref, q_ref, y0_ref, bonus_ref, z_ref, *, cpb):
    i = pl.program_id(0)
    jj = i % cpb
    ctx_chunks = TILE // CHUNK
    has_prev = (jj != 0) & (jj != ctx_chunks)
    has_next = (jj != ctx_chunks - 1) & (jj != cpb - 1)
    c = CHUNK
    zc = cur_ref[:, 0:RW_SHIFT_W]
    z_ref[0:8, :] = jnp.where(has_prev, prev_ref[:, 0:RW_SHIFT_W], 0.0)
    z_ref[8:8 + c, :] = zc
    z_ref[8 + c:16 + c, :] = jnp.where(has_next, next_ref[:, 0:RW_SHIFT_W], 0.0)
    bd = bd_ref[...]
    t_io = lax.broadcasted_iota(jnp.int32, (c, c), 0)
    s_io = lax.broadcasted_iota(jnp.int32, (c, c), 1)
    eye = jnp.where(t_io == s_io, 1.0, 0.0)
    lane_head = _head_of_lane((c, BW), RW_DH)
    row = lax.broadcasted_iota(jnp.int32, (BW, BW), 0)
    colm = lax.broadcasted_iota(jnp.int32, (BW, BW), 1)
    same_head = (row // RW_DH) == (colm // RW_DH)

    def stack(x):
        return jnp.concatenate(
            [jnp.where(lane_head == h, x, 0.0) for h in range(RW_HEADS)], axis=0)

    cat = lambda xs: jnp.concatenate(xs, axis=1)

    prep = []
    for d in range(2):
        zsh = z_ref[7:7 + c, :] if d == 0 else z_ref[9:9 + c, :]
        zs = zc + (zsh - zc) * mu_ref[d]
        r = zs[:, 0:BW]
        k = zs[:, BW:2 * BW]
        v = zs[:, 2 * BW:3 * BW]
        wa = zs[:, 3 * BW:RW_SHIFT_W]
        lw = -RW_DECAY_SCALE * _sigmoid(w0_ref[d] + _dot(jnp.tanh(wa), wup_ref[d]))
        a = _sigmoid(a0_ref[d] + _dot(wa, aup_ref[d]))
        kappa = k * kk_ref[d]
        kh = kappa * lax.rsqrt(jnp.maximum(_dot_sel(kappa * kappa, bd), 1e-12))
        kt = k * (1.0 + (a - 1.0) * ka_ref[d])
        akh = a * kh
        bonus_ref[d] = _dot_sel(r * kt * rk_ref[...], bd) * v

        ahead = (t_io - s_io) if d == 0 else (s_io - t_io)
        earlier = ahead > 0
        upto = ahead >= 0
        tri = jnp.where(upto, 1.0, 0.0).astype(BF16)
        cl = _sel_dot(tri, lw)
        tot = jnp.sum(lw, axis=0, keepdims=True)
        rho = 0.5 * tot
        cle = cl - lw
        a_true = -kh * jnp.exp(cle)
        r_true = r * jnp.exp(cl)
        a_c = -kh * jnp.exp(cle - rho)
        r_c = r * jnp.exp(cl - rho)
        b_c = akh * jnp.exp(rho - cl)
        k_c = kt * jnp.exp(rho - cl)
        b_end = akh * jnp.exp(tot - cl)
        k_end = kt * jnp.exp(tot - cl)

        pair = _dot(jnp.concatenate([stack(a_c), stack(r_c)], axis=0),
                    jnp.concatenate([b_c, k_c], axis=0), _NT)
        l_ab, l_ak, a_rb, a_rk = [], [], [], []
        for h in range(RW_HEADS):
            blk_a = pair[h * c:(h + 1) * c]
            blk_r = pair[(RW_HEADS + h) * c:(RW_HEADS + h + 1) * c]
            l_ab.append(jnp.where(earlier, blk_a[:, 0:c], 0.0))
            l_ak.append(jnp.where(earlier, blk_a[:, c:2 * c], 0.0))
            a_rb.append(jnp.where(upto, blk_r[:, 0:c], 0.0))
            a_rk.append(jnp.where(upto, blk_r[:, c:2 * c], 0.0))
        prep.append((l_ab, cat(l_ak), cat(a_rb), cat(a_rk), v, a_true, r_true, b_end, k_end, tot))

    t_all = _inv_unit_lower(prep[0][0] + prep[1][0], eye)

    for d in range(2):
        _, lak, arb, ark, v, a_true, r_true, b_end, k_end, tot = prep[d]
        t_inv = cat(t_all[d * RW_HEADS:(d + 1) * RW_HEADS])
        lv = _dot(lak, stack(v))
        w12 = _dot(t_inv, jnp.concatenate([stack(a_true), stack(lv)], axis=1))
        w1 = w12[:, 0:BW]
        w2 = w12[:, BW:2 * BW]
        q_ref[d, 0] = r_true + _dot(arb, stack(w1))
        y0_ref[d, 0] = _dot(arb, stack(w2)) + _dot(ark, stack(v))
        b_end_t = b_end.T
        decay = jnp.where(row == colm, jnp.broadcast_to(jnp.exp(tot), (BW, BW)), 0.0)
        m_ref[d, 0] = jnp.where(same_head, _dot(b_end_t, w1), 0.0) + decay
        n_ref[d, 0] = jnp.where(same_head, _dot(b_end_t, w2) + _dot(k_end.T, v), 0.0)


def _rw_chunk_call(u, lp, bd, tpb):
    n = u.shape[0]
    cpb = tpb * (TILE // CHUNK)
    nch = n // CHUNK
    per = CHUNK // 8
    n8 = n // 8
    wrw = 1024
    col = C_RW // wrw
    pvec = lambda w: pl.BlockSpec((2, 1, w), lambda i: (0, 0, 0))
    pad = jnp.zeros((2, RW_DECAY_RANK, BW), F32)
    wup = jnp.concatenate([lp['rw_w_up'], pad], axis=1).astype(BF16)
    aup = jnp.concatenate([pad, lp['rw_a_up']], axis=1).astype(BF16)
    mat = lambda rows: pl.BlockSpec((2, 1, rows, BW), lambda i: (0, i, 0, 0))
    return pl.pallas_call(
        functools.partial(_rw_chunk_kernel, cpb=cpb),
        grid=(nch,),
        in_specs=[pl.BlockSpec((8, wrw), lambda i: (jnp.maximum(i * per - 1, 0), col)),
                  pl.BlockSpec((CHUNK, wrw), lambda i: (i, col)),
                  pl.BlockSpec((8, wrw), lambda i: (jnp.minimum((i + 1) * per, n8 - 1), col)),
                  pvec(RW_SHIFT_W), pvec(BW), pvec(BW), pvec(BW), pvec(BW),
                  pl.BlockSpec((2, 2 * RW_DECAY_RANK, BW), lambda i: (0, 0, 0)),
                  pl.BlockSpec((2, 2 * RW_A_RANK, BW), lambda i: (0, 0, 0)),
                  pl.BlockSpec((1, BW), lambda i: (0, 0)),
                  pl.BlockSpec((BW, BW), lambda i: (0, 0))],
        out_specs=[mat(BW), mat(BW), mat(CHUNK), mat(CHUNK),
                   pl.BlockSpec((2, CHUNK, BW), lambda i: (0, i, 0))],
        out_shape=[jax.ShapeDtypeStruct((2, nch, BW, BW), F32),
                   jax.ShapeDtypeStruct((2, nch, BW, BW), F32),
                   jax.ShapeDtypeStruct((2, nch, CHUNK, BW), F32),
                   jax.ShapeDtypeStruct((2, nch, CHUNK, BW), F32),
                   jax.ShapeDtypeStruct((2, n, BW), F32)],
        scratch_shapes=[pltpu.VMEM((CHUNK + 16, RW_SHIFT_W), F32)],
        compiler_params=_cparams(("arbitrary",)),
        name="rwkv_chunk",
    )(u, u, u, lp['rw_mu'].reshape(2, 1, RW_SHIFT_W), lp['rw_w0'].reshape(2, 1, BW),
      lp['rw_a0'].reshape(2, 1, BW), lp['rw_kk'].reshape(2, 1, BW), lp['rw_ka'].reshape(2, 1, BW),
      wup, aup, lp['rw_rk'].reshape(1, BW), bd)


def _rw_scan_kernel(*refs, n_batch):
    m_refs, n_refs, q_refs, y0_refs = refs[0:2], refs[2:4], refs[4:6], refs[6:8]
    y_refs, x_ref = refs[8:10], refs[10]

    @pl.when(pl.program_id(0) == 0)
    def _():
        x_ref[...] = jnp.zeros(x_ref.shape, F32)

    for d in range(2):
        for b in range(n_batch):
            x = x_ref[d, b]
            y_refs[d][b, 0] = _dot3(q_refs[d][0, b, 0], x) + y0_refs[d][0, b, 0]
            x_ref[d, b] = _dot3(m_refs[d][0, b, 0], x) + n_refs[d][0, b, 0]


def _rw_scan_call(m, nn, q, y0, n_batch, tpb):
    cpb = tpb * (TILE // CHUNK)
    ctx_chunks = TILE // CHUNK

    def chunk(d, i):
        rev = jnp.where(i < ctx_chunks, ctx_chunks - 1 - i, cpb + ctx_chunks - 1 - i)
        return i if d == 0 else rev

    def mat(rows, d):
        return pl.BlockSpec((1, n_batch, 1, rows, BW), lambda i: (d, 0, chunk(d, i), 0, 0))

    ins, specs = [], []
    for arr, rows in ((m, BW), (nn, BW), (q, CHUNK), (y0, CHUNK)):
        arr = arr.reshape(2, n_batch, cpb, rows, BW)
        for d in range(2):
            ins.append(arr)
            specs.append(mat(rows, d))
    yshape = jax.ShapeDtypeStruct((n_batch, cpb, CHUNK, BW), F32)
    ys = pl.pallas_call(
        functools.partial(_rw_scan_kernel, n_batch=n_batch),
        grid=(cpb,),
        in_specs=specs,
        out_specs=[pl.BlockSpec((n_batch, 1, CHUNK, BW), lambda i, d=d: (0, chunk(d, i), 0, 0))
                   for d in range(2)],
        out_shape=[yshape, yshape],
        scratch_shapes=[pltpu.VMEM((2, n_batch, BW, BW), F32)],
        compiler_params=_cparams(("arbitrary",)),
        name="rwkv_scan",
    )(*ins)
    return [y.reshape(n_batch * cpb * CHUNK, BW) for y in ys]


def _rw_out_kernel(yf_ref, yb_ref, bonus_ref, u_ref, gup_ref, gn_ref, bd_ref, o_ref):
    bd = bd_ref[...]
    y = yf_ref[...] + yb_ref[...]
    mean = _dot_sel(y, bd) * (1.0 / RW_DH)
    yc = y - mean
    var = _dot_sel(yc * yc, bd) * (1.0 / RW_DH)
    yn = yc * lax.rsqrt(var + RW_GN_EPS) * gn_ref[0:1, :] + gn_ref[1:2, :]
    yn = yn + (bonus_ref[0] + bonus_ref[1])
    gd = u_ref[:, RW_SHIFT_W:RW_SHIFT_W + RW_G_RANK]
    o_ref[...] = yn * _dot(_sigmoid(gd), gup_ref[...])


def _rw_out_call(ys, bonus, u, gup, gn, bd):
    n = u.shape[0]
    wrw = 1024
    both = pl.BlockSpec((2, TILE, BW), lambda i: (0, i, 0))
    one = pl.BlockSpec((TILE, BW), lambda i: (i, 0))
    return pl.pallas_call(
        _rw_out_kernel,
        grid=(n // TILE,),
        in_specs=[one, one, both, pl.BlockSpec((TILE, wrw), lambda i: (i, C_RW // wrw)),
                  pl.BlockSpec((RW_G_RANK, BW), lambda i: (0, 0)),
                  pl.BlockSpec((2, BW), lambda i: (0, 0)),
                  pl.BlockSpec((BW, BW), lambda i: (0, 0))],
        out_specs=pl.BlockSpec((TILE, BW), lambda i: (i, 0)),
        out_shape=jax.ShapeDtypeStruct((n, BW), F32),
        compiler_params=_cparams(("arbitrary",)),
        name="rwkv_readout",
    )(ys[0], ys[1], bonus, u, gup, gn, bd)


def _merge_kernel(oa_ref, ob_ref, oc_ref, od_ref, gl_ref, wb_ref, wo_ref, x_ref, g_ref, mod_ref,
                  o_ref):
    d = x_ref.shape[1]
    m = None
    for i, o in enumerate((oa_ref, ob_ref, oc_ref, od_ref)):
        t = _sigmoid(gl_ref[:, i * d:(i + 1) * d]) * _dot(o[...], wb_ref[i])
        m = t if m is None else m + t
    y = _dot(m, wo_ref[...])
    y = y * lax.rsqrt(jnp.mean(y * y, axis=-1, keepdims=True) + EPS) * g_ref[...]
    o_ref[...] = x_ref[...] + mod_ref[0] * y


def _merge_call(outs, u, wb, wo, x, g, mod, tpb, n_batch):
    n, d = x.shape
    kind = lambda i: (jnp.where(i % tpb == 0, n_batch, i // tpb), 0, 0)
    br = pl.BlockSpec((TILE, BW), lambda i: (i, 0))
    wg = N_BRANCH * d
    return pl.pallas_call(
        _merge_kernel,
        grid=(n // TILE,),
        in_specs=[br, br, br, br,
                  pl.BlockSpec((TILE, wg), lambda i: (i, C_GATE // wg)),
                  pl.BlockSpec((N_BRANCH, BW, d), lambda i: (0, 0, 0)),
                  pl.BlockSpec((d, d), lambda i: (0, 0)),
                  pl.BlockSpec((TILE, d), lambda i: (i, 0)),
                  pl.BlockSpec((1, d), lambda i: (0, 0)),
                  pl.BlockSpec((1, 1, d), kind)],
        out_specs=pl.BlockSpec((TILE, d), lambda i: (i, 0)),
        out_shape=jax.ShapeDtypeStruct((n, d), F32),
        compiler_params=_cparams(("arbitrary",)),
        name="merge",
    )(*outs, u, wb, wo, x, g.reshape(1, d), mod)


def _router_kernel(t_ref, w_ref, b_ref, o_ref):
    tm = t_ref.shape[0]
    gsz = N_EXPERTS // N_GROUPS
    logits = _dot3(w_ref[...], t_ref[...], _NT)
    sc = _sigmoid(logits).reshape(N_GROUPS, gsz, tm)
    bi = sc + b_ref[...].reshape(N_GROUPS, gsz, 1)
    shape = (N_GROUPS, gsz, tm)
    g_io = lax.broadcasted_iota(jnp.int32, shape, 0)
    j_io = lax.broadcasted_iota(jnp.int32, shape, 1)
    e_io = g_io * gsz + j_io
    ninf = -jnp.inf
    m1 = jnp.max(bi, axis=1, keepdims=True)
    i1 = jnp.min(jnp.where(bi == m1, j_io, gsz), axis=1, keepdims=True)
    m2 = jnp.max(jnp.where(j_io == i1, ninf, bi), axis=1, keepdims=True)
    cur = jnp.broadcast_to(m1 + m2, shape)
    gsel = jnp.zeros(shape, F32)
    for _ in range(TOPK_GROUPS):
        mx = jnp.max(cur, axis=0, keepdims=True)
        ix = jnp.min(jnp.where(cur == mx, g_io, N_GROUPS), axis=0, keepdims=True)
        hit = g_io == ix
        gsel = jnp.where(hit, 1.0, gsel)
        cur = jnp.where(hit, ninf, cur)
    cur = jnp.where(gsel > 0.0, bi, ninf)
    esel = jnp.zeros(shape, F32)
    for _ in range(TOP_K):
        mx = jnp.max(jnp.max(cur, axis=0, keepdims=True), axis=1, keepdims=True)
        ix = jnp.min(jnp.min(jnp.where(cur == mx, e_io, N_EXPERTS), axis=0, keepdims=True),
                     axis=1, keepdims=True)
        hit = e_io == ix
        esel = jnp.where(hit, 1.0, esel)
        cur = jnp.where(hit, ninf, cur)
    wsel = sc * esel
    den = jnp.sum(jnp.sum(wsel, axis=0, keepdims=True), axis=1, keepdims=True)
    o_ref[...] = (wsel / den * ROUTE_SCALE).reshape(N_EXPERTS, tm)


def _router_call(tokens, rw_t, rb):
    n, d = tokens.shape
    return pl.pallas_call(
        _router_kernel,
        grid=(n // TILE,),
        in_specs=[pl.BlockSpec((TILE, d), lambda i: (i, 0)),
                  pl.BlockSpec((N_EXPERTS, d), lambda i: (0, 0)),
                  pl.BlockSpec((N_EXPERTS, 1), lambda i: (0, 0))],
        out_specs=pl.BlockSpec((N_EXPERTS, TILE), lambda i: (0, i)),
        out_shape=jax.ShapeDtypeStruct((N_EXPERTS, n), F32),
        compiler_params=_cparams(("arbitrary",)),
        name="router",
    )(tokens, rw_t, rb.reshape(N_EXPERTS, 1))


def _moe_kernel(x_ref, g_ref, wgu_ref, wd_ref, sgu_ref, sd_ref, o_ref):
    e = pl.program_id(1)

    def ffn(wgu, wd):
        hgu = _dot(x_ref[...], wgu)
        return _dot(_silu(hgu[:, :D_EXPERT]) * hgu[:, D_EXPERT:], wd)

    @pl.when(e == 0)
    def _():
        o_ref[...] = ffn(sgu_ref[...], sd_ref[...])

    gates = g_ref[...]
    lane = lax.broadcasted_iota(jnp.int32, gates.shape, 1)
    gcol = jnp.sum(jnp.where(lane == e, gates, 0.0), axis=1, keepdims=True)
    o_ref[...] += ffn(wgu_ref[0, 0], wd_ref[0, 0]) * gcol


def _moe_call(tok, gates, wgu, wd, layer, sgu, sd, tm):
    n, d = tok.shape
    return pl.pallas_call(
        _moe_kernel,
        grid=(n // tm, N_EXPERTS),
        in_specs=[pl.BlockSpec((tm, d), lambda i, e: (i, 0)),
                  pl.BlockSpec((tm, N_EXPERTS), lambda i, e: (i, 0)),
                  pl.BlockSpec((1, 1, d, 2 * D_EXPERT), lambda i, e: (layer, e, 0, 0)),
                  pl.BlockSpec((1, 1, D_EXPERT, d), lambda i, e: (layer, e, 0, 0)),
                  pl.BlockSpec((d, 2 * D_EXPERT), lambda i, e: (0, 0)),
                  pl.BlockSpec((D_EXPERT, d), lambda i, e: (0, 0))],
        out_specs=pl.BlockSpec((tm, d), lambda i, e: (i, 0)),
        out_shape=jax.ShapeDtypeStruct((n, d), F32),
        compiler_params=_cparams(("arbitrary", "arbitrary")),
        name="moe_experts",
    )(tok, gates, wgu, wd, sgu, sd)


def _resid_kernel(x_ref, f_ref, g_ref, mod_ref, o_ref):
    f = f_ref[...]
    y = f * lax.rsqrt(jnp.mean(f * f, axis=-1, keepdims=True) + EPS) * g_ref[...]
    o_ref[...] = x_ref[...] + mod_ref[0] * y


def _resid_call(x, f, g, mod, tpb, n_batch):
    n, d = x.shape
    kind = lambda i: (jnp.where(i % tpb == 0, n_batch, i // tpb), 0, 0)
    row = pl.BlockSpec((TILE, d), lambda i: (i, 0))
    return pl.pallas_call(
        _resid_kernel,
        grid=(n // TILE,),
        in_specs=[row, row, pl.BlockSpec((1, d), lambda i: (0, 0)), pl.BlockSpec((1, 1, d), kind)],
        out_specs=row,
        out_shape=jax.ShapeDtypeStruct((n, d), F32),
        compiler_params=_cparams(("arbitrary",)),
        name="moe_residual",
    )(x, f, g.reshape(1, d), mod)


def _partner(dim, nblocks):
    q = dim // 4
    base = np.concatenate([np.arange(q) + q, np.arange(q), np.arange(q) + 3 * q, np.arange(q) + 2 * q])
    return (np.arange(nblocks)[:, None] * dim + base[None, :]).reshape(-1)


def _proj_columns():
    da, cv, sw = 0, 3 * BW, 5 * BW
    rw = sw + (SW_HEADS + 2 * SW_KV) * SW_DH
    gate = rw + RW_SHIFT_W + RW_G_RANK
    rep = np.repeat(np.arange(SW_KV), SW_HEADS // SW_KV)
    kv_rep = (rep[:, None] * SW_DH + np.arange(SW_DH)[None, :]).reshape(-1)
    da_q = da + np.arange(BW)
    da_k = da + BW + np.arange(BW)
    da_v = da + 2 * BW + np.arange(BW)
    sw_q = sw + np.arange(BW)
    sw_k = sw + BW + kv_rep
    sw_v = sw + BW + SW_KV * SW_DH + kv_rep
    p32 = _partner(DA_HALF, BW // DA_HALF)
    p64 = _partner(SW_DH, BW // SW_DH)
    cols = np.concatenate([
        da_q, da_k, sw_q, sw_k,
        da_q[p32], da_k[p32], sw_q[p64], sw_k[p64],
        da_v, sw_v,
        cv + np.arange(2 * BW),
        rw + np.arange(RW_SHIFT_W + RW_G_RANK),
        gate + np.arange(N_BRANCH * 4 * BW)])
    assert cols.shape[0] == C_TOTAL
    return cols


def _rope_tables(seq):
    rows = seq // GRID_W
    row = jnp.repeat(jnp.arange(rows, dtype=F32), GRID_W)
    colp = jnp.tile(jnp.arange(GRID_W, dtype=F32), rows)

    def tables(dim, reps):
        q = dim // 4
        freqs = ROPE_BASE ** (-jnp.arange(q, dtype=F32) / q)
        ar, ac = row[:, None] * freqs, colp[:, None] * freqs
        cos = jnp.concatenate([jnp.cos(ar), jnp.cos(ar), jnp.cos(ac), jnp.cos(ac)], axis=1)
        sin = jnp.concatenate([-jnp.sin(ar), jnp.sin(ar), -jnp.sin(ac), jnp.sin(ac)], axis=1)
        return jnp.tile(cos, (1, reps)), jnp.tile(sin, (1, reps))

    c32, s32 = tables(DA_HALF, 2 * BW // DA_HALF)
    c64, s64 = tables(SW_DH, 2 * BW // SW_DH)
    cos = jnp.concatenate([c32, c64], axis=1)
    sin = jnp.concatenate([s32, s64], axis=1)
    w = cos.shape[1]
    qscale = jnp.where(jnp.arange(w) < BW, (DA_HALF ** -0.5) * math.log2(math.e), 1.0)
    cos = jnp.concatenate([jnp.ones((TILE, w), F32), cos], axis=0) * qscale
    sin = jnp.concatenate([jnp.zeros((TILE, w), F32), sin], axis=0) * qscale
    return cos, sin


def _moe_tile(rows_b):
    best = TILE
    for t in range(TILE, MOE_MAX_TILE + 1, 16):
        if rows_b % t == 0:
            best = t
    return best


def kernel(x, c, ctx, c_ctx, ada_w, ada_b, norm_g, w_in, w_branch, w_out, da_lambda, da_subln,
           cv_w, cv_b, cv_ln, sw_sink, rw_mu, rw_w0, rw_w_up, rw_a0, rw_a_up, rw_kk, rw_ka,
           rw_g_up, rw_rk, rw_gn, router_w, router_b, ex_w_gu, ex_w_down, sh_w_gu, sh_w_down):
    n_batch, seq, d = x.shape
    ctx_len = ctx.shape[1]
    depth = w_in.shape[0]
    assert ctx_len == TILE and seq % TILE == 0 and seq % GRID_W == 0
    assert n_batch + 1 <= 8
    rows_b = ctx_len + seq
    tpb = rows_b // TILE
    n = n_batch * rows_b

    xs = jnp.concatenate([ctx, x], axis=1).reshape(n, d)
    cond = jnp.zeros((8, d), F32).at[:n_batch].set(c).at[n_batch].set(c_ctx)
    mods = _ada_call(cond, ada_w, ada_b)[:, :n_batch + 1]
    cos_t, sin_t = _rope_tables(seq)
    cols = _proj_columns()
    hio = np.arange(BW) // RW_DH
    bd = jnp.asarray(hio[:, None] == hio[None, :], BF16)

    for l in range(depth):
        mod = [mods[l, :, i * d:(i + 1) * d].reshape(n_batch + 1, 1, d) for i in range(6)]
        ng = norm_g[l]
        lam_init = 0.8 - 0.6 * math.exp(-0.3 * l)
        lv = da_lambda[l]
        lam = (jnp.exp(jnp.sum(lv[0] * lv[1])) - jnp.exp(jnp.sum(lv[2] * lv[3])) + lam_init)
        lp = {'rw_mu': rw_mu[l], 'rw_w0': rw_w0[l], 'rw_w_up': rw_w_up[l], 'rw_a0': rw_a0[l],
              'rw_a_up': rw_a_up[l], 'rw_kk': rw_kk[l], 'rw_ka': rw_ka[l], 'rw_rk': rw_rk[l]}

        (h,) = _norm_mod_call(xs, ng[0], mod[0], mod[1], tpb, n_batch, False)
        u = _proj_call(h, w_in[l][:, cols].astype(BF16))
        qk, vt = _rope_call(u, cos_t, sin_t, tpb)
        oa = _da_call(lam.reshape(1), qk, vt, jnp.tile(da_subln[l], DA_HEADS).reshape(1, BW),
                      n_batch, tpb, lam_init)
        ob = _conv_call(u, cv_w[l], cv_b[l], cv_ln[l], tpb)
        oc = _sw_call(sw_sink[l], qk, u, n_batch, tpb)
        cm, cn, cq, cy0, bonus = _rw_chunk_call(u, lp, bd, tpb)
        yscan = _rw_scan_call(cm, cn, cq, cy0, n_batch, tpb)
        od = _rw_out_call(yscan, bonus, u, rw_g_up[l].astype(BF16), rw_gn[l], bd)
        xs = _merge_call((oa, ob, oc, od), u, w_branch[l].astype(BF16), w_out[l].astype(BF16),
                         xs, ng[1], mod[2], tpb, n_batch)

        tok, tok32 = _norm_mod_call(xs, ng[2], mod[3], mod[4], tpb, n_batch, True)
        gates = _router_call(tok32, router_w[l].T, router_b[l]).T
        f = _moe_call(tok, gates, ex_w_gu, ex_w_down, l, sh_w_gu[l].astype(BF16),
                      sh_w_down[l].astype(BF16), _moe_tile(rows_b))
        xs = _resid_call(xs, f, ng[3], mod[5], tpb, n_batch)

    return xs.reshape(n_batch, rows_b, d)[:, ctx_len:]
```

```python
import functools
import math

import numpy as np
import jax
import jax.numpy as jnp
from jax import lax
from jax.experimental import pallas as pl
from jax.experimental.pallas import tpu as pltpu

F32 = jnp.float32
BF16 = jnp.bfloat16

GRID_W = 64
EPS = 1e-6
ROPE_BASE = 10000.0
N_BRANCH = 4
BW = 256
DA_HEADS = 4
DA_HALF = 32
SW_HEADS = 4
SW_KV = 2
SW_DH = 64
WINDOW = 128
CONV_W = 31
RW_HEADS = 4
RW_DH = 64
RW_DECAY_RANK = 64
RW_A_RANK = 64
RW_G_RANK = 128
RW_DECAY_SCALE = math.exp(-0.5)
RW_GN_EPS = 64e-5
RW_SHIFT_W = 3 * BW + RW_DECAY_RANK + RW_A_RANK
N_EXPERTS = 64
TOP_K = 6
N_GROUPS = 8
TOPK_GROUPS = 4
D_EXPERT = 256
ROUTE_SCALE = 2.5

TILE = 256
CHUNK = 128
DA_KT = 1408
MOE_MAX_TILE = 2112
DA_VROWS = 80
NEG = -1e30

C_ROPE = 0
C_SWAP = 1024
C_DAV = 2048
C_SWV = 2304
C_CV = 2560
C_RW = 3072
C_GATE = 4096
C_TOTAL = 8192

VMEM_LIMIT = 48 * 1024 * 1024

_NT = (((1,), (1,)), ((), ()))
_NN = (((1,), (0,)), ((), ()))


def _cparams(sem):
    return pltpu.CompilerParams(dimension_semantics=sem, vmem_limit_bytes=VMEM_LIMIT)


def _dot(a, b, dims=_NN):
    return lax.dot_general(a.astype(BF16), b.astype(BF16), dims, preferred_element_type=F32)


def _split2(x):
    hi = x.astype(BF16)
    lo = (x - hi.astype(F32)).astype(BF16)
    return hi, lo


def _dot3(a, b, dims=_NN):
    ah, al = _split2(a)
    bh, bl = _split2(b)
    dg = lambda x, y: lax.dot_general(x, y, dims, preferred_element_type=F32)
    return dg(ah, bh) + (dg(ah, bl) + dg(al, bh))


def _dot_sel(x, sel, dims=_NN):
    h0 = x.astype(BF16)
    r1 = x - h0.astype(F32)
    h1 = r1.astype(BF16)
    h2 = (r1 - h1.astype(F32)).astype(BF16)
    dg = lambda y: lax.dot_general(y, sel, dims, preferred_element_type=F32)
    return dg(h0) + (dg(h1) + dg(h2))


def _sel_dot(sel, x):
    h0 = x.astype(BF16)
    r1 = x - h0.astype(F32)
    h1 = r1.astype(BF16)
    h2 = (r1 - h1.astype(F32)).astype(BF16)
    dg = lambda y: lax.dot_general(sel, y, _NN, preferred_element_type=F32)
    return dg(h0) + (dg(h1) + dg(h2))


def _sigmoid(x):
    return jax.nn.sigmoid(x)


def _silu(x):
    return x * jax.nn.sigmoid(x)


def _head_of_lane(shape, width):
    return lax.broadcasted_iota(jnp.int32, shape, len(shape) - 1) // width


def _ada_kernel(s_ref, w_ref, b_ref, o_ref):
    s = _silu(s_ref[...])
    o_ref[0] = _dot(s, w_ref[0]) + b_ref[0]


def _ada_call(cond, ada_w, ada_b):
    depth, d, cols = ada_w.shape
    tn = 1536
    return pl.pallas_call(
        _ada_kernel,
        grid=(depth, cols // tn),
        in_specs=[pl.BlockSpec((8, d), lambda l, j: (0, 0)),
                  pl.BlockSpec((1, d, tn), lambda l, j: (l, 0, j)),
                  pl.BlockSpec((1, 1, tn), lambda l, j: (l, 0, j))],
        out_specs=pl.BlockSpec((1, 8, tn), lambda l, j: (l, 0, j)),
        out_shape=jax.ShapeDtypeStruct((depth, 8, cols), F32),
        compiler_params=_cparams(("arbitrary", "arbitrary")),
        name="ada_mod",
    )(cond, ada_w, ada_b.reshape(depth, 1, cols))


def _norm_mod_kernel(x_ref, g_ref, sh_ref, sc_ref, o_ref):
    x = x_ref[...]
    y = x * lax.rsqrt(jnp.mean(x * x, axis=-1, keepdims=True) + EPS) * g_ref[...]
    o_ref[...] = (y * (1.0 + sc_ref[0]) + sh_ref[0]).astype(BF16)


def _norm_mod_call(x, g, shift, scale, tpb, n_batch):
    n, d = x.shape
    kind = lambda i: (jnp.where(i % tpb == 0, n_batch, i // tpb), 0, 0)
    row = pl.BlockSpec((TILE, d), lambda i: (i, 0))
    return pl.pallas_call(
        _norm_mod_kernel,
        grid=(n // TILE,),
        in_specs=[row, pl.BlockSpec((1, d), lambda i: (0, 0)),
                  pl.BlockSpec((1, 1, d), kind), pl.BlockSpec((1, 1, d), kind)],
        out_specs=row,
        out_shape=jax.ShapeDtypeStruct((n, d), BF16),
        compiler_params=_cparams(("arbitrary",)),
        name="norm_mod",
    )(x, g.reshape(1, d), shift, scale)


def _mm_kernel(a_ref, w_ref, o_ref):
    o_ref[...] = jnp.dot(a_ref[...], w_ref[...], preferred_element_type=F32)


def _proj_call(h, w):
    n, d = h.shape
    cols = w.shape[1]
    tm, tn = 2 * TILE, 4096
    return pl.pallas_call(
        _mm_kernel,
        grid=(cols // tn, n // tm),
        in_specs=[pl.BlockSpec((tm, d), lambda j, i: (i, 0)),
                  pl.BlockSpec((d, tn), lambda j, i: (0, j))],
        out_specs=pl.BlockSpec((tm, tn), lambda j, i: (i, j)),
        out_shape=jax.ShapeDtypeStruct((n, cols), F32),
        compiler_params=_cparams(("arbitrary", "arbitrary")),
        name="in_proj",
    )(h, w)


def _rope_kernel(um_ref, us_ref, c_ref, s_ref, v_ref, o_ref, vt_ref):
    o_ref[...] = (um_ref[...] * c_ref[...] + us_ref[...] * s_ref[...]).astype(BF16)
    vt = v_ref[...].T.astype(BF16)
    hd = 2 * DA_HALF
    for h in range(DA_HEADS):
        vt_ref[h * DA_VROWS:h * DA_VROWS + hd, :] = vt[h * hd:(h + 1) * hd, :]
        vt_ref[h * DA_VROWS + hd:(h + 1) * DA_VROWS, :] = jnp.ones((DA_VROWS - hd, TILE), BF16)


def _rope_call(u, cos_t, sin_t, tpb):
    n = u.shape[0]
    w = cos_t.shape[1]
    tab = pl.BlockSpec((TILE, w), lambda i: (i % tpb, 0))
    return pl.pallas_call(
        _rope_kernel,
        grid=(n // TILE,),
        in_specs=[pl.BlockSpec((TILE, w), lambda i: (i, C_ROPE // w)),
                  pl.BlockSpec((TILE, w), lambda i: (i, C_SWAP // w)), tab, tab,
                  pl.BlockSpec((TILE, BW), lambda i: (i, C_DAV // BW))],
        out_specs=[pl.BlockSpec((TILE, w), lambda i: (i, 0)),
                   pl.BlockSpec((DA_HEADS * DA_VROWS, TILE), lambda i: (i // tpb, i % tpb))],
        out_shape=[jax.ShapeDtypeStruct((n, w), BF16),
                   jax.ShapeDtypeStruct((n // (tpb * TILE) * DA_HEADS * DA_VROWS, tpb * TILE),
                                        BF16)],
        compiler_params=_cparams(("arbitrary",)),
        name="rope",
    )(u, u, cos_t, sin_t, u)


def _da_kernel(lam_ref, q_ref, k_ref, vt_ref, g_ref, o_ref,
               qs_ref, m_ref, acc_ref, *, nkt, lam_init):
    i = pl.program_id(1)
    tq = q_ref.shape[0]
    hd = 2 * DA_HALF
    q = q_ref[...]
    qmap = _head_of_lane((tq, BW), DA_HALF)
    for g in range(2 * DA_HEADS):
        qs_ref[g] = jnp.where(qmap == g, q, jnp.zeros_like(q))
    m_ref[...] = jnp.full(m_ref.shape, NEG, F32)
    acc_ref[...] = jnp.zeros(acc_ref.shape, F32)

    def tile(off, size):
        kt = k_ref[pl.ds(off, size), :]
        groups = range(2 * DA_HEADS)
        ss = [lax.dot_general(kt, qs_ref[g], _NT, preferred_element_type=F32) for g in groups]
        for g in groups:
            s = ss[g]
            m_old = m_ref[g]
            m_new = jnp.maximum(m_old, jnp.max(s, axis=0, keepdims=True))
            alpha = jnp.exp2(m_old - m_new)[0:1, :]
            p = jnp.exp2(s - m_new[0:1, :]).astype(BF16)
            m_ref[g] = m_new
            h, mm = g // 2, g % 2
            rows = slice(h * DA_VROWS, (h + 1) * DA_VROWS)
            pv = jnp.dot(vt_ref[rows, pl.ds(off, size)], p, preferred_element_type=F32)
            acc_ref[mm, rows, :] = acc_ref[mm, rows, :] * alpha + pv

    @pl.when(i == 0)
    def _():
        tile(0, TILE)

    @pl.when(i > 0)
    def _():
        def body(j, carry):
            tile(pl.multiple_of(j * DA_KT, DA_KT), DA_KT)
            return carry

        lax.fori_loop(0, nkt, body, 0)

    lam = lam_ref[0]
    parts = []
    for h in range(DA_HEADS):
        rows = slice(h * DA_VROWS, h * DA_VROWS + hd)
        den = slice(h * DA_VROWS + hd, h * DA_VROWS + hd + 1)
        o_h = (acc_ref[0, rows, :] / acc_ref[0, den, :]
               - lam * (acc_ref[1, rows, :] / acc_ref[1, den, :]))
        ms = jnp.mean(o_h * o_h, axis=0, keepdims=True)
        parts.append(o_h * lax.rsqrt(ms + EPS))
    y = jnp.concatenate(parts, axis=0).T * g_ref[...]
    o_ref[...] = y * (1.0 - lam_init)


def _da_call(lam, qk, vt, subln, n_batch, tpb, lam_init):
    n = qk.shape[0]
    rows_b = tpb * TILE
    assert rows_b % DA_KT == 0
    vrows = DA_HEADS * DA_VROWS
    kern = functools.partial(_da_kernel, nkt=rows_b // DA_KT, lam_init=lam_init)
    return pl.pallas_call(
        kern,
        grid=(n_batch, tpb),
        in_specs=[pl.BlockSpec(memory_space=pltpu.SMEM),
                  pl.BlockSpec((TILE, BW), lambda b, i: (b * tpb + i, 0)),
                  pl.BlockSpec((rows_b, BW), lambda b, i: (b, 1)),
                  pl.BlockSpec((vrows, rows_b), lambda b, i: (b, 0)),
                  pl.BlockSpec((1, BW), lambda b, i: (0, 0))],
        out_specs=pl.BlockSpec((TILE, BW), lambda b, i: (b * tpb + i, 0)),
        out_shape=jax.ShapeDtypeStruct((n, BW), F32),
        scratch_shapes=[pltpu.VMEM((2 * DA_HEADS, TILE, BW), BF16),
                        pltpu.VMEM((2 * DA_HEADS, 8, TILE), F32),
                        pltpu.VMEM((2, vrows, TILE), F32)],
        compiler_params=_cparams(("arbitrary", "arbitrary")),
        name="diff_attn",
    )(lam, qk, qk, vt, subln)


def _sw_kernel(sink_ref, bias_ref, q_ref, kp_ref, ko_ref, kn_ref, kc_ref,
               vp_ref, vo_ref, vn_ref, vc_ref, o_ref):
    tq = q_ref.shape[0]
    q = q_ref[...]
    kk = jnp.concatenate([kp_ref[...], ko_ref[...], kn_ref[...], kc_ref[...]], axis=0)
    vv = jnp.concatenate([vp_ref[...], vo_ref[...], vn_ref[...], vc_ref[...]], axis=0).astype(BF16)
    nk = kk.shape[0]
    bias = bias_ref[0]
    qhead = _head_of_lane((tq, BW), SW_DH)
    vhead = _head_of_lane((nk, BW), SW_DH)
    ps, vs = [], []
    for h in range(SW_HEADS):
        qm = jnp.where(qhead == h, q, jnp.zeros_like(q))
        s = lax.dot_general(qm, kk, _NT, preferred_element_type=F32) * (SW_DH ** -0.5) + bias
        sk = sink_ref[h]
        m = jnp.maximum(jnp.max(s, axis=1, keepdims=True), sk)
        p = jnp.exp(s - m)
        den = jnp.sum(p, axis=1, keepdims=True) + jnp.exp(sk - m)
        ps.append((p / den).astype(BF16))
        vs.append(jnp.where(vhead == h, vv, jnp.zeros_like(vv)))
    o_ref[...] = jnp.dot(jnp.concatenate(ps, axis=1), jnp.concatenate(vs, axis=0),
                         preferred_element_type=F32)


def _sw_bias():
    r = np.arange(CHUNK)[:, None]
    c = np.arange(CHUNK)[None, :]
    prev_ok, next_ok = c >= r, c <= r
    yes, no = np.ones((CHUNK, CHUNK), bool), np.zeros((CHUNK, CHUNK), bool)
    ctx = np.ones((CHUNK, TILE), bool)
    kinds = [np.concatenate([no, no, no, ctx], axis=1)]
    for no_next in (False, True):
        for no_prev in (False, True):
            kinds.append(np.concatenate([no if no_prev else prev_ok, yes,
                                         no if no_next else next_ok, ctx], axis=1))
    return jnp.asarray(np.where(np.stack(kinds), 0.0, NEG), F32)


def _sw_call(sink, qk, u, n_batch, tpb):
    n = qk.shape[0]
    cpb = tpb * (TILE // CHUNK)
    ctx_chunks = TILE // CHUNK
    bias = _sw_bias()

    def kind(b, j):
        lat = 1 + (j == ctx_chunks).astype(jnp.int32) + 2 * (j == cpb - 1).astype(jnp.int32)
        return (jnp.where(j < ctx_chunks, 0, lat), 0, 0)

    cur = lambda b, j: b * cpb + j
    prv = lambda b, j: b * cpb + jnp.maximum(j - 1, 0)
    nxt = lambda b, j: b * cpb + jnp.minimum(j + 1, cpb - 1)
    kcol, vcol = 3, C_SWV // BW
    blk = lambda f, col: pl.BlockSpec((CHUNK, BW), lambda b, j: (f(b, j), col))
    ctx = lambda col: pl.BlockSpec((TILE, BW), lambda b, j: (b * tpb, col))
    return pl.pallas_call(
        _sw_kernel,
        grid=(n_batch, cpb),
        in_specs=[pl.BlockSpec(memory_space=pltpu.SMEM),
                  pl.BlockSpec((1,) + bias.shape[1:], kind),
                  blk(cur, 2), blk(prv, kcol), blk(cur, kcol), blk(nxt, kcol), ctx(kcol),
                  blk(prv, vcol), blk(cur, vcol), blk(nxt, vcol), ctx(vcol)],
        out_specs=pl.BlockSpec((CHUNK, BW), lambda b, j: (cur(b, j), 0)),
        out_shape=jax.ShapeDtypeStruct((n, BW), F32),
        compiler_params=_cparams(("arbitrary", "arbitrary")),
        name="window_attn",
    )(sink, bias, qk, qk, qk, qk, qk, u, u, u, u)


def _conv_kernel(prev_ref, cur_ref, next_ref, w_ref, b_ref, ln_ref, o_ref, z_ref, *, tpb):
    i = pl.program_id(0)
    pos = i % tpb
    has_prev = pos >= 2
    has_next = (pos >= 1) & (pos < tpb - 1)
    halo = prev_ref.shape[0]

    def glu(x):
        return x[:, :BW] * _sigmoid(x[:, BW:])

    zp = glu(prev_ref[...])
    zn = glu(next_ref[...])
    z_ref[0:halo, :] = jnp.where(has_prev, zp, 0.0)
    z_ref[halo:halo + TILE, :] = glu(cur_ref[...])
    z_ref[halo + TILE:2 * halo + TILE, :] = jnp.where(has_next, zn, 0.0)
    acc = jnp.zeros((TILE, BW), F32) + b_ref[...]
    pad = CONV_W // 2
    for t in range(CONV_W):
        acc = acc + z_ref[halo - pad + t:halo - pad + t + TILE, :] * w_ref[t:t + 1, :]
    mu = jnp.mean(acc, axis=-1, keepdims=True)
    xc = acc - mu
    y = xc * lax.rsqrt(jnp.mean(xc * xc, axis=-1, keepdims=True) + EPS)
    y = y * ln_ref[0:1, :] + ln_ref[1:2, :]
    o_ref[...] = _silu(y)


def _conv_call(u, w, b, ln, tpb):
    n = u.shape[0]
    halo = 16
    per = TILE // halo
    nh = n // halo
    wcv = 2 * BW
    col = C_CV // wcv
    return pl.pallas_call(
        functools.partial(_conv_kernel, tpb=tpb),
        grid=(n // TILE,),
        in_specs=[pl.BlockSpec((halo, wcv), lambda i: (jnp.maximum(i * per - 1, 0), col)),
                  pl.BlockSpec((TILE, wcv), lambda i: (i, col)),
                  pl.BlockSpec((halo, wcv), lambda i: (jnp.minimum((i + 1) * per, nh - 1), col)),
                  pl.BlockSpec((CONV_W, BW), lambda i: (0, 0)),
                  pl.BlockSpec((1, BW), lambda i: (0, 0)),
                  pl.BlockSpec((2, BW), lambda i: (0, 0))],
        out_specs=pl.BlockSpec((TILE, BW), lambda i: (i, 0)),
        out_shape=jax.ShapeDtypeStruct((n, BW), F32),
        scratch_shapes=[pltpu.VMEM((TILE + 2 * halo, BW), F32)],
        compiler_params=_cparams(("arbitrary",)),
        name="conformer_conv",
    )(u, u, u, w, b.reshape(1, BW), ln)


def _inv_unit_lower(ls, eye):
    ts = [eye + l for l in ls]
    lps = list(ls)
    step = 1
    while step < ls[0].shape[0] // 2:
        lps = [_dot(lp, lp) for lp in lps]
        ts = [t + _dot(t, lp) for t, lp in zip(ts, lps)]
        step *= 2
    return ts


def _rw_chunk_kernel(prev_ref, cur_ref, next_ref, mu_ref, w0_ref, a0_ref, kk_ref, ka_ref,
                     wup_ref, aup_ref, rk_ref, bd_ref,
                     m_ref, n_ref, q_ref, y0_ref, bonus_ref, z_ref, *, cpb):
    i = pl.program_id(0)
    jj = i % cpb
    ctx_chunks = TILE // CHUNK
    has_prev = (jj != 0) & (jj != ctx_chunks)
    has_next = (jj != ctx_chunks - 1) & (jj != cpb - 1)
    c = CHUNK
    zc = cur_ref[:, 0:RW_SHIFT_W]
    z_ref[0:8, :] = jnp.where(has_prev, prev_ref[:, 0:RW_SHIFT_W], 0.0)
    z_ref[8:8 + c, :] = zc
    z_ref[8 + c:16 + c, :] = jnp.where(has_next, next_ref[:, 0:RW_SHIFT_W], 0.0)
    bd = bd_ref[...]
    t_io = lax.broadcasted_iota(jnp.int32, (c, c), 0)
    s_io = lax.broadcasted_iota(jnp.int32, (c, c), 1)
    eye = jnp.where(t_io == s_io, 1.0, 0.0)
    lane_head = _head_of_lane((c, BW), RW_DH)
    row = lax.broadcasted_iota(jnp.int32, (BW, BW), 0)
    colm = lax.broadcasted_iota(jnp.int32, (BW, BW), 1)
    same_head = (row // RW_DH) == (colm // RW_DH)

    def stack(x):
        return jnp.concatenate(
            [jnp.where(lane_head == h, x, 0.0) for h in range(RW_HEADS)], axis=0)

    cat = lambda xs: jnp.concatenate(xs, axis=1)

    prep = []
    for d in range(2):
        zsh = z_ref[7:7 + c, :] if d == 0 else z_ref[9:9 + c, :]
        zs = zc + (zsh - zc) * mu_ref[d]
        r = zs[:, 0:BW]
        k = zs[:, BW:2 * BW]
        v = zs[:, 2 * BW:3 * BW]
        wa = zs[:, 3 * BW:RW_SHIFT_W]
        lw = -RW_DECAY_SCALE * _sigmoid(w0_ref[d] + _dot(jnp.tanh(wa), wup_ref[d]))
        a = _sigmoid(a0_ref[d] + _dot(wa, aup_ref[d]))
        kappa = k * kk_ref[d]
        kh = kappa * lax.rsqrt(jnp.maximum(_dot_sel(kappa * kappa, bd), 1e-12))
        kt = k * (1.0 + (a - 1.0) * ka_ref[d])
        akh = a * kh
        bonus_ref[d] = _dot_sel(r * kt * rk_ref[...], bd) * v

        ahead = (t_io - s_io) if d == 0 else (s_io - t_io)
        earlier = ahead > 0
        upto = ahead >= 0
        tri = jnp.where(upto, 1.0, 0.0).astype(BF16)
        cl = _sel_dot(tri, lw)
        tot = jnp.sum(lw, axis=0, keepdims=True)
        rho = 0.5 * tot
        cle = cl - lw
        a_true = -kh * jnp.exp(cle)
        r_true = r * jnp.exp(cl)
        a_c = -kh * jnp.exp(cle - rho)
        r_c = r * jnp.exp(cl - rho)
        b_c = akh * jnp.exp(rho - cl)
        k_c = kt * jnp.exp(rho - cl)
        b_end = akh * jnp.exp(tot - cl)
        k_end = kt * jnp.exp(tot - cl)

        pair = _dot(jnp.concatenate([stack(a_c), stack(r_c)], axis=0),
                    jnp.concatenate([b_c, k_c], axis=0), _NT)
        l_ab, l_ak, a_rb, a_rk = [], [], [], []
        for h in range(RW_HEADS):
            blk_a = pair[h * c:(h + 1) * c]
            blk_r = pair[(RW_HEADS + h) * c:(RW_HEADS + h + 1) * c]
            l_ab.append(jnp.where(earlier, blk_a[:, 0:c], 0.0))
            l_ak.append(jnp.where(earlier, blk_a[:, c:2 * c], 0.0))
            a_rb.append(jnp.where(upto, blk_r[:, 0:c], 0.0))
            a_rk.append(jnp.where(upto, blk_r[:, c:2 * c], 0.0))
        prep.append((l_ab, cat(l_ak), cat(a_rb), cat(a_rk), v, a_true, r_true, b_end, k_end, tot))

    t_all = _inv_unit_lower(prep[0][0] + prep[1][0], eye)

    for d in range(2):
        _, lak, arb, ark, v, a_true, r_true, b_end, k_end, tot = prep[d]
        t_inv = cat(t_all[d * RW_HEADS:(d + 1) * RW_HEADS])
        lv = _dot(lak, stack(v))
        w12 = _dot(t_inv, jnp.concatenate([stack(a_true), stack(lv)], axis=1))
        w1 = w12[:, 0:BW]
        w2 = w12[:, BW:2 * BW]
        q_ref[d, 0] = r_true + _dot(arb, stack(w1))
        y0_ref[d, 0] = _dot(arb, stack(w2)) + _dot(ark, stack(v))
        b_end_t = b_end.T
        decay = jnp.where(row == colm, jnp.broadcast_to(jnp.exp(tot), (BW, BW)), 0.0)
        m_ref[d, 0] = jnp.where(same_head, _dot(b_end_t, w1), 0.0) + decay
        n_ref[d, 0] = jnp.where(same_head, _dot(b_end_t, w2) + _dot(k_end.T, v), 0.0)


def _rw_chunk_call(u, lp, bd, tpb):
    n = u.shape[0]
    cpb = tpb * (TILE // CHUNK)
    nch = n // CHUNK
    per = CHUNK // 8
    n8 = n // 8
    wrw = 1024
    col = C_RW // wrw
    pvec = lambda w: pl.BlockSpec((2, 1, w), lambda i: (0, 0, 0))
    pad = jnp.zeros((2, RW_DECAY_RANK, BW), F32)
    wup = jnp.concatenate([lp['rw_w_up'], pad], axis=1).astype(BF16)
    aup = jnp.concatenate([pad, lp['rw_a_up']], axis=1).astype(BF16)
    mat = lambda rows: pl.BlockSpec((2, 1, rows, BW), lambda i: (0, i, 0, 0))
    return pl.pallas_call(
        functools.partial(_rw_chunk_kernel, cpb=cpb),
        grid=(nch,),
        in_specs=[pl.BlockSpec((8, wrw), lambda i: (jnp.maximum(i * per - 1, 0), col)),
                  pl.BlockSpec((CHUNK, wrw), lambda i: (i, col)),
                  pl.BlockSpec((8, wrw), lambda i: (jnp.minimum((i + 1) * per, n8 - 1), col)),
                  pvec(RW_SHIFT_W), pvec(BW), pvec(BW), pvec(BW), pvec(BW),
                  pl.BlockSpec((2, 2 * RW_DECAY_RANK, BW), lambda i: (0, 0, 0)),
                  pl.BlockSpec((2, 2 * RW_A_RANK, BW), lambda i: (0, 0, 0)),
                  pl.BlockSpec((1, BW), lambda i: (0, 0)),
                  pl.BlockSpec((BW, BW), lambda i: (0, 0))],
        out_specs=[mat(BW), mat(BW), mat(CHUNK), mat(CHUNK),
                   pl.BlockSpec((2, CHUNK, BW), lambda i: (0, i, 0))],
        out_shape=[jax.ShapeDtypeStruct((2, nch, BW, BW), F32),
                   jax.ShapeDtypeStruct((2, nch, BW, BW), F32),
                   jax.ShapeDtypeStruct((2, nch, CHUNK, BW), F32),
                   jax.ShapeDtypeStruct((2, nch, CHUNK, BW), F32),
                   jax.ShapeDtypeStruct((2, n, BW), F32)],
        scratch_shapes=[pltpu.VMEM((CHUNK + 16, RW_SHIFT_W), F32)],
        compiler_params=_cparams(("arbitrary",)),
        name="rwkv_chunk",
    )(u, u, u, lp['rw_mu'].reshape(2, 1, RW_SHIFT_W), lp['rw_w0'].reshape(2, 1, BW),
      lp['rw_a0'].reshape(2, 1, BW), lp['rw_kk'].reshape(2, 1, BW), lp['rw_ka'].reshape(2, 1, BW),
      wup, aup, lp['rw_rk'].reshape(1, BW), bd)


def _rw_scan_kernel(*refs, n_batch):
    m_refs, n_refs, q_refs, y0_refs = refs[0:2], refs[2:4], refs[4:6], refs[6:8]
    y_refs, x_ref = refs[8:10], refs[10]

    @pl.when(pl.program_id(0) == 0)
    def _():
        x_ref[...] = jnp.zeros(x_ref.shape, F32)

    for d in range(2):
        for b in range(n_batch):
            x = x_ref[d, b]
            y_refs[d][b, 0] = _dot3(q_refs[d][0, b, 0], x) + y0_refs[d][0, b, 0]
            x_ref[d, b] = _dot3(m_refs[d][0, b, 0], x) + n_refs[d][0, b, 0]


def _rw_scan_call(m, nn, q, y0, n_batch, tpb):
    cpb = tpb * (TILE // CHUNK)
    ctx_chunks = TILE // CHUNK

    def chunk(d, i):
        rev = jnp.where(i < ctx_chunks, ctx_chunks - 1 - i, cpb + ctx_chunks - 1 - i)
        return i if d == 0 else rev

    def mat(rows, d):
        return pl.BlockSpec((1, n_batch, 1, rows, BW), lambda i: (d, 0, chunk(d, i), 0, 0))

    ins, specs = [], []
    for arr, rows in ((m, BW), (nn, BW), (q, CHUNK), (y0, CHUNK)):
        arr = arr.reshape(2, n_batch, cpb, rows, BW)
        for d in range(2):
            ins.append(arr)
            specs.append(mat(rows, d))
    yshape = jax.ShapeDtypeStruct((n_batch, cpb, CHUNK, BW), F32)
    ys = pl.pallas_call(
        functools.partial(_rw_scan_kernel, n_batch=n_batch),
        grid=(cpb,),
        in_specs=specs,
        out_specs=[pl.BlockSpec((n_batch, 1, CHUNK, BW), lambda i, d=d: (0, chunk(d, i), 0, 0))
                   for d in range(2)],
        out_shape=[yshape, yshape],
        scratch_shapes=[pltpu.VMEM((2, n_batch, BW, BW), F32)],
        compiler_params=_cparams(("arbitrary",)),
        name="rwkv_scan",
    )(*ins)
    return [y.reshape(n_batch * cpb * CHUNK, BW) for y in ys]


def _rw_out_kernel(yf_ref, yb_ref, bonus_ref, u_ref, gup_ref, gn_ref, bd_ref, o_ref):
    bd = bd_ref[...]
    y = yf_ref[...] + yb_ref[...]
    mean = _dot_sel(y, bd) * (1.0 / RW_DH)
    yc = y - mean
    var = _dot_sel(yc * yc, bd) * (1.0 / RW_DH)
    yn = yc * lax.rsqrt(var + RW_GN_EPS) * gn_ref[0:1, :] + gn_ref[1:2, :]
    yn = yn + (bonus_ref[0] + bonus_ref[1])
    gd = u_ref[:, RW_SHIFT_W:RW_SHIFT_W + RW_G_RANK]
    o_ref[...] = yn * _dot(_sigmoid(gd), gup_ref[...])


def _rw_out_call(ys, bonus, u, gup, gn, bd):
    n = u.shape[0]
    wrw = 1024
    both = pl.BlockSpec((2, TILE, BW), lambda i: (0, i, 0))
    one = pl.BlockSpec((TILE, BW), lambda i: (i, 0))
    return pl.pallas_call(
        _rw_out_kernel,
        grid=(n // TILE,),
        in_specs=[one, one, both, pl.BlockSpec((TILE, wrw), lambda i: (i, C_RW // wrw)),
                  pl.BlockSpec((RW_G_RANK, BW), lambda i: (0, 0)),
                  pl.BlockSpec((2, BW), lambda i: (0, 0)),
                  pl.BlockSpec((BW, BW), lambda i: (0, 0))],
        out_specs=pl.BlockSpec((TILE, BW), lambda i: (i, 0)),
        out_shape=jax.ShapeDtypeStruct((n, BW), F32),
        compiler_params=_cparams(("arbitrary",)),
        name="rwkv_readout",
    )(ys[0], ys[1], bonus, u, gup, gn, bd)


def _merge_kernel(oa_ref, ob_ref, oc_ref, od_ref, gl_ref, wb_ref, wo_ref, x_ref, g_ref, mod_ref,
                  g2_ref, sh_ref, sc_ref, rw_ref, rb_ref, o_ref, tok_ref, gate_ref):
    d = x_ref.shape[1]
    m = None
    for i, o in enumerate((oa_ref, ob_ref, oc_ref, od_ref)):
        t = _sigmoid(gl_ref[:, i * d:(i + 1) * d]) * _dot(o[...], wb_ref[i])
        m = t if m is None else m + t
    y = _dot(m, wo_ref[...])
    y = y * lax.rsqrt(jnp.mean(y * y, axis=-1, keepdims=True) + EPS) * g_ref[...]
    x = x_ref[...] + mod_ref[0] * y
    o_ref[...] = x
    t = x * lax.rsqrt(jnp.mean(x * x, axis=-1, keepdims=True) + EPS) * g2_ref[...]
    tok = t * (1.0 + sc_ref[0]) + sh_ref[0]
    tok_ref[...] = tok.astype(BF16)
    gate_ref[...] = _route(tok, rw_ref[...], rb_ref[...])


def _merge_call(outs, u, wb, wo, x, g, mod, g2, shift, scale, rw_t, rb, tpb, n_batch):
    n, d = x.shape
    kind = lambda i: (jnp.where(i % tpb == 0, n_batch, i // tpb), 0, 0)
    br = pl.BlockSpec((TILE, BW), lambda i: (i, 0))
    vec = pl.BlockSpec((1, d), lambda i: (0, 0))
    row = pl.BlockSpec((TILE, d), lambda i: (i, 0))
    wg = N_BRANCH * d
    return pl.pallas_call(
        _merge_kernel,
        grid=(n // TILE,),
        in_specs=[br, br, br, br,
                  pl.BlockSpec((TILE, wg), lambda i: (i, C_GATE // wg)),
                  pl.BlockSpec((N_BRANCH, BW, d), lambda i: (0, 0, 0)),
                  pl.BlockSpec((d, d), lambda i: (0, 0)),
                  row, vec, pl.BlockSpec((1, 1, d), kind),
                  vec, pl.BlockSpec((1, 1, d), kind), pl.BlockSpec((1, 1, d), kind),
                  pl.BlockSpec((N_EXPERTS, d), lambda i: (0, 0)),
                  pl.BlockSpec((N_EXPERTS, 1), lambda i: (0, 0))],
        out_specs=[row, row, pl.BlockSpec((N_EXPERTS, TILE), lambda i: (0, i))],
        out_shape=[jax.ShapeDtypeStruct((n, d), F32), jax.ShapeDtypeStruct((n, d), BF16),
                   jax.ShapeDtypeStruct((N_EXPERTS, n), F32)],
        compiler_params=_cparams(("arbitrary",)),
        name="merge",
    )(*outs, u, wb, wo, x, g.reshape(1, d), mod, g2.reshape(1, d), shift, scale,
      rw_t, rb.reshape(N_EXPERTS, 1))


def _route(tokens, w, b):
    tm = tokens.shape[0]
    gsz = N_EXPERTS // N_GROUPS
    logits = _dot3(w, tokens, _NT)
    sc = _sigmoid(logits).reshape(N_GROUPS, gsz, tm)
    bi = sc + b.reshape(N_GROUPS, gsz, 1)
    shape = (N_GROUPS, gsz, tm)
    g_io = lax.broadcasted_iota(jnp.int32, shape, 0)
    j_io = lax.broadcasted_iota(jnp.int32, shape, 1)
    e_io = g_io * gsz + j_io
    ninf = -jnp.inf
    m1 = jnp.max(bi, axis=1, keepdims=True)
    i1 = jnp.min(jnp.where(bi == m1, j_io, gsz), axis=1, keepdims=True)
    m2 = jnp.max(jnp.where(j_io == i1, ninf, bi), axis=1, keepdims=True)
    cur = jnp.broadcast_to(m1 + m2, shape)
    gsel = jnp.zeros(shape, F32)
    for _ in range(TOPK_GROUPS):
        mx = jnp.max(cur, axis=0, keepdims=True)
        ix = jnp.min(jnp.where(cur == mx, g_io, N_GROUPS), axis=0, keepdims=True)
        hit = g_io == ix
        gsel = jnp.where(hit, 1.0, gsel)
        cur = jnp.where(hit, ninf, cur)
    cur = jnp.where(gsel > 0.0, bi, ninf)
    esel = jnp.zeros(shape, F32)
    for _ in range(TOP_K):
        mx = jnp.max(jnp.max(cur, axis=0, keepdims=True), axis=1, keepdims=True)
        ix = jnp.min(jnp.min(jnp.where(cur == mx, e_io, N_EXPERTS), axis=0, keepdims=True),
                     axis=1, keepdims=True)
        hit = e_io == ix
        esel = jnp.where(hit, 1.0, esel)
        cur = jnp.where(hit, ninf, cur)
    wsel = sc * esel
    den = jnp.sum(jnp.sum(wsel, axis=0, keepdims=True), axis=1, keepdims=True)
    return (wsel / den * ROUTE_SCALE).reshape(N_EXPERTS, tm)


def _moe_kernel(x_ref, g_ref, wgu_ref, wd_ref, sgu_ref, sd_ref, o_ref):
    e = pl.program_id(1)

    def ffn(wgu, wd):
        hgu = _dot(x_ref[...], wgu)
        return _dot(_silu(hgu[:, :D_EXPERT]) * hgu[:, D_EXPERT:], wd)

    @pl.when(e == 0)
    def _():
        o_ref[...] = ffn(sgu_ref[...], sd_ref[...])

    gates = g_ref[...]
    lane = lax.broadcasted_iota(jnp.int32, gates.shape, 1)
    gcol = jnp.sum(jnp.where(lane == e, gates, 0.0), axis=1, keepdims=True)
    o_ref[...] += ffn(wgu_ref[0, 0], wd_ref[0, 0]) * gcol


def _moe_call(tok, gates, wgu, wd, layer, sgu, sd, tm):
    n, d = tok.shape
    return pl.pallas_call(
        _moe_kernel,
        grid=(n // tm, N_EXPERTS),
        in_specs=[pl.BlockSpec((tm, d), lambda i, e: (i, 0)),
                  pl.BlockSpec((tm, N_EXPERTS), lambda i, e: (i, 0)),
                  pl.BlockSpec((1, 1, d, 2 * D_EXPERT), lambda i, e: (layer, e, 0, 0)),
                  pl.BlockSpec((1, 1, D_EXPERT, d), lambda i, e: (layer, e, 0, 0)),
                  pl.BlockSpec((d, 2 * D_EXPERT), lambda i, e: (0, 0)),
                  pl.BlockSpec((D_EXPERT, d), lambda i, e: (0, 0))],
        out_specs=pl.BlockSpec((tm, d), lambda i, e: (i, 0)),
        out_shape=jax.ShapeDtypeStruct((n, d), F32),
        compiler_params=_cparams(("arbitrary", "arbitrary")),
        name="moe_experts",
    )(tok, gates, wgu, wd, sgu, sd)


def _resid_kernel(x_ref, f_ref, g_ref, mod_ref, o_ref):
    f = f_ref[...]
    y = f * lax.rsqrt(jnp.mean(f * f, axis=-1, keepdims=True) + EPS) * g_ref[...]
    o_ref[...] = x_ref[...] + mod_ref[0] * y


def _resid_call(x, f, g, mod, tpb, n_batch):
    n, d = x.shape
    kind = lambda i: (jnp.where(i % tpb == 0, n_batch, i // tpb), 0, 0)
    row = pl.BlockSpec((TILE, d), lambda i: (i, 0))
    return pl.pallas_call(
        _resid_kernel,
        grid=(n // TILE,),
        in_specs=[row, row, pl.BlockSpec((1, d), lambda i: (0, 0)), pl.BlockSpec((1, 1, d), kind)],
        out_specs=row,
        out_shape=jax.ShapeDtypeStruct((n, d), F32),
        compiler_params=_cparams(("arbitrary",)),
        name="moe_residual",
    )(x, f, g.reshape(1, d), mod)


def _partner(dim, nblocks):
    q = dim // 4
    base = np.concatenate([np.arange(q) + q, np.arange(q), np.arange(q) + 3 * q, np.arange(q) + 2 * q])
    return (np.arange(nblocks)[:, None] * dim + base[None, :]).reshape(-1)


def _proj_columns():
    da, cv, sw = 0, 3 * BW, 5 * BW
    rw = sw + (SW_HEADS + 2 * SW_KV) * SW_DH
    gate = rw + RW_SHIFT_W + RW_G_RANK
    rep = np.repeat(np.arange(SW_KV), SW_HEADS // SW_KV)
    kv_rep = (rep[:, None] * SW_DH + np.arange(SW_DH)[None, :]).reshape(-1)
    da_q = da + np.arange(BW)
    da_k = da + BW + np.arange(BW)
    da_v = da + 2 * BW + np.arange(BW)
    sw_q = sw + np.arange(BW)
    sw_k = sw + BW + kv_rep
    sw_v = sw + BW + SW_KV * SW_DH + kv_rep
    p32 = _partner(DA_HALF, BW // DA_HALF)
    p64 = _partner(SW_DH, BW // SW_DH)
    cols = np.concatenate([
        da_q, da_k, sw_q, sw_k,
        da_q[p32], da_k[p32], sw_q[p64], sw_k[p64],
        da_v, sw_v,
        cv + np.arange(2 * BW),
        rw + np.arange(RW_SHIFT_W + RW_G_RANK),
        gate + np.arange(N_BRANCH * 4 * BW)])
    assert cols.shape[0] == C_TOTAL
    return cols


def _rope_tables(seq):
    rows = seq // GRID_W
    row = jnp.repeat(jnp.arange(rows, dtype=F32), GRID_W)
    colp = jnp.tile(jnp.arange(GRID_W, dtype=F32), rows)

    def tables(dim, reps):
        q = dim // 4
        freqs = ROPE_BASE ** (-jnp.arange(q, dtype=F32) / q)
        ar, ac = row[:, None] * freqs, colp[:, None] * freqs
        cos = jnp.concatenate([jnp.cos(ar), jnp.cos(ar), jnp.cos(ac), jnp.cos(ac)], axis=1)
        sin = jnp.concatenate([-jnp.sin(ar), jnp.sin(ar), -jnp.sin(ac), jnp.sin(ac)], axis=1)
        return jnp.tile(cos, (1, reps)), jnp.tile(sin, (1, reps))

    c32, s32 = tables(DA_HALF, 2 * BW // DA_HALF)
    c64, s64 = tables(SW_DH, 2 * BW // SW_DH)
    cos = jnp.concatenate([c32, c64], axis=1)
    sin = jnp.concatenate([s32, s64], axis=1)
    w = cos.shape[1]
    qscale = jnp.where(jnp.arange(w) < BW, (DA_HALF ** -0.5) * math.log2(math.e), 1.0)
    cos = jnp.concatenate([jnp.ones((TILE, w), F32), cos], axis=0) * qscale
    sin = jnp.concatenate([jnp.zeros((TILE, w), F32), sin], axis=0) * qscale
    return cos, sin


def _moe_tile(rows_b):
    best = TILE
    for t in range(TILE, MOE_MAX_TILE + 1, 16):
        if rows_b % t == 0:
            best = t
    return best


def kernel(x, c, ctx, c_ctx, ada_w, ada_b, norm_g, w_in, w_branch, w_out, da_lambda, da_subln,
           cv_w, cv_b, cv_ln, sw_sink, rw_mu, rw_w0, rw_w_up, rw_a0, rw_a_up, rw_kk, rw_ka,
           rw_g_up, rw_rk, rw_gn, router_w, router_b, ex_w_gu, ex_w_down, sh_w_gu, sh_w_down):
    n_batch, seq, d = x.shape
    ctx_len = ctx.shape[1]
    depth = w_in.shape[0]
    assert ctx_len == TILE and seq % TILE == 0 and seq % GRID_W == 0
    assert n_batch + 1 <= 8
    rows_b = ctx_len + seq
    tpb = rows_b // TILE
    n = n_batch * rows_b

    xs = jnp.concatenate([ctx, x], axis=1).reshape(n, d)
    cond = jnp.zeros((8, d), F32).at[:n_batch].set(c).at[n_batch].set(c_ctx)
    mods = _ada_call(cond, ada_w, ada_b)[:, :n_batch + 1]
    cos_t, sin_t = _rope_tables(seq)
    cols = _proj_columns()
    hio = np.arange(BW) // RW_DH
    bd = jnp.asarray(hio[:, None] == hio[None, :], BF16)

    for l in range(depth):
        mod = [mods[l, :, i * d:(i + 1) * d].reshape(n_batch + 1, 1, d) for i in range(6)]
        ng = norm_g[l]
        lam_init = 0.8 - 0.6 * math.exp(-0.3 * l)
        lv = da_lambda[l]
        lam = (jnp.exp(jnp.sum(lv[0] * lv[1])) - jnp.exp(jnp.sum(lv[2] * lv[3])) + lam_init)
        lp = {'rw_mu': rw_mu[l], 'rw_w0': rw_w0[l], 'rw_w_up': rw_w_up[l], 'rw_a0': rw_a0[l],
              'rw_a_up': rw_a_up[l], 'rw_kk': rw_kk[l], 'rw_ka': rw_ka[l], 'rw_rk': rw_rk[l]}

        h = _norm_mod_call(xs, ng[0], mod[0], mod[1], tpb, n_batch)
        u = _proj_call(h, w_in[l][:, cols].astype(BF16))
        qk, vt = _rope_call(u, cos_t, sin_t, tpb)
        oa = _da_call(lam.reshape(1), qk, vt, jnp.tile(da_subln[l], DA_HEADS).reshape(1, BW),
                      n_batch, tpb, lam_init)
        ob = _conv_call(u, cv_w[l], cv_b[l], cv_ln[l], tpb)
        oc = _sw_call(sw_sink[l], qk, u, n_batch, tpb)
        cm, cn, cq, cy0, bonus = _rw_chunk_call(u, lp, bd, tpb)
        yscan = _rw_scan_call(cm, cn, cq, cy0, n_batch, tpb)
        od = _rw_out_call(yscan, bonus, u, rw_g_up[l].astype(BF16), rw_gn[l], bd)
        xs, tok, gates_t = _merge_call(
            (oa, ob, oc, od), u, w_branch[l].astype(BF16), w_out[l].astype(BF16), xs, ng[1],
            mod[2], ng[2], mod[3], mod[4], router_w[l].T, router_b[l], tpb, n_batch)
        f = _moe_call(tok, gates_t.T, ex_w_gu, ex_w_down, l, sh_w_gu[l].astype(BF16),
                      sh_w_down[l].astype(BF16), _moe_tile(rows_b))
        xs = _resid_call(xs, f, ng[3], mod[5], tpb, n_batch)

    return xs.reshape(n_batch, rows_b, d)[:, ctx_len:]
```

```python
import functools
import math

import numpy as np
import jax
import jax.numpy as jnp
from jax import lax
from jax.experimental import pallas as pl
from jax.experimental.pallas import tpu as pltpu

F32 = jnp.float32
BF16 = jnp.bfloat16

GRID_W = 64
EPS = 1e-6
ROPE_BASE = 10000.0
N_BRANCH = 4
BW = 256
DA_HEADS = 4
DA_HALF = 32
SW_HEADS = 4
SW_KV = 2
SW_DH = 64
WINDOW = 128
CONV_W = 31
RW_HEADS = 4
RW_DH = 64
RW_DECAY_RANK = 64
RW_A_RANK = 64
RW_G_RANK = 128
RW_DECAY_SCALE = math.exp(-0.5)
RW_GN_EPS = 64e-5
RW_SHIFT_W = 3 * BW + RW_DECAY_RANK + RW_A_RANK
N_EXPERTS = 64
TOP_K = 6
N_GROUPS = 8
TOPK_GROUPS = 4
D_EXPERT = 256
ROUTE_SCALE = 2.5

TILE = 256
CHUNK = 128
DA_KT = 1408
MOE_MAX_TILE = 2112
DA_VROWS = 80
NEG = -1e30

C_ROPE = 0
C_SWAP = 1024
C_DAV = 2048
C_SWV = 2304
C_CV = 2560
C_RW = 3072
C_GATE = 4096
C_TOTAL = 8192

VMEM_LIMIT = 48 * 1024 * 1024

_NT = (((1,), (1,)), ((), ()))
_NN = (((1,), (0,)), ((), ()))


def _cparams(sem):
    return pltpu.CompilerParams(dimension_semantics=sem, vmem_limit_bytes=VMEM_LIMIT)


def _dot(a, b, dims=_NN):
    return lax.dot_general(a.astype(BF16), b.astype(BF16), dims, preferred_element_type=F32)


def _split2(x):
    hi = x.astype(BF16)
    lo = (x - hi.astype(F32)).astype(BF16)
    return hi, lo


def _dot3(a, b, dims=_NN):
    ah, al = _split2(a)
    bh, bl = _split2(b)
    dg = lambda x, y: lax.dot_general(x, y, dims, preferred_element_type=F32)
    return dg(ah, bh) + (dg(ah, bl) + dg(al, bh))


def _dot_sel(x, sel, dims=_NN):
    h0 = x.astype(BF16)
    r1 = x - h0.astype(F32)
    h1 = r1.astype(BF16)
    h2 = (r1 - h1.astype(F32)).astype(BF16)
    dg = lambda y: lax.dot_general(y, sel, dims, preferred_element_type=F32)
    return dg(h0) + (dg(h1) + dg(h2))


def _sel_dot(sel, x):
    h0 = x.astype(BF16)
    r1 = x - h0.astype(F32)
    h1 = r1.astype(BF16)
    h2 = (r1 - h1.astype(F32)).astype(BF16)
    dg = lambda y: lax.dot_general(sel, y, _NN, preferred_element_type=F32)
    return dg(h0) + (dg(h1) + dg(h2))


def _sigmoid(x):
    return jax.nn.sigmoid(x)


def _silu(x):
    return x * jax.nn.sigmoid(x)


def _head_of_lane(shape, width):
    return lax.broadcasted_iota(jnp.int32, shape, len(shape) - 1) // width


def _ada_kernel(s_ref, w_ref, b_ref, o_ref):
    s = _silu(s_ref[...])
    o_ref[0] = _dot(s, w_ref[0]) + b_ref[0]


def _ada_call(cond, ada_w, ada_b):
    depth, d, cols = ada_w.shape
    tn = 1536
    return pl.pallas_call(
        _ada_kernel,
        grid=(depth, cols // tn),
        in_specs=[pl.BlockSpec((8, d), lambda l, j: (0, 0)),
                  pl.BlockSpec((1, d, tn), lambda l, j: (l, 0, j)),
                  pl.BlockSpec((1, 1, tn), lambda l, j: (l, 0, j))],
        out_specs=pl.BlockSpec((1, 8, tn), lambda l, j: (l, 0, j)),
        out_shape=jax.ShapeDtypeStruct((depth, 8, cols), F32),
        compiler_params=_cparams(("arbitrary", "arbitrary")),
        name="ada_mod",
    )(cond, ada_w, ada_b.reshape(depth, 1, cols))


def _norm_mod_kernel(x_ref, g_ref, sh_ref, sc_ref, o_ref):
    x = x_ref[...]
    y = x * lax.rsqrt(jnp.mean(x * x, axis=-1, keepdims=True) + EPS) * g_ref[...]
    o_ref[...] = (y * (1.0 + sc_ref[0]) + sh_ref[0]).astype(BF16)


def _norm_mod_call(x, g, shift, scale, tpb, n_batch):
    n, d = x.shape
    kind = lambda i: (jnp.where(i % tpb == 0, n_batch, i // tpb), 0, 0)
    row = pl.BlockSpec((TILE, d), lambda i: (i, 0))
    return pl.pallas_call(
        _norm_mod_kernel,
        grid=(n // TILE,),
        in_specs=[row, pl.BlockSpec((1, d), lambda i: (0, 0)),
                  pl.BlockSpec((1, 1, d), kind), pl.BlockSpec((1, 1, d), kind)],
        out_specs=row,
        out_shape=jax.ShapeDtypeStruct((n, d), BF16),
        compiler_params=_cparams(("arbitrary",)),
        name="norm_mod",
    )(x, g.reshape(1, d), shift, scale)


def _mm_kernel(a_ref, w_ref, o_ref):
    o_ref[...] = jnp.dot(a_ref[...], w_ref[...], preferred_element_type=F32)


def _proj_call(h, w):
    n, d = h.shape
    cols = w.shape[1]
    tm, tn = 2 * TILE, 4096
    return pl.pallas_call(
        _mm_kernel,
        grid=(cols // tn, n // tm),
        in_specs=[pl.BlockSpec((tm, d), lambda j, i: (i, 0)),
                  pl.BlockSpec((d, tn), lambda j, i: (0, j))],
        out_specs=pl.BlockSpec((tm, tn), lambda j, i: (i, j)),
        out_shape=jax.ShapeDtypeStruct((n, cols), F32),
        compiler_params=_cparams(("arbitrary", "arbitrary")),
        name="in_proj",
    )(h, w)


def _rope_kernel(um_ref, us_ref, c_ref, s_ref, v_ref, v2_ref, o_ref, vt_ref, vt2_ref):
    o_ref[...] = (um_ref[...] * c_ref[...] + us_ref[...] * s_ref[...]).astype(BF16)
    vt2_ref[...] = v2_ref[...].T.astype(BF16)
    vt = v_ref[...].T.astype(BF16)
    hd = 2 * DA_HALF
    for h in range(DA_HEADS):
        vt_ref[h * DA_VROWS:h * DA_VROWS + hd, :] = vt[h * hd:(h + 1) * hd, :]
        vt_ref[h * DA_VROWS + hd:(h + 1) * DA_VROWS, :] = jnp.ones((DA_VROWS - hd, TILE), BF16)


def _rope_call(u, cos_t, sin_t, tpb):
    n = u.shape[0]
    w = cos_t.shape[1]
    tab = pl.BlockSpec((TILE, w), lambda i: (i % tpb, 0))
    return pl.pallas_call(
        _rope_kernel,
        grid=(n // TILE,),
        in_specs=[pl.BlockSpec((TILE, w), lambda i: (i, C_ROPE // w)),
                  pl.BlockSpec((TILE, w), lambda i: (i, C_SWAP // w)), tab, tab,
                  pl.BlockSpec((TILE, BW), lambda i: (i, C_DAV // BW)),
                  pl.BlockSpec((TILE, BW), lambda i: (i, C_SWV // BW))],
        out_specs=[pl.BlockSpec((TILE, w), lambda i: (i, 0)),
                   pl.BlockSpec((DA_HEADS * DA_VROWS, TILE), lambda i: (i // tpb, i % tpb)),
                   pl.BlockSpec((BW, TILE), lambda i: (i // tpb, i % tpb))],
        out_shape=[jax.ShapeDtypeStruct((n, w), BF16),
                   jax.ShapeDtypeStruct((n // (tpb * TILE) * DA_HEADS * DA_VROWS, tpb * TILE),
                                        BF16),
                   jax.ShapeDtypeStruct((n // (tpb * TILE) * BW, tpb * TILE), BF16)],
        compiler_params=_cparams(("arbitrary",)),
        name="rope",
    )(u, u, cos_t, sin_t, u, u)


def _da_kernel(lam_ref, q_ref, k_ref, vt_ref, g_ref, o_ref,
               qs_ref, m_ref, acc_ref, *, nkt, lam_init):
    i = pl.program_id(1)
    tq = q_ref.shape[0]
    hd = 2 * DA_HALF
    q = q_ref[...]
    qmap = _head_of_lane((tq, BW), DA_HALF)
    for g in range(2 * DA_HEADS):
        qs_ref[g] = jnp.where(qmap == g, q, jnp.zeros_like(q))
    m_ref[...] = jnp.full(m_ref.shape, NEG, F32)
    acc_ref[...] = jnp.zeros(acc_ref.shape, F32)

    def tile(off, size):
        kt = k_ref[pl.ds(off, size), :]
        groups = range(2 * DA_HEADS)
        ss = [lax.dot_general(kt, qs_ref[g], _NT, preferred_element_type=F32) for g in groups]
        for g in groups:
            s = ss[g]
            m_old = m_ref[g]
            m_new = jnp.maximum(m_old, jnp.max(s, axis=0, keepdims=True))
            alpha = jnp.exp2(m_old - m_new)[0:1, :]
            p = jnp.exp2(s - m_new[0:1, :]).astype(BF16)
            m_ref[g] = m_new
            h, mm = g // 2, g % 2
            rows = slice(h * DA_VROWS, (h + 1) * DA_VROWS)
            pv = jnp.dot(vt_ref[rows, pl.ds(off, size)], p, preferred_element_type=F32)
            acc_ref[mm, rows, :] = acc_ref[mm, rows, :] * alpha + pv

    @pl.when(i == 0)
    def _():
        tile(0, TILE)

    @pl.when(i > 0)
    def _():
        def body(j, carry):
            tile(pl.multiple_of(j * DA_KT, DA_KT), DA_KT)
            return carry

        lax.fori_loop(0, nkt, body, 0)

    lam = lam_ref[0]
    parts = []
    for h in range(DA_HEADS):
        rows = slice(h * DA_VROWS, h * DA_VROWS + hd)
        den = slice(h * DA_VROWS + hd, h * DA_VROWS + hd + 1)
        o_h = (acc_ref[0, rows, :] / acc_ref[0, den, :]
               - lam * (acc_ref[1, rows, :] / acc_ref[1, den, :]))
        ms = jnp.mean(o_h * o_h, axis=0, keepdims=True)
        parts.append(o_h * lax.rsqrt(ms + EPS))
    y = jnp.concatenate(parts, axis=0).T * g_ref[...]
    o_ref[...] = y * (1.0 - lam_init)


def _da_call(lam, qk, vt, subln, n_batch, tpb, lam_init):
    n = qk.shape[0]
    rows_b = tpb * TILE
    assert rows_b % DA_KT == 0
    vrows = DA_HEADS * DA_VROWS
    kern = functools.partial(_da_kernel, nkt=rows_b // DA_KT, lam_init=lam_init)
    return pl.pallas_call(
        kern,
        grid=(n_batch, tpb),
        in_specs=[pl.BlockSpec(memory_space=pltpu.SMEM),
                  pl.BlockSpec((TILE, BW), lambda b, i: (b * tpb + i, 0)),
                  pl.BlockSpec((rows_b, BW), lambda b, i: (b, 1)),
                  pl.BlockSpec((vrows, rows_b), lambda b, i: (b, 0)),
                  pl.BlockSpec((1, BW), lambda b, i: (0, 0))],
        out_specs=pl.BlockSpec((TILE, BW), lambda b, i: (b * tpb + i, 0)),
        out_shape=jax.ShapeDtypeStruct((n, BW), F32),
        scratch_shapes=[pltpu.VMEM((2 * DA_HEADS, TILE, BW), BF16),
                        pltpu.VMEM((2 * DA_HEADS, 8, TILE), F32),
                        pltpu.VMEM((2, vrows, TILE), F32)],
        compiler_params=_cparams(("arbitrary", "arbitrary")),
        name="diff_attn",
    )(lam, qk, qk, vt, subln)


def _sw_kernel(sink_ref, bias_ref, q_ref, kp_ref, ko_ref, kn_ref, kc_ref,
               vp_ref, vo_ref, vn_ref, vc_ref, o_ref):
    tq = q_ref.shape[0]
    q = q_ref[...]
    kk = jnp.concatenate([kp_ref[...], ko_ref[...], kn_ref[...], kc_ref[...]], axis=0)
    vt = jnp.concatenate([vp_ref[...], vo_ref[...], vn_ref[...], vc_ref[...]], axis=1)
    bias = bias_ref[0]
    qhead = _head_of_lane((tq, BW), SW_DH)
    parts = []
    for h in range(SW_HEADS):
        qm = jnp.where(qhead == h, q, jnp.zeros_like(q))
        s = lax.dot_general(kk, qm, _NT, preferred_element_type=F32) * (SW_DH ** -0.5) + bias
        sk = sink_ref[h]
        m = jnp.maximum(jnp.max(s, axis=0, keepdims=True), sk)
        p = jnp.exp(s - m)
        den = jnp.sum(p, axis=0, keepdims=True) + jnp.exp(sk - m)
        parts.append(jnp.dot(vt[h * SW_DH:(h + 1) * SW_DH, :], (p / den).astype(BF16),
                             preferred_element_type=F32))
    o_ref[...] = jnp.concatenate(parts, axis=0).T


def _sw_bias():
    r = np.arange(CHUNK)[:, None]
    c = np.arange(CHUNK)[None, :]
    prev_ok, next_ok = c >= r, c <= r
    yes, no = np.ones((CHUNK, CHUNK), bool), np.zeros((CHUNK, CHUNK), bool)
    ctx = np.ones((CHUNK, TILE), bool)
    kinds = [np.concatenate([no, no, no, ctx], axis=1)]
    for no_next in (False, True):
        for no_prev in (False, True):
            kinds.append(np.concatenate([no if no_prev else prev_ok, yes,
                                         no if no_next else next_ok, ctx], axis=1))
    vis = np.stack(kinds).transpose(0, 2, 1)
    return jnp.asarray(np.where(vis, 0.0, NEG), F32)


def _sw_call(sink, qk, vt, n_batch, tpb):
    n = qk.shape[0]
    cpb = tpb * (TILE // CHUNK)
    ctx_chunks = TILE // CHUNK
    bias = _sw_bias()

    def kind(b, j):
        lat = 1 + (j == ctx_chunks).astype(jnp.int32) + 2 * (j == cpb - 1).astype(jnp.int32)
        return (jnp.where(j < ctx_chunks, 0, lat), 0, 0)

    cur = lambda j: j
    prv = lambda j: jnp.maximum(j - 1, 0)
    nxt = lambda j: jnp.minimum(j + 1, cpb - 1)
    kcol = 3
    kblk = lambda f: pl.BlockSpec((CHUNK, BW), lambda b, j: (b * cpb + f(j), kcol))
    vblk = lambda f: pl.BlockSpec((BW, CHUNK), lambda b, j: (b, f(j)))
    return pl.pallas_call(
        _sw_kernel,
        grid=(n_batch, cpb),
        in_specs=[pl.BlockSpec(memory_space=pltpu.SMEM),
                  pl.BlockSpec((1,) + bias.shape[1:], kind),
                  pl.BlockSpec((CHUNK, BW), lambda b, j: (b * cpb + j, 2)),
                  kblk(prv), kblk(cur), kblk(nxt),
                  pl.BlockSpec((TILE, BW), lambda b, j: (b * tpb, kcol)),
                  vblk(prv), vblk(cur), vblk(nxt),
                  pl.BlockSpec((BW, TILE), lambda b, j: (b, 0))],
        out_specs=pl.BlockSpec((CHUNK, BW), lambda b, j: (b * cpb + j, 0)),
        out_shape=jax.ShapeDtypeStruct((n, BW), F32),
        compiler_params=_cparams(("arbitrary", "arbitrary")),
        name="window_attn",
    )(sink, bias, qk, qk, qk, qk, qk, vt, vt, vt, vt)


def _conv_kernel(prev_ref, cur_ref, next_ref, w_ref, b_ref, ln_ref, o_ref, z_ref, *, tpb):
    i = pl.program_id(0)
    pos = i % tpb
    has_prev = pos >= 2
    has_next = (pos >= 1) & (pos < tpb - 1)
    halo = prev_ref.shape[0]

    def glu(x):
        return x[:, :BW] * _sigmoid(x[:, BW:])

    zp = glu(prev_ref[...])
    zn = glu(next_ref[...])
    z_ref[0:halo, :] = jnp.where(has_prev, zp, 0.0)
    z_ref[halo:halo + TILE, :] = glu(cur_ref[...])
    z_ref[halo + TILE:2 * halo + TILE, :] = jnp.where(has_next, zn, 0.0)
    acc = jnp.zeros((TILE, BW), F32) + b_ref[...]
    pad = CONV_W // 2
    for t in range(CONV_W):
        acc = acc + z_ref[halo - pad + t:halo - pad + t + TILE, :] * w_ref[t:t + 1, :]
    mu = jnp.mean(acc, axis=-1, keepdims=True)
    xc = acc - mu
    y = xc * lax.rsqrt(jnp.mean(xc * xc, axis=-1, keepdims=True) + EPS)
    y = y * ln_ref[0:1, :] + ln_ref[1:2, :]
    o_ref[...] = _silu(y)


def _conv_call(u, w, b, ln, tpb):
    n = u.shape[0]
    halo = 16
    per = TILE // halo
    nh = n // halo
    wcv = 2 * BW
    col = C_CV // wcv
    return pl.pallas_call(
        functools.partial(_conv_kernel, tpb=tpb),
        grid=(n // TILE,),
        in_specs=[pl.BlockSpec((halo, wcv), lambda i: (jnp.maximum(i * per - 1, 0), col)),
                  pl.BlockSpec((TILE, wcv), lambda i: (i, col)),
                  pl.BlockSpec((halo, wcv), lambda i: (jnp.minimum((i + 1) * per, nh - 1), col)),
                  pl.BlockSpec((CONV_W, BW), lambda i: (0, 0)),
                  pl.BlockSpec((1, BW), lambda i: (0, 0)),
                  pl.BlockSpec((2, BW), lambda i: (0, 0))],
        out_specs=pl.BlockSpec((TILE, BW), lambda i: (i, 0)),
        out_shape=jax.ShapeDtypeStruct((n, BW), F32),
        scratch_shapes=[pltpu.VMEM((TILE + 2 * halo, BW), F32)],
        compiler_params=_cparams(("arbitrary",)),
        name="conformer_conv",
    )(u, u, u, w, b.reshape(1, BW), ln)


def _inv_unit_lower(ls, eye):
    ts = [eye + l for l in ls]
    lps = list(ls)
    step = 1
    while step < ls[0].shape[0] // 2:
        lps = [_dot(lp, lp) for lp in lps]
        ts = [t + _dot(t, lp) for t, lp in zip(ts, lps)]
        step *= 2
    return ts


def _rw_chunk_kernel(prev_ref, cur_ref, next_ref, mu_ref, w0_ref, a0_ref, kk_ref, ka_ref,
                     wup_ref, aup_ref, rk_ref, bd_ref,
                     m_ref, n_ref, q_ref, y0_ref, bonus_ref, z_ref, *, cpb):
    i = pl.program_id(0)
    jj = i % cpb
    ctx_chunks = TILE // CHUNK
    has_prev = (jj != 0) & (jj != ctx_chunks)
    has_next = (jj != ctx_chunks - 1) & (jj != cpb - 1)
    c = CHUNK
    zc = cur_ref[:, 0:RW_SHIFT_W]
    z_ref[0:8, :] = jnp.where(has_prev, prev_ref[:, 0:RW_SHIFT_W], 0.0)
    z_ref[8:8 + c, :] = zc
    z_ref[8 + c:16 + c, :] = jnp.where(has_next, next_ref[:, 0:RW_SHIFT_W], 0.0)
    bd = bd_ref[...]
    t_io = lax.broadcasted_iota(jnp.int32, (c, c), 0)
    s_io = lax.broadcasted_iota(jnp.int32, (c, c), 1)
    eye = jnp.where(t_io == s_io, 1.0, 0.0)
    lane_head = _head_of_lane((c, BW), RW_DH)
    row = lax.broadcasted_iota(jnp.int32, (BW, BW), 0)
    colm = lax.broadcasted_iota(jnp.int32, (BW, BW), 1)
    same_head = (row // RW_DH) == (colm // RW_DH)

    def stack(x):
        return jnp.concatenate(
            [jnp.where(lane_head == h, x, 0.0) for h in range(RW_HEADS)], axis=0)

    cat = lambda xs: jnp.concatenate(xs, axis=1)

    prep = []
    for d in range(2):
        zsh = z_ref[7:7 + c, :] if d == 0 else z_ref[9:9 + c, :]
        zs = zc + (zsh - zc) * mu_ref[d]
        r = zs[:, 0:BW]
        k = zs[:, BW:2 * BW]
        v = zs[:, 2 * BW:3 * BW]
        wa = zs[:, 3 * BW:RW_SHIFT_W]
        lw = -RW_DECAY_SCALE * _sigmoid(w0_ref[d] + _dot(jnp.tanh(wa), wup_ref[d]))
        a = _sigmoid(a0_ref[d] + _dot(wa, aup_ref[d]))
        kappa = k * kk_ref[d]
        kh = kappa * lax.rsqrt(jnp.maximum(_dot_sel(kappa * kappa, bd), 1e-12))
        kt = k * (1.0 + (a - 1.0) * ka_ref[d])
        akh = a * kh
        bonus_ref[d] = _dot_sel(r * kt * rk_ref[...], bd) * v

        ahead = (t_io - s_io) if d == 0 else (s_io - t_io)
        earlier = ahead > 0
        upto = ahead >= 0
        tri = jnp.where(upto, 1.0, 0.0).astype(BF16)
        cl = _sel_dot(tri, lw)
        tot = jnp.sum(lw, axis=0, keepdims=True)
        rho = 0.5 * tot
        cle = cl - lw
        a_true = -kh * jnp.exp(cle)
        r_true = r * jnp.exp(cl)
        a_c = -kh * jnp.exp(cle - rho)
        r_c = r * jnp.exp(cl - rho)
        b_c = akh * jnp.exp(rho - cl)
        k_c = kt * jnp.exp(rho - cl)
        b_end = akh * jnp.exp(tot - cl)
        k_end = kt * jnp.exp(tot - cl)

        pair = _dot(jnp.concatenate([stack(a_c), stack(r_c)], axis=0),
                    jnp.concatenate([b_c, k_c], axis=0), _NT)
        l_ab, l_ak, a_rb, a_rk = [], [], [], []
        for h in range(RW_HEADS):
            blk_a = pair[h * c:(h + 1) * c]
            blk_r = pair[(RW_HEADS + h) * c:(RW_HEADS + h + 1) * c]
            l_ab.append(jnp.where(earlier, blk_a[:, 0:c], 0.0))
            l_ak.append(jnp.where(earlier, blk_a[:, c:2 * c], 0.0))
            a_rb.append(jnp.where(upto, blk_r[:, 0:c], 0.0))
            a_rk.append(jnp.where(upto, blk_r[:, c:2 * c], 0.0))
        prep.append((l_ab, cat(l_ak), cat(a_rb), cat(a_rk), v, a_true, r_true, b_end, k_end, tot))

    t_all = _inv_unit_lower(prep[0][0] + prep[1][0], eye)

    for d in range(2):
        _, lak, arb, ark, v, a_true, r_true, b_end, k_end, tot = prep[d]
        t_inv = cat(t_all[d * RW_HEADS:(d + 1) * RW_HEADS])
        lv = _dot(lak, stack(v))
        w12 = _dot(t_inv, jnp.concatenate([stack(a_true), stack(lv)], axis=1))
        w1 = w12[:, 0:BW]
        w2 = w12[:, BW:2 * BW]
        q_ref[d, 0] = r_true + _dot(arb, stack(w1))
        y0_ref[d, 0] = _dot(arb, stack(w2)) + _dot(ark, stack(v))
        b_end_t = b_end.T
        decay = jnp.where(row == colm, jnp.broadcast_to(jnp.exp(tot), (BW, BW)), 0.0)
        m_ref[d, 0] = jnp.where(same_head, _dot(b_end_t, w1), 0.0) + decay
        n_ref[d, 0] = jnp.where(same_head, _dot(b_end_t, w2) + _dot(k_end.T, v), 0.0)


def _rw_chunk_call(u, lp, bd, tpb):
    n = u.shape[0]
    cpb = tpb * (TILE // CHUNK)
    nch = n // CHUNK
    per = CHUNK // 8
    n8 = n // 8
    wrw = 1024
    col = C_RW // wrw
    pvec = lambda w: pl.BlockSpec((2, 1, w), lambda i: (0, 0, 0))
    pad = jnp.zeros((2, RW_DECAY_RANK, BW), F32)
    wup = jnp.concatenate([lp['rw_w_up'], pad], axis=1).astype(BF16)
    aup = jnp.concatenate([pad, lp['rw_a_up']], axis=1).astype(BF16)
    mat = lambda rows: pl.BlockSpec((2, 1, rows, BW), lambda i: (0, i, 0, 0))
    return pl.pallas_call(
        functools.partial(_rw_chunk_kernel, cpb=cpb),
        grid=(nch,),
        in_specs=[pl.BlockSpec((8, wrw), lambda i: (jnp.maximum(i * per - 1, 0), col)),
                  pl.BlockSpec((CHUNK, wrw), lambda i: (i, col)),
                  pl.BlockSpec((8, wrw), lambda i: (jnp.minimum((i + 1) * per, n8 - 1), col)),
                  pvec(RW_SHIFT_W), pvec(BW), pvec(BW), pvec(BW), pvec(BW),
                  pl.BlockSpec((2, 2 * RW_DECAY_RANK, BW), lambda i: (0, 0, 0)),
                  pl.BlockSpec((2, 2 * RW_A_RANK, BW), lambda i: (0, 0, 0)),
                  pl.BlockSpec((1, BW), lambda i: (0, 0)),
                  pl.BlockSpec((BW, BW), lambda i: (0, 0))],
        out_specs=[mat(BW), mat(BW), mat(CHUNK), mat(CHUNK),
                   pl.BlockSpec((2, CHUNK, BW), lambda i: (0, i, 0))],
        out_shape=[jax.ShapeDtypeStruct((2, nch, BW, BW), F32),
                   jax.ShapeDtypeStruct((2, nch, BW, BW), F32),
                   jax.ShapeDtypeStruct((2, nch, CHUNK, BW), F32),
                   jax.ShapeDtypeStruct((2, nch, CHUNK, BW), F32),
                   jax.ShapeDtypeStruct((2, n, BW), F32)],
        scratch_shapes=[pltpu.VMEM((CHUNK + 16, RW_SHIFT_W), F32)],
        compiler_params=_cparams(("arbitrary",)),
        name="rwkv_chunk",
    )(u, u, u, lp['rw_mu'].reshape(2, 1, RW_SHIFT_W), lp['rw_w0'].reshape(2, 1, BW),
      lp['rw_a0'].reshape(2, 1, BW), lp['rw_kk'].reshape(2, 1, BW), lp['rw_ka'].reshape(2, 1, BW),
      wup, aup, lp['rw_rk'].reshape(1, BW), bd)


def _rw_scan_kernel(*refs, n_batch):
    m_refs, n_refs, q_refs, y0_refs = refs[0:2], refs[2:4], refs[4:6], refs[6:8]
    y_refs, x_ref = refs[8:10], refs[10]

    @pl.when(pl.program_id(0) == 0)
    def _():
        x_ref[...] = jnp.zeros(x_ref.shape, F32)

    for d in range(2):
        for b in range(n_batch):
            x = x_ref[d, b]
            y_refs[d][b, 0] = _dot3(q_refs[d][0, b, 0], x) + y0_refs[d][0, b, 0]
            x_ref[d, b] = _dot3(m_refs[d][0, b, 0], x) + n_refs[d][0, b, 0]


def _rw_scan_call(m, nn, q, y0, n_batch, tpb):
    cpb = tpb * (TILE // CHUNK)
    ctx_chunks = TILE // CHUNK

    def chunk(d, i):
        rev = jnp.where(i < ctx_chunks, ctx_chunks - 1 - i, cpb + ctx_chunks - 1 - i)
        return i if d == 0 else rev

    def mat(rows, d):
        return pl.BlockSpec((1, n_batch, 1, rows, BW), lambda i: (d, 0, chunk(d, i), 0, 0))

    ins, specs = [], []
    for arr, rows in ((m, BW), (nn, BW), (q, CHUNK), (y0, CHUNK)):
        arr = arr.reshape(2, n_batch, cpb, rows, BW)
        for d in range(2):
            ins.append(arr)
            specs.append(mat(rows, d))
    yshape = jax.ShapeDtypeStruct((n_batch, cpb, CHUNK, BW), F32)
    ys = pl.pallas_call(
        functools.partial(_rw_scan_kernel, n_batch=n_batch),
        grid=(cpb,),
        in_specs=specs,
        out_specs=[pl.BlockSpec((n_batch, 1, CHUNK, BW), lambda i, d=d: (0, chunk(d, i), 0, 0))
                   for d in range(2)],
        out_shape=[yshape, yshape],
        scratch_shapes=[pltpu.VMEM((2, n_batch, BW, BW), F32)],
        compiler_params=_cparams(("arbitrary",)),
        name="rwkv_scan",
    )(*ins)
    return [y.reshape(n_batch * cpb * CHUNK, BW) for y in ys]


def _rw_out_kernel(yf_ref, yb_ref, bonus_ref, u_ref, gup_ref, gn_ref, bd_ref, o_ref):
    bd = bd_ref[...]
    y = yf_ref[...] + yb_ref[...]
    mean = _dot_sel(y, bd) * (1.0 / RW_DH)
    yc = y - mean
    var = _dot_sel(yc * yc, bd) * (1.0 / RW_DH)
    yn = yc * lax.rsqrt(var + RW_GN_EPS) * gn_ref[0:1, :] + gn_ref[1:2, :]
    yn = yn + (bonus_ref[0] + bonus_ref[1])
    o_ref[...] = yn * _dot(_sigmoid(u_ref[...]), gup_ref[...])


def _rw_out_call(ys, bonus, u, gup, gn, bd):
    n = u.shape[0]
    gcol = (C_RW + RW_SHIFT_W) // RW_G_RANK
    both = pl.BlockSpec((2, TILE, BW), lambda i: (0, i, 0))
    one = pl.BlockSpec((TILE, BW), lambda i: (i, 0))
    return pl.pallas_call(
        _rw_out_kernel,
        grid=(n // TILE,),
        in_specs=[one, one, both, pl.BlockSpec((TILE, RW_G_RANK), lambda i: (i, gcol)),
                  pl.BlockSpec((RW_G_RANK, BW), lambda i: (0, 0)),
                  pl.BlockSpec((2, BW), lambda i: (0, 0)),
                  pl.BlockSpec((BW, BW), lambda i: (0, 0))],
        out_specs=pl.BlockSpec((TILE, BW), lambda i: (i, 0)),
        out_shape=jax.ShapeDtypeStruct((n, BW), F32),
        compiler_params=_cparams(("arbitrary",)),
        name="rwkv_readout",
    )(ys[0], ys[1], bonus, u, gup, gn, bd)


def _merge_kernel(oa_ref, ob_ref, oc_ref, od_ref, gl_ref, wb_ref, wo_ref, x_ref, g_ref, mod_ref,
                  g2_ref, sh_ref, sc_ref, rw_ref, rb_ref, o_ref, tok_ref, gate_ref):
    d = x_ref.shape[1]
    m = None
    for i, o in enumerate((oa_ref, ob_ref, oc_ref, od_ref)):
        t = _sigmoid(gl_ref[:, i * d:(i + 1) * d]) * _dot(o[...], wb_ref[i])
        m = t if m is None else m + t
    y = _dot(m, wo_ref[...])
    y = y * lax.rsqrt(jnp.mean(y * y, axis=-1, keepdims=True) + EPS) * g_ref[...]
    x = x_ref[...] + mod_ref[0] * y
    o_ref[...] = x
    t = x * lax.rsqrt(jnp.mean(x * x, axis=-1, keepdims=True) + EPS) * g2_ref[...]
    tok = t * (1.0 + sc_ref[0]) + sh_ref[0]
    tok_ref[...] = tok.astype(BF16)
    gate_ref[...] = _route(tok, rw_ref[...], rb_ref[...])


def _merge_call(outs, u, wb, wo, x, g, mod, g2, shift, scale, rw_t, rb, tpb, n_batch):
    n, d = x.shape
    kind = lambda i: (jnp.where(i % tpb == 0, n_batch, i // tpb), 0, 0)
    br = pl.BlockSpec((TILE, BW), lambda i: (i, 0))
    vec = pl.BlockSpec((1, d), lambda i: (0, 0))
    row = pl.BlockSpec((TILE, d), lambda i: (i, 0))
    wg = N_BRANCH * d
    return pl.pallas_call(
        _merge_kernel,
        grid=(n // TILE,),
        in_specs=[br, br, br, br,
                  pl.BlockSpec((TILE, wg), lambda i: (i, C_GATE // wg)),
                  pl.BlockSpec((N_BRANCH, BW, d), lambda i: (0, 0, 0)),
                  pl.BlockSpec((d, d), lambda i: (0, 0)),
                  row, vec, pl.BlockSpec((1, 1, d), kind),
                  vec, pl.BlockSpec((1, 1, d), kind), pl.BlockSpec((1, 1, d), kind),
                  pl.BlockSpec((N_EXPERTS, d), lambda i: (0, 0)),
                  pl.BlockSpec((N_EXPERTS, 1), lambda i: (0, 0))],
        out_specs=[row, row, pl.BlockSpec((N_EXPERTS, TILE), lambda i: (0, i))],
        out_shape=[jax.ShapeDtypeStruct((n, d), F32), jax.ShapeDtypeStruct((n, d), BF16),
                   jax.ShapeDtypeStruct((N_EXPERTS, n), F32)],
        compiler_params=_cparams(("arbitrary",)),
        name="merge",
    )(*outs, u, wb, wo, x, g.reshape(1, d), mod, g2.reshape(1, d), shift, scale,
      rw_t, rb.reshape(N_EXPERTS, 1))


def _route(tokens, w, b):
    tm = tokens.shape[0]
    gsz = N_EXPERTS // N_GROUPS
    logits = _dot3(w, tokens, _NT)
    sc = _sigmoid(logits).reshape(N_GROUPS, gsz, tm)
    bi = sc + b.reshape(N_GROUPS, gsz, 1)
    shape = (N_GROUPS, gsz, tm)
    g_io = lax.broadcasted_iota(jnp.int32, shape, 0)
    j_io = lax.broadcasted_iota(jnp.int32, shape, 1)
    e_io = g_io * gsz + j_io
    ninf = -jnp.inf
    m1 = jnp.max(bi, axis=1, keepdims=True)
    i1 = jnp.min(jnp.where(bi == m1, j_io, gsz), axis=1, keepdims=True)
    m2 = jnp.max(jnp.where(j_io == i1, ninf, bi), axis=1, keepdims=True)
    cur = jnp.broadcast_to(m1 + m2, shape)
    gsel = jnp.zeros(shape, F32)
    for _ in range(TOPK_GROUPS):
        mx = jnp.max(cur, axis=0, keepdims=True)
        ix = jnp.min(jnp.where(cur == mx, g_io, N_GROUPS), axis=0, keepdims=True)
        hit = g_io == ix
        gsel = jnp.where(hit, 1.0, gsel)
        cur = jnp.where(hit, ninf, cur)
    cur = jnp.where(gsel > 0.0, bi, ninf)
    esel = jnp.zeros(shape, F32)
    for _ in range(TOP_K):
        mx = jnp.max(jnp.max(cur, axis=0, keepdims=True), axis=1, keepdims=True)
        ix = jnp.min(jnp.min(jnp.where(cur == mx, e_io, N_EXPERTS), axis=0, keepdims=True),
                     axis=1, keepdims=True)
        hit = e_io == ix
        esel = jnp.where(hit, 1.0, esel)
        cur = jnp.where(hit, ninf, cur)
    wsel = sc * esel
    den = jnp.sum(jnp.sum(wsel, axis=0, keepdims=True), axis=1, keepdims=True)
    return (wsel / den * ROUTE_SCALE).reshape(N_EXPERTS, tm)


def _moe_kernel(x_ref, g_ref, wgu_ref, wd_ref, sgu_ref, sd_ref, o_ref):
    e = pl.program_id(1)

    def ffn(wgu, wd):
        hgu = _dot(x_ref[...], wgu)
        return _dot(_silu(hgu[:, :D_EXPERT]) * hgu[:, D_EXPERT:], wd)

    @pl.when(e == 0)
    def _():
        o_ref[...] = ffn(sgu_ref[...], sd_ref[...])

    gates = g_ref[...]
    lane = lax.broadcasted_iota(jnp.int32, gates.shape, 1)
    gcol = jnp.sum(jnp.where(lane == e, gates, 0.0), axis=1, keepdims=True)
    o_ref[...] += ffn(wgu_ref[0, 0], wd_ref[0, 0]) * gcol


def _moe_call(tok, gates, wgu, wd, layer, sgu, sd, tm):
    n, d = tok.shape
    return pl.pallas_call(
        _moe_kernel,
        grid=(n // tm, N_EXPERTS),
        in_specs=[pl.BlockSpec((tm, d), lambda i, e: (i, 0)),
                  pl.BlockSpec((tm, N_EXPERTS), lambda i, e: (i, 0)),
                  pl.BlockSpec((1, 1, d, 2 * D_EXPERT), lambda i, e: (layer, e, 0, 0)),
                  pl.BlockSpec((1, 1, D_EXPERT, d), lambda i, e: (layer, e, 0, 0)),
                  pl.BlockSpec((d, 2 * D_EXPERT), lambda i, e: (0, 0)),
                  pl.BlockSpec((D_EXPERT, d), lambda i, e: (0, 0))],
        out_specs=pl.BlockSpec((tm, d), lambda i, e: (i, 0)),
        out_shape=jax.ShapeDtypeStruct((n, d), F32),
        compiler_params=_cparams(("arbitrary", "arbitrary")),
        name="moe_experts",
    )(tok, gates, wgu, wd, sgu, sd)


def _resid_kernel(x_ref, f_ref, g_ref, mod_ref, o_ref):
    f = f_ref[...]
    y = f * lax.rsqrt(jnp.mean(f * f, axis=-1, keepdims=True) + EPS) * g_ref[...]
    o_ref[...] = x_ref[...] + mod_ref[0] * y


def _resid_call(x, f, g, mod, tpb, n_batch):
    n, d = x.shape
    kind = lambda i: (jnp.where(i % tpb == 0, n_batch, i // tpb), 0, 0)
    row = pl.BlockSpec((TILE, d), lambda i: (i, 0))
    return pl.pallas_call(
        _resid_kernel,
        grid=(n // TILE,),
        in_specs=[row, row, pl.BlockSpec((1, d), lambda i: (0, 0)), pl.BlockSpec((1, 1, d), kind)],
        out_specs=row,
        out_shape=jax.ShapeDtypeStruct((n, d), F32),
        compiler_params=_cparams(("arbitrary",)),
        name="moe_residual",
    )(x, f, g.reshape(1, d), mod)


def _proj_weights(w):
    w = w.astype(BF16)
    k = w.shape[0]
    da, cv, sw = 0, 3 * BW, 5 * BW
    rw = sw + (SW_HEADS + 2 * SW_KV) * SW_DH
    gate = rw + RW_SHIFT_W + RW_G_RANK

    def partner(x, dim):
        q = dim // 4
        return x.reshape(k, -1, 2, 2, q)[:, :, :, ::-1, :].reshape(k, -1)

    def per_q_head(x):
        return jnp.repeat(x.reshape(k, SW_KV, SW_DH), SW_HEADS // SW_KV, axis=1).reshape(k, -1)

    da_q, da_k, da_v = (w[:, da + i * BW:da + (i + 1) * BW] for i in range(3))
    sw_q = w[:, sw:sw + BW]
    sw_k = per_q_head(w[:, sw + BW:sw + BW + SW_KV * SW_DH])
    sw_v = per_q_head(w[:, sw + BW + SW_KV * SW_DH:rw])
    out = jnp.concatenate([
        da_q, da_k, sw_q, sw_k,
        partner(da_q, DA_HALF), partner(da_k, DA_HALF), partner(sw_q, SW_DH), partner(sw_k, SW_DH),
        da_v, sw_v, w[:, cv:sw], w[:, rw:gate], w[:, gate:]], axis=1)
    assert out.shape[1] == C_TOTAL
    return out


def _rope_tables(seq):
    rows = seq // GRID_W
    row = jnp.repeat(jnp.arange(rows, dtype=F32), GRID_W)
    colp = jnp.tile(jnp.arange(GRID_W, dtype=F32), rows)

    def tables(dim, reps):
        q = dim // 4
        freqs = ROPE_BASE ** (-jnp.arange(q, dtype=F32) / q)
        ar, ac = row[:, None] * freqs, colp[:, None] * freqs
        cos = jnp.concatenate([jnp.cos(ar), jnp.cos(ar), jnp.cos(ac), jnp.cos(ac)], axis=1)
        sin = jnp.concatenate([-jnp.sin(ar), jnp.sin(ar), -jnp.sin(ac), jnp.sin(ac)], axis=1)
        return jnp.tile(cos, (1, reps)), jnp.tile(sin, (1, reps))

    c32, s32 = tables(DA_HALF, 2 * BW // DA_HALF)
    c64, s64 = tables(SW_DH, 2 * BW // SW_DH)
    cos = jnp.concatenate([c32, c64], axis=1)
    sin = jnp.concatenate([s32, s64], axis=1)
    w = cos.shape[1]
    qscale = jnp.where(jnp.arange(w) < BW, (DA_HALF ** -0.5) * math.log2(math.e), 1.0)
    cos = jnp.concatenate([jnp.ones((TILE, w), F32), cos], axis=0) * qscale
    sin = jnp.concatenate([jnp.zeros((TILE, w), F32), sin], axis=0) * qscale
    return cos, sin


def _moe_tile(rows_b):
    best = TILE
    for t in range(TILE, MOE_MAX_TILE + 1, 16):
        if rows_b % t == 0:
            best = t
    return best


def kernel(x, c, ctx, c_ctx, ada_w, ada_b, norm_g, w_in, w_branch, w_out, da_lambda, da_subln,
           cv_w, cv_b, cv_ln, sw_sink, rw_mu, rw_w0, rw_w_up, rw_a0, rw_a_up, rw_kk, rw_ka,
           rw_g_up, rw_rk, rw_gn, router_w, router_b, ex_w_gu, ex_w_down, sh_w_gu, sh_w_down):
    n_batch, seq, d = x.shape
    ctx_len = ctx.shape[1]
    depth = w_in.shape[0]
    assert ctx_len == TILE and seq % TILE == 0 and seq % GRID_W == 0
    assert n_batch + 1 <= 8
    rows_b = ctx_len + seq
    tpb = rows_b // TILE
    n = n_batch * rows_b

    xs = jnp.concatenate([ctx, x], axis=1).reshape(n, d)
    cond = jnp.zeros((8, d), F32).at[:n_batch].set(c).at[n_batch].set(c_ctx)
    mods = _ada_call(cond, ada_w, ada_b)[:, :n_batch + 1]
    cos_t, sin_t = _rope_tables(seq)
    hio = np.arange(BW) // RW_DH
    bd = jnp.asarray(hio[:, None] == hio[None, :], BF16)

    for l in range(depth):
        mod = [mods[l, :, i * d:(i + 1) * d].reshape(n_batch + 1, 1, d) for i in range(6)]
        ng = norm_g[l]
        lam_init = 0.8 - 0.6 * math.exp(-0.3 * l)
        lv = da_lambda[l]
        lam = (jnp.exp(jnp.sum(lv[0] * lv[1])) - jnp.exp(jnp.sum(lv[2] * lv[3])) + lam_init)
        lp = {'rw_mu': rw_mu[l], 'rw_w0': rw_w0[l], 'rw_w_up': rw_w_up[l], 'rw_a0': rw_a0[l],
              'rw_a_up': rw_a_up[l], 'rw_kk': rw_kk[l], 'rw_ka': rw_ka[l], 'rw_rk': rw_rk[l]}

        h = _norm_mod_call(xs, ng[0], mod[0], mod[1], tpb, n_batch)
        u = _proj_call(h, _proj_weights(w_in[l]))
        qk, vt, sw_vt = _rope_call(u, cos_t, sin_t, tpb)
        oa = _da_call(lam.reshape(1), qk, vt, jnp.tile(da_subln[l], DA_HEADS).reshape(1, BW),
                      n_batch, tpb, lam_init)
        ob = _conv_call(u, cv_w[l], cv_b[l], cv_ln[l], tpb)
        oc = _sw_call(sw_sink[l], qk, sw_vt, n_batch, tpb)
        cm, cn, cq, cy0, bonus = _rw_chunk_call(u, lp, bd, tpb)
        yscan = _rw_scan_call(cm, cn, cq, cy0, n_batch, tpb)
        od = _rw_out_call(yscan, bonus, u, rw_g_up[l].astype(BF16), rw_gn[l], bd)
        xs, tok, gates_t = _merge_call(
            (oa, ob, oc, od), u, w_branch[l].astype(BF16), w_out[l].astype(BF16), xs, ng[1],
            mod[2], ng[2], mod[3], mod[4], router_w[l].T, router_b[l], tpb, n_batch)
        f = _moe_call(tok, gates_t.T, ex_w_gu, ex_w_down, l, sh_w_gu[l].astype(BF16),
                      sh_w_down[l].astype(BF16), _moe_tile(rows_b))
        xs = _resid_call(xs, f, ng[3], mod[5], tpb, n_batch)

    return xs.reshape(n_batch, rows_b, d)[:, ctx_len:]
```

```python
import functools
import math

import numpy as np
import jax
import jax.numpy as jnp
from jax import lax
from jax.experimental import pallas as pl
from jax.experimental.pallas import tpu as pltpu

F32 = jnp.float32
BF16 = jnp.bfloat16

GRID_W = 64
EPS = 1e-6
ROPE_BASE = 10000.0
N_BRANCH = 4
BW = 256
DA_HEADS = 4
DA_HALF = 32
SW_HEADS = 4
SW_KV = 2
SW_DH = 64
WINDOW = 128
CONV_W = 31
RW_HEADS = 4
RW_DH = 64
RW_DECAY_RANK = 64
RW_A_RANK = 64
RW_G_RANK = 128
RW_DECAY_SCALE = math.exp(-0.5)
RW_GN_EPS = 64e-5
RW_SHIFT_W = 3 * BW + RW_DECAY_RANK + RW_A_RANK
N_EXPERTS = 64
TOP_K = 6
N_GROUPS = 8
TOPK_GROUPS = 4
D_EXPERT = 256
ROUTE_SCALE = 2.5

TILE = 256
CHUNK = 128
DA_KT = 1408
MOE_MAX_TILE = 2112
DA_VROWS = 80
NEG = -1e30

C_GATE = 0
C_ROPE = 4096
C_DAV = 5120
C_SWV = 5376
C_CV = 5632
C_RW = 6144
C_TOTAL = 7168

VMEM_LIMIT = 48 * 1024 * 1024

_NT = (((1,), (1,)), ((), ()))
_NN = (((1,), (0,)), ((), ()))


def _cparams(sem):
    return pltpu.CompilerParams(dimension_semantics=sem, vmem_limit_bytes=VMEM_LIMIT)


def _dot(a, b, dims=_NN):
    return lax.dot_general(a.astype(BF16), b.astype(BF16), dims, preferred_element_type=F32)


def _split2(x):
    hi = x.astype(BF16)
    lo = (x - hi.astype(F32)).astype(BF16)
    return hi, lo


def _dot3(a, b, dims=_NN):
    ah, al = _split2(a)
    bh, bl = _split2(b)
    dg = lambda x, y: lax.dot_general(x, y, dims, preferred_element_type=F32)
    return dg(ah, bh) + (dg(ah, bl) + dg(al, bh))


def _dot_sel(x, sel, dims=_NN):
    h0 = x.astype(BF16)
    r1 = x - h0.astype(F32)
    h1 = r1.astype(BF16)
    h2 = (r1 - h1.astype(F32)).astype(BF16)
    dg = lambda y: lax.dot_general(y, sel, dims, preferred_element_type=F32)
    return dg(h0) + (dg(h1) + dg(h2))


def _sel_dot(sel, x):
    h0 = x.astype(BF16)
    r1 = x - h0.astype(F32)
    h1 = r1.astype(BF16)
    h2 = (r1 - h1.astype(F32)).astype(BF16)
    dg = lambda y: lax.dot_general(sel, y, _NN, preferred_element_type=F32)
    return dg(h0) + (dg(h1) + dg(h2))


def _sigmoid(x):
    return jax.nn.sigmoid(x)


def _silu(x):
    return x * jax.nn.sigmoid(x)


def _head_of_lane(shape, width):
    return lax.broadcasted_iota(jnp.int32, shape, len(shape) - 1) // width


def _ada_kernel(s_ref, w_ref, b_ref, o_ref):
    s = _silu(s_ref[...])
    o_ref[0] = _dot(s, w_ref[0]) + b_ref[0]


def _ada_call(cond, ada_w, ada_b):
    depth, d, cols = ada_w.shape
    tn = 1536
    return pl.pallas_call(
        _ada_kernel,
        grid=(depth, cols // tn),
        in_specs=[pl.BlockSpec((8, d), lambda l, j: (0, 0)),
                  pl.BlockSpec((1, d, tn), lambda l, j: (l, 0, j)),
                  pl.BlockSpec((1, 1, tn), lambda l, j: (l, 0, j))],
        out_specs=pl.BlockSpec((1, 8, tn), lambda l, j: (l, 0, j)),
        out_shape=jax.ShapeDtypeStruct((depth, 8, cols), F32),
        compiler_params=_cparams(("arbitrary", "arbitrary")),
        name="ada_mod",
    )(cond, ada_w, ada_b.reshape(depth, 1, cols))


def _norm_mod_kernel(x_ref, g_ref, sh_ref, sc_ref, o_ref):
    x = x_ref[...]
    y = x * lax.rsqrt(jnp.mean(x * x, axis=-1, keepdims=True) + EPS) * g_ref[...]
    o_ref[...] = (y * (1.0 + sc_ref[0]) + sh_ref[0]).astype(BF16)


def _norm_mod_call(x, g, shift, scale, tpb, n_batch):
    n, d = x.shape
    kind = lambda i: (jnp.where(i % tpb == 0, n_batch, i // tpb), 0, 0)
    row = pl.BlockSpec((TILE, d), lambda i: (i, 0))
    return pl.pallas_call(
        _norm_mod_kernel,
        grid=(n // TILE,),
        in_specs=[row, pl.BlockSpec((1, d), lambda i: (0, 0)),
                  pl.BlockSpec((1, 1, d), kind), pl.BlockSpec((1, 1, d), kind)],
        out_specs=row,
        out_shape=jax.ShapeDtypeStruct((n, d), BF16),
        compiler_params=_cparams(("arbitrary",)),
        name="norm_mod",
    )(x, g.reshape(1, d), shift, scale)


def _mm_kernel(a_ref, w_ref, o_ref):
    o_ref[...] = jnp.dot(a_ref[...], w_ref[...], preferred_element_type=F32)


def _proj_call(h, w):
    n, d = h.shape
    cols = w.shape[1]
    tm, tn = 2 * TILE, C_TOTAL // 2
    return pl.pallas_call(
        _mm_kernel,
        grid=(cols // tn, n // tm),
        in_specs=[pl.BlockSpec((tm, d), lambda j, i: (i, 0)),
                  pl.BlockSpec((d, tn), lambda j, i: (0, j))],
        out_specs=pl.BlockSpec((tm, tn), lambda j, i: (i, j)),
        out_shape=jax.ShapeDtypeStruct((n, cols), F32),
        compiler_params=_cparams(("arbitrary", "arbitrary")),
        name="in_proj",
    )(h, w)


def _rope_kernel(u_ref, c_ref, s_ref, v_ref, o_ref, vt_ref):
    lanes = 128
    w = u_ref.shape[1]
    lane = lax.broadcasted_iota(jnp.int32, (u_ref.shape[0], lanes), 1)
    for k in range(w // lanes):
        cols = slice(k * lanes, (k + 1) * lanes)
        q = (DA_HALF if k * lanes < 2 * BW else SW_DH) // 4
        x = u_ref[:, cols]
        partner = jnp.where((lane // q) % 2 == 0, pltpu.roll(x, lanes - q, 1), pltpu.roll(x, q, 1))
        o_ref[:, cols] = (x * c_ref[:, cols] + partner * s_ref[:, cols]).astype(BF16)
    vt = v_ref[...].T.astype(BF16)
    hd = 2 * DA_HALF
    for h in range(DA_HEADS):
        vt_ref[h * DA_VROWS:h * DA_VROWS + hd, :] = vt[h * hd:(h + 1) * hd, :]
        vt_ref[h * DA_VROWS + hd:(h + 1) * DA_VROWS, :] = jnp.ones((DA_VROWS - hd, TILE), BF16)


def _rope_call(u, cos_t, sin_t, tpb):
    n = u.shape[0]
    w = cos_t.shape[1]
    tab = pl.BlockSpec((TILE, w), lambda i: (i % tpb, 0))
    return pl.pallas_call(
        _rope_kernel,
        grid=(n // TILE,),
        in_specs=[pl.BlockSpec((TILE, w), lambda i: (i, C_ROPE // w)), tab, tab,
                  pl.BlockSpec((TILE, BW), lambda i: (i, C_DAV // BW))],
        out_specs=[pl.BlockSpec((TILE, w), lambda i: (i, 0)),
                   pl.BlockSpec((DA_HEADS * DA_VROWS, TILE), lambda i: (i // tpb, i % tpb))],
        out_shape=[jax.ShapeDtypeStruct((n, w), BF16),
                   jax.ShapeDtypeStruct((n // (tpb * TILE) * DA_HEADS * DA_VROWS, tpb * TILE),
                                        BF16)],
        compiler_params=_cparams(("arbitrary",)),
        name="rope",
    )(u, cos_t, sin_t, u)


def _da_kernel(lam_ref, q_ref, k_ref, vt_ref, g_ref, o_ref,
               qs_ref, m_ref, acc_ref, *, nkt, lam_init):
    i = pl.program_id(1)
    tq = q_ref.shape[0]
    hd = 2 * DA_HALF
    q = q_ref[...]
    qmap = _head_of_lane((tq, BW), DA_HALF)
    for g in range(2 * DA_HEADS):
        qs_ref[g] = jnp.where(qmap == g, q, jnp.zeros_like(q))
    m_ref[...] = jnp.full(m_ref.shape, NEG, F32)
    acc_ref[...] = jnp.zeros(acc_ref.shape, F32)

    def tile(off, size):
        kt = k_ref[pl.ds(off, size), :]
        groups = range(2 * DA_HEADS)
        ss = [lax.dot_general(kt, qs_ref[g], _NT, preferred_element_type=F32) for g in groups]
        for g in groups:
            s = ss[g]
            m_old = m_ref[g]
            m_new = jnp.maximum(m_old, jnp.max(s, axis=0, keepdims=True))
            alpha = jnp.exp2(m_old - m_new)[0:1, :]
            p = jnp.exp2(s - m_new[0:1, :]).astype(BF16)
            m_ref[g] = m_new
            h, mm = g // 2, g % 2
            rows = slice(h * DA_VROWS, (h + 1) * DA_VROWS)
            pv = jnp.dot(vt_ref[rows, pl.ds(off, size)], p, preferred_element_type=F32)
            acc_ref[mm, rows, :] = acc_ref[mm, rows, :] * alpha + pv

    @pl.when(i == 0)
    def _():
        tile(0, TILE)

    @pl.when(i > 0)
    def _():
        def body(j, carry):
            tile(pl.multiple_of(j * DA_KT, DA_KT), DA_KT)
            return carry

        lax.fori_loop(0, nkt, body, 0)

    lam = lam_ref[0]
    parts = []
    for h in range(DA_HEADS):
        rows = slice(h * DA_VROWS, h * DA_VROWS + hd)
        den = slice(h * DA_VROWS + hd, h * DA_VROWS + hd + 1)
        o_h = (acc_ref[0, rows, :] / acc_ref[0, den, :]
               - lam * (acc_ref[1, rows, :] / acc_ref[1, den, :]))
        ms = jnp.mean(o_h * o_h, axis=0, keepdims=True)
        parts.append(o_h * lax.rsqrt(ms + EPS))
    y = jnp.concatenate(parts, axis=0).T * g_ref[...]
    o_ref[...] = y * (1.0 - lam_init)


def _da_call(lam, qk, vt, subln, n_batch, tpb, lam_init):
    n = qk.shape[0]
    rows_b = tpb * TILE
    assert rows_b % DA_KT == 0
    vrows = DA_HEADS * DA_VROWS
    kern = functools.partial(_da_kernel, nkt=rows_b // DA_KT, lam_init=lam_init)
    return pl.pallas_call(
        kern,
        grid=(n_batch, tpb),
        in_specs=[pl.BlockSpec(memory_space=pltpu.SMEM),
                  pl.BlockSpec((TILE, BW), lambda b, i: (b * tpb + i, 0)),
                  pl.BlockSpec((rows_b, BW), lambda b, i: (b, 1)),
                  pl.BlockSpec((vrows, rows_b), lambda b, i: (b, 0)),
                  pl.BlockSpec((1, BW), lambda b, i: (0, 0))],
        out_specs=pl.BlockSpec((TILE, BW), lambda b, i: (b * tpb + i, 0)),
        out_shape=jax.ShapeDtypeStruct((n, BW), F32),
        scratch_shapes=[pltpu.VMEM((2 * DA_HEADS, TILE, BW), BF16),
                        pltpu.VMEM((2 * DA_HEADS, 8, TILE), F32),
                        pltpu.VMEM((2, vrows, TILE), F32)],
        compiler_params=_cparams(("arbitrary", "arbitrary")),
        name="diff_attn",
    )(lam, qk, qk, vt, subln)


def _sw_kernel(sink_ref, bias_ref, q_ref, kp_ref, ko_ref, kn_ref, kc_ref,
               vp_ref, vo_ref, vn_ref, vc_ref, o_ref):
    tq = q_ref.shape[0]
    q = q_ref[...]
    kk = jnp.concatenate([kp_ref[...], ko_ref[...], kn_ref[...], kc_ref[...]], axis=0)
    vv = jnp.concatenate([vp_ref[...], vo_ref[...], vn_ref[...], vc_ref[...]], axis=0).astype(BF16)
    nk = kk.shape[0]
    bias = bias_ref[0]
    qhead = _head_of_lane((tq, BW), SW_DH)
    vhead = _head_of_lane((nk, BW), SW_DH)
    ps, vs = [], []
    for h in range(SW_HEADS):
        qm = jnp.where(qhead == h, q, jnp.zeros_like(q))
        s = lax.dot_general(qm, kk, _NT, preferred_element_type=F32) * (SW_DH ** -0.5) + bias
        sk = sink_ref[h]
        m = jnp.maximum(jnp.max(s, axis=1, keepdims=True), sk)
        p = jnp.exp(s - m)
        den = jnp.sum(p, axis=1, keepdims=True) + jnp.exp(sk - m)
        ps.append((p / den).astype(BF16))
        vs.append(jnp.where(vhead == h, vv, jnp.zeros_like(vv)))
    o_ref[...] = jnp.dot(jnp.concatenate(ps, axis=1), jnp.concatenate(vs, axis=0),
                         preferred_element_type=F32)


def _sw_bias():
    r = np.arange(CHUNK)[:, None]
    c = np.arange(CHUNK)[None, :]
    prev_ok, next_ok = c >= r, c <= r
    yes, no = np.ones((CHUNK, CHUNK), bool), np.zeros((CHUNK, CHUNK), bool)
    ctx = np.ones((CHUNK, TILE), bool)
    kinds = [np.concatenate([no, no, no, ctx], axis=1)]
    for no_next in (False, True):
        for no_prev in (False, True):
            kinds.append(np.concatenate([no if no_prev else prev_ok, yes,
                                         no if no_next else next_ok, ctx], axis=1))
    return jnp.asarray(np.where(np.stack(kinds), 0.0, NEG), F32)


def _sw_call(sink, qk, u, n_batch, tpb):
    n = qk.shape[0]
    cpb = tpb * (TILE // CHUNK)
    ctx_chunks = TILE // CHUNK
    bias = _sw_bias()

    def kind(b, j):
        lat = 1 + (j == ctx_chunks).astype(jnp.int32) + 2 * (j == cpb - 1).astype(jnp.int32)
        return (jnp.where(j < ctx_chunks, 0, lat), 0, 0)

    cur = lambda b, j: b * cpb + j
    prv = lambda b, j: b * cpb + jnp.maximum(j - 1, 0)
    nxt = lambda b, j: b * cpb + jnp.minimum(j + 1, cpb - 1)
    kcol, vcol = 3, C_SWV // BW
    blk = lambda f, col: pl.BlockSpec((CHUNK, BW), lambda b, j: (f(b, j), col))
    ctx = lambda col: pl.BlockSpec((TILE, BW), lambda b, j: (b * tpb, col))
    return pl.pallas_call(
        _sw_kernel,
        grid=(n_batch, cpb),
        in_specs=[pl.BlockSpec(memory_space=pltpu.SMEM),
                  pl.BlockSpec((1,) + bias.shape[1:], kind),
                  blk(cur, 2), blk(prv, kcol), blk(cur, kcol), blk(nxt, kcol), ctx(kcol),
                  blk(prv, vcol), blk(cur, vcol), blk(nxt, vcol), ctx(vcol)],
        out_specs=pl.BlockSpec((CHUNK, BW), lambda b, j: (cur(b, j), 0)),
        out_shape=jax.ShapeDtypeStruct((n, BW), F32),
        compiler_params=_cparams(("arbitrary", "arbitrary")),
        name="window_attn",
    )(sink, bias, qk, qk, qk, qk, qk, u, u, u, u)


def _conv_kernel(prev_ref, cur_ref, next_ref, w_ref, b_ref, ln_ref, o_ref, z_ref, *, tpb):
    i = pl.program_id(0)
    pos = i % tpb
    has_prev = pos >= 2
    has_next = (pos >= 1) & (pos < tpb - 1)
    halo = prev_ref.shape[0]

    def glu(x):
        return x[:, :BW] * _sigmoid(x[:, BW:])

    zp = glu(prev_ref[...])
    zn = glu(next_ref[...])
    z_ref[0:halo, :] = jnp.where(has_prev, zp, 0.0)
    z_ref[halo:halo + TILE, :] = glu(cur_ref[...])
    z_ref[halo + TILE:2 * halo + TILE, :] = jnp.where(has_next, zn, 0.0)
    acc = jnp.zeros((TILE, BW), F32) + b_ref[...]
    pad = CONV_W // 2
    for t in range(CONV_W):
        acc = acc + z_ref[halo - pad + t:halo - pad + t + TILE, :] * w_ref[t:t + 1, :]
    mu = jnp.mean(acc, axis=-1, keepdims=True)
    xc = acc - mu
    y = xc * lax.rsqrt(jnp.mean(xc * xc, axis=-1, keepdims=True) + EPS)
    y = y * ln_ref[0:1, :] + ln_ref[1:2, :]
    o_ref[...] = _silu(y)


def _conv_call(u, w, b, ln, tpb):
    n = u.shape[0]
    halo = 16
    per = TILE // halo
    nh = n // halo
    wcv = 2 * BW
    col = C_CV // wcv
    return pl.pallas_call(
        functools.partial(_conv_kernel, tpb=tpb),
        grid=(n // TILE,),
        in_specs=[pl.BlockSpec((halo, wcv), lambda i: (jnp.maximum(i * per - 1, 0), col)),
                  pl.BlockSpec((TILE, wcv), lambda i: (i, col)),
                  pl.BlockSpec((halo, wcv), lambda i: (jnp.minimum((i + 1) * per, nh - 1), col)),
                  pl.BlockSpec((CONV_W, BW), lambda i: (0, 0)),
                  pl.BlockSpec((1, BW), lambda i: (0, 0)),
                  pl.BlockSpec((2, BW), lambda i: (0, 0))],
        out_specs=pl.BlockSpec((TILE, BW), lambda i: (i, 0)),
        out_shape=jax.ShapeDtypeStruct((n, BW), F32),
        scratch_shapes=[pltpu.VMEM((TILE + 2 * halo, BW), F32)],
        compiler_params=_cparams(("arbitrary",)),
        name="conformer_conv",
    )(u, u, u, w, b.reshape(1, BW), ln)


def _inv_unit_lower(ls, eye):
    ts = [eye + l for l in ls]
    lps = list(ls)
    step = 1
    while step < ls[0].shape[0] // 2:
        lps = [_dot(lp, lp) for lp in lps]
        ts = [t + _dot(t, lp) for t, lp in zip(ts, lps)]
        step *= 2
    return ts


def _rw_chunk_kernel(prev_ref, cur_ref, next_ref, mu_ref, w0_ref, a0_ref, kk_ref, ka_ref,
                     wup_ref, aup_ref, rk_ref, bd_ref,
                     m_ref, n_ref, q_ref, y0_ref, bonus_ref, z_ref, *, cpb):
    i = pl.program_id(0)
    jj = i % cpb
    ctx_chunks = TILE // CHUNK
    has_prev = (jj != 0) & (jj != ctx_chunks)
    has_next = (jj != ctx_chunks - 1) & (jj != cpb - 1)
    c = CHUNK
    zc = cur_ref[:, 0:RW_SHIFT_W]
    z_ref[0:8, :] = jnp.where(has_prev, prev_ref[:, 0:RW_SHIFT_W], 0.0)
    z_ref[8:8 + c, :] = zc
    z_ref[8 + c:16 + c, :] = jnp.where(has_next, next_ref[:, 0:RW_SHIFT_W], 0.0)
    bd = bd_ref[...]
    t_io = lax.broadcasted_iota(jnp.int32, (c, c), 0)
    s_io = lax.broadcasted_iota(jnp.int32, (c, c), 1)
    eye = jnp.where(t_io == s_io, 1.0, 0.0)
    lane_head = _head_of_lane((c, BW), RW_DH)
    row = lax.broadcasted_iota(jnp.int32, (BW, BW), 0)
    colm = lax.broadcasted_iota(jnp.int32, (BW, BW), 1)
    same_head = (row // RW_DH) == (colm // RW_DH)

    def stack(x):
        return jnp.concatenate(
            [jnp.where(lane_head == h, x, 0.0) for h in range(RW_HEADS)], axis=0)

    cat = lambda xs: jnp.concatenate(xs, axis=1)

    prep = []
    for d in range(2):
        zsh = z_ref[7:7 + c, :] if d == 0 else z_ref[9:9 + c, :]
        zs = zc + (zsh - zc) * mu_ref[d]
        r = zs[:, 0:BW]
        k = zs[:, BW:2 * BW]
        v = zs[:, 2 * BW:3 * BW]
        wa = zs[:, 3 * BW:RW_SHIFT_W]
        lw = -RW_DECAY_SCALE * _sigmoid(w0_ref[d] + _dot(jnp.tanh(wa), wup_ref[d]))
        a = _sigmoid(a0_ref[d] + _dot(wa, aup_ref[d]))
        kappa = k * kk_ref[d]
        kh = kappa * lax.rsqrt(jnp.maximum(_dot_sel(kappa * kappa, bd), 1e-12))
        kt = k * (1.0 + (a - 1.0) * ka_ref[d])
        akh = a * kh
        bonus_ref[d] = _dot_sel(r * kt * rk_ref[...], bd) * v

        ahead = (t_io - s_io) if d == 0 else (s_io - t_io)
        earlier = ahead > 0
        upto = ahead >= 0
        tri = jnp.where(upto, 1.0, 0.0).astype(BF16)
        cl = _sel_dot(tri, lw)
        tot = jnp.sum(lw, axis=0, keepdims=True)
        rho = 0.5 * tot
        cle = cl - lw
        a_true = -kh * jnp.exp(cle)
        r_true = r * jnp.exp(cl)
        a_c = -kh * jnp.exp(cle - rho)
        r_c = r * jnp.exp(cl - rho)
        b_c = akh * jnp.exp(rho - cl)
        k_c = kt * jnp.exp(rho - cl)
        b_end = akh * jnp.exp(tot - cl)
        k_end = kt * jnp.exp(tot - cl)

        pair = _dot(jnp.concatenate([stack(a_c), stack(r_c)], axis=0),
                    jnp.concatenate([b_c, k_c], axis=0), _NT)
        l_ab, l_ak, a_rb, a_rk = [], [], [], []
        for h in range(RW_HEADS):
            blk_a = pair[h * c:(h + 1) * c]
            blk_r = pair[(RW_HEADS + h) * c:(RW_HEADS + h + 1) * c]
            l_ab.append(jnp.where(earlier, blk_a[:, 0:c], 0.0))
            l_ak.append(jnp.where(earlier, blk_a[:, c:2 * c], 0.0))
            a_rb.append(jnp.where(upto, blk_r[:, 0:c], 0.0))
            a_rk.append(jnp.where(upto, blk_r[:, c:2 * c], 0.0))
        prep.append((l_ab, cat(l_ak), cat(a_rb), cat(a_rk), v, a_true, r_true, b_end, k_end, tot))

    t_all = _inv_unit_lower(prep[0][0] + prep[1][0], eye)

    for d in range(2):
        _, lak, arb, ark, v, a_true, r_true, b_end, k_end, tot = prep[d]
        t_inv = cat(t_all[d * RW_HEADS:(d + 1) * RW_HEADS])
        lv = _dot(lak, stack(v))
        w12 = _dot(t_inv, jnp.concatenate([stack(a_true), stack(lv)], axis=1))
        w1 = w12[:, 0:BW]
        w2 = w12[:, BW:2 * BW]
        q_ref[d, 0] = r_true + _dot(arb, stack(w1))
        y0_ref[d, 0] = _dot(arb, stack(w2)) + _dot(ark, stack(v))
        b_end_t = b_end.T
        decay = jnp.where(row == colm, jnp.broadcast_to(jnp.exp(tot), (BW, BW)), 0.0)
        m_ref[d, 0] = jnp.where(same_head, _dot(b_end_t, w1), 0.0) + decay
        n_ref[d, 0] = jnp.where(same_head, _dot(b_end_t, w2) + _dot(k_end.T, v), 0.0)


def _rw_chunk_call(u, lp, bd, tpb):
    n = u.shape[0]
    cpb = tpb * (TILE // CHUNK)
    nch = n // CHUNK
    per = CHUNK // 8
    n8 = n // 8
    wrw = 1024
    col = C_RW // wrw
    pvec = lambda w: pl.BlockSpec((2, 1, w), lambda i: (0, 0, 0))
    pad = jnp.zeros((2, RW_DECAY_RANK, BW), F32)
    wup = jnp.concatenate([lp['rw_w_up'], pad], axis=1).astype(BF16)
    aup = jnp.concatenate([pad, lp['rw_a_up']], axis=1).astype(BF16)
    mat = lambda rows: pl.BlockSpec((2, 1, rows, BW), lambda i: (0, i, 0, 0))
    return pl.pallas_call(
        functools.partial(_rw_chunk_kernel, cpb=cpb),
        grid=(nch,),
        in_specs=[pl.BlockSpec((8, wrw), lambda i: (jnp.maximum(i * per - 1, 0), col)),
                  pl.BlockSpec((CHUNK, wrw), lambda i: (i, col)),
                  pl.BlockSpec((8, wrw), lambda i: (jnp.minimum((i + 1) * per, n8 - 1), col)),
                  pvec(RW_SHIFT_W), pvec(BW), pvec(BW), pvec(BW), pvec(BW),
                  pl.BlockSpec((2, 2 * RW_DECAY_RANK, BW), lambda i: (0, 0, 0)),
                  pl.BlockSpec((2, 2 * RW_A_RANK, BW), lambda i: (0, 0, 0)),
                  pl.BlockSpec((1, BW), lambda i: (0, 0)),
                  pl.BlockSpec((BW, BW), lambda i: (0, 0))],
        out_specs=[mat(BW), mat(BW), mat(CHUNK), mat(CHUNK),
                   pl.BlockSpec((2, CHUNK, BW), lambda i: (0, i, 0))],
        out_shape=[jax.ShapeDtypeStruct((2, nch, BW, BW), F32),
                   jax.ShapeDtypeStruct((2, nch, BW, BW), F32),
                   jax.ShapeDtypeStruct((2, nch, CHUNK, BW), F32),
                   jax.ShapeDtypeStruct((2, nch, CHUNK, BW), F32),
                   jax.ShapeDtypeStruct((2, n, BW), F32)],
        scratch_shapes=[pltpu.VMEM((CHUNK + 16, RW_SHIFT_W), F32)],
        compiler_params=_cparams(("arbitrary",)),
        name="rwkv_chunk",
    )(u, u, u, lp['rw_mu'].reshape(2, 1, RW_SHIFT_W), lp['rw_w0'].reshape(2, 1, BW),
      lp['rw_a0'].reshape(2, 1, BW), lp['rw_kk'].reshape(2, 1, BW), lp['rw_ka'].reshape(2, 1, BW),
      wup, aup, lp['rw_rk'].reshape(1, BW), bd)


def _rw_scan_kernel(*refs, n_batch):
    m_refs, n_refs, q_refs, y0_refs = refs[0:2], refs[2:4], refs[4:6], refs[6:8]
    y_refs, x_ref = refs[8:10], refs[10]

    @pl.when(pl.program_id(0) == 0)
    def _():
        x_ref[...] = jnp.zeros(x_ref.shape, F32)

    for d in range(2):
        for b in range(n_batch):
            x = x_ref[d, b]
            y_refs[d][b, 0] = _dot3(q_refs[d][0, b, 0], x) + y0_refs[d][0, b, 0]
            x_ref[d, b] = _dot3(m_refs[d][0, b, 0], x) + n_refs[d][0, b, 0]


def _rw_scan_call(m, nn, q, y0, n_batch, tpb):
    cpb = tpb * (TILE // CHUNK)
    ctx_chunks = TILE // CHUNK

    def chunk(d, i):
        rev = jnp.where(i < ctx_chunks, ctx_chunks - 1 - i, cpb + ctx_chunks - 1 - i)
        return i if d == 0 else rev

    def mat(rows, d):
        return pl.BlockSpec((1, n_batch, 1, rows, BW), lambda i: (d, 0, chunk(d, i), 0, 0))

    ins, specs = [], []
    for arr, rows in ((m, BW), (nn, BW), (q, CHUNK), (y0, CHUNK)):
        arr = arr.reshape(2, n_batch, cpb, rows, BW)
        for d in range(2):
            ins.append(arr)
            specs.append(mat(rows, d))
    yshape = jax.ShapeDtypeStruct((n_batch, cpb, CHUNK, BW), F32)
    ys = pl.pallas_call(
        functools.partial(_rw_scan_kernel, n_batch=n_batch),
        grid=(cpb,),
        in_specs=specs,
        out_specs=[pl.BlockSpec((n_batch, 1, CHUNK, BW), lambda i, d=d: (0, chunk(d, i), 0, 0))
                   for d in range(2)],
        out_shape=[yshape, yshape],
        scratch_shapes=[pltpu.VMEM((2, n_batch, BW, BW), F32)],
        compiler_params=_cparams(("arbitrary",)),
        name="rwkv_scan",
    )(*ins)
    return [y.reshape(n_batch * cpb * CHUNK, BW) for y in ys]


def _rw_out_kernel(yf_ref, yb_ref, bonus_ref, u_ref, gup_ref, gn_ref, bd_ref, o_ref):
    bd = bd_ref[...]
    y = yf_ref[...] + yb_ref[...]
    mean = _dot_sel(y, bd) * (1.0 / RW_DH)
    yc = y - mean
    var = _dot_sel(yc * yc, bd) * (1.0 / RW_DH)
    yn = yc * lax.rsqrt(var + RW_GN_EPS) * gn_ref[0:1, :] + gn_ref[1:2, :]
    yn = yn + (bonus_ref[0] + bonus_ref[1])
    o_ref[...] = yn * _dot(_sigmoid(u_ref[...]), gup_ref[...])


def _rw_out_call(ys, bonus, u, gup, gn, bd):
    n = u.shape[0]
    gcol = (C_RW + RW_SHIFT_W) // RW_G_RANK
    both = pl.BlockSpec((2, TILE, BW), lambda i: (0, i, 0))
    one = pl.BlockSpec((TILE, BW), lambda i: (i, 0))
    return pl.pallas_call(
        _rw_out_kernel,
        grid=(n // TILE,),
        in_specs=[one, one, both, pl.BlockSpec((TILE, RW_G_RANK), lambda i: (i, gcol)),
                  pl.BlockSpec((RW_G_RANK, BW), lambda i: (0, 0)),
                  pl.BlockSpec((2, BW), lambda i: (0, 0)),
                  pl.BlockSpec((BW, BW), lambda i: (0, 0))],
        out_specs=pl.BlockSpec((TILE, BW), lambda i: (i, 0)),
        out_shape=jax.ShapeDtypeStruct((n, BW), F32),
        compiler_params=_cparams(("arbitrary",)),
        name="rwkv_readout",
    )(ys[0], ys[1], bonus, u, gup, gn, bd)


def _merge_kernel(oa_ref, ob_ref, oc_ref, od_ref, gl_ref, wb_ref, wo_ref, x_ref, g_ref, mod_ref,
                  g2_ref, sh_ref, sc_ref, rw_ref, rb_ref, o_ref, tok_ref, gate_ref):
    d = x_ref.shape[1]
    m = None
    for i, o in enumerate((oa_ref, ob_ref, oc_ref, od_ref)):
        t = _sigmoid(gl_ref[:, i * d:(i + 1) * d]) * _dot(o[...], wb_ref[i])
        m = t if m is None else m + t
    y = _dot(m, wo_ref[...])
    y = y * lax.rsqrt(jnp.mean(y * y, axis=-1, keepdims=True) + EPS) * g_ref[...]
    x = x_ref[...] + mod_ref[0] * y
    o_ref[...] = x
    t = x * lax.rsqrt(jnp.mean(x * x, axis=-1, keepdims=True) + EPS) * g2_ref[...]
    tok = t * (1.0 + sc_ref[0]) + sh_ref[0]
    tok_ref[...] = tok.astype(BF16)
    gate_ref[...] = _route(tok, rw_ref[...], rb_ref[...])


def _merge_call(outs, u, wb, wo, x, g, mod, g2, shift, scale, rw_t, rb, tpb, n_batch):
    n, d = x.shape
    kind = lambda i: (jnp.where(i % tpb == 0, n_batch, i // tpb), 0, 0)
    br = pl.BlockSpec((TILE, BW), lambda i: (i, 0))
    vec = pl.BlockSpec((1, d), lambda i: (0, 0))
    row = pl.BlockSpec((TILE, d), lambda i: (i, 0))
    wg = N_BRANCH * d
    return pl.pallas_call(
        _merge_kernel,
        grid=(n // TILE,),
        in_specs=[br, br, br, br,
                  pl.BlockSpec((TILE, wg), lambda i: (i, C_GATE // wg)),
                  pl.BlockSpec((N_BRANCH, BW, d), lambda i: (0, 0, 0)),
                  pl.BlockSpec((d, d), lambda i: (0, 0)),
                  row, vec, pl.BlockSpec((1, 1, d), kind),
                  vec, pl.BlockSpec((1, 1, d), kind), pl.BlockSpec((1, 1, d), kind),
                  pl.BlockSpec((N_EXPERTS, d), lambda i: (0, 0)),
                  pl.BlockSpec((N_EXPERTS, 1), lambda i: (0, 0))],
        out_specs=[row, row, pl.BlockSpec((N_EXPERTS, TILE), lambda i: (0, i))],
        out_shape=[jax.ShapeDtypeStruct((n, d), F32), jax.ShapeDtypeStruct((n, d), BF16),
                   jax.ShapeDtypeStruct((N_EXPERTS, n), F32)],
        compiler_params=_cparams(("arbitrary",)),
        name="merge",
    )(*outs, u, wb, wo, x, g.reshape(1, d), mod, g2.reshape(1, d), shift, scale,
      rw_t, rb.reshape(N_EXPERTS, 1))


def _route(tokens, w, b):
    tm = tokens.shape[0]
    gsz = N_EXPERTS // N_GROUPS
    logits = _dot3(w, tokens, _NT)
    sc = _sigmoid(logits).reshape(N_GROUPS, gsz, tm)
    bi = sc + b.reshape(N_GROUPS, gsz, 1)
    shape = (N_GROUPS, gsz, tm)
    g_io = lax.broadcasted_iota(jnp.int32, shape, 0)
    j_io = lax.broadcasted_iota(jnp.int32, shape, 1)
    e_io = g_io * gsz + j_io
    ninf = -jnp.inf
    m1 = jnp.max(bi, axis=1, keepdims=True)
    i1 = jnp.min(jnp.where(bi == m1, j_io, gsz), axis=1, keepdims=True)
    m2 = jnp.max(jnp.where(j_io == i1, ninf, bi), axis=1, keepdims=True)
    cur = jnp.broadcast_to(m1 + m2, shape)
    gsel = jnp.zeros(shape, F32)
    for _ in range(TOPK_GROUPS):
        mx = jnp.max(cur, axis=0, keepdims=True)
        ix = jnp.min(jnp.where(cur == mx, g_io, N_GROUPS), axis=0, keepdims=True)
        hit = g_io == ix
        gsel = jnp.where(hit, 1.0, gsel)
        cur = jnp.where(hit, ninf, cur)
    cur = jnp.where(gsel > 0.0, bi, ninf)
    esel = jnp.zeros(shape, F32)
    for _ in range(TOP_K):
        mx = jnp.max(jnp.max(cur, axis=0, keepdims=True), axis=1, keepdims=True)
        ix = jnp.min(jnp.min(jnp.where(cur == mx, e_io, N_EXPERTS), axis=0, keepdims=True),
                     axis=1, keepdims=True)
        hit = e_io == ix
        esel = jnp.where(hit, 1.0, esel)
        cur = jnp.where(hit, ninf, cur)
    wsel = sc * esel
    den = jnp.sum(jnp.sum(wsel, axis=0, keepdims=True), axis=1, keepdims=True)
    return (wsel / den * ROUTE_SCALE).reshape(N_EXPERTS, tm)


def _moe_kernel(x_ref, g_ref, wgu_ref, wd_ref, sgu_ref, sd_ref, o_ref):
    e = pl.program_id(1)

    def ffn(wgu, wd):
        hgu = _dot(x_ref[...], wgu)
        return _dot(_silu(hgu[:, :D_EXPERT]) * hgu[:, D_EXPERT:], wd)

    @pl.when(e == 0)
    def _():
        o_ref[...] = ffn(sgu_ref[...], sd_ref[...])

    gates = g_ref[...]
    lane = lax.broadcasted_iota(jnp.int32, gates.shape, 1)
    gcol = jnp.sum(jnp.where(lane == e, gates, 0.0), axis=1, keepdims=True)
    o_ref[...] += ffn(wgu_ref[0, 0], wd_ref[0, 0]) * gcol


def _moe_call(tok, gates, wgu, wd, layer, sgu, sd, tm):
    n, d = tok.shape
    return pl.pallas_call(
        _moe_kernel,
        grid=(n // tm, N_EXPERTS),
        in_specs=[pl.BlockSpec((tm, d), lambda i, e: (i, 0)),
                  pl.BlockSpec((tm, N_EXPERTS), lambda i, e: (i, 0)),
                  pl.BlockSpec((1, 1, d, 2 * D_EXPERT), lambda i, e: (layer, e, 0, 0)),
                  pl.BlockSpec((1, 1, D_EXPERT, d), lambda i, e: (layer, e, 0, 0)),
                  pl.BlockSpec((d, 2 * D_EXPERT), lambda i, e: (0, 0)),
                  pl.BlockSpec((D_EXPERT, d), lambda i, e: (0, 0))],
        out_specs=pl.BlockSpec((tm, d), lambda i, e: (i, 0)),
        out_shape=jax.ShapeDtypeStruct((n, d), F32),
        compiler_params=_cparams(("arbitrary", "arbitrary")),
        name="moe_experts",
    )(tok, gates, wgu, wd, sgu, sd)


def _resid_kernel(x_ref, f_ref, g_ref, mod_ref, o_ref):
    f = f_ref[...]
    y = f * lax.rsqrt(jnp.mean(f * f, axis=-1, keepdims=True) + EPS) * g_ref[...]
    o_ref[...] = x_ref[...] + mod_ref[0] * y


def _resid_call(x, f, g, mod, tpb, n_batch):
    n, d = x.shape
    kind = lambda i: (jnp.where(i % tpb == 0, n_batch, i // tpb), 0, 0)
    row = pl.BlockSpec((TILE, d), lambda i: (i, 0))
    return pl.pallas_call(
        _resid_kernel,
        grid=(n // TILE,),
        in_specs=[row, row, pl.BlockSpec((1, d), lambda i: (0, 0)), pl.BlockSpec((1, 1, d), kind)],
        out_specs=row,
        out_shape=jax.ShapeDtypeStruct((n, d), F32),
        compiler_params=_cparams(("arbitrary",)),
        name="moe_residual",
    )(x, f, g.reshape(1, d), mod)


def _proj_weights(w):
    w = w.astype(BF16)
    k = w.shape[0]
    da, cv, sw = 0, 3 * BW, 5 * BW
    rw = sw + (SW_HEADS + 2 * SW_KV) * SW_DH
    gate = rw + RW_SHIFT_W + RW_G_RANK

    def per_q_head(x):
        return jnp.repeat(x.reshape(k, SW_KV, SW_DH), SW_HEADS // SW_KV, axis=1).reshape(k, -1)

    da_q, da_k, da_v = (w[:, da + i * BW:da + (i + 1) * BW] for i in range(3))
    sw_q = w[:, sw:sw + BW]
    sw_k = per_q_head(w[:, sw + BW:sw + BW + SW_KV * SW_DH])
    sw_v = per_q_head(w[:, sw + BW + SW_KV * SW_DH:rw])
    out = jnp.concatenate([
        w[:, gate:], da_q, da_k, sw_q, sw_k, da_v, sw_v, w[:, cv:sw], w[:, rw:gate]], axis=1)
    assert out.shape[1] == C_TOTAL
    return out


def _rope_tables(seq):
    rows = seq // GRID_W
    row = jnp.repeat(jnp.arange(rows, dtype=F32), GRID_W)
    colp = jnp.tile(jnp.arange(GRID_W, dtype=F32), rows)

    def tables(dim, reps):
        q = dim // 4
        freqs = ROPE_BASE ** (-jnp.arange(q, dtype=F32) / q)
        ar, ac = row[:, None] * freqs, colp[:, None] * freqs
        cos = jnp.concatenate([jnp.cos(ar), jnp.cos(ar), jnp.cos(ac), jnp.cos(ac)], axis=1)
        sin = jnp.concatenate([-jnp.sin(ar), jnp.sin(ar), -jnp.sin(ac), jnp.sin(ac)], axis=1)
        return jnp.tile(cos, (1, reps)), jnp.tile(sin, (1, reps))

    c32, s32 = tables(DA_HALF, 2 * BW // DA_HALF)
    c64, s64 = tables(SW_DH, 2 * BW // SW_DH)
    cos = jnp.concatenate([c32, c64], axis=1)
    sin = jnp.concatenate([s32, s64], axis=1)
    w = cos.shape[1]
    qscale = jnp.where(jnp.arange(w) < BW, (DA_HALF ** -0.5) * math.log2(math.e), 1.0)
    cos = jnp.concatenate([jnp.ones((TILE, w), F32), cos], axis=0) * qscale
    sin = jnp.concatenate([jnp.zeros((TILE, w), F32), sin], axis=0) * qscale
    return cos, sin


def _moe_tile(rows_b):
    best = TILE
    for t in range(TILE, MOE_MAX_TILE + 1, 16):
        if rows_b % t == 0:
            best = t
    return best


def kernel(x, c, ctx, c_ctx, ada_w, ada_b, norm_g, w_in, w_branch, w_out, da_lambda, da_subln,
           cv_w, cv_b, cv_ln, sw_sink, rw_mu, rw_w0, rw_w_up, rw_a0, rw_a_up, rw_kk, rw_ka,
           rw_g_up, rw_rk, rw_gn, router_w, router_b, ex_w_gu, ex_w_down, sh_w_gu, sh_w_down):
    n_batch, seq, d = x.shape
    ctx_len = ctx.shape[1]
    depth = w_in.shape[0]
    assert ctx_len == TILE and seq % TILE == 0 and seq % GRID_W == 0
    assert n_batch + 1 <= 8
    rows_b = ctx_len + seq
    tpb = rows_b // TILE
    n = n_batch * rows_b

    xs = jnp.concatenate([ctx, x], axis=1).reshape(n, d)
    cond = jnp.zeros((8, d), F32).at[:n_batch].set(c).at[n_batch].set(c_ctx)
    mods = _ada_call(cond, ada_w, ada_b)[:, :n_batch + 1]
    cos_t, sin_t = _rope_tables(seq)
    hio = np.arange(BW) // RW_DH
    bd = jnp.asarray(hio[:, None] == hio[None, :], BF16)

    for l in range(depth):
        mod = [mods[l, :, i * d:(i + 1) * d].reshape(n_batch + 1, 1, d) for i in range(6)]
        ng = norm_g[l]
        lam_init = 0.8 - 0.6 * math.exp(-0.3 * l)
        lv = da_lambda[l]
        lam = (jnp.exp(jnp.sum(lv[0] * lv[1])) - jnp.exp(jnp.sum(lv[2] * lv[3])) + lam_init)
        lp = {'rw_mu': rw_mu[l], 'rw_w0': rw_w0[l], 'rw_w_up': rw_w_up[l], 'rw_a0': rw_a0[l],
              'rw_a_up': rw_a_up[l], 'rw_kk': rw_kk[l], 'rw_ka': rw_ka[l], 'rw_rk': rw_rk[l]}

        h = _norm_mod_call(xs, ng[0], mod[0], mod[1], tpb, n_batch)
        u = _proj_call(h, _proj_weights(w_in[l]))
        qk, vt = _rope_call(u, cos_t, sin_t, tpb)
        oa = _da_call(lam.reshape(1), qk, vt, jnp.tile(da_subln[l], DA_HEADS).reshape(1, BW),
                      n_batch, tpb, lam_init)
        ob = _conv_call(u, cv_w[l], cv_b[l], cv_ln[l], tpb)
        oc = _sw_call(sw_sink[l], qk, u, n_batch, tpb)
        cm, cn, cq, cy0, bonus = _rw_chunk_call(u, lp, bd, tpb)
        yscan = _rw_scan_call(cm, cn, cq, cy0, n_batch, tpb)
        od = _rw_out_call(yscan, bonus, u, rw_g_up[l].astype(BF16), rw_gn[l], bd)
        xs, tok, gates_t = _merge_call(
            (oa, ob, oc, od), u, w_branch[l].astype(BF16), w_out[l].astype(BF16), xs, ng[1],
            mod[2], ng[2], mod[3], mod[4], router_w[l].T, router_b[l], tpb, n_batch)
        f = _moe_call(tok, gates_t.T, ex_w_gu, ex_w_down, l, sh_w_gu[l].astype(BF16),
                      sh_w_down[l].astype(BF16), _moe_tile(rows_b))
        xs = _resid_call(xs, f, ng[3], mod[5], tpb, n_batch)

    return xs.reshape(n_batch, rows_b, d)[:, ctx_len:]
```

```python
import functools
import math

import numpy as np
import jax
import jax.numpy as jnp
from jax import lax
from jax.experimental import pallas as pl
from jax.experimental.pallas import tpu as pltpu

F32 = jnp.float32
BF16 = jnp.bfloat16

GRID_W = 64
EPS = 1e-6
ROPE_BASE = 10000.0
N_BRANCH = 4
BW = 256
DA_HEADS = 4
DA_HALF = 32
SW_HEADS = 4
SW_KV = 2
SW_DH = 64
WINDOW = 128
CONV_W = 31
RW_HEADS = 4
RW_DH = 64
RW_DECAY_RANK = 64
RW_A_RANK = 64
RW_G_RANK = 128
RW_DECAY_SCALE = math.exp(-0.5)
RW_GN_EPS = 64e-5
RW_SHIFT_W = 3 * BW + RW_DECAY_RANK + RW_A_RANK
N_EXPERTS = 64
TOP_K = 6
N_GROUPS = 8
TOPK_GROUPS = 4
D_EXPERT = 256
ROUTE_SCALE = 2.5

TILE = 256
CHUNK = 128
DA_KT = 1408
MOE_MAX_TILE = 2112
DA_VROWS = 80
NEG = -1e30

C_GATE = 0
C_ROPE = 4096
C_DAV = 5120
C_SWV = 5376
C_CV = 5632
C_RW = 6144
C_TOTAL = 7168

VMEM_LIMIT = 48 * 1024 * 1024

_NT = (((1,), (1,)), ((), ()))
_NN = (((1,), (0,)), ((), ()))


def _cparams(sem):
    return pltpu.CompilerParams(dimension_semantics=sem, vmem_limit_bytes=VMEM_LIMIT)


def _dot(a, b, dims=_NN):
    return lax.dot_general(a.astype(BF16), b.astype(BF16), dims, preferred_element_type=F32)


def _split2(x):
    hi = x.astype(BF16)
    lo = (x - hi.astype(F32)).astype(BF16)
    return hi, lo


def _dot3(a, b, dims=_NN):
    ah, al = _split2(a)
    bh, bl = _split2(b)
    dg = lambda x, y: lax.dot_general(x, y, dims, preferred_element_type=F32)
    return dg(ah, bh) + (dg(ah, bl) + dg(al, bh))


def _dot_sel(x, sel, dims=_NN):
    h0 = x.astype(BF16)
    r1 = x - h0.astype(F32)
    h1 = r1.astype(BF16)
    h2 = (r1 - h1.astype(F32)).astype(BF16)
    dg = lambda y: lax.dot_general(y, sel, dims, preferred_element_type=F32)
    return dg(h0) + (dg(h1) + dg(h2))


def _sel_dot(sel, x):
    h0 = x.astype(BF16)
    r1 = x - h0.astype(F32)
    h1 = r1.astype(BF16)
    h2 = (r1 - h1.astype(F32)).astype(BF16)
    dg = lambda y: lax.dot_general(sel, y, _NN, preferred_element_type=F32)
    return dg(h0) + (dg(h1) + dg(h2))


def _sigmoid(x):
    return jax.nn.sigmoid(x)


def _silu(x):
    return x * jax.nn.sigmoid(x)


def _head_of_lane(shape, width):
    return lax.broadcasted_iota(jnp.int32, shape, len(shape) - 1) // width


def _ada_kernel(s_ref, w_ref, b_ref, o_ref):
    s = _silu(s_ref[...])
    o_ref[0] = _dot(s, w_ref[0]) + b_ref[0]


def _ada_call(cond, ada_w, ada_b):
    depth, d, cols = ada_w.shape
    tn = 1536
    return pl.pallas_call(
        _ada_kernel,
        grid=(depth, cols // tn),
        in_specs=[pl.BlockSpec((8, d), lambda l, j: (0, 0)),
                  pl.BlockSpec((1, d, tn), lambda l, j: (l, 0, j)),
                  pl.BlockSpec((1, 1, tn), lambda l, j: (l, 0, j))],
        out_specs=pl.BlockSpec((1, 8, tn), lambda l, j: (l, 0, j)),
        out_shape=jax.ShapeDtypeStruct((depth, 8, cols), F32),
        compiler_params=_cparams(("arbitrary", "arbitrary")),
        name="ada_mod",
    )(cond, ada_w, ada_b.reshape(depth, 1, cols))


def _norm_mod_kernel(x_ref, g_ref, sh_ref, sc_ref, o_ref):
    x = x_ref[...]
    y = x * lax.rsqrt(jnp.mean(x * x, axis=-1, keepdims=True) + EPS) * g_ref[...]
    o_ref[...] = (y * (1.0 + sc_ref[0]) + sh_ref[0]).astype(BF16)


def _norm_mod_call(x, g, shift, scale, tpb, n_batch):
    n, d = x.shape
    kind = lambda i: (jnp.where(i % tpb == 0, n_batch, i // tpb), 0, 0)
    row = pl.BlockSpec((TILE, d), lambda i: (i, 0))
    return pl.pallas_call(
        _norm_mod_kernel,
        grid=(n // TILE,),
        in_specs=[row, pl.BlockSpec((1, d), lambda i: (0, 0)),
                  pl.BlockSpec((1, 1, d), kind), pl.BlockSpec((1, 1, d), kind)],
        out_specs=row,
        out_shape=jax.ShapeDtypeStruct((n, d), BF16),
        compiler_params=_cparams(("arbitrary",)),
        name="norm_mod",
    )(x, g.reshape(1, d), shift, scale)


def _mm_kernel(a_ref, w_ref, o_ref):
    o_ref[...] = jnp.dot(a_ref[...], w_ref[...], preferred_element_type=F32)


def _proj_call(h, w):
    n, d = h.shape
    cols = w.shape[1]
    tm, tn = 2 * TILE, C_TOTAL // 2
    return pl.pallas_call(
        _mm_kernel,
        grid=(cols // tn, n // tm),
        in_specs=[pl.BlockSpec((tm, d), lambda j, i: (i, 0)),
                  pl.BlockSpec((d, tn), lambda j, i: (0, j))],
        out_specs=pl.BlockSpec((tm, tn), lambda j, i: (i, j)),
        out_shape=jax.ShapeDtypeStruct((n, cols), F32),
        compiler_params=_cparams(("arbitrary", "arbitrary")),
        name="in_proj",
    )(h, w)


def _rope_kernel(u_ref, c_ref, s_ref, v_ref, o_ref, vt_ref):
    lanes = 128
    n_batch, rows, w = u_ref.shape
    lane = lax.broadcasted_iota(jnp.int32, (rows, lanes), 1)
    hd = 2 * DA_HALF
    qscale = (DA_HALF ** -0.5) * math.log2(math.e)
    for b in range(n_batch):
        for k in range(w // lanes):
            cols = slice(k * lanes, (k + 1) * lanes)
            is_da = k * lanes < 2 * BW
            q = (DA_HALF if is_da else SW_DH) // 4
            tab = slice(0, lanes) if is_da else slice(lanes, 2 * lanes)
            x = u_ref[b, :, cols]
            partner = jnp.where((lane // q) % 2 == 0, pltpu.roll(x, lanes - q, 1),
                                pltpu.roll(x, q, 1))
            y = x * c_ref[:, tab] + partner * s_ref[:, tab]
            if k * lanes < BW:
                y = y * qscale
            o_ref[b, :, cols] = y.astype(BF16)
        vt = v_ref[b].T.astype(BF16)
        for h in range(DA_HEADS):
            vt_ref[b, h * DA_VROWS:h * DA_VROWS + hd, :] = vt[h * hd:(h + 1) * hd, :]
            vt_ref[b, h * DA_VROWS + hd:(h + 1) * DA_VROWS, :] = jnp.ones((DA_VROWS - hd, rows),
                                                                          BF16)


def _rope_call(u, cos_t, sin_t, n_batch, tpb):
    n, wu = u.shape
    rows_b = tpb * TILE
    w = 4 * BW
    vrows = DA_HEADS * DA_VROWS
    tab = pl.BlockSpec((TILE, cos_t.shape[1]), lambda i: (i, 0))
    u3 = u.reshape(n_batch, rows_b, wu)
    qk, vt = pl.pallas_call(
        _rope_kernel,
        grid=(tpb,),
        in_specs=[pl.BlockSpec((n_batch, TILE, w), lambda i: (0, i, C_ROPE // w)), tab, tab,
                  pl.BlockSpec((n_batch, TILE, BW), lambda i: (0, i, C_DAV // BW))],
        out_specs=[pl.BlockSpec((n_batch, TILE, w), lambda i: (0, i, 0)),
                   pl.BlockSpec((n_batch, vrows, TILE), lambda i: (0, 0, i))],
        out_shape=[jax.ShapeDtypeStruct((n_batch, rows_b, w), BF16),
                   jax.ShapeDtypeStruct((n_batch, vrows, rows_b), BF16)],
        compiler_params=_cparams(("arbitrary",)),
        name="rope",
    )(u3, cos_t, sin_t, u3)
    return qk.reshape(n, w), vt.reshape(n_batch * vrows, rows_b)


def _da_kernel(lam_ref, q_ref, k_ref, vt_ref, g_ref, o_ref,
               qs_ref, m_ref, acc_ref, *, nkt, lam_init):
    i = pl.program_id(1)
    tq = q_ref.shape[0]
    hd = 2 * DA_HALF
    q = q_ref[...]
    qmap = _head_of_lane((tq, BW), DA_HALF)
    for g in range(2 * DA_HEADS):
        qs_ref[g] = jnp.where(qmap == g, q, jnp.zeros_like(q))
    m_ref[...] = jnp.full(m_ref.shape, NEG, F32)
    acc_ref[...] = jnp.zeros(acc_ref.shape, F32)

    def tile(off, size):
        kt = k_ref[pl.ds(off, size), :]
        groups = range(2 * DA_HEADS)
        ss = [lax.dot_general(kt, qs_ref[g], _NT, preferred_element_type=F32) for g in groups]
        for g in groups:
            s = ss[g]
            m_old = m_ref[g]
            m_new = jnp.maximum(m_old, jnp.max(s, axis=0, keepdims=True))
            alpha = jnp.exp2(m_old - m_new)[0:1, :]
            p = jnp.exp2(s - m_new[0:1, :]).astype(BF16)
            m_ref[g] = m_new
            h, mm = g // 2, g % 2
            rows = slice(h * DA_VROWS, (h + 1) * DA_VROWS)
            pv = jnp.dot(vt_ref[rows, pl.ds(off, size)], p, preferred_element_type=F32)
            acc_ref[mm, rows, :] = acc_ref[mm, rows, :] * alpha + pv

    @pl.when(i == 0)
    def _():
        tile(0, TILE)

    @pl.when(i > 0)
    def _():
        def body(j, carry):
            tile(pl.multiple_of(j * DA_KT, DA_KT), DA_KT)
            return carry

        lax.fori_loop(0, nkt, body, 0)

    lam = lam_ref[0]
    parts = []
    for h in range(DA_HEADS):
        rows = slice(h * DA_VROWS, h * DA_VROWS + hd)
        den = slice(h * DA_VROWS + hd, h * DA_VROWS + hd + 1)
        o_h = (acc_ref[0, rows, :] / acc_ref[0, den, :]
               - lam * (acc_ref[1, rows, :] / acc_ref[1, den, :]))
        ms = jnp.mean(o_h * o_h, axis=0, keepdims=True)
        parts.append(o_h * lax.rsqrt(ms + EPS))
    y = jnp.concatenate(parts, axis=0).T * g_ref[...]
    o_ref[...] = y * (1.0 - lam_init)


def _da_call(lam, qk, vt, subln, n_batch, tpb, lam_init):
    n = qk.shape[0]
    rows_b = tpb * TILE
    assert rows_b % DA_KT == 0
    vrows = DA_HEADS * DA_VROWS
    kern = functools.partial(_da_kernel, nkt=rows_b // DA_KT, lam_init=lam_init)
    return pl.pallas_call(
        kern,
        grid=(n_batch, tpb),
        in_specs=[pl.BlockSpec(memory_space=pltpu.SMEM),
                  pl.BlockSpec((TILE, BW), lambda b, i: (b * tpb + i, 0)),
                  pl.BlockSpec((rows_b, BW), lambda b, i: (b, 1)),
                  pl.BlockSpec((vrows, rows_b), lambda b, i: (b, 0)),
                  pl.BlockSpec((1, BW), lambda b, i: (0, 0))],
        out_specs=pl.BlockSpec((TILE, BW), lambda b, i: (b * tpb + i, 0)),
        out_shape=jax.ShapeDtypeStruct((n, BW), F32),
        scratch_shapes=[pltpu.VMEM((2 * DA_HEADS, TILE, BW), BF16),
                        pltpu.VMEM((2 * DA_HEADS, 8, TILE), F32),
                        pltpu.VMEM((2, vrows, TILE), F32)],
        compiler_params=_cparams(("arbitrary", "arbitrary")),
        name="diff_attn",
    )(lam, qk, qk, vt, subln)


def _sw_kernel(sink_ref, bias_ref, q_ref, kp_ref, ko_ref, kn_ref, kc_ref,
               vp_ref, vo_ref, vn_ref, vc_ref, o_ref):
    tq = q_ref.shape[0]
    q = q_ref[...]
    kk = jnp.concatenate([kp_ref[...], ko_ref[...], kn_ref[...], kc_ref[...]], axis=0)
    vv = jnp.concatenate([vp_ref[...], vo_ref[...], vn_ref[...], vc_ref[...]], axis=0).astype(BF16)
    nk = kk.shape[0]
    bias = bias_ref[0]
    qhead = _head_of_lane((tq, BW), SW_DH)
    vhead = _head_of_lane((nk, BW), SW_DH)
    ps, vs = [], []
    for h in range(SW_HEADS):
        qm = jnp.where(qhead == h, q, jnp.zeros_like(q))
        s = lax.dot_general(qm, kk, _NT, preferred_element_type=F32) * (SW_DH ** -0.5) + bias
        sk = sink_ref[h]
        m = jnp.maximum(jnp.max(s, axis=1, keepdims=True), sk)
        p = jnp.exp(s - m)
        den = jnp.sum(p, axis=1, keepdims=True) + jnp.exp(sk - m)
        ps.append((p / den).astype(BF16))
        vs.append(jnp.where(vhead == h, vv, jnp.zeros_like(vv)))
    o_ref[...] = jnp.dot(jnp.concatenate(ps, axis=1), jnp.concatenate(vs, axis=0),
                         preferred_element_type=F32)


def _sw_bias():
    r = np.arange(CHUNK)[:, None]
    c = np.arange(CHUNK)[None, :]
    prev_ok, next_ok = c >= r, c <= r
    yes, no = np.ones((CHUNK, CHUNK), bool), np.zeros((CHUNK, CHUNK), bool)
    ctx = np.ones((CHUNK, TILE), bool)
    kinds = [np.concatenate([no, no, no, ctx], axis=1)]
    for no_next in (False, True):
        for no_prev in (False, True):
            kinds.append(np.concatenate([no if no_prev else prev_ok, yes,
                                         no if no_next else next_ok, ctx], axis=1))
    return jnp.asarray(np.where(np.stack(kinds), 0.0, NEG), F32)


def _sw_call(sink, qk, u, n_batch, tpb):
    n = qk.shape[0]
    cpb = tpb * (TILE // CHUNK)
    ctx_chunks = TILE // CHUNK
    bias = _sw_bias()

    def kind(b, j):
        lat = 1 + (j == ctx_chunks).astype(jnp.int32) + 2 * (j == cpb - 1).astype(jnp.int32)
        return (jnp.where(j < ctx_chunks, 0, lat), 0, 0)

    cur = lambda b, j: b * cpb + j
    prv = lambda b, j: b * cpb + jnp.maximum(j - 1, 0)
    nxt = lambda b, j: b * cpb + jnp.minimum(j + 1, cpb - 1)
    kcol, vcol = 3, C_SWV // BW
    blk = lambda f, col: pl.BlockSpec((CHUNK, BW), lambda b, j: (f(b, j), col))
    ctx = lambda col: pl.BlockSpec((TILE, BW), lambda b, j: (b * tpb, col))
    return pl.pallas_call(
        _sw_kernel,
        grid=(n_batch, cpb),
        in_specs=[pl.BlockSpec(memory_space=pltpu.SMEM),
                  pl.BlockSpec((1,) + bias.shape[1:], kind),
                  blk(cur, 2), blk(prv, kcol), blk(cur, kcol), blk(nxt, kcol), ctx(kcol),
                  blk(prv, vcol), blk(cur, vcol), blk(nxt, vcol), ctx(vcol)],
        out_specs=pl.BlockSpec((CHUNK, BW), lambda b, j: (cur(b, j), 0)),
        out_shape=jax.ShapeDtypeStruct((n, BW), F32),
        compiler_params=_cparams(("arbitrary", "arbitrary")),
        name="window_attn",
    )(sink, bias, qk, qk, qk, qk, qk, u, u, u, u)


def _conv_kernel(prev_ref, cur_ref, next_ref, w_ref, b_ref, ln_ref, o_ref, z_ref, zs_ref, *, tpb):
    i = pl.program_id(0)
    pos = i % tpb
    has_prev = pos >= 2
    has_next = (pos >= 1) & (pos < tpb - 1)
    halo = prev_ref.shape[0]

    def glu(x):
        return x[:, :BW] * _sigmoid(x[:, BW:])

    zp = glu(prev_ref[...])
    zn = glu(next_ref[...])
    z_ref[0:halo, :] = jnp.where(has_prev, zp, 0.0)
    z_ref[halo:halo + TILE, :] = glu(cur_ref[...])
    z_ref[halo + TILE:2 * halo + TILE, :] = jnp.where(has_next, zn, 0.0)
    acc = jnp.zeros((TILE, BW), F32) + b_ref[...]
    pad = CONV_W // 2
    span = TILE + 2 * halo - 8
    for r in range(8):
        zs_ref[r] = z_ref[r:r + span, :]
    for t in range(CONV_W):
        off = halo - pad + t
        acc = acc + zs_ref[off % 8, off - off % 8:off - off % 8 + TILE, :] * w_ref[t:t + 1, :]
    mu = jnp.mean(acc, axis=-1, keepdims=True)
    xc = acc - mu
    y = xc * lax.rsqrt(jnp.mean(xc * xc, axis=-1, keepdims=True) + EPS)
    y = y * ln_ref[0:1, :] + ln_ref[1:2, :]
    o_ref[...] = _silu(y)


def _conv_call(u, w, b, ln, tpb):
    n = u.shape[0]
    halo = 16
    per = TILE // halo
    nh = n // halo
    wcv = 2 * BW
    col = C_CV // wcv
    return pl.pallas_call(
        functools.partial(_conv_kernel, tpb=tpb),
        grid=(n // TILE,),
        in_specs=[pl.BlockSpec((halo, wcv), lambda i: (jnp.maximum(i * per - 1, 0), col)),
                  pl.BlockSpec((TILE, wcv), lambda i: (i, col)),
                  pl.BlockSpec((halo, wcv), lambda i: (jnp.minimum((i + 1) * per, nh - 1), col)),
                  pl.BlockSpec((CONV_W, BW), lambda i: (0, 0)),
                  pl.BlockSpec((1, BW), lambda i: (0, 0)),
                  pl.BlockSpec((2, BW), lambda i: (0, 0))],
        out_specs=pl.BlockSpec((TILE, BW), lambda i: (i, 0)),
        out_shape=jax.ShapeDtypeStruct((n, BW), F32),
        scratch_shapes=[pltpu.VMEM((TILE + 2 * halo, BW), F32),
                        pltpu.VMEM((8, TILE + 2 * halo - 8, BW), F32)],
        compiler_params=_cparams(("arbitrary",)),
        name="conformer_conv",
    )(u, u, u, w, b.reshape(1, BW), ln)


def _inv_unit_lower(ls, eye):
    ts = [eye + l for l in ls]
    lps = list(ls)
    step = 1
    while step < ls[0].shape[0] // 2:
        lps = [_dot(lp, lp) for lp in lps]
        ts = [t + _dot(t, lp) for t, lp in zip(ts, lps)]
        step *= 2
    return ts


def _rw_chunk_kernel(prev_ref, cur_ref, next_ref, mu_ref, w0_ref, a0_ref, kk_ref, ka_ref,
                     wup_ref, aup_ref, rk_ref, bd_ref,
                     m_ref, n_ref, q_ref, y0_ref, bonus_ref, z_ref, *, cpb):
    i = pl.program_id(0)
    jj = i % cpb
    ctx_chunks = TILE // CHUNK
    has_prev = (jj != 0) & (jj != ctx_chunks)
    has_next = (jj != ctx_chunks - 1) & (jj != cpb - 1)
    c = CHUNK
    zc = cur_ref[:, 0:RW_SHIFT_W]
    z_ref[0:8, :] = jnp.where(has_prev, prev_ref[:, 0:RW_SHIFT_W], 0.0)
    z_ref[8:8 + c, :] = zc
    z_ref[8 + c:16 + c, :] = jnp.where(has_next, next_ref[:, 0:RW_SHIFT_W], 0.0)
    bd = bd_ref[...]
    t_io = lax.broadcasted_iota(jnp.int32, (c, c), 0)
    s_io = lax.broadcasted_iota(jnp.int32, (c, c), 1)
    eye = jnp.where(t_io == s_io, 1.0, 0.0)
    lane_head = _head_of_lane((c, BW), RW_DH)
    row = lax.broadcasted_iota(jnp.int32, (BW, BW), 0)
    colm = lax.broadcasted_iota(jnp.int32, (BW, BW), 1)
    same_head = (row // RW_DH) == (colm // RW_DH)

    def stack(x):
        return jnp.concatenate(
            [jnp.where(lane_head == h, x, 0.0) for h in range(RW_HEADS)], axis=0)

    cat = lambda xs: jnp.concatenate(xs, axis=1)

    prep = []
    for d in range(2):
        zsh = z_ref[7:7 + c, :] if d == 0 else z_ref[9:9 + c, :]
        zs = zc + (zsh - zc) * mu_ref[d]
        r = zs[:, 0:BW]
        k = zs[:, BW:2 * BW]
        v = zs[:, 2 * BW:3 * BW]
        wa = zs[:, 3 * BW:RW_SHIFT_W]
        lw = -RW_DECAY_SCALE * _sigmoid(w0_ref[d] + _dot(jnp.tanh(wa), wup_ref[d]))
        a = _sigmoid(a0_ref[d] + _dot(wa, aup_ref[d]))
        kappa = k * kk_ref[d]
        kh = kappa * lax.rsqrt(jnp.maximum(_dot_sel(kappa * kappa, bd), 1e-12))
        kt = k * (1.0 + (a - 1.0) * ka_ref[d])
        akh = a * kh
        bonus_ref[d] = _dot_sel(r * kt * rk_ref[...], bd) * v

        ahead = (t_io - s_io) if d == 0 else (s_io - t_io)
        earlier = ahead > 0
        upto = ahead >= 0
        tri = jnp.where(upto, 1.0, 0.0).astype(BF16)
        cl = _sel_dot(tri, lw)
        tot = jnp.sum(lw, axis=0, keepdims=True)
        rho = 0.5 * tot
        cle = cl - lw
        a_true = -kh * jnp.exp(cle)
        r_true = r * jnp.exp(cl)
        a_c = -kh * jnp.exp(cle - rho)
        r_c = r * jnp.exp(cl - rho)
        b_c = akh * jnp.exp(rho - cl)
        k_c = kt * jnp.exp(rho - cl)
        b_end = akh * jnp.exp(tot - cl)
        k_end = kt * jnp.exp(tot - cl)

        pair = _dot(jnp.concatenate([stack(a_c), stack(r_c)], axis=0),
                    jnp.concatenate([b_c, k_c], axis=0), _NT)
        l_ab, l_ak, a_rb, a_rk = [], [], [], []
        for h in range(RW_HEADS):
            blk_a = pair[h * c:(h + 1) * c]
            blk_r = pair[(RW_HEADS + h) * c:(RW_HEADS + h + 1) * c]
            l_ab.append(jnp.where(earlier, blk_a[:, 0:c], 0.0))
            l_ak.append(jnp.where(earlier, blk_a[:, c:2 * c], 0.0))
            a_rb.append(jnp.where(upto, blk_r[:, 0:c], 0.0))
            a_rk.append(jnp.where(upto, blk_r[:, c:2 * c], 0.0))
        prep.append((l_ab, cat(l_ak), cat(a_rb), cat(a_rk), v, a_true, r_true, b_end, k_end, tot))

    t_all = _inv_unit_lower(prep[0][0] + prep[1][0], eye)

    for d in range(2):
        _, lak, arb, ark, v, a_true, r_true, b_end, k_end, tot = prep[d]
        t_inv = cat(t_all[d * RW_HEADS:(d + 1) * RW_HEADS])
        lv = _dot(lak, stack(v))
        w12 = _dot(t_inv, jnp.concatenate([stack(a_true), stack(lv)], axis=1))
        w1 = w12[:, 0:BW]
        w2 = w12[:, BW:2 * BW]
        q_ref[d, 0] = r_true + _dot(arb, stack(w1))
        y0_ref[d, 0] = _dot(arb, stack(w2)) + _dot(ark, stack(v))
        b_end_t = b_end.T
        decay = jnp.where(row == colm, jnp.broadcast_to(jnp.exp(tot), (BW, BW)), 0.0)
        m_ref[d, 0] = jnp.where(same_head, _dot(b_end_t, w1), 0.0) + decay
        n_ref[d, 0] = jnp.where(same_head, _dot(b_end_t, w2) + _dot(k_end.T, v), 0.0)


def _rw_chunk_call(u, lp, bd, tpb):
    n = u.shape[0]
    cpb = tpb * (TILE // CHUNK)
    nch = n // CHUNK
    per = CHUNK // 8
    n8 = n // 8
    wrw = 1024
    col = C_RW // wrw
    pvec = lambda w: pl.BlockSpec((2, 1, w), lambda i: (0, 0, 0))
    pad = jnp.zeros((2, RW_DECAY_RANK, BW), F32)
    wup = jnp.concatenate([lp['rw_w_up'], pad], axis=1).astype(BF16)
    aup = jnp.concatenate([pad, lp['rw_a_up']], axis=1).astype(BF16)
    mat = lambda rows: pl.BlockSpec((2, 1, rows, BW), lambda i: (0, i, 0, 0))
    return pl.pallas_call(
        functools.partial(_rw_chunk_kernel, cpb=cpb),
        grid=(nch,),
        in_specs=[pl.BlockSpec((8, wrw), lambda i: (jnp.maximum(i * per - 1, 0), col)),
                  pl.BlockSpec((CHUNK, wrw), lambda i: (i, col)),
                  pl.BlockSpec((8, wrw), lambda i: (jnp.minimum((i + 1) * per, n8 - 1), col)),
                  pvec(RW_SHIFT_W), pvec(BW), pvec(BW), pvec(BW), pvec(BW),
                  pl.BlockSpec((2, 2 * RW_DECAY_RANK, BW), lambda i: (0, 0, 0)),
                  pl.BlockSpec((2, 2 * RW_A_RANK, BW), lambda i: (0, 0, 0)),
                  pl.BlockSpec((1, BW), lambda i: (0, 0)),
                  pl.BlockSpec((BW, BW), lambda i: (0, 0))],
        out_specs=[mat(BW), mat(BW), mat(CHUNK), mat(CHUNK),
                   pl.BlockSpec((2, CHUNK, BW), lambda i: (0, i, 0))],
        out_shape=[jax.ShapeDtypeStruct((2, nch, BW, BW), F32),
                   jax.ShapeDtypeStruct((2, nch, BW, BW), F32),
                   jax.ShapeDtypeStruct((2, nch, CHUNK, BW), F32),
                   jax.ShapeDtypeStruct((2, nch, CHUNK, BW), F32),
                   jax.ShapeDtypeStruct((2, n, BW), F32)],
        scratch_shapes=[pltpu.VMEM((CHUNK + 16, RW_SHIFT_W), F32)],
        compiler_params=_cparams(("arbitrary",)),
        name="rwkv_chunk",
    )(u, u, u, lp['rw_mu'].reshape(2, 1, RW_SHIFT_W), lp['rw_w0'].reshape(2, 1, BW),
      lp['rw_a0'].reshape(2, 1, BW), lp['rw_kk'].reshape(2, 1, BW), lp['rw_ka'].reshape(2, 1, BW),
      wup, aup, lp['rw_rk'].reshape(1, BW), bd)


def _rw_scan_kernel(*refs, n_batch):
    m_refs, n_refs, q_refs, y0_refs = refs[0:2], refs[2:4], refs[4:6], refs[6:8]
    y_refs, x_ref = refs[8:10], refs[10]

    @pl.when(pl.program_id(0) == 0)
    def _():
        x_ref[...] = jnp.zeros(x_ref.shape, F32)

    for d in range(2):
        for b in range(n_batch):
            x = x_ref[d, b]
            y_refs[d][b, 0] = _dot(q_refs[d][0, b, 0], x) + y0_refs[d][0, b, 0]
            x_ref[d, b] = _dot(m_refs[d][0, b, 0], x) + n_refs[d][0, b, 0]


def _rw_scan_call(m, nn, q, y0, n_batch, tpb):
    cpb = tpb * (TILE // CHUNK)
    ctx_chunks = TILE // CHUNK

    def chunk(d, i):
        rev = jnp.where(i < ctx_chunks, ctx_chunks - 1 - i, cpb + ctx_chunks - 1 - i)
        return i if d == 0 else rev

    def mat(rows, d):
        return pl.BlockSpec((1, n_batch, 1, rows, BW), lambda i: (d, 0, chunk(d, i), 0, 0))

    ins, specs = [], []
    for arr, rows in ((m, BW), (nn, BW), (q, CHUNK), (y0, CHUNK)):
        arr = arr.reshape(2, n_batch, cpb, rows, BW)
        for d in range(2):
            ins.append(arr)
            specs.append(mat(rows, d))
    yshape = jax.ShapeDtypeStruct((n_batch, cpb, CHUNK, BW), F32)
    ys = pl.pallas_call(
        functools.partial(_rw_scan_kernel, n_batch=n_batch),
        grid=(cpb,),
        in_specs=specs,
        out_specs=[pl.BlockSpec((n_batch, 1, CHUNK, BW), lambda i, d=d: (0, chunk(d, i), 0, 0))
                   for d in range(2)],
        out_shape=[yshape, yshape],
        scratch_shapes=[pltpu.VMEM((2, n_batch, BW, BW), F32)],
        compiler_params=_cparams(("arbitrary",)),
        name="rwkv_scan",
    )(*ins)
    return [y.reshape(n_batch * cpb * CHUNK, BW) for y in ys]


def _rw_out_kernel(yf_ref, yb_ref, bonus_ref, u_ref, gup_ref, gn_ref, bd_ref, o_ref):
    bd = bd_ref[...]
    y = yf_ref[...] + yb_ref[...]
    mean = _dot_sel(y, bd) * (1.0 / RW_DH)
    yc = y - mean
    var = _dot_sel(yc * yc, bd) * (1.0 / RW_DH)
    yn = yc * lax.rsqrt(var + RW_GN_EPS) * gn_ref[0:1, :] + gn_ref[1:2, :]
    yn = yn + (bonus_ref[0] + bonus_ref[1])
    o_ref[...] = yn * _dot(_sigmoid(u_ref[...]), gup_ref[...])


def _rw_out_call(ys, bonus, u, gup, gn, bd):
    n = u.shape[0]
    gcol = (C_RW + RW_SHIFT_W) // RW_G_RANK
    both = pl.BlockSpec((2, TILE, BW), lambda i: (0, i, 0))
    one = pl.BlockSpec((TILE, BW), lambda i: (i, 0))
    return pl.pallas_call(
        _rw_out_kernel,
        grid=(n // TILE,),
        in_specs=[one, one, both, pl.BlockSpec((TILE, RW_G_RANK), lambda i: (i, gcol)),
                  pl.BlockSpec((RW_G_RANK, BW), lambda i: (0, 0)),
                  pl.BlockSpec((2, BW), lambda i: (0, 0)),
                  pl.BlockSpec((BW, BW), lambda i: (0, 0))],
        out_specs=pl.BlockSpec((TILE, BW), lambda i: (i, 0)),
        out_shape=jax.ShapeDtypeStruct((n, BW), F32),
        compiler_params=_cparams(("arbitrary",)),
        name="rwkv_readout",
    )(ys[0], ys[1], bonus, u, gup, gn, bd)


def _merge_kernel(oa_ref, ob_ref, oc_ref, od_ref, gl_ref, wb_ref, wo_ref, x_ref, g_ref, mod_ref,
                  g2_ref, sh_ref, sc_ref, rw_ref, rb_ref, o_ref, tok_ref, gate_ref):
    d = x_ref.shape[1]
    m = None
    for i, o in enumerate((oa_ref, ob_ref, oc_ref, od_ref)):
        t = _sigmoid(gl_ref[:, i * d:(i + 1) * d]) * _dot(o[...], wb_ref[i])
        m = t if m is None else m + t
    y = _dot(m, wo_ref[...])
    y = y * lax.rsqrt(jnp.mean(y * y, axis=-1, keepdims=True) + EPS) * g_ref[...]
    x = x_ref[...] + mod_ref[0] * y
    o_ref[...] = x
    t = x * lax.rsqrt(jnp.mean(x * x, axis=-1, keepdims=True) + EPS) * g2_ref[...]
    tok = t * (1.0 + sc_ref[0]) + sh_ref[0]
    tok_ref[...] = tok.astype(BF16)
    gate_ref[...] = _route(tok, rw_ref[...], rb_ref[...])


def _merge_call(outs, u, wb, wo, x, g, mod, g2, shift, scale, rw_t, rb, tpb, n_batch):
    n, d = x.shape
    kind = lambda i: (jnp.where(i % tpb == 0, n_batch, i // tpb), 0, 0)
    br = pl.BlockSpec((TILE, BW), lambda i: (i, 0))
    vec = pl.BlockSpec((1, d), lambda i: (0, 0))
    row = pl.BlockSpec((TILE, d), lambda i: (i, 0))
    wg = N_BRANCH * d
    return pl.pallas_call(
        _merge_kernel,
        grid=(n // TILE,),
        in_specs=[br, br, br, br,
                  pl.BlockSpec((TILE, wg), lambda i: (i, C_GATE // wg)),
                  pl.BlockSpec((N_BRANCH, BW, d), lambda i: (0, 0, 0)),
                  pl.BlockSpec((d, d), lambda i: (0, 0)),
                  row, vec, pl.BlockSpec((1, 1, d), kind),
                  vec, pl.BlockSpec((1, 1, d), kind), pl.BlockSpec((1, 1, d), kind),
                  pl.BlockSpec((N_EXPERTS, d), lambda i: (0, 0)),
                  pl.BlockSpec((N_EXPERTS, 1), lambda i: (0, 0))],
        out_specs=[row, row, pl.BlockSpec((N_EXPERTS, TILE), lambda i: (0, i))],
        out_shape=[jax.ShapeDtypeStruct((n, d), F32), jax.ShapeDtypeStruct((n, d), BF16),
                   jax.ShapeDtypeStruct((N_EXPERTS, n), F32)],
        compiler_params=_cparams(("arbitrary",)),
        name="merge",
    )(*outs, u, wb, wo, x, g.reshape(1, d), mod, g2.reshape(1, d), shift, scale,
      rw_t, rb.reshape(N_EXPERTS, 1))


def _route(tokens, w, b):
    tm = tokens.shape[0]
    gsz = N_EXPERTS // N_GROUPS
    logits = _dot3(w, tokens, _NT)
    sc = _sigmoid(logits).reshape(N_GROUPS, gsz, tm)
    bi = sc + b.reshape(N_GROUPS, gsz, 1)
    shape = (N_GROUPS, gsz, tm)
    g_io = lax.broadcasted_iota(jnp.int32, shape, 0)
    j_io = lax.broadcasted_iota(jnp.int32, shape, 1)
    e_io = g_io * gsz + j_io
    ninf = -jnp.inf
    m1 = jnp.max(bi, axis=1, keepdims=True)
    i1 = jnp.min(jnp.where(bi == m1, j_io, gsz), axis=1, keepdims=True)
    m2 = jnp.max(jnp.where(j_io == i1, ninf, bi), axis=1, keepdims=True)
    cur = jnp.broadcast_to(m1 + m2, shape)
    gsel = jnp.zeros(shape, F32)
    for _ in range(TOPK_GROUPS):
        mx = jnp.max(cur, axis=0, keepdims=True)
        ix = jnp.min(jnp.where(cur == mx, g_io, N_GROUPS), axis=0, keepdims=True)
        hit = g_io == ix
        gsel = jnp.where(hit, 1.0, gsel)
        cur = jnp.where(hit, ninf, cur)
    cur = jnp.where(gsel > 0.0, bi, ninf)
    esel = jnp.zeros(shape, F32)
    for _ in range(TOP_K):
        mx = jnp.max(jnp.max(cur, axis=0, keepdims=True), axis=1, keepdims=True)
        ix = jnp.min(jnp.min(jnp.where(cur == mx, e_io, N_EXPERTS), axis=0, keepdims=True),
                     axis=1, keepdims=True)
        hit = e_io == ix
        esel = jnp.where(hit, 1.0, esel)
        cur = jnp.where(hit, ninf, cur)
    wsel = sc * esel
    den = jnp.sum(jnp.sum(wsel, axis=0, keepdims=True), axis=1, keepdims=True)
    return (wsel / den * ROUTE_SCALE).reshape(N_EXPERTS, tm)


def _moe_kernel(x_ref, g_ref, wgu_ref, wd_ref, sgu_ref, sd_ref, o_ref):
    e = pl.program_id(1)

    def ffn(wgu, wd):
        hgu = _dot(x_ref[...], wgu)
        return _dot(_silu(hgu[:, :D_EXPERT]) * hgu[:, D_EXPERT:], wd)

    @pl.when(e == 0)
    def _():
        o_ref[...] = ffn(sgu_ref[...], sd_ref[...])

    gates = g_ref[...]
    lane = lax.broadcasted_iota(jnp.int32, gates.shape, 1)
    gcol = jnp.sum(jnp.where(lane == e, gates, 0.0), axis=1, keepdims=True)
    o_ref[...] += ffn(wgu_ref[0, 0], wd_ref[0, 0]) * gcol


def _moe_call(tok, gates, wgu, wd, layer, sgu, sd, tm):
    n, d = tok.shape
    return pl.pallas_call(
        _moe_kernel,
        grid=(n // tm, N_EXPERTS),
        in_specs=[pl.BlockSpec((tm, d), lambda i, e: (i, 0)),
                  pl.BlockSpec((tm, N_EXPERTS), lambda i, e: (i, 0)),
                  pl.BlockSpec((1, 1, d, 2 * D_EXPERT), lambda i, e: (layer, e, 0, 0)),
                  pl.BlockSpec((1, 1, D_EXPERT, d), lambda i, e: (layer, e, 0, 0)),
                  pl.BlockSpec((d, 2 * D_EXPERT), lambda i, e: (0, 0)),
                  pl.BlockSpec((D_EXPERT, d), lambda i, e: (0, 0))],
        out_specs=pl.BlockSpec((tm, d), lambda i, e: (i, 0)),
        out_shape=jax.ShapeDtypeStruct((n, d), F32),
        compiler_params=_cparams(("arbitrary", "arbitrary")),
        name="moe_experts",
    )(tok, gates, wgu, wd, sgu, sd)


def _resid_kernel(x_ref, f_ref, g_ref, mod_ref, o_ref):
    f = f_ref[...]
    y = f * lax.rsqrt(jnp.mean(f * f, axis=-1, keepdims=True) + EPS) * g_ref[...]
    o_ref[...] = x_ref[...] + mod_ref[0] * y


def _resid_call(x, f, g, mod, tpb, n_batch, latent_only=False):
    n, d = x.shape
    vec = pl.BlockSpec((1, d), lambda *_: (0, 0))
    if latent_only:
        lpb = tpb - 1
        grid = (n_batch, lpb)
        row = pl.BlockSpec((TILE, d), lambda b, i: (b * tpb + 1 + i, 0))
        out = pl.BlockSpec((TILE, d), lambda b, i: (b * lpb + i, 0))
        kind = pl.BlockSpec((1, 1, d), lambda b, i: (b, 0, 0))
        n_out = n_batch * lpb * TILE
    else:
        grid = (n // TILE,)
        row = out = pl.BlockSpec((TILE, d), lambda i: (i, 0))
        kind = pl.BlockSpec((1, 1, d), lambda i: (jnp.where(i % tpb == 0, n_batch, i // tpb), 0, 0))
        n_out = n
    return pl.pallas_call(
        _resid_kernel,
        grid=grid,
        in_specs=[row, row, vec, kind],
        out_specs=out,
        out_shape=jax.ShapeDtypeStruct((n_out, d), F32),
        compiler_params=_cparams(("arbitrary",) * len(grid)),
        name="moe_residual",
    )(x, f, g.reshape(1, d), mod)


def _proj_weights(w):
    w = w.astype(BF16)
    k = w.shape[0]
    da, cv, sw = 0, 3 * BW, 5 * BW
    rw = sw + (SW_HEADS + 2 * SW_KV) * SW_DH
    gate = rw + RW_SHIFT_W + RW_G_RANK

    def per_q_head(x):
        return jnp.repeat(x.reshape(k, SW_KV, SW_DH), SW_HEADS // SW_KV, axis=1).reshape(k, -1)

    da_q, da_k, da_v = (w[:, da + i * BW:da + (i + 1) * BW] for i in range(3))
    sw_q = w[:, sw:sw + BW]
    sw_k = per_q_head(w[:, sw + BW:sw + BW + SW_KV * SW_DH])
    sw_v = per_q_head(w[:, sw + BW + SW_KV * SW_DH:rw])
    out = jnp.concatenate([
        w[:, gate:], da_q, da_k, sw_q, sw_k, da_v, sw_v, w[:, cv:sw], w[:, rw:gate]], axis=1)
    assert out.shape[1] == C_TOTAL
    return out


def _rope_tables(seq):
    rows = seq // GRID_W
    row = jnp.repeat(jnp.arange(rows, dtype=F32), GRID_W)
    colp = jnp.tile(jnp.arange(GRID_W, dtype=F32), rows)

    def tables(dim):
        q = dim // 4
        freqs = ROPE_BASE ** (-jnp.arange(q, dtype=F32) / q)
        ar, ac = row[:, None] * freqs, colp[:, None] * freqs
        cos = jnp.concatenate([jnp.cos(ar), jnp.cos(ar), jnp.cos(ac), jnp.cos(ac)], axis=1)
        sin = jnp.concatenate([-jnp.sin(ar), jnp.sin(ar), -jnp.sin(ac), jnp.sin(ac)], axis=1)
        return jnp.tile(cos, (1, 128 // dim)), jnp.tile(sin, (1, 128 // dim))

    c32, s32 = tables(DA_HALF)
    c64, s64 = tables(SW_DH)
    cos = jnp.concatenate([c32, c64], axis=1)
    sin = jnp.concatenate([s32, s64], axis=1)
    w = cos.shape[1]
    cos = jnp.concatenate([jnp.ones((TILE, w), F32), cos], axis=0)
    sin = jnp.concatenate([jnp.zeros((TILE, w), F32), sin], axis=0)
    return cos, sin


def _moe_tile(rows_b):
    best = TILE
    for t in range(TILE, MOE_MAX_TILE + 1, 16):
        if rows_b % t == 0:
            best = t
    return best


def kernel(x, c, ctx, c_ctx, ada_w, ada_b, norm_g, w_in, w_branch, w_out, da_lambda, da_subln,
           cv_w, cv_b, cv_ln, sw_sink, rw_mu, rw_w0, rw_w_up, rw_a0, rw_a_up, rw_kk, rw_ka,
           rw_g_up, rw_rk, rw_gn, router_w, router_b, ex_w_gu, ex_w_down, sh_w_gu, sh_w_down):
    n_batch, seq, d = x.shape
    ctx_len = ctx.shape[1]
    depth = w_in.shape[0]
    assert ctx_len == TILE and seq % TILE == 0 and seq % GRID_W == 0
    assert n_batch + 1 <= 8
    rows_b = ctx_len + seq
    tpb = rows_b // TILE
    n = n_batch * rows_b

    xs = jnp.concatenate([ctx, x], axis=1).reshape(n, d)
    cond = jnp.zeros((8, d), F32).at[:n_batch].set(c).at[n_batch].set(c_ctx)
    mods = _ada_call(cond, ada_w, ada_b)[:, :n_batch + 1]
    cos_t, sin_t = _rope_tables(seq)
    hio = np.arange(BW) // RW_DH
    bd = jnp.asarray(hio[:, None] == hio[None, :], BF16)

    for l in range(depth):
        mod = [mods[l, :, i * d:(i + 1) * d].reshape(n_batch + 1, 1, d) for i in range(6)]
        ng = norm_g[l]
        lam_init = 0.8 - 0.6 * math.exp(-0.3 * l)
        lv = da_lambda[l]
        lam = (jnp.exp(jnp.sum(lv[0] * lv[1])) - jnp.exp(jnp.sum(lv[2] * lv[3])) + lam_init)
        lp = {'rw_mu': rw_mu[l], 'rw_w0': rw_w0[l], 'rw_w_up': rw_w_up[l], 'rw_a0': rw_a0[l],
              'rw_a_up': rw_a_up[l], 'rw_kk': rw_kk[l], 'rw_ka': rw_ka[l], 'rw_rk': rw_rk[l]}

        h = _norm_mod_call(xs, ng[0], mod[0], mod[1], tpb, n_batch)
        u = _proj_call(h, _proj_weights(w_in[l]))
        qk, vt = _rope_call(u, cos_t, sin_t, n_batch, tpb)
        oa = _da_call(lam.reshape(1), qk, vt, jnp.tile(da_subln[l], DA_HEADS).reshape(1, BW),
                      n_batch, tpb, lam_init)
        ob = _conv_call(u, cv_w[l], cv_b[l], cv_ln[l], tpb)
        oc = _sw_call(sw_sink[l], qk, u, n_batch, tpb)
        cm, cn, cq, cy0, bonus = _rw_chunk_call(u, lp, bd, tpb)
        yscan = _rw_scan_call(cm, cn, cq, cy0, n_batch, tpb)
        od = _rw_out_call(yscan, bonus, u, rw_g_up[l].astype(BF16), rw_gn[l], bd)
        xs, tok, gates_t = _merge_call(
            (oa, ob, oc, od), u, w_branch[l].astype(BF16), w_out[l].astype(BF16), xs, ng[1],
            mod[2], ng[2], mod[3], mod[4], router_w[l].T, router_b[l], tpb, n_batch)
        f = _moe_call(tok, gates_t.T, ex_w_gu, ex_w_down, l, sh_w_gu[l].astype(BF16),
                      sh_w_down[l].astype(BF16), _moe_tile(rows_b))
        xs = _resid_call(xs, f, ng[3], mod[5], tpb, n_batch, latent_only=(l == depth - 1))

    return xs.reshape(n_batch, seq, d)
```

```python
import functools
import math

import numpy as np
import jax
import jax.numpy as jnp
from jax import lax
from jax.experimental import pallas as pl
from jax.experimental.pallas import tpu as pltpu

F32 = jnp.float32
BF16 = jnp.bfloat16

GRID_W = 64
EPS = 1e-6
ROPE_BASE = 10000.0
N_BRANCH = 4
BW = 256
DA_HEADS = 4
DA_HALF = 32
SW_HEADS = 4
SW_KV = 2
SW_DH = 64
WINDOW = 128
CONV_W = 31
RW_HEADS = 4
RW_DH = 64
RW_DECAY_RANK = 64
RW_A_RANK = 64
RW_G_RANK = 128
RW_DECAY_SCALE = math.exp(-0.5)
RW_GN_EPS = 64e-5
RW_SHIFT_W = 3 * BW + RW_DECAY_RANK + RW_A_RANK
N_EXPERTS = 64
TOP_K = 6
N_GROUPS = 8
TOPK_GROUPS = 4
D_EXPERT = 256
ROUTE_SCALE = 2.5

TILE = 256
CHUNK = 128
DA_KT = 1408
MOE_MAX_TILE = 2816
MOE_VMEM_LIMIT = 58 * 1024 * 1024
DA_VROWS = 80
NEG = -1e30

C_GATE = 0
C_ROPE = 4096
C_DAV = 5120
C_SWV = 5376
C_CV = 5632
C_RW = 6144
C_TOTAL = 7168

VMEM_LIMIT = 48 * 1024 * 1024

_NT = (((1,), (1,)), ((), ()))
_NN = (((1,), (0,)), ((), ()))


def _cparams(sem, vmem=VMEM_LIMIT):
    return pltpu.CompilerParams(dimension_semantics=sem, vmem_limit_bytes=vmem)


def _dot(a, b, dims=_NN):
    return lax.dot_general(a.astype(BF16), b.astype(BF16), dims, preferred_element_type=F32)


def _split2(x):
    hi = x.astype(BF16)
    lo = (x - hi.astype(F32)).astype(BF16)
    return hi, lo


def _dot3(a, b, dims=_NN):
    ah, al = _split2(a)
    bh, bl = _split2(b)
    dg = lambda x, y: lax.dot_general(x, y, dims, preferred_element_type=F32)
    return dg(ah, bh) + (dg(ah, bl) + dg(al, bh))


def _dot_sel(x, sel, dims=_NN):
    h0 = x.astype(BF16)
    r1 = x - h0.astype(F32)
    h1 = r1.astype(BF16)
    h2 = (r1 - h1.astype(F32)).astype(BF16)
    dg = lambda y: lax.dot_general(y, sel, dims, preferred_element_type=F32)
    return dg(h0) + (dg(h1) + dg(h2))


def _sel_dot(sel, x):
    h0 = x.astype(BF16)
    r1 = x - h0.astype(F32)
    h1 = r1.astype(BF16)
    h2 = (r1 - h1.astype(F32)).astype(BF16)
    dg = lambda y: lax.dot_general(sel, y, _NN, preferred_element_type=F32)
    return dg(h0) + (dg(h1) + dg(h2))


def _sigmoid(x):
    return jax.nn.sigmoid(x)


def _silu(x):
    return x * jax.nn.sigmoid(x)


def _head_of_lane(shape, width):
    return lax.broadcasted_iota(jnp.int32, shape, len(shape) - 1) // width


def _ada_kernel(s_ref, w_ref, b_ref, o_ref):
    s = _silu(s_ref[...])
    o_ref[0] = _dot(s, w_ref[0]) + b_ref[0]


def _ada_call(cond, ada_w, ada_b):
    depth, d, cols = ada_w.shape
    tn = 1536
    return pl.pallas_call(
        _ada_kernel,
        grid=(depth, cols // tn),
        in_specs=[pl.BlockSpec((8, d), lambda l, j: (0, 0)),
                  pl.BlockSpec((1, d, tn), lambda l, j: (l, 0, j)),
                  pl.BlockSpec((1, 1, tn), lambda l, j: (l, 0, j))],
        out_specs=pl.BlockSpec((1, 8, tn), lambda l, j: (l, 0, j)),
        out_shape=jax.ShapeDtypeStruct((depth, 8, cols), F32),
        compiler_params=_cparams(("arbitrary", "arbitrary")),
        name="ada_mod",
    )(cond, ada_w, ada_b.reshape(depth, 1, cols))


def _norm_mod_kernel(x_ref, g_ref, sh_ref, sc_ref, o_ref):
    x = x_ref[...]
    y = x * lax.rsqrt(jnp.mean(x * x, axis=-1, keepdims=True) + EPS) * g_ref[...]
    o_ref[...] = (y * (1.0 + sc_ref[0]) + sh_ref[0]).astype(BF16)


def _norm_mod_call(x, g, shift, scale, tpb, n_batch):
    n, d = x.shape
    kind = lambda i: (jnp.where(i % tpb == 0, n_batch, i // tpb), 0, 0)
    row = pl.BlockSpec((TILE, d), lambda i: (i, 0))
    return pl.pallas_call(
        _norm_mod_kernel,
        grid=(n // TILE,),
        in_specs=[row, pl.BlockSpec((1, d), lambda i: (0, 0)),
                  pl.BlockSpec((1, 1, d), kind), pl.BlockSpec((1, 1, d), kind)],
        out_specs=row,
        out_shape=jax.ShapeDtypeStruct((n, d), BF16),
        compiler_params=_cparams(("arbitrary",)),
        name="norm_mod",
    )(x, g.reshape(1, d), shift, scale)


def _mm_kernel(a_ref, w_ref, o_ref):
    o_ref[...] = jnp.dot(a_ref[...], w_ref[...], preferred_element_type=F32)


def _proj_call(h, w):
    n, d = h.shape
    cols = w.shape[1]
    tm, tn = 2 * TILE, C_TOTAL // 2
    return pl.pallas_call(
        _mm_kernel,
        grid=(cols // tn, n // tm),
        in_specs=[pl.BlockSpec((tm, d), lambda j, i: (i, 0)),
                  pl.BlockSpec((d, tn), lambda j, i: (0, j))],
        out_specs=pl.BlockSpec((tm, tn), lambda j, i: (i, j)),
        out_shape=jax.ShapeDtypeStruct((n, cols), F32),
        compiler_params=_cparams(("arbitrary", "arbitrary")),
        name="in_proj",
    )(h, w)


def _rope_kernel(u_ref, c_ref, s_ref, v_ref, o_ref, vt_ref):
    lanes = 128
    n_batch, rows, w = u_ref.shape
    lane = lax.broadcasted_iota(jnp.int32, (rows, lanes), 1)
    hd = 2 * DA_HALF
    qscale = (DA_HALF ** -0.5) * math.log2(math.e)
    for b in range(n_batch):
        for k in range(w // lanes):
            cols = slice(k * lanes, (k + 1) * lanes)
            is_da = k * lanes < 2 * BW
            q = (DA_HALF if is_da else SW_DH) // 4
            tab = slice(0, lanes) if is_da else slice(lanes, 2 * lanes)
            x = u_ref[b, :, cols]
            partner = jnp.where((lane // q) % 2 == 0, pltpu.roll(x, lanes - q, 1),
                                pltpu.roll(x, q, 1))
            y = x * c_ref[:, tab] + partner * s_ref[:, tab]
            if k * lanes < BW:
                y = y * qscale
            o_ref[b, :, cols] = y.astype(BF16)
        vt = v_ref[b].T.astype(BF16)
        for h in range(DA_HEADS):
            vt_ref[b, h * DA_VROWS:h * DA_VROWS + hd, :] = vt[h * hd:(h + 1) * hd, :]
            vt_ref[b, h * DA_VROWS + hd:(h + 1) * DA_VROWS, :] = jnp.ones((DA_VROWS - hd, rows),
                                                                          BF16)


def _rope_call(u, cos_t, sin_t, n_batch, tpb):
    n, wu = u.shape
    rows_b = tpb * TILE
    w = 4 * BW
    vrows = DA_HEADS * DA_VROWS
    tab = pl.BlockSpec((TILE, cos_t.shape[1]), lambda i: (i, 0))
    u3 = u.reshape(n_batch, rows_b, wu)
    qk, vt = pl.pallas_call(
        _rope_kernel,
        grid=(tpb,),
        in_specs=[pl.BlockSpec((n_batch, TILE, w), lambda i: (0, i, C_ROPE // w)), tab, tab,
                  pl.BlockSpec((n_batch, TILE, BW), lambda i: (0, i, C_DAV // BW))],
        out_specs=[pl.BlockSpec((n_batch, TILE, w), lambda i: (0, i, 0)),
                   pl.BlockSpec((n_batch, vrows, TILE), lambda i: (0, 0, i))],
        out_shape=[jax.ShapeDtypeStruct((n_batch, rows_b, w), BF16),
                   jax.ShapeDtypeStruct((n_batch, vrows, rows_b), BF16)],
        compiler_params=_cparams(("arbitrary",)),
        name="rope",
    )(u3, cos_t, sin_t, u3)
    return qk.reshape(n, w), vt.reshape(n_batch * vrows, rows_b)


def _da_kernel(lam_ref, q_ref, k_ref, vt_ref, g_ref, o_ref,
               qs_ref, m_ref, acc_ref, *, nkt, lam_init):
    i = pl.program_id(1)
    tq = q_ref.shape[0]
    hd = 2 * DA_HALF
    q = q_ref[...]
    qmap = _head_of_lane((tq, BW), DA_HALF)
    for g in range(2 * DA_HEADS):
        qs_ref[g] = jnp.where(qmap == g, q, jnp.zeros_like(q))
    m_ref[...] = jnp.full(m_ref.shape, NEG, F32)
    acc_ref[...] = jnp.zeros(acc_ref.shape, F32)

    def tile(off, size):
        kt = k_ref[pl.ds(off, size), :]
        groups = range(2 * DA_HEADS)
        ss = [lax.dot_general(kt, qs_ref[g], _NT, preferred_element_type=F32) for g in groups]
        for g in groups:
            s = ss[g]
            m_old = m_ref[g]
            m_new = jnp.maximum(m_old, jnp.max(s, axis=0, keepdims=True))
            alpha = jnp.exp2(m_old - m_new)[0:1, :]
            p = jnp.exp2(s - m_new[0:1, :]).astype(BF16)
            m_ref[g] = m_new
            h, mm = g // 2, g % 2
            rows = slice(h * DA_VROWS, (h + 1) * DA_VROWS)
            pv = jnp.dot(vt_ref[rows, pl.ds(off, size)], p, preferred_element_type=F32)
            acc_ref[mm, rows, :] = acc_ref[mm, rows, :] * alpha + pv

    @pl.when(i == 0)
    def _():
        tile(0, TILE)

    @pl.when(i > 0)
    def _():
        def body(j, carry):
            tile(pl.multiple_of(j * DA_KT, DA_KT), DA_KT)
            return carry

        lax.fori_loop(0, nkt, body, 0)

    lam = lam_ref[0]
    parts = []
    for h in range(DA_HEADS):
        rows = slice(h * DA_VROWS, h * DA_VROWS + hd)
        den = slice(h * DA_VROWS + hd, h * DA_VROWS + hd + 1)
        o_h = (acc_ref[0, rows, :] / acc_ref[0, den, :]
               - lam * (acc_ref[1, rows, :] / acc_ref[1, den, :]))
        ms = jnp.mean(o_h * o_h, axis=0, keepdims=True)
        parts.append(o_h * lax.rsqrt(ms + EPS))
    y = jnp.concatenate(parts, axis=0).T * g_ref[...]
    o_ref[...] = y * (1.0 - lam_init)


def _da_call(lam, qk, vt, subln, n_batch, tpb, lam_init):
    n = qk.shape[0]
    rows_b = tpb * TILE
    assert rows_b % DA_KT == 0
    vrows = DA_HEADS * DA_VROWS
    kern = functools.partial(_da_kernel, nkt=rows_b // DA_KT, lam_init=lam_init)
    return pl.pallas_call(
        kern,
        grid=(n_batch, tpb),
        in_specs=[pl.BlockSpec(memory_space=pltpu.SMEM),
                  pl.BlockSpec((TILE, BW), lambda b, i: (b * tpb + i, 0)),
                  pl.BlockSpec((rows_b, BW), lambda b, i: (b, 1)),
                  pl.BlockSpec((vrows, rows_b), lambda b, i: (b, 0)),
                  pl.BlockSpec((1, BW), lambda b, i: (0, 0))],
        out_specs=pl.BlockSpec((TILE, BW), lambda b, i: (b * tpb + i, 0)),
        out_shape=jax.ShapeDtypeStruct((n, BW), F32),
        scratch_shapes=[pltpu.VMEM((2 * DA_HEADS, TILE, BW), BF16),
                        pltpu.VMEM((2 * DA_HEADS, 8, TILE), F32),
                        pltpu.VMEM((2, vrows, TILE), F32)],
        compiler_params=_cparams(("arbitrary", "arbitrary")),
        name="diff_attn",
    )(lam, qk, qk, vt, subln)


def _sw_kernel(sink_ref, bias_ref, q_ref, kp_ref, ko_ref, kn_ref, kc_ref,
               vp_ref, vo_ref, vn_ref, vc_ref, o_ref):
    tq = q_ref.shape[0]
    q = q_ref[...]
    kk = jnp.concatenate([kp_ref[...], ko_ref[...], kn_ref[...], kc_ref[...]], axis=0)
    vv = jnp.concatenate([vp_ref[...], vo_ref[...], vn_ref[...], vc_ref[...]], axis=0).astype(BF16)
    nk = kk.shape[0]
    bias = bias_ref[0]
    qhead = _head_of_lane((tq, BW), SW_DH)
    vhead = _head_of_lane((nk, BW), SW_DH)
    ps, vs = [], []
    for h in range(SW_HEADS):
        qm = jnp.where(qhead == h, q, jnp.zeros_like(q))
        s = lax.dot_general(qm, kk, _NT, preferred_element_type=F32) * (SW_DH ** -0.5) + bias
        sk = sink_ref[h]
        m = jnp.maximum(jnp.max(s, axis=1, keepdims=True), sk)
        p = jnp.exp(s - m)
        den = jnp.sum(p, axis=1, keepdims=True) + jnp.exp(sk - m)
        ps.append((p / den).astype(BF16))
        vs.append(jnp.where(vhead == h, vv, jnp.zeros_like(vv)))
    o_ref[...] = jnp.dot(jnp.concatenate(ps, axis=1), jnp.concatenate(vs, axis=0),
                         preferred_element_type=F32)


def _sw_bias():
    r = np.arange(CHUNK)[:, None]
    c = np.arange(CHUNK)[None, :]
    prev_ok, next_ok = c >= r, c <= r
    yes, no = np.ones((CHUNK, CHUNK), bool), np.zeros((CHUNK, CHUNK), bool)
    ctx = np.ones((CHUNK, TILE), bool)
    kinds = [np.concatenate([no, no, no, ctx], axis=1)]
    for no_next in (False, True):
        for no_prev in (False, True):
            kinds.append(np.concatenate([no if no_prev else prev_ok, yes,
                                         no if no_next else next_ok, ctx], axis=1))
    return jnp.asarray(np.where(np.stack(kinds), 0.0, NEG), F32)


def _sw_call(sink, qk, u, n_batch, tpb):
    n = qk.shape[0]
    cpb = tpb * (TILE // CHUNK)
    ctx_chunks = TILE // CHUNK
    bias = _sw_bias()

    def kind(b, j):
        lat = 1 + (j == ctx_chunks).astype(jnp.int32) + 2 * (j == cpb - 1).astype(jnp.int32)
        return (jnp.where(j < ctx_chunks, 0, lat), 0, 0)

    cur = lambda b, j: b * cpb + j
    prv = lambda b, j: b * cpb + jnp.maximum(j - 1, 0)
    nxt = lambda b, j: b * cpb + jnp.minimum(j + 1, cpb - 1)
    kcol, vcol = 3, C_SWV // BW
    blk = lambda f, col: pl.BlockSpec((CHUNK, BW), lambda b, j: (f(b, j), col))
    ctx = lambda col: pl.BlockSpec((TILE, BW), lambda b, j: (b * tpb, col))
    return pl.pallas_call(
        _sw_kernel,
        grid=(n_batch, cpb),
        in_specs=[pl.BlockSpec(memory_space=pltpu.SMEM),
                  pl.BlockSpec((1,) + bias.shape[1:], kind),
                  blk(cur, 2), blk(prv, kcol), blk(cur, kcol), blk(nxt, kcol), ctx(kcol),
                  blk(prv, vcol), blk(cur, vcol), blk(nxt, vcol), ctx(vcol)],
        out_specs=pl.BlockSpec((CHUNK, BW), lambda b, j: (cur(b, j), 0)),
        out_shape=jax.ShapeDtypeStruct((n, BW), F32),
        compiler_params=_cparams(("arbitrary", "arbitrary")),
        name="window_attn",
    )(sink, bias, qk, qk, qk, qk, qk, u, u, u, u)


def _conv_kernel(prev_ref, cur_ref, next_ref, w_ref, b_ref, ln_ref, o_ref, z_ref, zs_ref, *, tpb):
    i = pl.program_id(0)
    pos = i % tpb
    has_prev = pos >= 2
    has_next = (pos >= 1) & (pos < tpb - 1)
    halo = prev_ref.shape[0]

    def glu(x):
        return x[:, :BW] * _sigmoid(x[:, BW:])

    zp = glu(prev_ref[...])
    zn = glu(next_ref[...])
    z_ref[0:halo, :] = jnp.where(has_prev, zp, 0.0)
    z_ref[halo:halo + TILE, :] = glu(cur_ref[...])
    z_ref[halo + TILE:2 * halo + TILE, :] = jnp.where(has_next, zn, 0.0)
    acc = jnp.zeros((TILE, BW), F32) + b_ref[...]
    pad = CONV_W // 2
    span = TILE + 2 * halo - 8
    for r in range(8):
        zs_ref[r] = z_ref[r:r + span, :]
    for t in range(CONV_W):
        off = halo - pad + t
        acc = acc + zs_ref[off % 8, off - off % 8:off - off % 8 + TILE, :] * w_ref[t:t + 1, :]
    mu = jnp.mean(acc, axis=-1, keepdims=True)
    xc = acc - mu
    y = xc * lax.rsqrt(jnp.mean(xc * xc, axis=-1, keepdims=True) + EPS)
    y = y * ln_ref[0:1, :] + ln_ref[1:2, :]
    o_ref[...] = _silu(y)


def _conv_call(u, w, b, ln, tpb):
    n = u.shape[0]
    halo = 16
    per = TILE // halo
    nh = n // halo
    wcv = 2 * BW
    col = C_CV // wcv
    return pl.pallas_call(
        functools.partial(_conv_kernel, tpb=tpb),
        grid=(n // TILE,),
        in_specs=[pl.BlockSpec((halo, wcv), lambda i: (jnp.maximum(i * per - 1, 0), col)),
                  pl.BlockSpec((TILE, wcv), lambda i: (i, col)),
                  pl.BlockSpec((halo, wcv), lambda i: (jnp.minimum((i + 1) * per, nh - 1), col)),
                  pl.BlockSpec((CONV_W, BW), lambda i: (0, 0)),
                  pl.BlockSpec((1, BW), lambda i: (0, 0)),
                  pl.BlockSpec((2, BW), lambda i: (0, 0))],
        out_specs=pl.BlockSpec((TILE, BW), lambda i: (i, 0)),
        out_shape=jax.ShapeDtypeStruct((n, BW), F32),
        scratch_shapes=[pltpu.VMEM((TILE + 2 * halo, BW), F32),
                        pltpu.VMEM((8, TILE + 2 * halo - 8, BW), F32)],
        compiler_params=_cparams(("arbitrary",)),
        name="conformer_conv",
    )(u, u, u, w, b.reshape(1, BW), ln)


def _inv_unit_lower(ls, eye):
    ts = [eye + l for l in ls]
    lps = list(ls)
    step = 1
    while step < ls[0].shape[0] // 2:
        lps = [_dot(lp, lp) for lp in lps]
        ts = [t + _dot(t, lp) for t, lp in zip(ts, lps)]
        step *= 2
    return ts


def _rw_chunk_kernel(prev_ref, cur_ref, next_ref, mu_ref, w0_ref, a0_ref, kk_ref, ka_ref,
                     wup_ref, aup_ref, rk_ref, bd_ref,
                     m_ref, n_ref, q_ref, y0_ref, bonus_ref, z_ref, *, cpb):
    i = pl.program_id(0)
    jj = i % cpb
    ctx_chunks = TILE // CHUNK
    has_prev = (jj != 0) & (jj != ctx_chunks)
    has_next = (jj != ctx_chunks - 1) & (jj != cpb - 1)
    c = CHUNK
    zc = cur_ref[:, 0:RW_SHIFT_W]
    z_ref[0:8, :] = jnp.where(has_prev, prev_ref[:, 0:RW_SHIFT_W], 0.0)
    z_ref[8:8 + c, :] = zc
    z_ref[8 + c:16 + c, :] = jnp.where(has_next, next_ref[:, 0:RW_SHIFT_W], 0.0)
    bd = bd_ref[...]
    t_io = lax.broadcasted_iota(jnp.int32, (c, c), 0)
    s_io = lax.broadcasted_iota(jnp.int32, (c, c), 1)
    eye = jnp.where(t_io == s_io, 1.0, 0.0)
    lane_head = _head_of_lane((c, BW), RW_DH)
    row = lax.broadcasted_iota(jnp.int32, (BW, BW), 0)
    colm = lax.broadcasted_iota(jnp.int32, (BW, BW), 1)
    same_head = (row // RW_DH) == (colm // RW_DH)

    def stack(x):
        return jnp.concatenate(
            [jnp.where(lane_head == h, x, 0.0) for h in range(RW_HEADS)], axis=0)

    cat = lambda xs: jnp.concatenate(xs, axis=1)

    prep = []
    for d in range(2):
        zsh = z_ref[7:7 + c, :] if d == 0 else z_ref[9:9 + c, :]
        zs = zc + (zsh - zc) * mu_ref[d]
        r = zs[:, 0:BW]
        k = zs[:, BW:2 * BW]
        v = zs[:, 2 * BW:3 * BW]
        wa = zs[:, 3 * BW:RW_SHIFT_W]
        lw = -RW_DECAY_SCALE * _sigmoid(w0_ref[d] + _dot(jnp.tanh(wa), wup_ref[d]))
        a = _sigmoid(a0_ref[d] + _dot(wa, aup_ref[d]))
        kappa = k * kk_ref[d]
        kh = kappa * lax.rsqrt(jnp.maximum(_dot_sel(kappa * kappa, bd), 1e-12))
        kt = k * (1.0 + (a - 1.0) * ka_ref[d])
        akh = a * kh
        bonus_ref[d] = _dot_sel(r * kt * rk_ref[...], bd) * v

        ahead = (t_io - s_io) if d == 0 else (s_io - t_io)
        earlier = ahead > 0
        upto = ahead >= 0
        tri = jnp.where(upto, 1.0, 0.0).astype(BF16)
        cl = _sel_dot(tri, lw)
        tot = jnp.sum(lw, axis=0, keepdims=True)
        rho = 0.5 * tot
        cle = cl - lw
        a_true = -kh * jnp.exp(cle)
        r_true = r * jnp.exp(cl)
        a_c = -kh * jnp.exp(cle - rho)
        r_c = r * jnp.exp(cl - rho)
        b_c = akh * jnp.exp(rho - cl)
        k_c = kt * jnp.exp(rho - cl)
        b_end = akh * jnp.exp(tot - cl)
        k_end = kt * jnp.exp(tot - cl)

        pair = _dot(jnp.concatenate([stack(a_c), stack(r_c)], axis=0),
                    jnp.concatenate([b_c, k_c], axis=0), _NT)
        l_ab, l_ak, a_rb, a_rk = [], [], [], []
        for h in range(RW_HEADS):
            blk_a = pair[h * c:(h + 1) * c]
            blk_r = pair[(RW_HEADS + h) * c:(RW_HEADS + h + 1) * c]
            l_ab.append(jnp.where(earlier, blk_a[:, 0:c], 0.0))
            l_ak.append(jnp.where(earlier, blk_a[:, c:2 * c], 0.0))
            a_rb.append(jnp.where(upto, blk_r[:, 0:c], 0.0))
            a_rk.append(jnp.where(upto, blk_r[:, c:2 * c], 0.0))
        prep.append((l_ab, cat(l_ak), cat(a_rb), cat(a_rk), v, a_true, r_true, b_end, k_end, tot))

    t_all = _inv_unit_lower(prep[0][0] + prep[1][0], eye)

    for d in range(2):
        _, lak, arb, ark, v, a_true, r_true, b_end, k_end, tot = prep[d]
        t_inv = cat(t_all[d * RW_HEADS:(d + 1) * RW_HEADS])
        lv = _dot(lak, stack(v))
        w12 = _dot(t_inv, jnp.concatenate([stack(a_true), stack(lv)], axis=1))
        w1 = w12[:, 0:BW]
        w2 = w12[:, BW:2 * BW]
        q_ref[d, 0] = (r_true + _dot(arb, stack(w1))).astype(BF16)
        y0_ref[d, 0] = _dot(arb, stack(w2)) + _dot(ark, stack(v))
        b_end_t = b_end.T
        decay = jnp.where(row == colm, jnp.broadcast_to(jnp.exp(tot), (BW, BW)), 0.0)
        m_ref[d, 0] = (jnp.where(same_head, _dot(b_end_t, w1), 0.0) + decay).astype(BF16)
        n_ref[d, 0] = jnp.where(same_head, _dot(b_end_t, w2) + _dot(k_end.T, v), 0.0)


def _rw_chunk_call(u, lp, bd, tpb):
    n = u.shape[0]
    cpb = tpb * (TILE // CHUNK)
    nch = n // CHUNK
    per = CHUNK // 8
    n8 = n // 8
    wrw = 1024
    col = C_RW // wrw
    pvec = lambda w: pl.BlockSpec((2, 1, w), lambda i: (0, 0, 0))
    pad = jnp.zeros((2, RW_DECAY_RANK, BW), F32)
    wup = jnp.concatenate([lp['rw_w_up'], pad], axis=1).astype(BF16)
    aup = jnp.concatenate([pad, lp['rw_a_up']], axis=1).astype(BF16)
    mat = lambda rows: pl.BlockSpec((2, 1, rows, BW), lambda i: (0, i, 0, 0))
    return pl.pallas_call(
        functools.partial(_rw_chunk_kernel, cpb=cpb),
        grid=(nch,),
        in_specs=[pl.BlockSpec((8, wrw), lambda i: (jnp.maximum(i * per - 1, 0), col)),
                  pl.BlockSpec((CHUNK, wrw), lambda i: (i, col)),
                  pl.BlockSpec((8, wrw), lambda i: (jnp.minimum((i + 1) * per, n8 - 1), col)),
                  pvec(RW_SHIFT_W), pvec(BW), pvec(BW), pvec(BW), pvec(BW),
                  pl.BlockSpec((2, 2 * RW_DECAY_RANK, BW), lambda i: (0, 0, 0)),
                  pl.BlockSpec((2, 2 * RW_A_RANK, BW), lambda i: (0, 0, 0)),
                  pl.BlockSpec((1, BW), lambda i: (0, 0)),
                  pl.BlockSpec((BW, BW), lambda i: (0, 0))],
        out_specs=[mat(BW), mat(BW), mat(CHUNK), mat(CHUNK),
                   pl.BlockSpec((2, CHUNK, BW), lambda i: (0, i, 0))],
        out_shape=[jax.ShapeDtypeStruct((2, nch, BW, BW), BF16),
                   jax.ShapeDtypeStruct((2, nch, BW, BW), F32),
                   jax.ShapeDtypeStruct((2, nch, CHUNK, BW), BF16),
                   jax.ShapeDtypeStruct((2, nch, CHUNK, BW), F32),
                   jax.ShapeDtypeStruct((2, n, BW), F32)],
        scratch_shapes=[pltpu.VMEM((CHUNK + 16, RW_SHIFT_W), F32)],
        compiler_params=_cparams(("arbitrary",)),
        name="rwkv_chunk",
    )(u, u, u, lp['rw_mu'].reshape(2, 1, RW_SHIFT_W), lp['rw_w0'].reshape(2, 1, BW),
      lp['rw_a0'].reshape(2, 1, BW), lp['rw_kk'].reshape(2, 1, BW), lp['rw_ka'].reshape(2, 1, BW),
      wup, aup, lp['rw_rk'].reshape(1, BW), bd)


def _rw_scan_kernel(*refs, n_batch):
    m_refs, n_refs, q_refs, y0_refs = refs[0:2], refs[2:4], refs[4:6], refs[6:8]
    y_refs, x_ref = refs[8:10], refs[10]

    @pl.when(pl.program_id(0) == 0)
    def _():
        x_ref[...] = jnp.zeros(x_ref.shape, F32)

    for d in range(2):
        for b in range(n_batch):
            x = x_ref[d, b]
            y_refs[d][b, 0] = _dot(q_refs[d][0, b, 0], x) + y0_refs[d][0, b, 0]
            x_ref[d, b] = _dot(m_refs[d][0, b, 0], x) + n_refs[d][0, b, 0]


def _rw_scan_call(m, nn, q, y0, n_batch, tpb):
    cpb = tpb * (TILE // CHUNK)
    ctx_chunks = TILE // CHUNK

    def chunk(d, i):
        rev = jnp.where(i < ctx_chunks, ctx_chunks - 1 - i, cpb + ctx_chunks - 1 - i)
        return i if d == 0 else rev

    def mat(rows, d):
        return pl.BlockSpec((1, n_batch, 1, rows, BW), lambda i: (d, 0, chunk(d, i), 0, 0))

    ins, specs = [], []
    for arr, rows in ((m, BW), (nn, BW), (q, CHUNK), (y0, CHUNK)):
        arr = arr.reshape(2, n_batch, cpb, rows, BW)
        for d in range(2):
            ins.append(arr)
            specs.append(mat(rows, d))
    yshape = jax.ShapeDtypeStruct((n_batch, cpb, CHUNK, BW), F32)
    ys = pl.pallas_call(
        functools.partial(_rw_scan_kernel, n_batch=n_batch),
        grid=(cpb,),
        in_specs=specs,
        out_specs=[pl.BlockSpec((n_batch, 1, CHUNK, BW), lambda i, d=d: (0, chunk(d, i), 0, 0))
                   for d in range(2)],
        out_shape=[yshape, yshape],
        scratch_shapes=[pltpu.VMEM((2, n_batch, BW, BW), F32)],
        compiler_params=_cparams(("arbitrary",)),
        name="rwkv_scan",
    )(*ins)
    return [y.reshape(n_batch * cpb * CHUNK, BW) for y in ys]


def _rw_out_kernel(yf_ref, yb_ref, bonus_ref, u_ref, gup_ref, gn_ref, bd_ref, o_ref):
    bd = bd_ref[...]
    y = yf_ref[...] + yb_ref[...]
    mean = _dot_sel(y, bd) * (1.0 / RW_DH)
    yc = y - mean
    var = _dot_sel(yc * yc, bd) * (1.0 / RW_DH)
    yn = yc * lax.rsqrt(var + RW_GN_EPS) * gn_ref[0:1, :] + gn_ref[1:2, :]
    yn = yn + (bonus_ref[0] + bonus_ref[1])
    o_ref[...] = yn * _dot(_sigmoid(u_ref[...]), gup_ref[...])


def _rw_out_call(ys, bonus, u, gup, gn, bd):
    n = u.shape[0]
    gcol = (C_RW + RW_SHIFT_W) // RW_G_RANK
    both = pl.BlockSpec((2, TILE, BW), lambda i: (0, i, 0))
    one = pl.BlockSpec((TILE, BW), lambda i: (i, 0))
    return pl.pallas_call(
        _rw_out_kernel,
        grid=(n // TILE,),
        in_specs=[one, one, both, pl.BlockSpec((TILE, RW_G_RANK), lambda i: (i, gcol)),
                  pl.BlockSpec((RW_G_RANK, BW), lambda i: (0, 0)),
                  pl.BlockSpec((2, BW), lambda i: (0, 0)),
                  pl.BlockSpec((BW, BW), lambda i: (0, 0))],
        out_specs=pl.BlockSpec((TILE, BW), lambda i: (i, 0)),
        out_shape=jax.ShapeDtypeStruct((n, BW), F32),
        compiler_params=_cparams(("arbitrary",)),
        name="rwkv_readout",
    )(ys[0], ys[1], bonus, u, gup, gn, bd)


def _merge_kernel(oa_ref, ob_ref, oc_ref, od_ref, gl_ref, wb_ref, wo_ref, x_ref, g_ref, mod_ref,
                  g2_ref, sh_ref, sc_ref, rw_ref, rb_ref, o_ref, tok_ref, gate_ref):
    d = x_ref.shape[1]
    m = None
    for i, o in enumerate((oa_ref, ob_ref, oc_ref, od_ref)):
        t = _sigmoid(gl_ref[:, i * d:(i + 1) * d]) * _dot(o[...], wb_ref[i])
        m = t if m is None else m + t
    y = _dot(m, wo_ref[...])
    y = y * lax.rsqrt(jnp.mean(y * y, axis=-1, keepdims=True) + EPS) * g_ref[...]
    x = x_ref[...] + mod_ref[0] * y
    o_ref[...] = x
    t = x * lax.rsqrt(jnp.mean(x * x, axis=-1, keepdims=True) + EPS) * g2_ref[...]
    tok = t * (1.0 + sc_ref[0]) + sh_ref[0]
    tok_ref[...] = tok.astype(BF16)
    gate_ref[...] = _route(tok, rw_ref[...], rb_ref[...])


def _merge_call(outs, u, wb, wo, x, g, mod, g2, shift, scale, rw_t, rb, tpb, n_batch):
    n, d = x.shape
    kind = lambda i: (jnp.where(i % tpb == 0, n_batch, i // tpb), 0, 0)
    br = pl.BlockSpec((TILE, BW), lambda i: (i, 0))
    vec = pl.BlockSpec((1, d), lambda i: (0, 0))
    row = pl.BlockSpec((TILE, d), lambda i: (i, 0))
    wg = N_BRANCH * d
    return pl.pallas_call(
        _merge_kernel,
        grid=(n // TILE,),
        in_specs=[br, br, br, br,
                  pl.BlockSpec((TILE, wg), lambda i: (i, C_GATE // wg)),
                  pl.BlockSpec((N_BRANCH, BW, d), lambda i: (0, 0, 0)),
                  pl.BlockSpec((d, d), lambda i: (0, 0)),
                  row, vec, pl.BlockSpec((1, 1, d), kind),
                  vec, pl.BlockSpec((1, 1, d), kind), pl.BlockSpec((1, 1, d), kind),
                  pl.BlockSpec((N_EXPERTS, d), lambda i: (0, 0)),
                  pl.BlockSpec((N_EXPERTS, 1), lambda i: (0, 0))],
        out_specs=[row, row, pl.BlockSpec((N_EXPERTS, TILE), lambda i: (0, i))],
        out_shape=[jax.ShapeDtypeStruct((n, d), F32), jax.ShapeDtypeStruct((n, d), BF16),
                   jax.ShapeDtypeStruct((N_EXPERTS, n), F32)],
        compiler_params=_cparams(("arbitrary",)),
        name="merge",
    )(*outs, u, wb, wo, x, g.reshape(1, d), mod, g2.reshape(1, d), shift, scale,
      rw_t, rb.reshape(N_EXPERTS, 1))


def _route(tokens, w, b):
    tm = tokens.shape[0]
    gsz = N_EXPERTS // N_GROUPS
    logits = _dot3(w, tokens, _NT)
    sc = _sigmoid(logits).reshape(N_GROUPS, gsz, tm)
    bi = sc + b.reshape(N_GROUPS, gsz, 1)
    shape = (N_GROUPS, gsz, tm)
    g_io = lax.broadcasted_iota(jnp.int32, shape, 0)
    j_io = lax.broadcasted_iota(jnp.int32, shape, 1)
    e_io = g_io * gsz + j_io
    ninf = -jnp.inf
    m1 = jnp.max(bi, axis=1, keepdims=True)
    i1 = jnp.min(jnp.where(bi == m1, j_io, gsz), axis=1, keepdims=True)
    m2 = jnp.max(jnp.where(j_io == i1, ninf, bi), axis=1, keepdims=True)
    cur = jnp.broadcast_to(m1 + m2, shape)
    gsel = jnp.zeros(shape, F32)
    for _ in range(TOPK_GROUPS):
        mx = jnp.max(cur, axis=0, keepdims=True)
        ix = jnp.min(jnp.where(cur == mx, g_io, N_GROUPS), axis=0, keepdims=True)
        hit = g_io == ix
        gsel = jnp.where(hit, 1.0, gsel)
        cur = jnp.where(hit, ninf, cur)
    cur = jnp.where(gsel > 0.0, bi, ninf)
    esel = jnp.zeros(shape, F32)
    for _ in range(TOP_K):
        mx = jnp.max(jnp.max(cur, axis=0, keepdims=True), axis=1, keepdims=True)
        ix = jnp.min(jnp.min(jnp.where(cur == mx, e_io, N_EXPERTS), axis=0, keepdims=True),
                     axis=1, keepdims=True)
        hit = e_io == ix
        esel = jnp.where(hit, 1.0, esel)
        cur = jnp.where(hit, ninf, cur)
    wsel = sc * esel
    den = jnp.sum(jnp.sum(wsel, axis=0, keepdims=True), axis=1, keepdims=True)
    return (wsel / den * ROUTE_SCALE).reshape(N_EXPERTS, tm)


def _moe_kernel(x_ref, g_ref, wgu_ref, wd_ref, sgu_ref, sd_ref, o_ref):
    e = pl.program_id(1)

    def ffn(wgu, wd):
        hgu = _dot(x_ref[...], wgu)
        return _dot(_silu(hgu[:, :D_EXPERT]) * hgu[:, D_EXPERT:], wd)

    @pl.when(e == 0)
    def _():
        o_ref[...] = ffn(sgu_ref[...], sd_ref[...])

    gates = g_ref[...]
    lane = lax.broadcasted_iota(jnp.int32, gates.shape, 1)
    gcol = jnp.sum(jnp.where(lane == e, gates, 0.0), axis=1, keepdims=True)
    o_ref[...] += ffn(wgu_ref[0, 0], wd_ref[0, 0]) * gcol


def _moe_call(tok, gates, wgu, wd, layer, sgu, sd, tm):
    n, d = tok.shape
    return pl.pallas_call(
        _moe_kernel,
        grid=(n // tm, N_EXPERTS),
        in_specs=[pl.BlockSpec((tm, d), lambda i, e: (i, 0)),
                  pl.BlockSpec((tm, N_EXPERTS), lambda i, e: (i, 0)),
                  pl.BlockSpec((1, 1, d, 2 * D_EXPERT), lambda i, e: (layer, e, 0, 0)),
                  pl.BlockSpec((1, 1, D_EXPERT, d), lambda i, e: (layer, e, 0, 0)),
                  pl.BlockSpec((d, 2 * D_EXPERT), lambda i, e: (0, 0)),
                  pl.BlockSpec((D_EXPERT, d), lambda i, e: (0, 0))],
        out_specs=pl.BlockSpec((tm, d), lambda i, e: (i, 0)),
        out_shape=jax.ShapeDtypeStruct((n, d), F32),
        compiler_params=_cparams(("arbitrary", "arbitrary"), MOE_VMEM_LIMIT),
        name="moe_experts",
    )(tok, gates, wgu, wd, sgu, sd)


def _resid_kernel(x_ref, f_ref, g_ref, mod_ref, o_ref):
    f = f_ref[...]
    y = f * lax.rsqrt(jnp.mean(f * f, axis=-1, keepdims=True) + EPS) * g_ref[...]
    o_ref[...] = x_ref[...] + mod_ref[0] * y


def _resid_call(x, f, g, mod, tpb, n_batch, latent_only=False):
    n, d = x.shape
    vec = pl.BlockSpec((1, d), lambda *_: (0, 0))
    if latent_only:
        lpb = tpb - 1
        grid = (n_batch, lpb)
        row = pl.BlockSpec((TILE, d), lambda b, i: (b * tpb + 1 + i, 0))
        out = pl.BlockSpec((TILE, d), lambda b, i: (b * lpb + i, 0))
        kind = pl.BlockSpec((1, 1, d), lambda b, i: (b, 0, 0))
        n_out = n_batch * lpb * TILE
    else:
        grid = (n // TILE,)
        row = out = pl.BlockSpec((TILE, d), lambda i: (i, 0))
        kind = pl.BlockSpec((1, 1, d), lambda i: (jnp.where(i % tpb == 0, n_batch, i // tpb), 0, 0))
        n_out = n
    return pl.pallas_call(
        _resid_kernel,
        grid=grid,
        in_specs=[row, row, vec, kind],
        out_specs=out,
        out_shape=jax.ShapeDtypeStruct((n_out, d), F32),
        compiler_params=_cparams(("arbitrary",) * len(grid)),
        name="moe_residual",
    )(x, f, g.reshape(1, d), mod)


def _proj_weights(w):
    w = w.astype(BF16)
    k = w.shape[0]
    da, cv, sw = 0, 3 * BW, 5 * BW
    rw = sw + (SW_HEADS + 2 * SW_KV) * SW_DH
    gate = rw + RW_SHIFT_W + RW_G_RANK

    def per_q_head(x):
        return jnp.repeat(x.reshape(k, SW_KV, SW_DH), SW_HEADS // SW_KV, axis=1).reshape(k, -1)

    da_q, da_k, da_v = (w[:, da + i * BW:da + (i + 1) * BW] for i in range(3))
    sw_q = w[:, sw:sw + BW]
    sw_k = per_q_head(w[:, sw + BW:sw + BW + SW_KV * SW_DH])
    sw_v = per_q_head(w[:, sw + BW + SW_KV * SW_DH:rw])
    out = jnp.concatenate([
        w[:, gate:], da_q, da_k, sw_q, sw_k, da_v, sw_v, w[:, cv:sw], w[:, rw:gate]], axis=1)
    assert out.shape[1] == C_TOTAL
    return out


def _rope_tables(seq):
    rows = seq // GRID_W
    row = jnp.repeat(jnp.arange(rows, dtype=F32), GRID_W)
    colp = jnp.tile(jnp.arange(GRID_W, dtype=F32), rows)

    def tables(dim):
        q = dim // 4
        freqs = ROPE_BASE ** (-jnp.arange(q, dtype=F32) / q)
        ar, ac = row[:, None] * freqs, colp[:, None] * freqs
        cos = jnp.concatenate([jnp.cos(ar), jnp.cos(ar), jnp.cos(ac), jnp.cos(ac)], axis=1)
        sin = jnp.concatenate([-jnp.sin(ar), jnp.sin(ar), -jnp.sin(ac), jnp.sin(ac)], axis=1)
        return jnp.tile(cos, (1, 128 // dim)), jnp.tile(sin, (1, 128 // dim))

    c32, s32 = tables(DA_HALF)
    c64, s64 = tables(SW_DH)
    cos = jnp.concatenate([c32, c64], axis=1)
    sin = jnp.concatenate([s32, s64], axis=1)
    w = cos.shape[1]
    cos = jnp.concatenate([jnp.ones((TILE, w), F32), cos], axis=0)
    sin = jnp.concatenate([jnp.zeros((TILE, w), F32), sin], axis=0)
    return cos, sin


def _moe_tile(rows_b):
    best = TILE
    for t in range(TILE, MOE_MAX_TILE + 1, 16):
        if rows_b % t == 0:
            best = t
    return best


def kernel(x, c, ctx, c_ctx, ada_w, ada_b, norm_g, w_in, w_branch, w_out, da_lambda, da_subln,
           cv_w, cv_b, cv_ln, sw_sink, rw_mu, rw_w0, rw_w_up, rw_a0, rw_a_up, rw_kk, rw_ka,
           rw_g_up, rw_rk, rw_gn, router_w, router_b, ex_w_gu, ex_w_down, sh_w_gu, sh_w_down):
    n_batch, seq, d = x.shape
    ctx_len = ctx.shape[1]
    depth = w_in.shape[0]
    assert ctx_len == TILE and seq % TILE == 0 and seq % GRID_W == 0
    assert n_batch + 1 <= 8
    rows_b = ctx_len + seq
    tpb = rows_b // TILE
    n = n_batch * rows_b

    xs = jnp.concatenate([ctx, x], axis=1).reshape(n, d)
    cond = jnp.zeros((8, d), F32).at[:n_batch].set(c).at[n_batch].set(c_ctx)
    mods = _ada_call(cond, ada_w, ada_b)[:, :n_batch + 1]
    cos_t, sin_t = _rope_tables(seq)
    hio = np.arange(BW) // RW_DH
    bd = jnp.asarray(hio[:, None] == hio[None, :], BF16)

    for l in range(depth):
        mod = [mods[l, :, i * d:(i + 1) * d].reshape(n_batch + 1, 1, d) for i in range(6)]
        ng = norm_g[l]
        lam_init = 0.8 - 0.6 * math.exp(-0.3 * l)
        lv = da_lambda[l]
        lam = (jnp.exp(jnp.sum(lv[0] * lv[1])) - jnp.exp(jnp.sum(lv[2] * lv[3])) + lam_init)
        lp = {'rw_mu': rw_mu[l], 'rw_w0': rw_w0[l], 'rw_w_up': rw_w_up[l], 'rw_a0': rw_a0[l],
              'rw_a_up': rw_a_up[l], 'rw_kk': rw_kk[l], 'rw_ka': rw_ka[l], 'rw_rk': rw_rk[l]}

        h = _norm_mod_call(xs, ng[0], mod[0], mod[1], tpb, n_batch)
        u = _proj_call(h, _proj_weights(w_in[l]))
        qk, vt = _rope_call(u, cos_t, sin_t, n_batch, tpb)
        oa = _da_call(lam.reshape(1), qk, vt, jnp.tile(da_subln[l], DA_HEADS).reshape(1, BW),
                      n_batch, tpb, lam_init)
        ob = _conv_call(u, cv_w[l], cv_b[l], cv_ln[l], tpb)
        oc = _sw_call(sw_sink[l], qk, u, n_batch, tpb)
        cm, cn, cq, cy0, bonus = _rw_chunk_call(u, lp, bd, tpb)
        yscan = _rw_scan_call(cm, cn, cq, cy0, n_batch, tpb)
        od = _rw_out_call(yscan, bonus, u, rw_g_up[l].astype(BF16), rw_gn[l], bd)
        xs, tok, gates_t = _merge_call(
            (oa, ob, oc, od), u, w_branch[l].astype(BF16), w_out[l].astype(BF16), xs, ng[1],
            mod[2], ng[2], mod[3], mod[4], router_w[l].T, router_b[l], tpb, n_batch)
        f = _moe_call(tok, gates_t.T, ex_w_gu, ex_w_down, l, sh_w_gu[l].astype(BF16),
                      sh_w_down[l].astype(BF16), _moe_tile(rows_b))
        xs = _resid_call(xs, f, ng[3], mod[5], tpb, n_batch, latent_only=(l == depth - 1))

    return xs.reshape(n_batch, seq, d)
```

```python
import functools
import math

import numpy as np
import jax
import jax.numpy as jnp
from jax import lax
from jax.experimental import pallas as pl
from jax.experimental.pallas import tpu as pltpu

F32 = jnp.float32
BF16 = jnp.bfloat16
SUBLANES, LANES = 8, 128

GRID_W = 64
EPS = 1e-6
ROPE_BASE = 10000.0
N_BRANCH = 4
BW = 256
DA_HEADS = 4
DA_HALF = 32
SW_HEADS = 4
SW_KV = 2
SW_DH = 64
WINDOW = 128
CONV_W = 31
RW_HEADS = 4
RW_DH = 64
RW_DECAY_RANK = 64
RW_A_RANK = 64
RW_G_RANK = 128
RW_DECAY_SCALE = math.exp(-0.5)
RW_GN_EPS = 64e-5
RW_SHIFT_W = 3 * BW + RW_DECAY_RANK + RW_A_RANK
N_EXPERTS = 64
TOP_K = 6
N_GROUPS = 8
TOPK_GROUPS = 4
D_EXPERT = 256
ROUTE_SCALE = 2.5

TILE = 256
CHUNK = 128
DA_KT = 1408
MOE_MAX_TILE = 2816
MOE_VMEM_LIMIT = 58 * 1024 * 1024
DA_VROWS = 80
NEG = -1e30

C_GATE = 0
C_ROPE = 4096
C_DAV = 5120
C_SWV = 5376
C_CV = 5632
C_RW = 6144
C_TOTAL = 7168

VMEM_LIMIT = 48 * 1024 * 1024

_NT = (((1,), (1,)), ((), ()))
_NN = (((1,), (0,)), ((), ()))


def _cparams(sem, vmem=VMEM_LIMIT):
    return pltpu.CompilerParams(dimension_semantics=sem, vmem_limit_bytes=vmem)


def _dot(a, b, dims=_NN):
    return lax.dot_general(a.astype(BF16), b.astype(BF16), dims, preferred_element_type=F32)


def _split2(x):
    hi = x.astype(BF16)
    lo = (x - hi.astype(F32)).astype(BF16)
    return hi, lo


def _dot3(a, b, dims=_NN):
    ah, al = _split2(a)
    bh, bl = _split2(b)
    dg = lambda x, y: lax.dot_general(x, y, dims, preferred_element_type=F32)
    return dg(ah, bh) + (dg(ah, bl) + dg(al, bh))


def _dot_sel(x, sel, dims=_NN):
    h0 = x.astype(BF16)
    r1 = x - h0.astype(F32)
    h1 = r1.astype(BF16)
    h2 = (r1 - h1.astype(F32)).astype(BF16)
    dg = lambda y: lax.dot_general(y, sel, dims, preferred_element_type=F32)
    return dg(h0) + (dg(h1) + dg(h2))


def _sel_dot(sel, x):
    h0 = x.astype(BF16)
    r1 = x - h0.astype(F32)
    h1 = r1.astype(BF16)
    h2 = (r1 - h1.astype(F32)).astype(BF16)
    dg = lambda y: lax.dot_general(sel, y, _NN, preferred_element_type=F32)
    return dg(h0) + (dg(h1) + dg(h2))


def _sigmoid(x):
    return jax.nn.sigmoid(x)


def _silu(x):
    return x * jax.nn.sigmoid(x)


def _head_of_lane(shape, width):
    return lax.broadcasted_iota(jnp.int32, shape, len(shape) - 1) // width


def _ada_kernel(s_ref, w_ref, b_ref, o_ref):
    s = _silu(s_ref[...])
    o_ref[0] = _dot(s, w_ref[0]) + b_ref[0]


def _ada_call(cond, ada_w, ada_b):
    depth, d, cols = ada_w.shape
    tn = 1536
    return pl.pallas_call(
        _ada_kernel,
        grid=(depth, cols // tn),
        in_specs=[pl.BlockSpec((SUBLANES, d), lambda l, j: (0, 0)),
                  pl.BlockSpec((1, d, tn), lambda l, j: (l, 0, j)),
                  pl.BlockSpec((1, 1, tn), lambda l, j: (l, 0, j))],
        out_specs=pl.BlockSpec((1, SUBLANES, tn), lambda l, j: (l, 0, j)),
        out_shape=jax.ShapeDtypeStruct((depth, SUBLANES, cols), F32),
        compiler_params=_cparams(("arbitrary", "arbitrary")),
        name="ada_mod",
    )(cond, ada_w, ada_b.reshape(depth, 1, cols))


PROJ_TILES = 2


def _proj_kernel(x_ref, g_ref, *refs):
    mods, w_ref, o_ref = refs[:2 * PROJ_TILES], refs[2 * PROJ_TILES], refs[2 * PROJ_TILES + 1]
    x = x_ref[...]
    y = x * lax.rsqrt(jnp.mean(x * x, axis=-1, keepdims=True) + EPS) * g_ref[...]
    h = jnp.concatenate(
        [(y[t * TILE:(t + 1) * TILE] * (1.0 + mods[2 * t + 1][0]) + mods[2 * t][0]).astype(BF16)
         for t in range(PROJ_TILES)], axis=0)
    o_ref[...] = jnp.dot(h, w_ref[...], preferred_element_type=F32)


def _proj_call(x, g, shift, scale, w, tpb, n_batch):
    n, d = x.shape
    cols = w.shape[1]
    tm, tn = PROJ_TILES * TILE, cols // 2

    def kind(t):
        def index(j, i):
            tile = i * PROJ_TILES + t
            return (jnp.where(tile % tpb == 0, n_batch, tile // tpb), 0, 0)
        return pl.BlockSpec((1, 1, d), index)

    mod_specs, mod_args = [], []
    for t in range(PROJ_TILES):
        mod_specs += [kind(t), kind(t)]
        mod_args += [shift, scale]
    return pl.pallas_call(
        _proj_kernel,
        grid=(cols // tn, n // tm),
        in_specs=[pl.BlockSpec((tm, d), lambda j, i: (i, 0)),
                  pl.BlockSpec((1, d), lambda j, i: (0, 0)), *mod_specs,
                  pl.BlockSpec((d, tn), lambda j, i: (0, j))],
        out_specs=pl.BlockSpec((tm, tn), lambda j, i: (i, j)),
        out_shape=jax.ShapeDtypeStruct((n, cols), F32),
        compiler_params=_cparams(("arbitrary", "arbitrary")),
        name="in_proj",
    )(x, g.reshape(1, d), *mod_args, w)


def _rope_kernel(u_ref, c_ref, s_ref, v_ref, o_ref, vt_ref):
    lanes = LANES
    n_batch, rows, w = u_ref.shape
    lane = lax.broadcasted_iota(jnp.int32, (rows, lanes), 1)
    hd = 2 * DA_HALF
    qscale = (DA_HALF ** -0.5) * math.log2(math.e)
    for b in range(n_batch):
        for k in range(w // lanes):
            cols = slice(k * lanes, (k + 1) * lanes)
            is_da = k * lanes < 2 * BW
            q = (DA_HALF if is_da else SW_DH) // 4
            tab = slice(0, lanes) if is_da else slice(lanes, 2 * lanes)
            x = u_ref[b, :, cols]
            partner = jnp.where((lane // q) % 2 == 0, pltpu.roll(x, lanes - q, 1),
                                pltpu.roll(x, q, 1))
            y = x * c_ref[:, tab] + partner * s_ref[:, tab]
            if k * lanes < BW:
                y = y * qscale
            o_ref[b, :, cols] = y.astype(BF16)
        vt = v_ref[b].T.astype(BF16)
        for h in range(DA_HEADS):
            vt_ref[b, h * DA_VROWS:h * DA_VROWS + hd, :] = vt[h * hd:(h + 1) * hd, :]
            vt_ref[b, h * DA_VROWS + hd:(h + 1) * DA_VROWS, :] = jnp.ones((DA_VROWS - hd, rows),
                                                                          BF16)


def _rope_call(u, cos_t, sin_t, n_batch, tpb):
    n, wu = u.shape
    rows_b = tpb * TILE
    w = 4 * BW
    vrows = DA_HEADS * DA_VROWS
    tab = pl.BlockSpec((TILE, cos_t.shape[1]), lambda i: (i, 0))
    u3 = u.reshape(n_batch, rows_b, wu)
    qk, vt = pl.pallas_call(
        _rope_kernel,
        grid=(tpb,),
        in_specs=[pl.BlockSpec((n_batch, TILE, w), lambda i: (0, i, C_ROPE // w)), tab, tab,
                  pl.BlockSpec((n_batch, TILE, BW), lambda i: (0, i, C_DAV // BW))],
        out_specs=[pl.BlockSpec((n_batch, TILE, w), lambda i: (0, i, 0)),
                   pl.BlockSpec((n_batch, vrows, TILE), lambda i: (0, 0, i))],
        out_shape=[jax.ShapeDtypeStruct((n_batch, rows_b, w), BF16),
                   jax.ShapeDtypeStruct((n_batch, vrows, rows_b), BF16)],
        compiler_params=_cparams(("arbitrary",)),
        name="rope",
    )(u3, cos_t, sin_t, u3)
    return qk.reshape(n, w), vt.reshape(n_batch * vrows, rows_b)


def _da_kernel(lam_ref, q_ref, k_ref, vt_ref, g_ref, o_ref,
               qs_ref, m_ref, acc_ref, *, nkt, lam_init):
    i = pl.program_id(1)
    tq = q_ref.shape[0]
    hd = 2 * DA_HALF
    q = q_ref[...]
    qmap = _head_of_lane((tq, BW), DA_HALF)
    for g in range(2 * DA_HEADS):
        qs_ref[g] = jnp.where(qmap == g, q, jnp.zeros_like(q))
    m_ref[...] = jnp.full(m_ref.shape, NEG, F32)
    acc_ref[...] = jnp.zeros(acc_ref.shape, F32)

    def tile(off, size):
        kt = k_ref[pl.ds(off, size), :]
        groups = range(2 * DA_HEADS)
        ss = [lax.dot_general(kt, qs_ref[g], _NT, preferred_element_type=F32) for g in groups]
        for g in groups:
            s = ss[g]
            m_old = m_ref[g]
            m_new = jnp.maximum(m_old, jnp.max(s, axis=0, keepdims=True))
            alpha = jnp.exp2(m_old - m_new)[0:1, :]
            p = jnp.exp2(s - m_new[0:1, :]).astype(BF16)
            m_ref[g] = m_new
            h, mm = g // 2, g % 2
            rows = slice(h * DA_VROWS, (h + 1) * DA_VROWS)
            pv = jnp.dot(vt_ref[rows, pl.ds(off, size)], p, preferred_element_type=F32)
            acc_ref[mm, rows, :] = acc_ref[mm, rows, :] * alpha + pv

    @pl.when(i == 0)
    def _():
        tile(0, TILE)

    @pl.when(i > 0)
    def _():
        def body(j, carry):
            tile(pl.multiple_of(j * DA_KT, DA_KT), DA_KT)
            return carry

        lax.fori_loop(0, nkt, body, 0)

    lam = lam_ref[0]
    parts = []
    for h in range(DA_HEADS):
        rows = slice(h * DA_VROWS, h * DA_VROWS + hd)
        den = slice(h * DA_VROWS + hd, h * DA_VROWS + hd + 1)
        o_h = (acc_ref[0, rows, :] / acc_ref[0, den, :]
               - lam * (acc_ref[1, rows, :] / acc_ref[1, den, :]))
        ms = jnp.mean(o_h * o_h, axis=0, keepdims=True)
        parts.append(o_h * lax.rsqrt(ms + EPS))
    y = jnp.concatenate(parts, axis=0).T * g_ref[...]
    o_ref[...] = y * (1.0 - lam_init)


def _da_call(lam, qk, vt, subln, n_batch, tpb, lam_init):
    n = qk.shape[0]
    rows_b = tpb * TILE
    assert rows_b % DA_KT == 0
    vrows = DA_HEADS * DA_VROWS
    kern = functools.partial(_da_kernel, nkt=rows_b // DA_KT, lam_init=lam_init)
    return pl.pallas_call(
        kern,
        grid=(n_batch, tpb),
        in_specs=[pl.BlockSpec(memory_space=pltpu.SMEM),
                  pl.BlockSpec((TILE, BW), lambda b, i: (b * tpb + i, 0)),
                  pl.BlockSpec((rows_b, BW), lambda b, i: (b, 1)),
                  pl.BlockSpec((vrows, rows_b), lambda b, i: (b, 0)),
                  pl.BlockSpec((1, BW), lambda b, i: (0, 0))],
        out_specs=pl.BlockSpec((TILE, BW), lambda b, i: (b * tpb + i, 0)),
        out_shape=jax.ShapeDtypeStruct((n, BW), F32),
        scratch_shapes=[pltpu.VMEM((2 * DA_HEADS, TILE, BW), BF16),
                        pltpu.VMEM((2 * DA_HEADS, SUBLANES, TILE), F32),
                        pltpu.VMEM((2, vrows, TILE), F32)],
        compiler_params=_cparams(("arbitrary", "arbitrary")),
        name="diff_attn",
    )(lam, qk, qk, vt, subln)


def _sw_kernel(sink_ref, bias_ref, q_ref, kp_ref, ko_ref, kn_ref, kc_ref,
               vp_ref, vo_ref, vn_ref, vc_ref, o_ref):
    tq = q_ref.shape[0]
    q = q_ref[...]
    kk = jnp.concatenate([kp_ref[...], ko_ref[...], kn_ref[...], kc_ref[...]], axis=0)
    vv = jnp.concatenate([vp_ref[...], vo_ref[...], vn_ref[...], vc_ref[...]], axis=0).astype(BF16)
    nk = kk.shape[0]
    bias = bias_ref[0]
    qhead = _head_of_lane((tq, BW), SW_DH)
    vhead = _head_of_lane((nk, BW), SW_DH)
    ps, vs = [], []
    for h in range(SW_HEADS):
        qm = jnp.where(qhead == h, q, jnp.zeros_like(q))
        s = lax.dot_general(qm, kk, _NT, preferred_element_type=F32) * (SW_DH ** -0.5) + bias
        sk = sink_ref[h]
        m = jnp.maximum(jnp.max(s, axis=1, keepdims=True), sk)
        p = jnp.exp(s - m)
        den = jnp.sum(p, axis=1, keepdims=True) + jnp.exp(sk - m)
        ps.append((p / den).astype(BF16))
        vs.append(jnp.where(vhead == h, vv, jnp.zeros_like(vv)))
    o_ref[...] = jnp.dot(jnp.concatenate(ps, axis=1), jnp.concatenate(vs, axis=0),
                         preferred_element_type=F32)


def _sw_bias():
    r = np.arange(CHUNK)[:, None]
    c = np.arange(CHUNK)[None, :]
    prev_ok, next_ok = c >= r, c <= r
    yes, no = np.ones((CHUNK, CHUNK), bool), np.zeros((CHUNK, CHUNK), bool)
    ctx = np.ones((CHUNK, TILE), bool)
    kinds = [np.concatenate([no, no, no, ctx], axis=1)]
    for no_next in (False, True):
        for no_prev in (False, True):
            kinds.append(np.concatenate([no if no_prev else prev_ok, yes,
                                         no if no_next else next_ok, ctx], axis=1))
    return jnp.asarray(np.where(np.stack(kinds), 0.0, NEG), F32)


def _sw_call(sink, qk, u, n_batch, tpb):
    n = qk.shape[0]
    cpb = tpb * (TILE // CHUNK)
    ctx_chunks = TILE // CHUNK
    bias = _sw_bias()

    def kind(b, j):
        lat = 1 + (j == ctx_chunks).astype(jnp.int32) + 2 * (j == cpb - 1).astype(jnp.int32)
        return (jnp.where(j < ctx_chunks, 0, lat), 0, 0)

    cur = lambda b, j: b * cpb + j
    prv = lambda b, j: b * cpb + jnp.maximum(j - 1, 0)
    nxt = lambda b, j: b * cpb + jnp.minimum(j + 1, cpb - 1)
    kcol, vcol = 3, C_SWV // BW
    blk = lambda f, col: pl.BlockSpec((CHUNK, BW), lambda b, j: (f(b, j), col))
    ctx = lambda col: pl.BlockSpec((TILE, BW), lambda b, j: (b * tpb, col))
    return pl.pallas_call(
        _sw_kernel,
        grid=(n_batch, cpb),
        in_specs=[pl.BlockSpec(memory_space=pltpu.SMEM),
                  pl.BlockSpec((1,) + bias.shape[1:], kind),
                  blk(cur, 2), blk(prv, kcol), blk(cur, kcol), blk(nxt, kcol), ctx(kcol),
                  blk(prv, vcol), blk(cur, vcol), blk(nxt, vcol), ctx(vcol)],
        out_specs=pl.BlockSpec((CHUNK, BW), lambda b, j: (cur(b, j), 0)),
        out_shape=jax.ShapeDtypeStruct((n, BW), F32),
        compiler_params=_cparams(("arbitrary", "arbitrary")),
        name="window_attn",
    )(sink, bias, qk, qk, qk, qk, qk, u, u, u, u)


def _conv_kernel(prev_ref, cur_ref, next_ref, w_ref, b_ref, ln_ref, o_ref, z_ref, zs_ref, *, tpb):
    i = pl.program_id(0)
    pos = i % tpb
    has_prev = pos >= 2
    has_next = (pos >= 1) & (pos < tpb - 1)
    halo = prev_ref.shape[0]

    def glu(x):
        return x[:, :BW] * _sigmoid(x[:, BW:])

    zp = glu(prev_ref[...])
    zn = glu(next_ref[...])
    z_ref[0:halo, :] = jnp.where(has_prev, zp, 0.0)
    z_ref[halo:halo + TILE, :] = glu(cur_ref[...])
    z_ref[halo + TILE:2 * halo + TILE, :] = jnp.where(has_next, zn, 0.0)
    acc = jnp.zeros((TILE, BW), F32) + b_ref[...]
    pad = CONV_W // 2
    span = TILE + 2 * halo - SUBLANES
    for r in range(SUBLANES):
        zs_ref[r] = z_ref[r:r + span, :]
    for t in range(CONV_W):
        off = halo - pad + t
        shift, base = off % SUBLANES, off - off % SUBLANES
        acc = acc + zs_ref[shift, base:base + TILE, :] * w_ref[t:t + 1, :]
    mu = jnp.mean(acc, axis=-1, keepdims=True)
    xc = acc - mu
    y = xc * lax.rsqrt(jnp.mean(xc * xc, axis=-1, keepdims=True) + EPS)
    y = y * ln_ref[0:1, :] + ln_ref[1:2, :]
    o_ref[...] = _silu(y)


def _conv_call(u, w, b, ln, tpb):
    n = u.shape[0]
    halo = 16
    per = TILE // halo
    nh = n // halo
    wcv = 2 * BW
    col = C_CV // wcv
    return pl.pallas_call(
        functools.partial(_conv_kernel, tpb=tpb),
        grid=(n // TILE,),
        in_specs=[pl.BlockSpec((halo, wcv), lambda i: (jnp.maximum(i * per - 1, 0), col)),
                  pl.BlockSpec((TILE, wcv), lambda i: (i, col)),
                  pl.BlockSpec((halo, wcv), lambda i: (jnp.minimum((i + 1) * per, nh - 1), col)),
                  pl.BlockSpec((CONV_W, BW), lambda i: (0, 0)),
                  pl.BlockSpec((1, BW), lambda i: (0, 0)),
                  pl.BlockSpec((2, BW), lambda i: (0, 0))],
        out_specs=pl.BlockSpec((TILE, BW), lambda i: (i, 0)),
        out_shape=jax.ShapeDtypeStruct((n, BW), F32),
        scratch_shapes=[pltpu.VMEM((TILE + 2 * halo, BW), F32),
                        pltpu.VMEM((SUBLANES, TILE + 2 * halo - SUBLANES, BW), F32)],
        compiler_params=_cparams(("arbitrary",)),
        name="conformer_conv",
    )(u, u, u, w, b.reshape(1, BW), ln)


def _inv_unit_lower(ls, eye):
    ts = [eye + l for l in ls]
    lps = list(ls)
    step = 1
    while step < ls[0].shape[0] // 2:
        lps = [_dot(lp, lp) for lp in lps]
        ts = [t + _dot(t, lp) for t, lp in zip(ts, lps)]
        step *= 2
    return ts


def _rw_chunk_kernel(prev_ref, cur_ref, next_ref, mu_ref, w0_ref, a0_ref, kk_ref, ka_ref,
                     wup_ref, aup_ref, rk_ref, bd_ref,
                     m_ref, n_ref, q_ref, y0_ref, bonus_ref, z_ref, *, cpb):
    i = pl.program_id(0)
    jj = i % cpb
    ctx_chunks = TILE // CHUNK
    has_prev = (jj != 0) & (jj != ctx_chunks)
    has_next = (jj != ctx_chunks - 1) & (jj != cpb - 1)
    c = CHUNK
    zc = cur_ref[:, 0:RW_SHIFT_W]
    e = SUBLANES
    z_ref[0:e, :] = jnp.where(has_prev, prev_ref[:, 0:RW_SHIFT_W], 0.0)
    z_ref[e:e + c, :] = zc
    z_ref[e + c:2 * e + c, :] = jnp.where(has_next, next_ref[:, 0:RW_SHIFT_W], 0.0)
    bd = bd_ref[...]
    t_io = lax.broadcasted_iota(jnp.int32, (c, c), 0)
    s_io = lax.broadcasted_iota(jnp.int32, (c, c), 1)
    eye = jnp.where(t_io == s_io, 1.0, 0.0)
    lane_head = _head_of_lane((c, BW), RW_DH)
    row = lax.broadcasted_iota(jnp.int32, (BW, BW), 0)
    colm = lax.broadcasted_iota(jnp.int32, (BW, BW), 1)
    same_head = (row // RW_DH) == (colm // RW_DH)

    def stack(x):
        return jnp.concatenate(
            [jnp.where(lane_head == h, x, 0.0) for h in range(RW_HEADS)], axis=0)

    cat = lambda xs: jnp.concatenate(xs, axis=1)

    prep = []
    for d in range(2):
        zsh = z_ref[e - 1:e - 1 + c, :] if d == 0 else z_ref[e + 1:e + 1 + c, :]
        zs = zc + (zsh - zc) * mu_ref[d]
        r = zs[:, 0:BW]
        k = zs[:, BW:2 * BW]
        v = zs[:, 2 * BW:3 * BW]
        wa = zs[:, 3 * BW:RW_SHIFT_W]
        lw = -RW_DECAY_SCALE * _sigmoid(w0_ref[d] + _dot(jnp.tanh(wa), wup_ref[d]))
        a = _sigmoid(a0_ref[d] + _dot(wa, aup_ref[d]))
        kappa = k * kk_ref[d]
        kh = kappa * lax.rsqrt(jnp.maximum(_dot_sel(kappa * kappa, bd), 1e-12))
        kt = k * (1.0 + (a - 1.0) * ka_ref[d])
        akh = a * kh
        bonus_ref[d] = _dot_sel(r * kt * rk_ref[...], bd) * v

        ahead = (t_io - s_io) if d == 0 else (s_io - t_io)
        earlier = ahead > 0
        upto = ahead >= 0
        tri = jnp.where(upto, 1.0, 0.0).astype(BF16)
        cl = _sel_dot(tri, lw)
        tot = jnp.sum(lw, axis=0, keepdims=True)
        rho = 0.5 * tot
        cle = cl - lw
        a_true = -kh * jnp.exp(cle)
        r_true = r * jnp.exp(cl)
        a_c = -kh * jnp.exp(cle - rho)
        r_c = r * jnp.exp(cl - rho)
        b_c = akh * jnp.exp(rho - cl)
        k_c = kt * jnp.exp(rho - cl)
        b_end = akh * jnp.exp(tot - cl)
        k_end = kt * jnp.exp(tot - cl)

        pair = _dot(jnp.concatenate([stack(a_c), stack(r_c)], axis=0),
                    jnp.concatenate([b_c, k_c], axis=0), _NT)
        l_ab, l_ak, a_rb, a_rk = [], [], [], []
        for h in range(RW_HEADS):
            blk_a = pair[h * c:(h + 1) * c]
            blk_r = pair[(RW_HEADS + h) * c:(RW_HEADS + h + 1) * c]
            l_ab.append(jnp.where(earlier, blk_a[:, 0:c], 0.0))
            l_ak.append(jnp.where(earlier, blk_a[:, c:2 * c], 0.0))
            a_rb.append(jnp.where(upto, blk_r[:, 0:c], 0.0))
            a_rk.append(jnp.where(upto, blk_r[:, c:2 * c], 0.0))
        prep.append((l_ab, cat(l_ak), cat(a_rb), cat(a_rk), v, a_true, r_true, b_end, k_end, tot))

    t_all = _inv_unit_lower(prep[0][0] + prep[1][0], eye)

    for d in range(2):
        _, lak, arb, ark, v, a_true, r_true, b_end, k_end, tot = prep[d]
        t_inv = cat(t_all[d * RW_HEADS:(d + 1) * RW_HEADS])
        lv = _dot(lak, stack(v))
        w12 = _dot(t_inv, jnp.concatenate([stack(a_true), stack(lv)], axis=1))
        w1 = w12[:, 0:BW]
        w2 = w12[:, BW:2 * BW]
        q_ref[d, 0] = (r_true + _dot(arb, stack(w1))).astype(BF16)
        y0_ref[d, 0] = _dot(arb, stack(w2)) + _dot(ark, stack(v))
        b_end_t = b_end.T
        decay = jnp.where(row == colm, jnp.broadcast_to(jnp.exp(tot), (BW, BW)), 0.0)
        m_ref[d, 0] = (jnp.where(same_head, _dot(b_end_t, w1), 0.0) + decay).astype(BF16)
        n_ref[d, 0] = jnp.where(same_head, _dot(b_end_t, w2) + _dot(k_end.T, v), 0.0)


def _rw_chunk_call(u, lp, bd, tpb):
    n = u.shape[0]
    cpb = tpb * (TILE // CHUNK)
    nch = n // CHUNK
    edge = SUBLANES
    per = CHUNK // edge
    n8 = n // edge
    wrw = RW_SHIFT_W + RW_G_RANK
    col = C_RW // wrw
    pvec = lambda w: pl.BlockSpec((2, 1, w), lambda i: (0, 0, 0))
    pad = jnp.zeros((2, RW_DECAY_RANK, BW), F32)
    wup = jnp.concatenate([lp['rw_w_up'], pad], axis=1).astype(BF16)
    aup = jnp.concatenate([pad, lp['rw_a_up']], axis=1).astype(BF16)
    mat = lambda rows: pl.BlockSpec((2, 1, rows, BW), lambda i: (0, i, 0, 0))
    return pl.pallas_call(
        functools.partial(_rw_chunk_kernel, cpb=cpb),
        grid=(nch,),
        in_specs=[pl.BlockSpec((edge, wrw), lambda i: (jnp.maximum(i * per - 1, 0), col)),
                  pl.BlockSpec((CHUNK, wrw), lambda i: (i, col)),
                  pl.BlockSpec((edge, wrw), lambda i: (jnp.minimum((i + 1) * per, n8 - 1), col)),
                  pvec(RW_SHIFT_W), pvec(BW), pvec(BW), pvec(BW), pvec(BW),
                  pl.BlockSpec((2, 2 * RW_DECAY_RANK, BW), lambda i: (0, 0, 0)),
                  pl.BlockSpec((2, 2 * RW_A_RANK, BW), lambda i: (0, 0, 0)),
                  pl.BlockSpec((1, BW), lambda i: (0, 0)),
                  pl.BlockSpec((BW, BW), lambda i: (0, 0))],
        out_specs=[mat(BW), mat(BW), mat(CHUNK), mat(CHUNK),
                   pl.BlockSpec((2, CHUNK, BW), lambda i: (0, i, 0))],
        out_shape=[jax.ShapeDtypeStruct((2, nch, BW, BW), BF16),
                   jax.ShapeDtypeStruct((2, nch, BW, BW), F32),
                   jax.ShapeDtypeStruct((2, nch, CHUNK, BW), BF16),
                   jax.ShapeDtypeStruct((2, nch, CHUNK, BW), F32),
                   jax.ShapeDtypeStruct((2, n, BW), F32)],
        scratch_shapes=[pltpu.VMEM((CHUNK + 2 * edge, RW_SHIFT_W), F32)],
        compiler_params=_cparams(("arbitrary",)),
        name="rwkv_chunk",
    )(u, u, u, lp['rw_mu'].reshape(2, 1, RW_SHIFT_W), lp['rw_w0'].reshape(2, 1, BW),
      lp['rw_a0'].reshape(2, 1, BW), lp['rw_kk'].reshape(2, 1, BW), lp['rw_ka'].reshape(2, 1, BW),
      wup, aup, lp['rw_rk'].reshape(1, BW), bd)


def _rw_scan_kernel(*refs, n_batch):
    m_refs, n_refs, q_refs, y0_refs = refs[0:2], refs[2:4], refs[4:6], refs[6:8]
    y_refs, x_ref = refs[8:10], refs[10]

    @pl.when(pl.program_id(0) == 0)
    def _():
        x_ref[...] = jnp.zeros(x_ref.shape, F32)

    for d in range(2):
        for b in range(n_batch):
            x = x_ref[d, b]
            y_refs[d][b, 0] = _dot(q_refs[d][0, b, 0], x) + y0_refs[d][0, b, 0]
            x_ref[d, b] = _dot(m_refs[d][0, b, 0], x) + n_refs[d][0, b, 0]


def _rw_scan_call(m, nn, q, y0, n_batch, tpb):
    cpb = tpb * (TILE // CHUNK)
    ctx_chunks = TILE // CHUNK

    def chunk(d, i):
        rev = jnp.where(i < ctx_chunks, ctx_chunks - 1 - i, cpb + ctx_chunks - 1 - i)
        return i if d == 0 else rev

    def mat(rows, d):
        return pl.BlockSpec((1, n_batch, 1, rows, BW), lambda i: (d, 0, chunk(d, i), 0, 0))

    ins, specs = [], []
    for arr, rows in ((m, BW), (nn, BW), (q, CHUNK), (y0, CHUNK)):
        arr = arr.reshape(2, n_batch, cpb, rows, BW)
        for d in range(2):
            ins.append(arr)
            specs.append(mat(rows, d))
    yshape = jax.ShapeDtypeStruct((n_batch, cpb, CHUNK, BW), F32)
    ys = pl.pallas_call(
        functools.partial(_rw_scan_kernel, n_batch=n_batch),
        grid=(cpb,),
        in_specs=specs,
        out_specs=[pl.BlockSpec((n_batch, 1, CHUNK, BW), lambda i, d=d: (0, chunk(d, i), 0, 0))
                   for d in range(2)],
        out_shape=[yshape, yshape],
        scratch_shapes=[pltpu.VMEM((2, n_batch, BW, BW), F32)],
        compiler_params=_cparams(("arbitrary",)),
        name="rwkv_scan",
    )(*ins)
    return [y.reshape(n_batch * cpb * CHUNK, BW) for y in ys]


def _rw_out_kernel(yf_ref, yb_ref, bonus_ref, u_ref, gup_ref, gn_ref, bd_ref, o_ref):
    bd = bd_ref[...]
    y = yf_ref[...] + yb_ref[...]
    mean = _dot_sel(y, bd) * (1.0 / RW_DH)
    yc = y - mean
    var = _dot_sel(yc * yc, bd) * (1.0 / RW_DH)
    yn = yc * lax.rsqrt(var + RW_GN_EPS) * gn_ref[0:1, :] + gn_ref[1:2, :]
    yn = yn + (bonus_ref[0] + bonus_ref[1])
    o_ref[...] = yn * _dot(_sigmoid(u_ref[...]), gup_ref[...])


def _rw_out_call(ys, bonus, u, gup, gn, bd):
    n = u.shape[0]
    gcol = (C_RW + RW_SHIFT_W) // RW_G_RANK
    both = pl.BlockSpec((2, TILE, BW), lambda i: (0, i, 0))
    one = pl.BlockSpec((TILE, BW), lambda i: (i, 0))
    return pl.pallas_call(
        _rw_out_kernel,
        grid=(n // TILE,),
        in_specs=[one, one, both, pl.BlockSpec((TILE, RW_G_RANK), lambda i: (i, gcol)),
                  pl.BlockSpec((RW_G_RANK, BW), lambda i: (0, 0)),
                  pl.BlockSpec((2, BW), lambda i: (0, 0)),
                  pl.BlockSpec((BW, BW), lambda i: (0, 0))],
        out_specs=pl.BlockSpec((TILE, BW), lambda i: (i, 0)),
        out_shape=jax.ShapeDtypeStruct((n, BW), F32),
        compiler_params=_cparams(("arbitrary",)),
        name="rwkv_readout",
    )(ys[0], ys[1], bonus, u, gup, gn, bd)


def _merge_kernel(oa_ref, ob_ref, oc_ref, od_ref, gl_ref, wb_ref, wo_ref, x_ref, g_ref, mod_ref,
                  g2_ref, sh_ref, sc_ref, rw_ref, rb_ref, o_ref, tok_ref, gate_ref):
    d = x_ref.shape[1]
    m = None
    for i, o in enumerate((oa_ref, ob_ref, oc_ref, od_ref)):
        t = _sigmoid(gl_ref[:, i * d:(i + 1) * d]) * _dot(o[...], wb_ref[i])
        m = t if m is None else m + t
    y = _dot(m, wo_ref[...])
    y = y * lax.rsqrt(jnp.mean(y * y, axis=-1, keepdims=True) + EPS) * g_ref[...]
    x = x_ref[...] + mod_ref[0] * y
    o_ref[...] = x
    t = x * lax.rsqrt(jnp.mean(x * x, axis=-1, keepdims=True) + EPS) * g2_ref[...]
    tok = t * (1.0 + sc_ref[0]) + sh_ref[0]
    tok_ref[...] = tok.astype(BF16)
    gate_ref[...] = _route(tok, rw_ref[...], rb_ref[...])


def _merge_call(outs, u, wb, wo, x, g, mod, g2, shift, scale, rw_t, rb, tpb, n_batch):
    n, d = x.shape
    kind = lambda i: (jnp.where(i % tpb == 0, n_batch, i // tpb), 0, 0)
    br = pl.BlockSpec((TILE, BW), lambda i: (i, 0))
    vec = pl.BlockSpec((1, d), lambda i: (0, 0))
    row = pl.BlockSpec((TILE, d), lambda i: (i, 0))
    wg = N_BRANCH * d
    return pl.pallas_call(
        _merge_kernel,
        grid=(n // TILE,),
        in_specs=[br, br, br, br,
                  pl.BlockSpec((TILE, wg), lambda i: (i, C_GATE // wg)),
                  pl.BlockSpec((N_BRANCH, BW, d), lambda i: (0, 0, 0)),
                  pl.BlockSpec((d, d), lambda i: (0, 0)),
                  row, vec, pl.BlockSpec((1, 1, d), kind),
                  vec, pl.BlockSpec((1, 1, d), kind), pl.BlockSpec((1, 1, d), kind),
                  pl.BlockSpec((N_EXPERTS, d), lambda i: (0, 0)),
                  pl.BlockSpec((N_EXPERTS, 1), lambda i: (0, 0))],
        out_specs=[row, row, pl.BlockSpec((N_EXPERTS, TILE), lambda i: (0, i))],
        out_shape=[jax.ShapeDtypeStruct((n, d), F32), jax.ShapeDtypeStruct((n, d), BF16),
                   jax.ShapeDtypeStruct((N_EXPERTS, n), F32)],
        compiler_params=_cparams(("arbitrary",)),
        name="merge",
    )(*outs, u, wb, wo, x, g.reshape(1, d), mod, g2.reshape(1, d), shift, scale,
      rw_t, rb.reshape(N_EXPERTS, 1))


def _route(tokens, w, b):
    tm = tokens.shape[0]
    gsz = N_EXPERTS // N_GROUPS
    logits = _dot3(w, tokens, _NT)
    sc = _sigmoid(logits).reshape(N_GROUPS, gsz, tm)
    bi = sc + b.reshape(N_GROUPS, gsz, 1)
    shape = (N_GROUPS, gsz, tm)
    g_io = lax.broadcasted_iota(jnp.int32, shape, 0)
    j_io = lax.broadcasted_iota(jnp.int32, shape, 1)
    e_io = g_io * gsz + j_io
    ninf = -jnp.inf
    m1 = jnp.max(bi, axis=1, keepdims=True)
    i1 = jnp.min(jnp.where(bi == m1, j_io, gsz), axis=1, keepdims=True)
    m2 = jnp.max(jnp.where(j_io == i1, ninf, bi), axis=1, keepdims=True)
    cur = jnp.broadcast_to(m1 + m2, shape)
    gsel = jnp.zeros(shape, F32)
    for _ in range(TOPK_GROUPS):
        mx = jnp.max(cur, axis=0, keepdims=True)
        ix = jnp.min(jnp.where(cur == mx, g_io, N_GROUPS), axis=0, keepdims=True)
        hit = g_io == ix
        gsel = jnp.where(hit, 1.0, gsel)
        cur = jnp.where(hit, ninf, cur)
    cur = jnp.where(gsel > 0.0, bi, ninf)
    esel = jnp.zeros(shape, F32)
    for _ in range(TOP_K):
        mx = jnp.max(jnp.max(cur, axis=0, keepdims=True), axis=1, keepdims=True)
        ix = jnp.min(jnp.min(jnp.where(cur == mx, e_io, N_EXPERTS), axis=0, keepdims=True),
                     axis=1, keepdims=True)
        hit = e_io == ix
        esel = jnp.where(hit, 1.0, esel)
        cur = jnp.where(hit, ninf, cur)
    wsel = sc * esel
    den = jnp.sum(jnp.sum(wsel, axis=0, keepdims=True), axis=1, keepdims=True)
    return (wsel / den * ROUTE_SCALE).reshape(N_EXPERTS, tm)


def _moe_kernel(x_ref, g_ref, wgu_ref, wd_ref, sgu_ref, sd_ref, o_ref):
    e = pl.program_id(1)

    def ffn(wgu, wd):
        hgu = _dot(x_ref[...], wgu)
        return _dot(_silu(hgu[:, :D_EXPERT]) * hgu[:, D_EXPERT:], wd)

    @pl.when(e == 0)
    def _():
        o_ref[...] = ffn(sgu_ref[...], sd_ref[...])

    gates = g_ref[...]
    lane = lax.broadcasted_iota(jnp.int32, gates.shape, 1)
    gcol = jnp.sum(jnp.where(lane == e, gates, 0.0), axis=1, keepdims=True)
    o_ref[...] += ffn(wgu_ref[0, 0], wd_ref[0, 0]) * gcol


def _moe_call(tok, gates, wgu, wd, layer, sgu, sd, tm):
    n, d = tok.shape
    return pl.pallas_call(
        _moe_kernel,
        grid=(n // tm, N_EXPERTS),
        in_specs=[pl.BlockSpec((tm, d), lambda i, e: (i, 0)),
                  pl.BlockSpec((tm, N_EXPERTS), lambda i, e: (i, 0)),
                  pl.BlockSpec((1, 1, d, 2 * D_EXPERT), lambda i, e: (layer, e, 0, 0)),
                  pl.BlockSpec((1, 1, D_EXPERT, d), lambda i, e: (layer, e, 0, 0)),
                  pl.BlockSpec((d, 2 * D_EXPERT), lambda i, e: (0, 0)),
                  pl.BlockSpec((D_EXPERT, d), lambda i, e: (0, 0))],
        out_specs=pl.BlockSpec((tm, d), lambda i, e: (i, 0)),
        out_shape=jax.ShapeDtypeStruct((n, d), F32),
        compiler_params=_cparams(("arbitrary", "arbitrary"), MOE_VMEM_LIMIT),
        name="moe_experts",
    )(tok, gates, wgu, wd, sgu, sd)


def _resid_kernel(x_ref, f_ref, g_ref, mod_ref, o_ref):
    f = f_ref[...]
    y = f * lax.rsqrt(jnp.mean(f * f, axis=-1, keepdims=True) + EPS) * g_ref[...]
    o_ref[...] = x_ref[...] + mod_ref[0] * y


def _resid_call(x, f, g, mod, tpb, n_batch, latent_only=False):
    n, d = x.shape
    vec = pl.BlockSpec((1, d), lambda *_: (0, 0))
    if latent_only:
        lpb = tpb - 1
        grid = (n_batch, lpb)
        row = pl.BlockSpec((TILE, d), lambda b, i: (b * tpb + 1 + i, 0))
        out = pl.BlockSpec((TILE, d), lambda b, i: (b * lpb + i, 0))
        kind = pl.BlockSpec((1, 1, d), lambda b, i: (b, 0, 0))
        n_out = n_batch * lpb * TILE
    else:
        grid = (n // TILE,)
        row = out = pl.BlockSpec((TILE, d), lambda i: (i, 0))
        kind = pl.BlockSpec((1, 1, d), lambda i: (jnp.where(i % tpb == 0, n_batch, i // tpb), 0, 0))
        n_out = n
    return pl.pallas_call(
        _resid_kernel,
        grid=grid,
        in_specs=[row, row, vec, kind],
        out_specs=out,
        out_shape=jax.ShapeDtypeStruct((n_out, d), F32),
        compiler_params=_cparams(("arbitrary",) * len(grid)),
        name="moe_residual",
    )(x, f, g.reshape(1, d), mod)


def _proj_weights(w):
    w = w.astype(BF16)
    k = w.shape[0]
    da, cv, sw = 0, 3 * BW, 5 * BW
    rw = sw + (SW_HEADS + 2 * SW_KV) * SW_DH
    gate = rw + RW_SHIFT_W + RW_G_RANK

    def per_q_head(x):
        return jnp.repeat(x.reshape(k, SW_KV, SW_DH), SW_HEADS // SW_KV, axis=1).reshape(k, -1)

    da_q, da_k, da_v = (w[:, da + i * BW:da + (i + 1) * BW] for i in range(3))
    sw_q = w[:, sw:sw + BW]
    sw_k = per_q_head(w[:, sw + BW:sw + BW + SW_KV * SW_DH])
    sw_v = per_q_head(w[:, sw + BW + SW_KV * SW_DH:rw])
    out = jnp.concatenate([
        w[:, gate:], da_q, da_k, sw_q, sw_k, da_v, sw_v, w[:, cv:sw], w[:, rw:gate]], axis=1)
    assert out.shape[1] == C_TOTAL
    return out


def _rope_tables(seq):
    rows = seq // GRID_W
    row = jnp.repeat(jnp.arange(rows, dtype=F32), GRID_W)
    colp = jnp.tile(jnp.arange(GRID_W, dtype=F32), rows)

    def tables(dim):
        q = dim // 4
        freqs = ROPE_BASE ** (-jnp.arange(q, dtype=F32) / q)
        ar, ac = row[:, None] * freqs, colp[:, None] * freqs
        cos = jnp.concatenate([jnp.cos(ar), jnp.cos(ar), jnp.cos(ac), jnp.cos(ac)], axis=1)
        sin = jnp.concatenate([-jnp.sin(ar), jnp.sin(ar), -jnp.sin(ac), jnp.sin(ac)], axis=1)
        return jnp.tile(cos, (1, LANES // dim)), jnp.tile(sin, (1, LANES // dim))

    c32, s32 = tables(DA_HALF)
    c64, s64 = tables(SW_DH)
    cos = jnp.concatenate([c32, c64], axis=1)
    sin = jnp.concatenate([s32, s64], axis=1)
    w = cos.shape[1]
    cos = jnp.concatenate([jnp.ones((TILE, w), F32), cos], axis=0)
    sin = jnp.concatenate([jnp.zeros((TILE, w), F32), sin], axis=0)
    return cos, sin


def _moe_tile(rows_b):
    best = TILE
    for t in range(TILE, MOE_MAX_TILE + 1, 16):
        if rows_b % t == 0:
            best = t
    return best


def kernel(x, c, ctx, c_ctx, ada_w, ada_b, norm_g, w_in, w_branch, w_out, da_lambda, da_subln,
           cv_w, cv_b, cv_ln, sw_sink, rw_mu, rw_w0, rw_w_up, rw_a0, rw_a_up, rw_kk, rw_ka,
           rw_g_up, rw_rk, rw_gn, router_w, router_b, ex_w_gu, ex_w_down, sh_w_gu, sh_w_down):
    n_batch, seq, d = x.shape
    ctx_len = ctx.shape[1]
    depth = w_in.shape[0]
    assert ctx_len == TILE and seq % TILE == 0 and seq % GRID_W == 0
    assert n_batch + 1 <= SUBLANES
    rows_b = ctx_len + seq
    tpb = rows_b // TILE
    n = n_batch * rows_b

    xs = jnp.concatenate([ctx, x], axis=1).reshape(n, d)
    cond = jnp.zeros((SUBLANES, d), F32).at[:n_batch].set(c).at[n_batch].set(c_ctx)
    mods = _ada_call(cond, ada_w, ada_b)[:, :n_batch + 1]
    cos_t, sin_t = _rope_tables(seq)
    hio = np.arange(BW) // RW_DH
    bd = jnp.asarray(hio[:, None] == hio[None, :], BF16)

    for l in range(depth):
        mod = [mods[l, :, i * d:(i + 1) * d].reshape(n_batch + 1, 1, d) for i in range(6)]
        ng = norm_g[l]
        lam_init = 0.8 - 0.6 * math.exp(-0.3 * l)
        lv = da_lambda[l]
        lam = (jnp.exp(jnp.sum(lv[0] * lv[1])) - jnp.exp(jnp.sum(lv[2] * lv[3])) + lam_init)
        lp = {'rw_mu': rw_mu[l], 'rw_w0': rw_w0[l], 'rw_w_up': rw_w_up[l], 'rw_a0': rw_a0[l],
              'rw_a_up': rw_a_up[l], 'rw_kk': rw_kk[l], 'rw_ka': rw_ka[l], 'rw_rk': rw_rk[l]}

        u = _proj_call(xs, ng[0], mod[0], mod[1], _proj_weights(w_in[l]), tpb, n_batch)
        qk, vt = _rope_call(u, cos_t, sin_t, n_batch, tpb)
        oa = _da_call(lam.reshape(1), qk, vt, jnp.tile(da_subln[l], DA_HEADS).reshape(1, BW),
                      n_batch, tpb, lam_init)
        ob = _conv_call(u, cv_w[l], cv_b[l], cv_ln[l], tpb)
        oc = _sw_call(sw_sink[l], qk, u, n_batch, tpb)
        cm, cn, cq, cy0, bonus = _rw_chunk_call(u, lp, bd, tpb)
        yscan = _rw_scan_call(cm, cn, cq, cy0, n_batch, tpb)
        od = _rw_out_call(yscan, bonus, u, rw_g_up[l].astype(BF16), rw_gn[l], bd)
        xs, tok, gates_t = _merge_call(
            (oa, ob, oc, od), u, w_branch[l].astype(BF16), w_out[l].astype(BF16), xs, ng[1],
            mod[2], ng[2], mod[3], mod[4], router_w[l].T, router_b[l], tpb, n_batch)
        f = _moe_call(tok, gates_t.T, ex_w_gu, ex_w_down, l, sh_w_gu[l].astype(BF16),
                      sh_w_down[l].astype(BF16), _moe_tile(rows_b))
        xs = _resid_call(xs, f, ng[3], mod[5], tpb, n_batch, latent_only=(l == depth - 1))

    return xs.reshape(n_batch, seq, d)
```

```python
import functools
import math

import numpy as np
import jax
import jax.numpy as jnp
from jax import lax
from jax.experimental import pallas as pl
from jax.experimental.pallas import tpu as pltpu

F32 = jnp.float32
BF16 = jnp.bfloat16
SUBLANES, LANES = 8, 128

GRID_W = 64
EPS = 1e-6
ROPE_BASE = 10000.0
N_BRANCH = 4
BW = 256
DA_HEADS = 4
DA_HALF = 32
SW_HEADS = 4
SW_KV = 2
SW_DH = 64
WINDOW = 128
CONV_W = 31
RW_HEADS = 4
RW_DH = 64
RW_DECAY_RANK = 64
RW_A_RANK = 64
RW_G_RANK = 128
RW_DECAY_SCALE = math.exp(-0.5)
RW_GN_EPS = 64e-5
RW_SHIFT_W = 3 * BW + RW_DECAY_RANK + RW_A_RANK
N_EXPERTS = 64
TOP_K = 6
N_GROUPS = 8
TOPK_GROUPS = 4
D_EXPERT = 256
ROUTE_SCALE = 2.5

TILE = 256
CHUNK = 128
DA_KT = 1408
MOE_MAX_TILE = 2816
MOE_VMEM_LIMIT = 58 * 1024 * 1024
DA_VROWS = 80
NEG = -1e30

C_GATE = 0
C_ROPE = 4096
C_DAV = 5120
C_SWV = 5376
C_CV = 5632
C_RW = 6144
C_TOTAL = 7168

VMEM_LIMIT = 48 * 1024 * 1024

_NT = (((1,), (1,)), ((), ()))
_NN = (((1,), (0,)), ((), ()))


def _cparams(sem, vmem=VMEM_LIMIT):
    return pltpu.CompilerParams(dimension_semantics=sem, vmem_limit_bytes=vmem)


def _dot(a, b, dims=_NN):
    return lax.dot_general(a.astype(BF16), b.astype(BF16), dims, preferred_element_type=F32)


def _split2(x):
    hi = x.astype(BF16)
    lo = (x - hi.astype(F32)).astype(BF16)
    return hi, lo


def _dot3(a, b, dims=_NN):
    ah, al = _split2(a)
    bh, bl = _split2(b)
    dg = lambda x, y: lax.dot_general(x, y, dims, preferred_element_type=F32)
    return dg(ah, bh) + (dg(ah, bl) + dg(al, bh))


def _dot_sel(x, sel, dims=_NN):
    h0 = x.astype(BF16)
    r1 = x - h0.astype(F32)
    h1 = r1.astype(BF16)
    h2 = (r1 - h1.astype(F32)).astype(BF16)
    dg = lambda y: lax.dot_general(y, sel, dims, preferred_element_type=F32)
    return dg(h0) + (dg(h1) + dg(h2))


def _sel_dot(sel, x):
    h0 = x.astype(BF16)
    r1 = x - h0.astype(F32)
    h1 = r1.astype(BF16)
    h2 = (r1 - h1.astype(F32)).astype(BF16)
    dg = lambda y: lax.dot_general(sel, y, _NN, preferred_element_type=F32)
    return dg(h0) + (dg(h1) + dg(h2))


def _sigmoid(x):
    return jax.nn.sigmoid(x)


def _silu(x):
    return x * jax.nn.sigmoid(x)


def _head_of_lane(shape, width):
    return lax.broadcasted_iota(jnp.int32, shape, len(shape) - 1) // width


def _ada_kernel(s_ref, w_ref, b_ref, o_ref):
    s = _silu(s_ref[...])
    o_ref[0] = _dot(s, w_ref[0]) + b_ref[0]


def _ada_call(cond, ada_w, ada_b):
    depth, d, cols = ada_w.shape
    tn = 1536
    return pl.pallas_call(
        _ada_kernel,
        grid=(depth, cols // tn),
        in_specs=[pl.BlockSpec((SUBLANES, d), lambda l, j: (0, 0)),
                  pl.BlockSpec((1, d, tn), lambda l, j: (l, 0, j)),
                  pl.BlockSpec((1, 1, tn), lambda l, j: (l, 0, j))],
        out_specs=pl.BlockSpec((1, SUBLANES, tn), lambda l, j: (l, 0, j)),
        out_shape=jax.ShapeDtypeStruct((depth, SUBLANES, cols), F32),
        compiler_params=_cparams(("arbitrary", "arbitrary")),
        name="ada_mod",
    )(cond, ada_w, ada_b.reshape(depth, 1, cols))


PROJ_TILES = 2


def _proj_kernel(x_ref, g_ref, *refs):
    mods, w_ref, o_ref = refs[:2 * PROJ_TILES], refs[2 * PROJ_TILES], refs[2 * PROJ_TILES + 1]
    x = x_ref[...]
    y = x * lax.rsqrt(jnp.mean(x * x, axis=-1, keepdims=True) + EPS) * g_ref[...]
    h = jnp.concatenate(
        [(y[t * TILE:(t + 1) * TILE] * (1.0 + mods[2 * t + 1][0]) + mods[2 * t][0]).astype(BF16)
         for t in range(PROJ_TILES)], axis=0)
    o_ref[...] = jnp.dot(h, w_ref[...], preferred_element_type=F32)


def _proj_call(x, g, shift, scale, w, tpb, n_batch):
    n, d = x.shape
    cols = w.shape[1]
    tm, tn = PROJ_TILES * TILE, cols // 2

    def kind(t):
        def index(j, i):
            tile = i * PROJ_TILES + t
            return (jnp.where(tile % tpb == 0, n_batch, tile // tpb), 0, 0)
        return pl.BlockSpec((1, 1, d), index)

    mod_specs, mod_args = [], []
    for t in range(PROJ_TILES):
        mod_specs += [kind(t), kind(t)]
        mod_args += [shift, scale]
    return pl.pallas_call(
        _proj_kernel,
        grid=(cols // tn, n // tm),
        in_specs=[pl.BlockSpec((tm, d), lambda j, i: (i, 0)),
                  pl.BlockSpec((1, d), lambda j, i: (0, 0)), *mod_specs,
                  pl.BlockSpec((d, tn), lambda j, i: (0, j))],
        out_specs=pl.BlockSpec((tm, tn), lambda j, i: (i, j)),
        out_shape=jax.ShapeDtypeStruct((n, cols), F32),
        compiler_params=_cparams(("arbitrary", "arbitrary")),
        name="in_proj",
    )(x, g.reshape(1, d), *mod_args, w)


def _rope_kernel(u_ref, c_ref, s_ref, v_ref, o_ref, vt_ref):
    lanes = LANES
    n_batch, rows, w = u_ref.shape
    lane = lax.broadcasted_iota(jnp.int32, (rows, lanes), 1)
    hd = 2 * DA_HALF
    qscale = (DA_HALF ** -0.5) * math.log2(math.e)
    for b in range(n_batch):
        for k in range(w // lanes):
            cols = slice(k * lanes, (k + 1) * lanes)
            is_da = k * lanes < 2 * BW
            q = (DA_HALF if is_da else SW_DH) // 4
            tab = slice(0, lanes) if is_da else slice(lanes, 2 * lanes)
            x = u_ref[b, :, cols]
            partner = jnp.where((lane // q) % 2 == 0, pltpu.roll(x, lanes - q, 1),
                                pltpu.roll(x, q, 1))
            y = x * c_ref[:, tab] + partner * s_ref[:, tab]
            if k * lanes < BW:
                y = y * qscale
            o_ref[b, :, cols] = y.astype(BF16)
        vt = v_ref[b].T.astype(BF16)
        for h in range(DA_HEADS):
            vt_ref[b, h * DA_VROWS:h * DA_VROWS + hd, :] = vt[h * hd:(h + 1) * hd, :]
            vt_ref[b, h * DA_VROWS + hd:(h + 1) * DA_VROWS, :] = jnp.ones((DA_VROWS - hd, rows),
                                                                          BF16)


def _rope_call(u, cos_t, sin_t, n_batch, tpb):
    n, wu = u.shape
    rows_b = tpb * TILE
    w = 4 * BW
    vrows = DA_HEADS * DA_VROWS
    tab = pl.BlockSpec((TILE, cos_t.shape[1]), lambda i: (i, 0))
    u3 = u.reshape(n_batch, rows_b, wu)
    qk, vt = pl.pallas_call(
        _rope_kernel,
        grid=(tpb,),
        in_specs=[pl.BlockSpec((n_batch, TILE, w), lambda i: (0, i, C_ROPE // w)), tab, tab,
                  pl.BlockSpec((n_batch, TILE, BW), lambda i: (0, i, C_DAV // BW))],
        out_specs=[pl.BlockSpec((n_batch, TILE, w), lambda i: (0, i, 0)),
                   pl.BlockSpec((n_batch, vrows, TILE), lambda i: (0, 0, i))],
        out_shape=[jax.ShapeDtypeStruct((n_batch, rows_b, w), BF16),
                   jax.ShapeDtypeStruct((n_batch, vrows, rows_b), BF16)],
        compiler_params=_cparams(("arbitrary",)),
        name="rope",
    )(u3, cos_t, sin_t, u3)
    return qk.reshape(n, w), vt.reshape(n_batch * vrows, rows_b)


def _da_kernel(lam_ref, q_ref, k_ref, vt_ref, g_ref, o_ref,
               qs_ref, m_ref, acc_ref, *, nkt, lam_init):
    i = pl.program_id(1)
    tq = q_ref.shape[0]
    hd = 2 * DA_HALF
    q = q_ref[...]
    qmap = _head_of_lane((tq, BW), DA_HALF)
    for g in range(2 * DA_HEADS):
        qs_ref[g] = jnp.where(qmap == g, q, jnp.zeros_like(q))
    m_ref[...] = jnp.full(m_ref.shape, NEG, F32)
    acc_ref[...] = jnp.zeros(acc_ref.shape, F32)

    def tile(off, size):
        kt = k_ref[pl.ds(off, size), :]
        groups = range(2 * DA_HEADS)
        ss = [lax.dot_general(kt, qs_ref[g], _NT, preferred_element_type=F32) for g in groups]
        for g in groups:
            s = ss[g]
            m_old = m_ref[g]
            m_new = jnp.maximum(m_old, jnp.max(s, axis=0, keepdims=True))
            alpha = jnp.exp2(m_old - m_new)[0:1, :]
            p = jnp.exp2(s - m_new[0:1, :]).astype(BF16)
            m_ref[g] = m_new
            h, mm = g // 2, g % 2
            rows = slice(h * DA_VROWS, (h + 1) * DA_VROWS)
            pv = jnp.dot(vt_ref[rows, pl.ds(off, size)], p, preferred_element_type=F32)
            acc_ref[mm, rows, :] = acc_ref[mm, rows, :] * alpha + pv

    @pl.when(i == 0)
    def _():
        tile(0, TILE)

    @pl.when(i > 0)
    def _():
        def body(j, carry):
            tile(pl.multiple_of(j * DA_KT, DA_KT), DA_KT)
            return carry

        lax.fori_loop(0, nkt, body, 0)

    lam = lam_ref[0]
    parts = []
    for h in range(DA_HEADS):
        rows = slice(h * DA_VROWS, h * DA_VROWS + hd)
        den = slice(h * DA_VROWS + hd, h * DA_VROWS + hd + 1)
        o_h = (acc_ref[0, rows, :] / acc_ref[0, den, :]
               - lam * (acc_ref[1, rows, :] / acc_ref[1, den, :]))
        ms = jnp.mean(o_h * o_h, axis=0, keepdims=True)
        parts.append(o_h * lax.rsqrt(ms + EPS))
    y = jnp.concatenate(parts, axis=0).T * g_ref[...]
    o_ref[...] = y * (1.0 - lam_init)


def _da_call(lam, qk, vt, subln, n_batch, tpb, lam_init):
    n = qk.shape[0]
    rows_b = tpb * TILE
    assert rows_b % DA_KT == 0
    vrows = DA_HEADS * DA_VROWS
    kern = functools.partial(_da_kernel, nkt=rows_b // DA_KT, lam_init=lam_init)
    return pl.pallas_call(
        kern,
        grid=(n_batch, tpb),
        in_specs=[pl.BlockSpec(memory_space=pltpu.SMEM),
                  pl.BlockSpec((TILE, BW), lambda b, i: (b * tpb + i, 0)),
                  pl.BlockSpec((rows_b, BW), lambda b, i: (b, 1)),
                  pl.BlockSpec((vrows, rows_b), lambda b, i: (b, 0)),
                  pl.BlockSpec((1, BW), lambda b, i: (0, 0))],
        out_specs=pl.BlockSpec((TILE, BW), lambda b, i: (b * tpb + i, 0)),
        out_shape=jax.ShapeDtypeStruct((n, BW), F32),
        scratch_shapes=[pltpu.VMEM((2 * DA_HEADS, TILE, BW), BF16),
                        pltpu.VMEM((2 * DA_HEADS, SUBLANES, TILE), F32),
                        pltpu.VMEM((2, vrows, TILE), F32)],
        compiler_params=_cparams(("arbitrary", "arbitrary")),
        name="diff_attn",
    )(lam, qk, qk, vt, subln)


def _sw_kernel(sink_ref, bias_ref, q_ref, kp_ref, ko_ref, kn_ref, kc_ref,
               vp_ref, vo_ref, vn_ref, vc_ref, o_ref):
    tq = q_ref.shape[0]
    q = q_ref[...]
    kk = jnp.concatenate([kp_ref[...], ko_ref[...], kn_ref[...], kc_ref[...]], axis=0)
    vv = jnp.concatenate([vp_ref[...], vo_ref[...], vn_ref[...], vc_ref[...]], axis=0).astype(BF16)
    nk = kk.shape[0]
    bias = bias_ref[0]
    qhead = _head_of_lane((tq, BW), SW_DH)
    vhead = _head_of_lane((nk, BW), SW_DH)
    ps, vs = [], []
    for h in range(SW_HEADS):
        qm = jnp.where(qhead == h, q, jnp.zeros_like(q))
        s = lax.dot_general(qm, kk, _NT, preferred_element_type=F32) * (SW_DH ** -0.5) + bias
        sk = sink_ref[h]
        m = jnp.maximum(jnp.max(s, axis=1, keepdims=True), sk)
        p = jnp.exp(s - m)
        den = jnp.sum(p, axis=1, keepdims=True) + jnp.exp(sk - m)
        ps.append((p / den).astype(BF16))
        vs.append(jnp.where(vhead == h, vv, jnp.zeros_like(vv)))
    o_ref[...] = jnp.dot(jnp.concatenate(ps, axis=1), jnp.concatenate(vs, axis=0),
                         preferred_element_type=F32)


def _sw_bias():
    r = np.arange(TILE)[:, None]
    prev_ok = np.arange(CHUNK)[None, :] - CHUNK - r >= -WINDOW
    own_ok = np.abs(np.arange(TILE)[None, :] - r) <= WINDOW
    next_ok = np.arange(CHUNK)[None, :] + TILE - r <= WINDOW
    hide = lambda m: np.zeros_like(m)
    ctx = np.ones((TILE, TILE), bool)
    kinds = [np.concatenate([hide(prev_ok), hide(own_ok), hide(next_ok), ctx], axis=1)]
    for no_next in (False, True):
        for no_prev in (False, True):
            kinds.append(np.concatenate([hide(prev_ok) if no_prev else prev_ok, own_ok,
                                         hide(next_ok) if no_next else next_ok, ctx], axis=1))
    return jnp.asarray(np.where(np.stack(kinds), 0.0, NEG), F32)


def _sw_call(sink, qk, u, n_batch, tpb):
    n = qk.shape[0]
    per = TILE // CHUNK
    cpb = tpb * per
    bias = _sw_bias()

    def kind(b, j):
        lat = 1 + (j == 1).astype(jnp.int32) + 2 * (j == tpb - 1).astype(jnp.int32)
        return (jnp.where(j == 0, 0, lat), 0, 0)

    own = lambda b, j: b * tpb + j
    prv = lambda b, j: b * cpb + jnp.maximum(j * per - 1, 0)
    nxt = lambda b, j: b * cpb + jnp.minimum((j + 1) * per, cpb - 1)
    kcol, vcol = 3, C_SWV // BW
    edge = lambda f, col: pl.BlockSpec((CHUNK, BW), lambda b, j: (f(b, j), col))
    tile = lambda f, col: pl.BlockSpec((TILE, BW), lambda b, j: (f(b, j), col))
    ctx = lambda b, j: b * tpb
    return pl.pallas_call(
        _sw_kernel,
        grid=(n_batch, tpb),
        in_specs=[pl.BlockSpec(memory_space=pltpu.SMEM),
                  pl.BlockSpec((1,) + bias.shape[1:], kind),
                  tile(own, 2), edge(prv, kcol), tile(own, kcol), edge(nxt, kcol), tile(ctx, kcol),
                  edge(prv, vcol), tile(own, vcol), edge(nxt, vcol), tile(ctx, vcol)],
        out_specs=pl.BlockSpec((TILE, BW), lambda b, j: (own(b, j), 0)),
        out_shape=jax.ShapeDtypeStruct((n, BW), F32),
        compiler_params=_cparams(("arbitrary", "arbitrary")),
        name="window_attn",
    )(sink, bias, qk, qk, qk, qk, qk, u, u, u, u)


def _conv_kernel(prev_ref, cur_ref, next_ref, w_ref, b_ref, ln_ref, o_ref, z_ref, zs_ref, *, tpb):
    i = pl.program_id(0)
    pos = i % tpb
    has_prev = pos >= 2
    has_next = (pos >= 1) & (pos < tpb - 1)
    halo = prev_ref.shape[0]

    def glu(x):
        return x[:, :BW] * _sigmoid(x[:, BW:])

    zp = glu(prev_ref[...])
    zn = glu(next_ref[...])
    z_ref[0:halo, :] = jnp.where(has_prev, zp, 0.0)
    z_ref[halo:halo + TILE, :] = glu(cur_ref[...])
    z_ref[halo + TILE:2 * halo + TILE, :] = jnp.where(has_next, zn, 0.0)
    acc = jnp.zeros((TILE, BW), F32) + b_ref[...]
    pad = CONV_W // 2
    span = TILE + 2 * halo - SUBLANES
    for r in range(SUBLANES):
        zs_ref[r] = z_ref[r:r + span, :]
    for t in range(CONV_W):
        off = halo - pad + t
        shift, base = off % SUBLANES, off - off % SUBLANES
        acc = acc + zs_ref[shift, base:base + TILE, :] * w_ref[t:t + 1, :]
    mu = jnp.mean(acc, axis=-1, keepdims=True)
    xc = acc - mu
    y = xc * lax.rsqrt(jnp.mean(xc * xc, axis=-1, keepdims=True) + EPS)
    y = y * ln_ref[0:1, :] + ln_ref[1:2, :]
    o_ref[...] = _silu(y)


def _conv_call(u, w, b, ln, tpb):
    n = u.shape[0]
    halo = 16
    per = TILE // halo
    nh = n // halo
    wcv = 2 * BW
    col = C_CV // wcv
    return pl.pallas_call(
        functools.partial(_conv_kernel, tpb=tpb),
        grid=(n // TILE,),
        in_specs=[pl.BlockSpec((halo, wcv), lambda i: (jnp.maximum(i * per - 1, 0), col)),
                  pl.BlockSpec((TILE, wcv), lambda i: (i, col)),
                  pl.BlockSpec((halo, wcv), lambda i: (jnp.minimum((i + 1) * per, nh - 1), col)),
                  pl.BlockSpec((CONV_W, BW), lambda i: (0, 0)),
                  pl.BlockSpec((1, BW), lambda i: (0, 0)),
                  pl.BlockSpec((2, BW), lambda i: (0, 0))],
        out_specs=pl.BlockSpec((TILE, BW), lambda i: (i, 0)),
        out_shape=jax.ShapeDtypeStruct((n, BW), F32),
        scratch_shapes=[pltpu.VMEM((TILE + 2 * halo, BW), F32),
                        pltpu.VMEM((SUBLANES, TILE + 2 * halo - SUBLANES, BW), F32)],
        compiler_params=_cparams(("arbitrary",)),
        name="conformer_conv",
    )(u, u, u, w, b.reshape(1, BW), ln)


def _inv_unit_lower(ls, eye):
    ts = [eye + l for l in ls]
    lps = list(ls)
    step = 1
    while step < ls[0].shape[0] // 2:
        lps = [_dot(lp, lp) for lp in lps]
        ts = [t + _dot(t, lp) for t, lp in zip(ts, lps)]
        step *= 2
    return ts


def _rw_chunk_kernel(prev_ref, cur_ref, next_ref, mu_ref, w0_ref, a0_ref, kk_ref, ka_ref,
                     wup_ref, aup_ref, rk_ref, bd_ref,
                     m_ref, n_ref, q_ref, y0_ref, bonus_ref, z_ref, *, cpb):
    i = pl.program_id(0)
    jj = i % cpb
    ctx_chunks = TILE // CHUNK
    has_prev = (jj != 0) & (jj != ctx_chunks)
    has_next = (jj != ctx_chunks - 1) & (jj != cpb - 1)
    c = CHUNK
    zc = cur_ref[:, 0:RW_SHIFT_W]
    e = SUBLANES
    z_ref[0:e, :] = jnp.where(has_prev, prev_ref[:, 0:RW_SHIFT_W], 0.0)
    z_ref[e:e + c, :] = zc
    z_ref[e + c:2 * e + c, :] = jnp.where(has_next, next_ref[:, 0:RW_SHIFT_W], 0.0)
    bd = bd_ref[...]
    t_io = lax.broadcasted_iota(jnp.int32, (c, c), 0)
    s_io = lax.broadcasted_iota(jnp.int32, (c, c), 1)
    eye = jnp.where(t_io == s_io, 1.0, 0.0)
    lane_head = _head_of_lane((c, BW), RW_DH)
    row = lax.broadcasted_iota(jnp.int32, (BW, BW), 0)
    colm = lax.broadcasted_iota(jnp.int32, (BW, BW), 1)
    same_head = (row // RW_DH) == (colm // RW_DH)

    def stack(x):
        return jnp.concatenate(
            [jnp.where(lane_head == h, x, 0.0) for h in range(RW_HEADS)], axis=0)

    cat = lambda xs: jnp.concatenate(xs, axis=1)

    prep = []
    for d in range(2):
        zsh = z_ref[e - 1:e - 1 + c, :] if d == 0 else z_ref[e + 1:e + 1 + c, :]
        zs = zc + (zsh - zc) * mu_ref[d]
        r = zs[:, 0:BW]
        k = zs[:, BW:2 * BW]
        v = zs[:, 2 * BW:3 * BW]
        wa = zs[:, 3 * BW:RW_SHIFT_W]
        lw = -RW_DECAY_SCALE * _sigmoid(w0_ref[d] + _dot(jnp.tanh(wa), wup_ref[d]))
        a = _sigmoid(a0_ref[d] + _dot(wa, aup_ref[d]))
        kappa = k * kk_ref[d]
        kh = kappa * lax.rsqrt(jnp.maximum(_dot_sel(kappa * kappa, bd), 1e-12))
        kt = k * (1.0 + (a - 1.0) * ka_ref[d])
        akh = a * kh
        bonus_ref[d] = _dot_sel(r * kt * rk_ref[...], bd) * v

        ahead = (t_io - s_io) if d == 0 else (s_io - t_io)
        earlier = ahead > 0
        upto = ahead >= 0
        tri = jnp.where(upto, 1.0, 0.0).astype(BF16)
        cl = _sel_dot(tri, lw)
        tot = jnp.sum(lw, axis=0, keepdims=True)
        rho = 0.5 * tot
        cle = cl - lw
        a_true = -kh * jnp.exp(cle)
        r_true = r * jnp.exp(cl)
        a_c = -kh * jnp.exp(cle - rho)
        r_c = r * jnp.exp(cl - rho)
        b_c = akh * jnp.exp(rho - cl)
        k_c = kt * jnp.exp(rho - cl)
        b_end = akh * jnp.exp(tot - cl)
        k_end = kt * jnp.exp(tot - cl)

        pair = _dot(jnp.concatenate([stack(a_c), stack(r_c)], axis=0),
                    jnp.concatenate([b_c, k_c], axis=0), _NT)
        l_ab, l_ak, a_rb, a_rk = [], [], [], []
        for h in range(RW_HEADS):
            blk_a = pair[h * c:(h + 1) * c]
            blk_r = pair[(RW_HEADS + h) * c:(RW_HEADS + h + 1) * c]
            l_ab.append(jnp.where(earlier, blk_a[:, 0:c], 0.0))
            l_ak.append(jnp.where(earlier, blk_a[:, c:2 * c], 0.0))
            a_rb.append(jnp.where(upto, blk_r[:, 0:c], 0.0))
            a_rk.append(jnp.where(upto, blk_r[:, c:2 * c], 0.0))
        prep.append((l_ab, cat(l_ak), cat(a_rb), cat(a_rk), v, a_true, r_true, b_end, k_end, tot))

    t_all = _inv_unit_lower(prep[0][0] + prep[1][0], eye)

    for d in range(2):
        _, lak, arb, ark, v, a_true, r_true, b_end, k_end, tot = prep[d]
        t_inv = cat(t_all[d * RW_HEADS:(d + 1) * RW_HEADS])
        lv = _dot(lak, stack(v))
        w12 = _dot(t_inv, jnp.concatenate([stack(a_true), stack(lv)], axis=1))
        w1 = w12[:, 0:BW]
        w2 = w12[:, BW:2 * BW]
        q_ref[d, 0] = (r_true + _dot(arb, stack(w1))).astype(BF16)
        y0_ref[d, 0] = _dot(arb, stack(w2)) + _dot(ark, stack(v))
        b_end_t = b_end.T
        decay = jnp.where(row == colm, jnp.broadcast_to(jnp.exp(tot), (BW, BW)), 0.0)
        m_ref[d, 0] = (jnp.where(same_head, _dot(b_end_t, w1), 0.0) + decay).astype(BF16)
        n_ref[d, 0] = jnp.where(same_head, _dot(b_end_t, w2) + _dot(k_end.T, v), 0.0)


def _rw_chunk_call(u, lp, bd, tpb):
    n = u.shape[0]
    cpb = tpb * (TILE // CHUNK)
    nch = n // CHUNK
    edge = SUBLANES
    per = CHUNK // edge
    n8 = n // edge
    wrw = RW_SHIFT_W + RW_G_RANK
    col = C_RW // wrw
    pvec = lambda w: pl.BlockSpec((2, 1, w), lambda i: (0, 0, 0))
    pad = jnp.zeros((2, RW_DECAY_RANK, BW), F32)
    wup = jnp.concatenate([lp['rw_w_up'], pad], axis=1).astype(BF16)
    aup = jnp.concatenate([pad, lp['rw_a_up']], axis=1).astype(BF16)
    mat = lambda rows: pl.BlockSpec((2, 1, rows, BW), lambda i: (0, i, 0, 0))
    return pl.pallas_call(
        functools.partial(_rw_chunk_kernel, cpb=cpb),
        grid=(nch,),
        in_specs=[pl.BlockSpec((edge, wrw), lambda i: (jnp.maximum(i * per - 1, 0), col)),
                  pl.BlockSpec((CHUNK, wrw), lambda i: (i, col)),
                  pl.BlockSpec((edge, wrw), lambda i: (jnp.minimum((i + 1) * per, n8 - 1), col)),
                  pvec(RW_SHIFT_W), pvec(BW), pvec(BW), pvec(BW), pvec(BW),
                  pl.BlockSpec((2, 2 * RW_DECAY_RANK, BW), lambda i: (0, 0, 0)),
                  pl.BlockSpec((2, 2 * RW_A_RANK, BW), lambda i: (0, 0, 0)),
                  pl.BlockSpec((1, BW), lambda i: (0, 0)),
                  pl.BlockSpec((BW, BW), lambda i: (0, 0))],
        out_specs=[mat(BW), mat(BW), mat(CHUNK), mat(CHUNK),
                   pl.BlockSpec((2, CHUNK, BW), lambda i: (0, i, 0))],
        out_shape=[jax.ShapeDtypeStruct((2, nch, BW, BW), BF16),
                   jax.ShapeDtypeStruct((2, nch, BW, BW), F32),
                   jax.ShapeDtypeStruct((2, nch, CHUNK, BW), BF16),
                   jax.ShapeDtypeStruct((2, nch, CHUNK, BW), F32),
                   jax.ShapeDtypeStruct((2, n, BW), F32)],
        scratch_shapes=[pltpu.VMEM((CHUNK + 2 * edge, RW_SHIFT_W), F32)],
        compiler_params=_cparams(("arbitrary",)),
        name="rwkv_chunk",
    )(u, u, u, lp['rw_mu'].reshape(2, 1, RW_SHIFT_W), lp['rw_w0'].reshape(2, 1, BW),
      lp['rw_a0'].reshape(2, 1, BW), lp['rw_kk'].reshape(2, 1, BW), lp['rw_ka'].reshape(2, 1, BW),
      wup, aup, lp['rw_rk'].reshape(1, BW), bd)


def _rw_scan_kernel(*refs, n_batch):
    m_refs, n_refs, q_refs, y0_refs = refs[0:2], refs[2:4], refs[4:6], refs[6:8]
    y_refs, x_ref = refs[8:10], refs[10]

    @pl.when(pl.program_id(0) == 0)
    def _():
        x_ref[...] = jnp.zeros(x_ref.shape, F32)

    for d in range(2):
        for b in range(n_batch):
            x = x_ref[d, b]
            y_refs[d][b, 0] = _dot(q_refs[d][0, b, 0], x) + y0_refs[d][0, b, 0]
            x_ref[d, b] = _dot(m_refs[d][0, b, 0], x) + n_refs[d][0, b, 0]


def _rw_scan_call(m, nn, q, y0, n_batch, tpb):
    cpb = tpb * (TILE // CHUNK)
    ctx_chunks = TILE // CHUNK

    def chunk(d, i):
        rev = jnp.where(i < ctx_chunks, ctx_chunks - 1 - i, cpb + ctx_chunks - 1 - i)
        return i if d == 0 else rev

    def mat(rows, d):
        return pl.BlockSpec((1, n_batch, 1, rows, BW), lambda i: (d, 0, chunk(d, i), 0, 0))

    ins, specs = [], []
    for arr, rows in ((m, BW), (nn, BW), (q, CHUNK), (y0, CHUNK)):
        arr = arr.reshape(2, n_batch, cpb, rows, BW)
        for d in range(2):
            ins.append(arr)
            specs.append(mat(rows, d))
    yshape = jax.ShapeDtypeStruct((n_batch, cpb, CHUNK, BW), F32)
    ys = pl.pallas_call(
        functools.partial(_rw_scan_kernel, n_batch=n_batch),
        grid=(cpb,),
        in_specs=specs,
        out_specs=[pl.BlockSpec((n_batch, 1, CHUNK, BW), lambda i, d=d: (0, chunk(d, i), 0, 0))
                   for d in range(2)],
        out_shape=[yshape, yshape],
        scratch_shapes=[pltpu.VMEM((2, n_batch, BW, BW), F32)],
        compiler_params=_cparams(("arbitrary",)),
        name="rwkv_scan",
    )(*ins)
    return [y.reshape(n_batch * cpb * CHUNK, BW) for y in ys]


def _rw_out_kernel(yf_ref, yb_ref, bonus_ref, u_ref, gup_ref, gn_ref, bd_ref, o_ref):
    bd = bd_ref[...]
    y = yf_ref[...] + yb_ref[...]
    mean = _dot_sel(y, bd) * (1.0 / RW_DH)
    yc = y - mean
    var = _dot_sel(yc * yc, bd) * (1.0 / RW_DH)
    yn = yc * lax.rsqrt(var + RW_GN_EPS) * gn_ref[0:1, :] + gn_ref[1:2, :]
    yn = yn + (bonus_ref[0] + bonus_ref[1])
    o_ref[...] = yn * _dot(_sigmoid(u_ref[...]), gup_ref[...])


def _rw_out_call(ys, bonus, u, gup, gn, bd):
    n = u.shape[0]
    gcol = (C_RW + RW_SHIFT_W) // RW_G_RANK
    both = pl.BlockSpec((2, TILE, BW), lambda i: (0, i, 0))
    one = pl.BlockSpec((TILE, BW), lambda i: (i, 0))
    return pl.pallas_call(
        _rw_out_kernel,
        grid=(n // TILE,),
        in_specs=[one, one, both, pl.BlockSpec((TILE, RW_G_RANK), lambda i: (i, gcol)),
                  pl.BlockSpec((RW_G_RANK, BW), lambda i: (0, 0)),
                  pl.BlockSpec((2, BW), lambda i: (0, 0)),
                  pl.BlockSpec((BW, BW), lambda i: (0, 0))],
        out_specs=pl.BlockSpec((TILE, BW), lambda i: (i, 0)),
        out_shape=jax.ShapeDtypeStruct((n, BW), F32),
        compiler_params=_cparams(("arbitrary",)),
        name="rwkv_readout",
    )(ys[0], ys[1], bonus, u, gup, gn, bd)


def _merge_kernel(oa_ref, ob_ref, oc_ref, od_ref, gl_ref, wb_ref, wo_ref, x_ref, g_ref, mod_ref,
                  g2_ref, sh_ref, sc_ref, rw_ref, rb_ref, o_ref, tok_ref, gate_ref):
    d = x_ref.shape[1]
    m = None
    for i, o in enumerate((oa_ref, ob_ref, oc_ref, od_ref)):
        t = _sigmoid(gl_ref[:, i * d:(i + 1) * d]) * _dot(o[...], wb_ref[i])
        m = t if m is None else m + t
    y = _dot(m, wo_ref[...])
    y = y * lax.rsqrt(jnp.mean(y * y, axis=-1, keepdims=True) + EPS) * g_ref[...]
    x = x_ref[...] + mod_ref[0] * y
    o_ref[...] = x
    t = x * lax.rsqrt(jnp.mean(x * x, axis=-1, keepdims=True) + EPS) * g2_ref[...]
    tok = t * (1.0 + sc_ref[0]) + sh_ref[0]
    tok_ref[...] = tok.astype(BF16)
    gate_ref[...] = _route(tok, rw_ref[...], rb_ref[...])


def _merge_call(outs, u, wb, wo, x, g, mod, g2, shift, scale, rw_t, rb, tpb, n_batch):
    n, d = x.shape
    kind = lambda i: (jnp.where(i % tpb == 0, n_batch, i // tpb), 0, 0)
    br = pl.BlockSpec((TILE, BW), lambda i: (i, 0))
    vec = pl.BlockSpec((1, d), lambda i: (0, 0))
    row = pl.BlockSpec((TILE, d), lambda i: (i, 0))
    wg = N_BRANCH * d
    return pl.pallas_call(
        _merge_kernel,
        grid=(n // TILE,),
        in_specs=[br, br, br, br,
                  pl.BlockSpec((TILE, wg), lambda i: (i, C_GATE // wg)),
                  pl.BlockSpec((N_BRANCH, BW, d), lambda i: (0, 0, 0)),
                  pl.BlockSpec((d, d), lambda i: (0, 0)),
                  row, vec, pl.BlockSpec((1, 1, d), kind),
                  vec, pl.BlockSpec((1, 1, d), kind), pl.BlockSpec((1, 1, d), kind),
                  pl.BlockSpec((N_EXPERTS, d), lambda i: (0, 0)),
                  pl.BlockSpec((N_EXPERTS, 1), lambda i: (0, 0))],
        out_specs=[row, row, pl.BlockSpec((N_EXPERTS, TILE), lambda i: (0, i))],
        out_shape=[jax.ShapeDtypeStruct((n, d), F32), jax.ShapeDtypeStruct((n, d), BF16),
                   jax.ShapeDtypeStruct((N_EXPERTS, n), F32)],
        compiler_params=_cparams(("arbitrary",)),
        name="merge",
    )(*outs, u, wb, wo, x, g.reshape(1, d), mod, g2.reshape(1, d), shift, scale,
      rw_t, rb.reshape(N_EXPERTS, 1))


def _route(tokens, w, b):
    tm = tokens.shape[0]
    gsz = N_EXPERTS // N_GROUPS
    logits = _dot3(w, tokens, _NT)
    sc = _sigmoid(logits).reshape(N_GROUPS, gsz, tm)
    bi = sc + b.reshape(N_GROUPS, gsz, 1)
    shape = (N_GROUPS, gsz, tm)
    g_io = lax.broadcasted_iota(jnp.int32, shape, 0)
    j_io = lax.broadcasted_iota(jnp.int32, shape, 1)
    e_io = g_io * gsz + j_io
    ninf = -jnp.inf
    m1 = jnp.max(bi, axis=1, keepdims=True)
    i1 = jnp.min(jnp.where(bi == m1, j_io, gsz), axis=1, keepdims=True)
    m2 = jnp.max(jnp.where(j_io == i1, ninf, bi), axis=1, keepdims=True)
    cur = jnp.broadcast_to(m1 + m2, shape)
    gsel = jnp.zeros(shape, F32)
    for _ in range(TOPK_GROUPS):
        mx = jnp.max(cur, axis=0, keepdims=True)
        ix = jnp.min(jnp.where(cur == mx, g_io, N_GROUPS), axis=0, keepdims=True)
        hit = g_io == ix
        gsel = jnp.where(hit, 1.0, gsel)
        cur = jnp.where(hit, ninf, cur)
    cur = jnp.where(gsel > 0.0, bi, ninf)
    esel = jnp.zeros(shape, F32)
    for _ in range(TOP_K):
        mx = jnp.max(jnp.max(cur, axis=0, keepdims=True), axis=1, keepdims=True)
        ix = jnp.min(jnp.min(jnp.where(cur == mx, e_io, N_EXPERTS), axis=0, keepdims=True),
                     axis=1, keepdims=True)
        hit = e_io == ix
        esel = jnp.where(hit, 1.0, esel)
        cur = jnp.where(hit, ninf, cur)
    wsel = sc * esel
    den = jnp.sum(jnp.sum(wsel, axis=0, keepdims=True), axis=1, keepdims=True)
    return (wsel / den * ROUTE_SCALE).reshape(N_EXPERTS, tm)


def _moe_kernel(x_ref, g_ref, wgu_ref, wd_ref, sgu_ref, sd_ref, o_ref):
    e = pl.program_id(1)

    def ffn(wgu, wd):
        hgu = _dot(x_ref[...], wgu)
        return _dot(_silu(hgu[:, :D_EXPERT]) * hgu[:, D_EXPERT:], wd)

    @pl.when(e == 0)
    def _():
        o_ref[...] = ffn(sgu_ref[...], sd_ref[...])

    gates = g_ref[...]
    lane = lax.broadcasted_iota(jnp.int32, gates.shape, 1)
    gcol = jnp.sum(jnp.where(lane == e, gates, 0.0), axis=1, keepdims=True)
    o_ref[...] += ffn(wgu_ref[0, 0], wd_ref[0, 0]) * gcol


def _moe_call(tok, gates, wgu, wd, layer, sgu, sd, tm):
    n, d = tok.shape
    return pl.pallas_call(
        _moe_kernel,
        grid=(n // tm, N_EXPERTS),
        in_specs=[pl.BlockSpec((tm, d), lambda i, e: (i, 0)),
                  pl.BlockSpec((tm, N_EXPERTS), lambda i, e: (i, 0)),
                  pl.BlockSpec((1, 1, d, 2 * D_EXPERT), lambda i, e: (layer, e, 0, 0)),
                  pl.BlockSpec((1, 1, D_EXPERT, d), lambda i, e: (layer, e, 0, 0)),
                  pl.BlockSpec((d, 2 * D_EXPERT), lambda i, e: (0, 0)),
                  pl.BlockSpec((D_EXPERT, d), lambda i, e: (0, 0))],
        out_specs=pl.BlockSpec((tm, d), lambda i, e: (i, 0)),
        out_shape=jax.ShapeDtypeStruct((n, d), F32),
        compiler_params=_cparams(("arbitrary", "arbitrary"), MOE_VMEM_LIMIT),
        name="moe_experts",
    )(tok, gates, wgu, wd, sgu, sd)


def _resid_kernel(x_ref, f_ref, g_ref, mod_ref, o_ref):
    f = f_ref[...]
    y = f * lax.rsqrt(jnp.mean(f * f, axis=-1, keepdims=True) + EPS) * g_ref[...]
    o_ref[...] = x_ref[...] + mod_ref[0] * y


def _resid_call(x, f, g, mod, tpb, n_batch, latent_only=False):
    n, d = x.shape
    vec = pl.BlockSpec((1, d), lambda *_: (0, 0))
    if latent_only:
        lpb = tpb - 1
        grid = (n_batch, lpb)
        row = pl.BlockSpec((TILE, d), lambda b, i: (b * tpb + 1 + i, 0))
        out = pl.BlockSpec((TILE, d), lambda b, i: (b * lpb + i, 0))
        kind = pl.BlockSpec((1, 1, d), lambda b, i: (b, 0, 0))
        n_out = n_batch * lpb * TILE
    else:
        grid = (n // TILE,)
        row = out = pl.BlockSpec((TILE, d), lambda i: (i, 0))
        kind = pl.BlockSpec((1, 1, d), lambda i: (jnp.where(i % tpb == 0, n_batch, i // tpb), 0, 0))
        n_out = n
    return pl.pallas_call(
        _resid_kernel,
        grid=grid,
        in_specs=[row, row, vec, kind],
        out_specs=out,
        out_shape=jax.ShapeDtypeStruct((n_out, d), F32),
        compiler_params=_cparams(("arbitrary",) * len(grid)),
        name="moe_residual",
    )(x, f, g.reshape(1, d), mod)


def _proj_weights(w):
    w = w.astype(BF16)
    k = w.shape[0]
    da, cv, sw = 0, 3 * BW, 5 * BW
    rw = sw + (SW_HEADS + 2 * SW_KV) * SW_DH
    gate = rw + RW_SHIFT_W + RW_G_RANK

    def per_q_head(x):
        return jnp.repeat(x.reshape(k, SW_KV, SW_DH), SW_HEADS // SW_KV, axis=1).reshape(k, -1)

    da_q, da_k, da_v = (w[:, da + i * BW:da + (i + 1) * BW] for i in range(3))
    sw_q = w[:, sw:sw + BW]
    sw_k = per_q_head(w[:, sw + BW:sw + BW + SW_KV * SW_DH])
    sw_v = per_q_head(w[:, sw + BW + SW_KV * SW_DH:rw])
    out = jnp.concatenate([
        w[:, gate:], da_q, da_k, sw_q, sw_k, da_v, sw_v, w[:, cv:sw], w[:, rw:gate]], axis=1)
    assert out.shape[1] == C_TOTAL
    return out


def _rope_tables(seq):
    rows = seq // GRID_W
    row = jnp.repeat(jnp.arange(rows, dtype=F32), GRID_W)
    colp = jnp.tile(jnp.arange(GRID_W, dtype=F32), rows)

    def tables(dim):
        q = dim // 4
        freqs = ROPE_BASE ** (-jnp.arange(q, dtype=F32) / q)
        ar, ac = row[:, None] * freqs, colp[:, None] * freqs
        cos = jnp.concatenate([jnp.cos(ar), jnp.cos(ar), jnp.cos(ac), jnp.cos(ac)], axis=1)
        sin = jnp.concatenate([-jnp.sin(ar), jnp.sin(ar), -jnp.sin(ac), jnp.sin(ac)], axis=1)
        return jnp.tile(cos, (1, LANES // dim)), jnp.tile(sin, (1, LANES // dim))

    c32, s32 = tables(DA_HALF)
    c64, s64 = tables(SW_DH)
    cos = jnp.concatenate([c32, c64], axis=1)
    sin = jnp.concatenate([s32, s64], axis=1)
    w = cos.shape[1]
    cos = jnp.concatenate([jnp.ones((TILE, w), F32), cos], axis=0)
    sin = jnp.concatenate([jnp.zeros((TILE, w), F32), sin], axis=0)
    return cos, sin


def _moe_tile(rows_b):
    best = TILE
    for t in range(TILE, MOE_MAX_TILE + 1, 16):
        if rows_b % t == 0:
            best = t
    return best


def kernel(x, c, ctx, c_ctx, ada_w, ada_b, norm_g, w_in, w_branch, w_out, da_lambda, da_subln,
           cv_w, cv_b, cv_ln, sw_sink, rw_mu, rw_w0, rw_w_up, rw_a0, rw_a_up, rw_kk, rw_ka,
           rw_g_up, rw_rk, rw_gn, router_w, router_b, ex_w_gu, ex_w_down, sh_w_gu, sh_w_down):
    n_batch, seq, d = x.shape
    ctx_len = ctx.shape[1]
    depth = w_in.shape[0]
    assert ctx_len == TILE and seq % TILE == 0 and seq % GRID_W == 0
    assert n_batch + 1 <= SUBLANES
    rows_b = ctx_len + seq
    tpb = rows_b // TILE
    n = n_batch * rows_b

    xs = jnp.concatenate([ctx, x], axis=1).reshape(n, d)
    cond = jnp.zeros((SUBLANES, d), F32).at[:n_batch].set(c).at[n_batch].set(c_ctx)
    mods = _ada_call(cond, ada_w, ada_b)[:, :n_batch + 1]
    cos_t, sin_t = _rope_tables(seq)
    hio = np.arange(BW) // RW_DH
    bd = jnp.asarray(hio[:, None] == hio[None, :], BF16)

    for l in range(depth):
        mod = [mods[l, :, i * d:(i + 1) * d].reshape(n_batch + 1, 1, d) for i in range(6)]
        ng = norm_g[l]
        lam_init = 0.8 - 0.6 * math.exp(-0.3 * l)
        lv = da_lambda[l]
        lam = (jnp.exp(jnp.sum(lv[0] * lv[1])) - jnp.exp(jnp.sum(lv[2] * lv[3])) + lam_init)
        lp = {'rw_mu': rw_mu[l], 'rw_w0': rw_w0[l], 'rw_w_up': rw_w_up[l], 'rw_a0': rw_a0[l],
              'rw_a_up': rw_a_up[l], 'rw_kk': rw_kk[l], 'rw_ka': rw_ka[l], 'rw_rk': rw_rk[l]}

        u = _proj_call(xs, ng[0], mod[0], mod[1], _proj_weights(w_in[l]), tpb, n_batch)
        qk, vt = _rope_call(u, cos_t, sin_t, n_batch, tpb)
        oa = _da_call(lam.reshape(1), qk, vt, jnp.tile(da_subln[l], DA_HEADS).reshape(1, BW),
                      n_batch, tpb, lam_init)
        ob = _conv_call(u, cv_w[l], cv_b[l], cv_ln[l], tpb)
        oc = _sw_call(sw_sink[l], qk, u, n_batch, tpb)
        cm, cn, cq, cy0, bonus = _rw_chunk_call(u, lp, bd, tpb)
        yscan = _rw_scan_call(cm, cn, cq, cy0, n_batch, tpb)
        od = _rw_out_call(yscan, bonus, u, rw_g_up[l].astype(BF16), rw_gn[l], bd)
        xs, tok, gates_t = _merge_call(
            (oa, ob, oc, od), u, w_branch[l].astype(BF16), w_out[l].astype(BF16), xs, ng[1],
            mod[2], ng[2], mod[3], mod[4], router_w[l].T, router_b[l], tpb, n_batch)
        f = _moe_call(tok, gates_t.T, ex_w_gu, ex_w_down, l, sh_w_gu[l].astype(BF16),
                      sh_w_down[l].astype(BF16), _moe_tile(rows_b))
        xs = _resid_call(xs, f, ng[3], mod[5], tpb, n_batch, latent_only=(l == depth - 1))

    return xs.reshape(n_batch, seq, d)
```

```python
import functools
import math

import numpy as np
import jax
import jax.numpy as jnp
from jax import lax
from jax.experimental import pallas as pl
from jax.experimental.pallas import tpu as pltpu

F32 = jnp.float32
BF16 = jnp.bfloat16
SUBLANES, LANES = 8, 128

GRID_W = 64
EPS = 1e-6
ROPE_BASE = 10000.0
N_BRANCH = 4
BW = 256
DA_HEADS = 4
DA_HALF = 32
SW_HEADS = 4
SW_KV = 2
SW_DH = 64
WINDOW = 128
CONV_W = 31
RW_HEADS = 4
RW_DH = 64
RW_DECAY_RANK = 64
RW_A_RANK = 64
RW_G_RANK = 128
RW_DECAY_SCALE = math.exp(-0.5)
RW_GN_EPS = 64e-5
RW_SHIFT_W = 3 * BW + RW_DECAY_RANK + RW_A_RANK
N_EXPERTS = 64
TOP_K = 6
N_GROUPS = 8
TOPK_GROUPS = 4
D_EXPERT = 256
ROUTE_SCALE = 2.5

TILE = 256
CHUNK = 128
DA_KT = 1408
MOE_MAX_TILE = 2816
MOE_VMEM_LIMIT = 58 * 1024 * 1024
DA_VROWS = 80
NEG = -1e30

C_GATE = 0
C_ROPE = 4096
C_DAV = 5120
C_SWV = 5376
C_CV = 5632
C_RW = 6144
C_TOTAL = 7168

VMEM_LIMIT = 48 * 1024 * 1024

_NT = (((1,), (1,)), ((), ()))
_NN = (((1,), (0,)), ((), ()))


def _cparams(sem, vmem=VMEM_LIMIT):
    return pltpu.CompilerParams(dimension_semantics=sem, vmem_limit_bytes=vmem)


def _dot(a, b, dims=_NN):
    return lax.dot_general(a.astype(BF16), b.astype(BF16), dims, preferred_element_type=F32)


def _split2(x):
    hi = x.astype(BF16)
    lo = (x - hi.astype(F32)).astype(BF16)
    return hi, lo


def _dot3(a, b, dims=_NN):
    ah, al = _split2(a)
    bh, bl = _split2(b)
    dg = lambda x, y: lax.dot_general(x, y, dims, preferred_element_type=F32)
    return dg(ah, bh) + (dg(ah, bl) + dg(al, bh))


def _dot_sel(x, sel, dims=_NN):
    h0 = x.astype(BF16)
    r1 = x - h0.astype(F32)
    h1 = r1.astype(BF16)
    h2 = (r1 - h1.astype(F32)).astype(BF16)
    dg = lambda y: lax.dot_general(y, sel, dims, preferred_element_type=F32)
    return dg(h0) + (dg(h1) + dg(h2))


def _sel_dot(sel, x):
    h0 = x.astype(BF16)
    r1 = x - h0.astype(F32)
    h1 = r1.astype(BF16)
    h2 = (r1 - h1.astype(F32)).astype(BF16)
    dg = lambda y: lax.dot_general(sel, y, _NN, preferred_element_type=F32)
    return dg(h0) + (dg(h1) + dg(h2))


def _sigmoid(x):
    return jax.nn.sigmoid(x)


def _silu(x):
    return x * jax.nn.sigmoid(x)


def _head_of_lane(shape, width):
    return lax.broadcasted_iota(jnp.int32, shape, len(shape) - 1) // width


def _ada_kernel(s_ref, w_ref, b_ref, o_ref):
    s = _silu(s_ref[...])
    o_ref[0] = _dot(s, w_ref[0]) + b_ref[0]


def _ada_call(cond, ada_w, ada_b):
    depth, d, cols = ada_w.shape
    tn = 1536
    return pl.pallas_call(
        _ada_kernel,
        grid=(depth, cols // tn),
        in_specs=[pl.BlockSpec((SUBLANES, d), lambda l, j: (0, 0)),
                  pl.BlockSpec((1, d, tn), lambda l, j: (l, 0, j)),
                  pl.BlockSpec((1, 1, tn), lambda l, j: (l, 0, j))],
        out_specs=pl.BlockSpec((1, SUBLANES, tn), lambda l, j: (l, 0, j)),
        out_shape=jax.ShapeDtypeStruct((depth, SUBLANES, cols), F32),
        compiler_params=_cparams(("arbitrary", "arbitrary")),
        name="ada_mod",
    )(cond, ada_w, ada_b.reshape(depth, 1, cols))


PROJ_TILES = 2


def _proj_kernel(x_ref, g_ref, *refs):
    mods, w_ref, o_ref = refs[:2 * PROJ_TILES], refs[2 * PROJ_TILES], refs[2 * PROJ_TILES + 1]
    x = x_ref[...]
    y = x * lax.rsqrt(jnp.mean(x * x, axis=-1, keepdims=True) + EPS) * g_ref[...]
    h = jnp.concatenate(
        [(y[t * TILE:(t + 1) * TILE] * (1.0 + mods[2 * t + 1][0]) + mods[2 * t][0]).astype(BF16)
         for t in range(PROJ_TILES)], axis=0)
    o_ref[...] = jnp.dot(h, w_ref[...], preferred_element_type=F32)


def _proj_call(x, g, shift, scale, w, tpb, n_batch):
    n, d = x.shape
    cols = w.shape[1]
    tm, tn = PROJ_TILES * TILE, cols // 2

    def kind(t):
        def index(j, i):
            tile = i * PROJ_TILES + t
            return (jnp.where(tile % tpb == 0, n_batch, tile // tpb), 0, 0)
        return pl.BlockSpec((1, 1, d), index)

    mod_specs, mod_args = [], []
    for t in range(PROJ_TILES):
        mod_specs += [kind(t), kind(t)]
        mod_args += [shift, scale]
    return pl.pallas_call(
        _proj_kernel,
        grid=(cols // tn, n // tm),
        in_specs=[pl.BlockSpec((tm, d), lambda j, i: (i, 0)),
                  pl.BlockSpec((1, d), lambda j, i: (0, 0)), *mod_specs,
                  pl.BlockSpec((d, tn), lambda j, i: (0, j))],
        out_specs=pl.BlockSpec((tm, tn), lambda j, i: (i, j)),
        out_shape=jax.ShapeDtypeStruct((n, cols), F32),
        compiler_params=_cparams(("arbitrary", "arbitrary")),
        name="in_proj",
    )(x, g.reshape(1, d), *mod_args, w)


def _rope_kernel(u_ref, c_ref, s_ref, v_ref, o_ref, vt_ref):
    lanes = LANES
    n_batch, rows, w = u_ref.shape
    lane = lax.broadcasted_iota(jnp.int32, (rows, lanes), 1)
    hd = 2 * DA_HALF
    qscale = (DA_HALF ** -0.5) * math.log2(math.e)
    for b in range(n_batch):
        for k in range(w // lanes):
            cols = slice(k * lanes, (k + 1) * lanes)
            is_da = k * lanes < 2 * BW
            q = (DA_HALF if is_da else SW_DH) // 4
            tab = slice(0, lanes) if is_da else slice(lanes, 2 * lanes)
            x = u_ref[b, :, cols]
            partner = jnp.where((lane // q) % 2 == 0, pltpu.roll(x, lanes - q, 1),
                                pltpu.roll(x, q, 1))
            y = x * c_ref[:, tab] + partner * s_ref[:, tab]
            if k * lanes < BW:
                y = y * qscale
            o_ref[b, :, cols] = y.astype(BF16)
        vt = v_ref[b].T.astype(BF16)
        for h in range(DA_HEADS):
            vt_ref[b, h * DA_VROWS:h * DA_VROWS + hd, :] = vt[h * hd:(h + 1) * hd, :]
            vt_ref[b, h * DA_VROWS + hd:(h + 1) * DA_VROWS, :] = jnp.ones((DA_VROWS - hd, rows),
                                                                          BF16)


def _rope_call(u, cos_t, sin_t, n_batch, tpb):
    n, wu = u.shape
    rows_b = tpb * TILE
    w = 4 * BW
    vrows = DA_HEADS * DA_VROWS
    tab = pl.BlockSpec((TILE, cos_t.shape[1]), lambda i: (i, 0))
    u3 = u.reshape(n_batch, rows_b, wu)
    qk, vt = pl.pallas_call(
        _rope_kernel,
        grid=(tpb,),
        in_specs=[pl.BlockSpec((n_batch, TILE, w), lambda i: (0, i, C_ROPE // w)), tab, tab,
                  pl.BlockSpec((n_batch, TILE, BW), lambda i: (0, i, C_DAV // BW))],
        out_specs=[pl.BlockSpec((n_batch, TILE, w), lambda i: (0, i, 0)),
                   pl.BlockSpec((n_batch, vrows, TILE), lambda i: (0, 0, i))],
        out_shape=[jax.ShapeDtypeStruct((n_batch, rows_b, w), BF16),
                   jax.ShapeDtypeStruct((n_batch, vrows, rows_b), BF16)],
        compiler_params=_cparams(("arbitrary",)),
        name="rope",
    )(u3, cos_t, sin_t, u3)
    return qk.reshape(n, w), vt.reshape(n_batch * vrows, rows_b)


def _da_kernel(lam_ref, q_ref, k_ref, vt_ref, g_ref, o_ref,
               qs_ref, m_ref, acc_ref, *, nkt, lam_init):
    i = pl.program_id(1)
    tq = q_ref.shape[0]
    hd = 2 * DA_HALF
    q = q_ref[...]
    qmap = _head_of_lane((tq, BW), DA_HALF)
    for g in range(2 * DA_HEADS):
        qs_ref[g] = jnp.where(qmap == g, q, jnp.zeros_like(q))
    m_ref[...] = jnp.full(m_ref.shape, NEG, F32)
    acc_ref[...] = jnp.zeros(acc_ref.shape, F32)

    def tile(off, size):
        kt = k_ref[pl.ds(off, size), :]
        groups = range(2 * DA_HEADS)
        ss = [lax.dot_general(kt, qs_ref[g], _NT, preferred_element_type=F32) for g in groups]
        for g in groups:
            s = ss[g]
            m_old = m_ref[g]
            m_new = jnp.maximum(m_old, jnp.max(s, axis=0, keepdims=True))
            alpha = jnp.exp2(m_old - m_new)[0:1, :]
            p = jnp.exp2(s - m_new[0:1, :]).astype(BF16)
            m_ref[g] = m_new
            h, mm = g // 2, g % 2
            rows = slice(h * DA_VROWS, (h + 1) * DA_VROWS)
            pv = jnp.dot(vt_ref[rows, pl.ds(off, size)], p, preferred_element_type=F32)
            acc_ref[mm, rows, :] = acc_ref[mm, rows, :] * alpha + pv

    @pl.when(i == 0)
    def _():
        tile(0, TILE)

    @pl.when(i > 0)
    def _():
        def body(j, carry):
            tile(pl.multiple_of(j * DA_KT, DA_KT), DA_KT)
            return carry

        lax.fori_loop(0, nkt, body, 0)

    lam = lam_ref[0]
    parts = []
    for h in range(DA_HEADS):
        rows = slice(h * DA_VROWS, h * DA_VROWS + hd)
        den = slice(h * DA_VROWS + hd, h * DA_VROWS + hd + 1)
        o_h = (acc_ref[0, rows, :] / acc_ref[0, den, :]
               - lam * (acc_ref[1, rows, :] / acc_ref[1, den, :]))
        ms = jnp.mean(o_h * o_h, axis=0, keepdims=True)
        parts.append(o_h * lax.rsqrt(ms + EPS))
    y = jnp.concatenate(parts, axis=0).T * g_ref[...]
    o_ref[...] = y * (1.0 - lam_init)


def _da_call(lam, qk, vt, subln, n_batch, tpb, lam_init):
    n = qk.shape[0]
    rows_b = tpb * TILE
    assert rows_b % DA_KT == 0
    vrows = DA_HEADS * DA_VROWS
    kern = functools.partial(_da_kernel, nkt=rows_b // DA_KT, lam_init=lam_init)
    return pl.pallas_call(
        kern,
        grid=(n_batch, tpb),
        in_specs=[pl.BlockSpec(memory_space=pltpu.SMEM),
                  pl.BlockSpec((TILE, BW), lambda b, i: (b * tpb + i, 0)),
                  pl.BlockSpec((rows_b, BW), lambda b, i: (b, 1)),
                  pl.BlockSpec((vrows, rows_b), lambda b, i: (b, 0)),
                  pl.BlockSpec((1, BW), lambda b, i: (0, 0))],
        out_specs=pl.BlockSpec((TILE, BW), lambda b, i: (b * tpb + i, 0)),
        out_shape=jax.ShapeDtypeStruct((n, BW), F32),
        scratch_shapes=[pltpu.VMEM((2 * DA_HEADS, TILE, BW), BF16),
                        pltpu.VMEM((2 * DA_HEADS, SUBLANES, TILE), F32),
                        pltpu.VMEM((2, vrows, TILE), F32)],
        compiler_params=_cparams(("arbitrary", "arbitrary")),
        name="diff_attn",
    )(lam, qk, qk, vt, subln)


def _sw_kernel(sink_ref, bias_ref, q_ref, kp_ref, ko_ref, kn_ref, kc_ref,
               vp_ref, vo_ref, vn_ref, vc_ref, o_ref):
    tq = q_ref.shape[0]
    q = q_ref[...]
    kk = jnp.concatenate([kp_ref[...], ko_ref[...], kn_ref[...], kc_ref[...]], axis=0)
    vv = jnp.concatenate([vp_ref[...], vo_ref[...], vn_ref[...], vc_ref[...]], axis=0).astype(BF16)
    nk = kk.shape[0]
    bias = bias_ref[0]
    qhead = _head_of_lane((tq, BW), SW_DH)
    vhead = _head_of_lane((nk, BW), SW_DH)
    ps, vs = [], []
    for h in range(SW_HEADS):
        qm = jnp.where(qhead == h, q, jnp.zeros_like(q))
        s = lax.dot_general(qm, kk, _NT, preferred_element_type=F32) * (SW_DH ** -0.5) + bias
        sk = sink_ref[h]
        m = jnp.maximum(jnp.max(s, axis=1, keepdims=True), sk)
        p = jnp.exp(s - m)
        den = jnp.sum(p, axis=1, keepdims=True) + jnp.exp(sk - m)
        ps.append((p / den).astype(BF16))
        vs.append(jnp.where(vhead == h, vv, jnp.zeros_like(vv)))
    o_ref[...] = jnp.dot(jnp.concatenate(ps, axis=1), jnp.concatenate(vs, axis=0),
                         preferred_element_type=F32)


def _sw_bias():
    r = np.arange(TILE)[:, None]
    prev_ok = np.arange(CHUNK)[None, :] - CHUNK - r >= -WINDOW
    own_ok = np.abs(np.arange(TILE)[None, :] - r) <= WINDOW
    next_ok = np.arange(CHUNK)[None, :] + TILE - r <= WINDOW
    hide = lambda m: np.zeros_like(m)
    ctx = np.ones((TILE, TILE), bool)
    kinds = [np.concatenate([hide(prev_ok), hide(own_ok), hide(next_ok), ctx], axis=1)]
    for no_next in (False, True):
        for no_prev in (False, True):
            kinds.append(np.concatenate([hide(prev_ok) if no_prev else prev_ok, own_ok,
                                         hide(next_ok) if no_next else next_ok, ctx], axis=1))
    return jnp.asarray(np.where(np.stack(kinds), 0.0, NEG), F32)


def _sw_call(sink, qk, u, n_batch, tpb):
    n = qk.shape[0]
    per = TILE // CHUNK
    cpb = tpb * per
    bias = _sw_bias()

    def kind(b, j):
        lat = 1 + (j == 1).astype(jnp.int32) + 2 * (j == tpb - 1).astype(jnp.int32)
        return (jnp.where(j == 0, 0, lat), 0, 0)

    own = lambda b, j: b * tpb + j
    prv = lambda b, j: b * cpb + jnp.maximum(j * per - 1, 0)
    nxt = lambda b, j: b * cpb + jnp.minimum((j + 1) * per, cpb - 1)
    kcol, vcol = 3, C_SWV // BW
    edge = lambda f, col: pl.BlockSpec((CHUNK, BW), lambda b, j: (f(b, j), col))
    tile = lambda f, col: pl.BlockSpec((TILE, BW), lambda b, j: (f(b, j), col))
    ctx = lambda b, j: b * tpb
    return pl.pallas_call(
        _sw_kernel,
        grid=(n_batch, tpb),
        in_specs=[pl.BlockSpec(memory_space=pltpu.SMEM),
                  pl.BlockSpec((1,) + bias.shape[1:], kind),
                  tile(own, 2), edge(prv, kcol), tile(own, kcol), edge(nxt, kcol), tile(ctx, kcol),
                  edge(prv, vcol), tile(own, vcol), edge(nxt, vcol), tile(ctx, vcol)],
        out_specs=pl.BlockSpec((TILE, BW), lambda b, j: (own(b, j), 0)),
        out_shape=jax.ShapeDtypeStruct((n, BW), F32),
        compiler_params=_cparams(("arbitrary", "arbitrary")),
        name="window_attn",
    )(sink, bias, qk, qk, qk, qk, qk, u, u, u, u)


def _conv_kernel(prev_ref, cur_ref, next_ref, w_ref, b_ref, ln_ref, o_ref, z_ref, zs_ref, *, tpb):
    i = pl.program_id(0)
    pos = i % tpb
    has_prev = pos >= 2
    has_next = (pos >= 1) & (pos < tpb - 1)
    halo = prev_ref.shape[0]

    def glu(x):
        return x[:, :BW] * _sigmoid(x[:, BW:])

    zp = glu(prev_ref[...])
    zn = glu(next_ref[...])
    z_ref[0:halo, :] = jnp.where(has_prev, zp, 0.0)
    z_ref[halo:halo + TILE, :] = glu(cur_ref[...])
    z_ref[halo + TILE:2 * halo + TILE, :] = jnp.where(has_next, zn, 0.0)
    acc = jnp.zeros((TILE, BW), F32) + b_ref[...]
    pad = CONV_W // 2
    span = TILE + 2 * halo - SUBLANES
    for r in range(SUBLANES):
        zs_ref[r] = z_ref[r:r + span, :]
    for t in range(CONV_W):
        off = halo - pad + t
        shift, base = off % SUBLANES, off - off % SUBLANES
        acc = acc + zs_ref[shift, base:base + TILE, :] * w_ref[t:t + 1, :]
    mu = jnp.mean(acc, axis=-1, keepdims=True)
    xc = acc - mu
    y = xc * lax.rsqrt(jnp.mean(xc * xc, axis=-1, keepdims=True) + EPS)
    y = y * ln_ref[0:1, :] + ln_ref[1:2, :]
    o_ref[...] = _silu(y)


def _conv_call(u, w, b, ln, tpb):
    n = u.shape[0]
    halo = 16
    per = TILE // halo
    nh = n // halo
    wcv = 2 * BW
    col = C_CV // wcv
    return pl.pallas_call(
        functools.partial(_conv_kernel, tpb=tpb),
        grid=(n // TILE,),
        in_specs=[pl.BlockSpec((halo, wcv), lambda i: (jnp.maximum(i * per - 1, 0), col)),
                  pl.BlockSpec((TILE, wcv), lambda i: (i, col)),
                  pl.BlockSpec((halo, wcv), lambda i: (jnp.minimum((i + 1) * per, nh - 1), col)),
                  pl.BlockSpec((CONV_W, BW), lambda i: (0, 0)),
                  pl.BlockSpec((1, BW), lambda i: (0, 0)),
                  pl.BlockSpec((2, BW), lambda i: (0, 0))],
        out_specs=pl.BlockSpec((TILE, BW), lambda i: (i, 0)),
        out_shape=jax.ShapeDtypeStruct((n, BW), F32),
        scratch_shapes=[pltpu.VMEM((TILE + 2 * halo, BW), F32),
                        pltpu.VMEM((SUBLANES, TILE + 2 * halo - SUBLANES, BW), F32)],
        compiler_params=_cparams(("arbitrary",)),
        name="conformer_conv",
    )(u, u, u, w, b.reshape(1, BW), ln)


def _inv_unit_lower(ls, eye):
    ts = [eye + l for l in ls]
    lps = list(ls)
    step = 1
    while step < ls[0].shape[0] // 2:
        lps = [_dot(lp, lp) for lp in lps]
        ts = [t + _dot(t, lp) for t, lp in zip(ts, lps)]
        step *= 2
    return ts


def _rw_chunk_kernel(prev_ref, cur_ref, next_ref, mu_ref, w0_ref, a0_ref, kk_ref, ka_ref,
                     wup_ref, aup_ref, rk_ref, bd_ref,
                     m_ref, n_ref, q_ref, y0_ref, bonus_ref, z_ref, *, cpb):
    i = pl.program_id(0)
    jj = i % cpb
    ctx_chunks = TILE // CHUNK
    has_prev = (jj != 0) & (jj != ctx_chunks)
    has_next = (jj != ctx_chunks - 1) & (jj != cpb - 1)
    c = CHUNK
    zc = cur_ref[:, 0:RW_SHIFT_W]
    e = SUBLANES
    z_ref[0:e, :] = jnp.where(has_prev, prev_ref[:, 0:RW_SHIFT_W], 0.0)
    z_ref[e:e + c, :] = zc
    z_ref[e + c:2 * e + c, :] = jnp.where(has_next, next_ref[:, 0:RW_SHIFT_W], 0.0)
    bd = bd_ref[...]
    t_io = lax.broadcasted_iota(jnp.int32, (c, c), 0)
    s_io = lax.broadcasted_iota(jnp.int32, (c, c), 1)
    eye = jnp.where(t_io == s_io, 1.0, 0.0)
    lane_head = _head_of_lane((c, BW), RW_DH)
    row = lax.broadcasted_iota(jnp.int32, (BW, BW), 0)
    colm = lax.broadcasted_iota(jnp.int32, (BW, BW), 1)
    same_head = (row // RW_DH) == (colm // RW_DH)

    def stack(x):
        return jnp.concatenate(
            [jnp.where(lane_head == h, x, 0.0) for h in range(RW_HEADS)], axis=0)

    cat = lambda xs: jnp.concatenate(xs, axis=1)

    prep = []
    for d in range(2):
        zsh = z_ref[e - 1:e - 1 + c, :] if d == 0 else z_ref[e + 1:e + 1 + c, :]
        zs = zc + (zsh - zc) * mu_ref[d]
        r = zs[:, 0:BW]
        k = zs[:, BW:2 * BW]
        v = zs[:, 2 * BW:3 * BW]
        wa = zs[:, 3 * BW:RW_SHIFT_W]
        lw = -RW_DECAY_SCALE * _sigmoid(w0_ref[d] + _dot(jnp.tanh(wa), wup_ref[d]))
        a = _sigmoid(a0_ref[d] + _dot(wa, aup_ref[d]))
        kappa = k * kk_ref[d]
        kh = kappa * lax.rsqrt(jnp.maximum(_dot_sel(kappa * kappa, bd), 1e-12))
        kt = k * (1.0 + (a - 1.0) * ka_ref[d])
        akh = a * kh
        bonus_ref[d] = _dot_sel(r * kt * rk_ref[...], bd) * v

        ahead = (t_io - s_io) if d == 0 else (s_io - t_io)
        earlier = ahead > 0
        upto = ahead >= 0
        tri = jnp.where(upto, 1.0, 0.0).astype(BF16)
        cl = _sel_dot(tri, lw)
        tot = jnp.sum(lw, axis=0, keepdims=True)
        rho = 0.5 * tot
        cle = cl - lw
        a_true = -kh * jnp.exp(cle)
        r_true = r * jnp.exp(cl)
        a_c = -kh * jnp.exp(cle - rho)
        r_c = r * jnp.exp(cl - rho)
        b_c = akh * jnp.exp(rho - cl)
        k_c = kt * jnp.exp(rho - cl)
        b_end = akh * jnp.exp(tot - cl)
        k_end = kt * jnp.exp(tot - cl)

        pair = _dot(jnp.concatenate([stack(a_c), stack(r_c)], axis=0),
                    jnp.concatenate([b_c, k_c], axis=0), _NT)
        l_ab, l_ak, a_rb, a_rk = [], [], [], []
        for h in range(RW_HEADS):
            blk_a = pair[h * c:(h + 1) * c]
            blk_r = pair[(RW_HEADS + h) * c:(RW_HEADS + h + 1) * c]
            l_ab.append(jnp.where(earlier, blk_a[:, 0:c], 0.0))
            l_ak.append(jnp.where(earlier, blk_a[:, c:2 * c], 0.0))
            a_rb.append(jnp.where(upto, blk_r[:, 0:c], 0.0))
            a_rk.append(jnp.where(upto, blk_r[:, c:2 * c], 0.0))
        prep.append((l_ab, cat(l_ak), cat(a_rb), cat(a_rk), v, a_true, r_true, b_end, k_end, tot))

    t_all = _inv_unit_lower(prep[0][0] + prep[1][0], eye)

    for d in range(2):
        _, lak, arb, ark, v, a_true, r_true, b_end, k_end, tot = prep[d]
        t_inv = cat(t_all[d * RW_HEADS:(d + 1) * RW_HEADS])
        lv = _dot(lak, stack(v))
        w12 = _dot(t_inv, jnp.concatenate([stack(a_true), stack(lv)], axis=1))
        w1 = w12[:, 0:BW]
        w2 = w12[:, BW:2 * BW]
        q_ref[d, 0] = (r_true + _dot(arb, stack(w1))).astype(BF16)
        y0_ref[d, 0] = _dot(arb, stack(w2)) + _dot(ark, stack(v))
        b_end_t = b_end.T
        decay = jnp.where(row == colm, jnp.broadcast_to(jnp.exp(tot), (BW, BW)), 0.0)
        m_ref[d, 0] = (jnp.where(same_head, _dot(b_end_t, w1), 0.0) + decay).astype(BF16)
        n_ref[d, 0] = jnp.where(same_head, _dot(b_end_t, w2) + _dot(k_end.T, v), 0.0)


def _rw_chunk_call(u, lp, bd, tpb):
    n = u.shape[0]
    cpb = tpb * (TILE // CHUNK)
    nch = n // CHUNK
    edge = SUBLANES
    per = CHUNK // edge
    n8 = n // edge
    wrw = RW_SHIFT_W + RW_G_RANK
    col = C_RW // wrw
    pvec = lambda w: pl.BlockSpec((2, 1, w), lambda i: (0, 0, 0))
    pad = jnp.zeros((2, RW_DECAY_RANK, BW), F32)
    wup = jnp.concatenate([lp['rw_w_up'], pad], axis=1).astype(BF16)
    aup = jnp.concatenate([pad, lp['rw_a_up']], axis=1).astype(BF16)
    mat = lambda rows: pl.BlockSpec((2, 1, rows, BW), lambda i: (0, i, 0, 0))
    return pl.pallas_call(
        functools.partial(_rw_chunk_kernel, cpb=cpb),
        grid=(nch,),
        in_specs=[pl.BlockSpec((edge, wrw), lambda i: (jnp.maximum(i * per - 1, 0), col)),
                  pl.BlockSpec((CHUNK, wrw), lambda i: (i, col)),
                  pl.BlockSpec((edge, wrw), lambda i: (jnp.minimum((i + 1) * per, n8 - 1), col)),
                  pvec(RW_SHIFT_W), pvec(BW), pvec(BW), pvec(BW), pvec(BW),
                  pl.BlockSpec((2, 2 * RW_DECAY_RANK, BW), lambda i: (0, 0, 0)),
                  pl.BlockSpec((2, 2 * RW_A_RANK, BW), lambda i: (0, 0, 0)),
                  pl.BlockSpec((1, BW), lambda i: (0, 0)),
                  pl.BlockSpec((BW, BW), lambda i: (0, 0))],
        out_specs=[mat(BW), mat(BW), mat(CHUNK), mat(CHUNK),
                   pl.BlockSpec((2, CHUNK, BW), lambda i: (0, i, 0))],
        out_shape=[jax.ShapeDtypeStruct((2, nch, BW, BW), BF16),
                   jax.ShapeDtypeStruct((2, nch, BW, BW), F32),
                   jax.ShapeDtypeStruct((2, nch, CHUNK, BW), BF16),
                   jax.ShapeDtypeStruct((2, nch, CHUNK, BW), F32),
                   jax.ShapeDtypeStruct((2, n, BW), F32)],
        scratch_shapes=[pltpu.VMEM((CHUNK + 2 * edge, RW_SHIFT_W), F32)],
        compiler_params=_cparams(("arbitrary",)),
        name="rwkv_chunk",
    )(u, u, u, lp['rw_mu'].reshape(2, 1, RW_SHIFT_W), lp['rw_w0'].reshape(2, 1, BW),
      lp['rw_a0'].reshape(2, 1, BW), lp['rw_kk'].reshape(2, 1, BW), lp['rw_ka'].reshape(2, 1, BW),
      wup, aup, lp['rw_rk'].reshape(1, BW), bd)


def _rw_scan_kernel(*refs, n_batch):
    m_refs, n_refs, q_refs, y0_refs = refs[0:2], refs[2:4], refs[4:6], refs[6:8]
    y_refs, x_ref = refs[8:10], refs[10]

    @pl.when(pl.program_id(0) == 0)
    def _():
        x_ref[...] = jnp.zeros(x_ref.shape, F32)

    for d in range(2):
        for b in range(n_batch):
            x = x_ref[d, b]
            y_refs[d][b, 0] = _dot(q_refs[d][0, b, 0], x) + y0_refs[d][0, b, 0]
            x_ref[d, b] = _dot(m_refs[d][0, b, 0], x) + n_refs[d][0, b, 0]


def _rw_scan_call(m, nn, q, y0, n_batch, tpb):
    cpb = tpb * (TILE // CHUNK)
    ctx_chunks = TILE // CHUNK

    def chunk(d, i):
        rev = jnp.where(i < ctx_chunks, ctx_chunks - 1 - i, cpb + ctx_chunks - 1 - i)
        return i if d == 0 else rev

    def mat(rows, d):
        return pl.BlockSpec((1, n_batch, 1, rows, BW), lambda i: (d, 0, chunk(d, i), 0, 0))

    ins, specs = [], []
    for arr, rows in ((m, BW), (nn, BW), (q, CHUNK), (y0, CHUNK)):
        arr = arr.reshape(2, n_batch, cpb, rows, BW)
        for d in range(2):
            ins.append(arr)
            specs.append(mat(rows, d))
    yshape = jax.ShapeDtypeStruct((n_batch, cpb, CHUNK, BW), F32)
    ys = pl.pallas_call(
        functools.partial(_rw_scan_kernel, n_batch=n_batch),
        grid=(cpb,),
        in_specs=specs,
        out_specs=[pl.BlockSpec((n_batch, 1, CHUNK, BW), lambda i, d=d: (0, chunk(d, i), 0, 0))
                   for d in range(2)],
        out_shape=[yshape, yshape],
        scratch_shapes=[pltpu.VMEM((2, n_batch, BW, BW), F32)],
        compiler_params=_cparams(("arbitrary",)),
        name="rwkv_scan",
    )(*ins)
    return [y.reshape(n_batch * cpb * CHUNK, BW) for y in ys]


def _rw_readout(yf_ref, yb_ref, bonus_ref, gd_ref, gup_ref, gn_ref, bd_ref):
    bd = bd_ref[...]
    y = yf_ref[...] + yb_ref[...]
    mean = _dot_sel(y, bd) * (1.0 / RW_DH)
    yc = y - mean
    var = _dot_sel(yc * yc, bd) * (1.0 / RW_DH)
    yn = yc * lax.rsqrt(var + RW_GN_EPS) * gn_ref[0:1, :] + gn_ref[1:2, :]
    yn = yn + (bonus_ref[0] + bonus_ref[1])
    return yn * _dot(_sigmoid(gd_ref[...]), gup_ref[...])


def _merge_kernel(oa_ref, ob_ref, oc_ref, yf_ref, yb_ref, bonus_ref, gd_ref, gup_ref, gn_ref, bd_ref,
                  gl_ref, wb_ref, wo_ref, x_ref, g_ref, mod_ref,
                  g2_ref, sh_ref, sc_ref, rw_ref, rb_ref, o_ref, tok_ref, gate_ref):
    d = x_ref.shape[1]
    od = _rw_readout(yf_ref, yb_ref, bonus_ref, gd_ref, gup_ref, gn_ref, bd_ref)
    m = None
    for i, o in enumerate((oa_ref[...], ob_ref[...], oc_ref[...], od)):
        t = _sigmoid(gl_ref[:, i * d:(i + 1) * d]) * _dot(o, wb_ref[i])
        m = t if m is None else m + t
    y = _dot(m, wo_ref[...])
    y = y * lax.rsqrt(jnp.mean(y * y, axis=-1, keepdims=True) + EPS) * g_ref[...]
    x = x_ref[...] + mod_ref[0] * y
    o_ref[...] = x
    t = x * lax.rsqrt(jnp.mean(x * x, axis=-1, keepdims=True) + EPS) * g2_ref[...]
    tok = t * (1.0 + sc_ref[0]) + sh_ref[0]
    tok_ref[...] = tok.astype(BF16)
    gate_ref[...] = _route(tok, rw_ref[...], rb_ref[...])


def _merge_call(outs, ys, bonus, gup, gn, bd, u, wb, wo, x, g, mod, g2, shift, scale, rw_t, rb,
                tpb, n_batch):
    n, d = x.shape
    kind = lambda i: (jnp.where(i % tpb == 0, n_batch, i // tpb), 0, 0)
    br = pl.BlockSpec((TILE, BW), lambda i: (i, 0))
    gcol = (C_RW + RW_SHIFT_W) // RW_G_RANK
    readout = [br, br, pl.BlockSpec((2, TILE, BW), lambda i: (0, i, 0)),
               pl.BlockSpec((TILE, RW_G_RANK), lambda i: (i, gcol)),
               pl.BlockSpec((RW_G_RANK, BW), lambda i: (0, 0)),
               pl.BlockSpec((2, BW), lambda i: (0, 0)),
               pl.BlockSpec((BW, BW), lambda i: (0, 0))]
    vec = pl.BlockSpec((1, d), lambda i: (0, 0))
    row = pl.BlockSpec((TILE, d), lambda i: (i, 0))
    wg = N_BRANCH * d
    return pl.pallas_call(
        _merge_kernel,
        grid=(n // TILE,),
        in_specs=[br, br, br, *readout,
                  pl.BlockSpec((TILE, wg), lambda i: (i, C_GATE // wg)),
                  pl.BlockSpec((N_BRANCH, BW, d), lambda i: (0, 0, 0)),
                  pl.BlockSpec((d, d), lambda i: (0, 0)),
                  row, vec, pl.BlockSpec((1, 1, d), kind),
                  vec, pl.BlockSpec((1, 1, d), kind), pl.BlockSpec((1, 1, d), kind),
                  pl.BlockSpec((N_EXPERTS, d), lambda i: (0, 0)),
                  pl.BlockSpec((N_EXPERTS, 1), lambda i: (0, 0))],
        out_specs=[row, row, pl.BlockSpec((N_EXPERTS, TILE), lambda i: (0, i))],
        out_shape=[jax.ShapeDtypeStruct((n, d), F32), jax.ShapeDtypeStruct((n, d), BF16),
                   jax.ShapeDtypeStruct((N_EXPERTS, n), F32)],
        compiler_params=_cparams(("arbitrary",)),
        name="merge",
    )(*outs, ys[0], ys[1], bonus, u, gup, gn, bd, u, wb, wo, x, g.reshape(1, d), mod,
      g2.reshape(1, d), shift, scale, rw_t, rb.reshape(N_EXPERTS, 1))


def _route(tokens, w, b):
    tm = tokens.shape[0]
    gsz = N_EXPERTS // N_GROUPS
    logits = _dot3(w, tokens, _NT)
    sc = _sigmoid(logits).reshape(N_GROUPS, gsz, tm)
    bi = sc + b.reshape(N_GROUPS, gsz, 1)
    shape = (N_GROUPS, gsz, tm)
    g_io = lax.broadcasted_iota(jnp.int32, shape, 0)
    j_io = lax.broadcasted_iota(jnp.int32, shape, 1)
    e_io = g_io * gsz + j_io
    ninf = -jnp.inf
    m1 = jnp.max(bi, axis=1, keepdims=True)
    i1 = jnp.min(jnp.where(bi == m1, j_io, gsz), axis=1, keepdims=True)
    m2 = jnp.max(jnp.where(j_io == i1, ninf, bi), axis=1, keepdims=True)
    cur = jnp.broadcast_to(m1 + m2, shape)
    gsel = jnp.zeros(shape, F32)
    for _ in range(TOPK_GROUPS):
        mx = jnp.max(cur, axis=0, keepdims=True)
        ix = jnp.min(jnp.where(cur == mx, g_io, N_GROUPS), axis=0, keepdims=True)
        hit = g_io == ix
        gsel = jnp.where(hit, 1.0, gsel)
        cur = jnp.where(hit, ninf, cur)
    cur = jnp.where(gsel > 0.0, bi, ninf)
    esel = jnp.zeros(shape, F32)
    for _ in range(TOP_K):
        mx = jnp.max(jnp.max(cur, axis=0, keepdims=True), axis=1, keepdims=True)
        ix = jnp.min(jnp.min(jnp.where(cur == mx, e_io, N_EXPERTS), axis=0, keepdims=True),
                     axis=1, keepdims=True)
        hit = e_io == ix
        esel = jnp.where(hit, 1.0, esel)
        cur = jnp.where(hit, ninf, cur)
    wsel = sc * esel
    den = jnp.sum(jnp.sum(wsel, axis=0, keepdims=True), axis=1, keepdims=True)
    return (wsel / den * ROUTE_SCALE).reshape(N_EXPERTS, tm)


def _moe_kernel(x_ref, g_ref, wgu_ref, wd_ref, sgu_ref, sd_ref, o_ref):
    e = pl.program_id(1)

    def ffn(wgu, wd):
        hgu = _dot(x_ref[...], wgu)
        return _dot(_silu(hgu[:, :D_EXPERT]) * hgu[:, D_EXPERT:], wd)

    @pl.when(e == 0)
    def _():
        o_ref[...] = ffn(sgu_ref[...], sd_ref[...])

    gates = g_ref[...]
    lane = lax.broadcasted_iota(jnp.int32, gates.shape, 1)
    gcol = jnp.sum(jnp.where(lane == e, gates, 0.0), axis=1, keepdims=True)
    o_ref[...] += ffn(wgu_ref[0, 0], wd_ref[0, 0]) * gcol


def _moe_call(tok, gates, wgu, wd, layer, sgu, sd, tm):
    n, d = tok.shape
    return pl.pallas_call(
        _moe_kernel,
        grid=(n // tm, N_EXPERTS),
        in_specs=[pl.BlockSpec((tm, d), lambda i, e: (i, 0)),
                  pl.BlockSpec((tm, N_EXPERTS), lambda i, e: (i, 0)),
                  pl.BlockSpec((1, 1, d, 2 * D_EXPERT), lambda i, e: (layer, e, 0, 0)),
                  pl.BlockSpec((1, 1, D_EXPERT, d), lambda i, e: (layer, e, 0, 0)),
                  pl.BlockSpec((d, 2 * D_EXPERT), lambda i, e: (0, 0)),
                  pl.BlockSpec((D_EXPERT, d), lambda i, e: (0, 0))],
        out_specs=pl.BlockSpec((tm, d), lambda i, e: (i, 0)),
        out_shape=jax.ShapeDtypeStruct((n, d), F32),
        compiler_params=_cparams(("arbitrary", "arbitrary"), MOE_VMEM_LIMIT),
        name="moe_experts",
    )(tok, gates, wgu, wd, sgu, sd)


def _resid_kernel(x_ref, f_ref, g_ref, mod_ref, o_ref):
    f = f_ref[...]
    y = f * lax.rsqrt(jnp.mean(f * f, axis=-1, keepdims=True) + EPS) * g_ref[...]
    o_ref[...] = x_ref[...] + mod_ref[0] * y


def _resid_call(x, f, g, mod, tpb, n_batch, latent_only=False):
    n, d = x.shape
    vec = pl.BlockSpec((1, d), lambda *_: (0, 0))
    if latent_only:
        lpb = tpb - 1
        grid = (n_batch, lpb)
        row = pl.BlockSpec((TILE, d), lambda b, i: (b * tpb + 1 + i, 0))
        out = pl.BlockSpec((TILE, d), lambda b, i: (b * lpb + i, 0))
        kind = pl.BlockSpec((1, 1, d), lambda b, i: (b, 0, 0))
        n_out = n_batch * lpb * TILE
    else:
        grid = (n // TILE,)
        row = out = pl.BlockSpec((TILE, d), lambda i: (i, 0))
        kind = pl.BlockSpec((1, 1, d), lambda i: (jnp.where(i % tpb == 0, n_batch, i // tpb), 0, 0))
        n_out = n
    return pl.pallas_call(
        _resid_kernel,
        grid=grid,
        in_specs=[row, row, vec, kind],
        out_specs=out,
        out_shape=jax.ShapeDtypeStruct((n_out, d), F32),
        compiler_params=_cparams(("arbitrary",) * len(grid)),
        name="moe_residual",
    )(x, f, g.reshape(1, d), mod)


def _proj_weights(w):
    w = w.astype(BF16)
    k = w.shape[0]
    da, cv, sw = 0, 3 * BW, 5 * BW
    rw = sw + (SW_HEADS + 2 * SW_KV) * SW_DH
    gate = rw + RW_SHIFT_W + RW_G_RANK

    def per_q_head(x):
        return jnp.repeat(x.reshape(k, SW_KV, SW_DH), SW_HEADS // SW_KV, axis=1).reshape(k, -1)

    da_q, da_k, da_v = (w[:, da + i * BW:da + (i + 1) * BW] for i in range(3))
    sw_q = w[:, sw:sw + BW]
    sw_k = per_q_head(w[:, sw + BW:sw + BW + SW_KV * SW_DH])
    sw_v = per_q_head(w[:, sw + BW + SW_KV * SW_DH:rw])
    out = jnp.concatenate([
        w[:, gate:], da_q, da_k, sw_q, sw_k, da_v, sw_v, w[:, cv:sw], w[:, rw:gate]], axis=1)
    assert out.shape[1] == C_TOTAL
    return out


def _rope_tables(seq):
    rows = seq // GRID_W
    row = jnp.repeat(jnp.arange(rows, dtype=F32), GRID_W)
    colp = jnp.tile(jnp.arange(GRID_W, dtype=F32), rows)

    def tables(dim):
        q = dim // 4
        freqs = ROPE_BASE ** (-jnp.arange(q, dtype=F32) / q)
        ar, ac = row[:, None] * freqs, colp[:, None] * freqs
        cos = jnp.concatenate([jnp.cos(ar), jnp.cos(ar), jnp.cos(ac), jnp.cos(ac)], axis=1)
        sin = jnp.concatenate([-jnp.sin(ar), jnp.sin(ar), -jnp.sin(ac), jnp.sin(ac)], axis=1)
        return jnp.tile(cos, (1, LANES // dim)), jnp.tile(sin, (1, LANES // dim))

    c32, s32 = tables(DA_HALF)
    c64, s64 = tables(SW_DH)
    cos = jnp.concatenate([c32, c64], axis=1)
    sin = jnp.concatenate([s32, s64], axis=1)
    w = cos.shape[1]
    cos = jnp.concatenate([jnp.ones((TILE, w), F32), cos], axis=0)
    sin = jnp.concatenate([jnp.zeros((TILE, w), F32), sin], axis=0)
    return cos, sin


def _moe_tile(rows_b):
    best = TILE
    for t in range(TILE, MOE_MAX_TILE + 1, 16):
        if rows_b % t == 0:
            best = t
    return best


def kernel(x, c, ctx, c_ctx, ada_w, ada_b, norm_g, w_in, w_branch, w_out, da_lambda, da_subln,
           cv_w, cv_b, cv_ln, sw_sink, rw_mu, rw_w0, rw_w_up, rw_a0, rw_a_up, rw_kk, rw_ka,
           rw_g_up, rw_rk, rw_gn, router_w, router_b, ex_w_gu, ex_w_down, sh_w_gu, sh_w_down):
    n_batch, seq, d = x.shape
    ctx_len = ctx.shape[1]
    depth = w_in.shape[0]
    assert ctx_len == TILE and seq % TILE == 0 and seq % GRID_W == 0
    assert n_batch + 1 <= SUBLANES
    rows_b = ctx_len + seq
    tpb = rows_b // TILE
    n = n_batch * rows_b

    xs = jnp.concatenate([ctx, x], axis=1).reshape(n, d)
    cond = jnp.zeros((SUBLANES, d), F32).at[:n_batch].set(c).at[n_batch].set(c_ctx)
    mods = _ada_call(cond, ada_w, ada_b)[:, :n_batch + 1]
    cos_t, sin_t = _rope_tables(seq)
    hio = np.arange(BW) // RW_DH
    bd = jnp.asarray(hio[:, None] == hio[None, :], BF16)

    for l in range(depth):
        mod = [mods[l, :, i * d:(i + 1) * d].reshape(n_batch + 1, 1, d) for i in range(6)]
        ng = norm_g[l]
        lam_init = 0.8 - 0.6 * math.exp(-0.3 * l)
        lv = da_lambda[l]
        lam = (jnp.exp(jnp.sum(lv[0] * lv[1])) - jnp.exp(jnp.sum(lv[2] * lv[3])) + lam_init)
        lp = {'rw_mu': rw_mu[l], 'rw_w0': rw_w0[l], 'rw_w_up': rw_w_up[l], 'rw_a0': rw_a0[l],
              'rw_a_up': rw_a_up[l], 'rw_kk': rw_kk[l], 'rw_ka': rw_ka[l], 'rw_rk': rw_rk[l]}

        u = _proj_call(xs, ng[0], mod[0], mod[1], _proj_weights(w_in[l]), tpb, n_batch)
        qk, vt = _rope_call(u, cos_t, sin_t, n_batch, tpb)
        oa = _da_call(lam.reshape(1), qk, vt, jnp.tile(da_subln[l], DA_HEADS).reshape(1, BW),
                      n_batch, tpb, lam_init)
        ob = _conv_call(u, cv_w[l], cv_b[l], cv_ln[l], tpb)
        oc = _sw_call(sw_sink[l], qk, u, n_batch, tpb)
        cm, cn, cq, cy0, bonus = _rw_chunk_call(u, lp, bd, tpb)
        yscan = _rw_scan_call(cm, cn, cq, cy0, n_batch, tpb)
        xs, tok, gates_t = _merge_call(
            (oa, ob, oc), yscan, bonus, rw_g_up[l].astype(BF16), rw_gn[l], bd,
            u, w_branch[l].astype(BF16), w_out[l].astype(BF16), xs, ng[1],
            mod[2], ng[2], mod[3], mod[4], router_w[l].T, router_b[l], tpb, n_batch)
        f = _moe_call(tok, gates_t.T, ex_w_gu, ex_w_down, l, sh_w_gu[l].astype(BF16),
                      sh_w_down[l].astype(BF16), _moe_tile(rows_b))
        xs = _resid_call(xs, f, ng[3], mod[5], tpb, n_batch, latent_only=(l == depth - 1))

    return xs.reshape(n_batch, seq, d)
```

```python
import functools
import math

import numpy as np
import jax
import jax.numpy as jnp
from jax import lax
from jax.experimental import pallas as pl
from jax.experimental.pallas import tpu as pltpu

F32 = jnp.float32
BF16 = jnp.bfloat16
SUBLANES, LANES = 8, 128

GRID_W = 64
EPS = 1e-6
ROPE_BASE = 10000.0
N_BRANCH = 4
BW = 256
DA_HEADS = 4
DA_HALF = 32
SW_HEADS = 4
SW_KV = 2
SW_DH = 64
WINDOW = 128
CONV_W = 31
RW_HEADS = 4
RW_DH = 64
RW_DECAY_RANK = 64
RW_A_RANK = 64
RW_G_RANK = 128
RW_DECAY_SCALE = math.exp(-0.5)
RW_GN_EPS = 64e-5
RW_SHIFT_W = 3 * BW + RW_DECAY_RANK + RW_A_RANK
N_EXPERTS = 64
TOP_K = 6
N_GROUPS = 8
TOPK_GROUPS = 4
D_EXPERT = 256
ROUTE_SCALE = 2.5

TILE = 256
CHUNK = 128
DA_KT = 1408
MOE_MAX_TILE = 2816
MOE_VMEM_LIMIT = 58 * 1024 * 1024
DA_VROWS = 80
NEG = -1e30

C_GATE = 0
C_ROPE = 4096
C_DAV = 5120
C_SWV = 5376
C_CV = 5632
C_RW = 6144
C_TOTAL = 7168

VMEM_LIMIT = 48 * 1024 * 1024

_NT = (((1,), (1,)), ((), ()))
_NN = (((1,), (0,)), ((), ()))


def _cparams(sem, vmem=VMEM_LIMIT):
    return pltpu.CompilerParams(dimension_semantics=sem, vmem_limit_bytes=vmem)


def _dot(a, b, dims=_NN):
    return lax.dot_general(a.astype(BF16), b.astype(BF16), dims, preferred_element_type=F32)


def _split2(x):
    hi = x.astype(BF16)
    lo = (x - hi.astype(F32)).astype(BF16)
    return hi, lo


def _dot3(a, b, dims=_NN):
    ah, al = _split2(a)
    bh, bl = _split2(b)
    dg = lambda x, y: lax.dot_general(x, y, dims, preferred_element_type=F32)
    return dg(ah, bh) + (dg(ah, bl) + dg(al, bh))


def _dot_sel(x, sel, dims=_NN):
    h0 = x.astype(BF16)
    r1 = x - h0.astype(F32)
    h1 = r1.astype(BF16)
    h2 = (r1 - h1.astype(F32)).astype(BF16)
    dg = lambda y: lax.dot_general(y, sel, dims, preferred_element_type=F32)
    return dg(h0) + (dg(h1) + dg(h2))


def _sel_dot(sel, x):
    h0 = x.astype(BF16)
    r1 = x - h0.astype(F32)
    h1 = r1.astype(BF16)
    h2 = (r1 - h1.astype(F32)).astype(BF16)
    dg = lambda y: lax.dot_general(sel, y, _NN, preferred_element_type=F32)
    return dg(h0) + (dg(h1) + dg(h2))


def _sigmoid(x):
    return jax.nn.sigmoid(x)


def _silu(x):
    return x * jax.nn.sigmoid(x)


def _head_of_lane(shape, width):
    return lax.broadcasted_iota(jnp.int32, shape, len(shape) - 1) // width


def _ada_kernel(s_ref, w_ref, b_ref, o_ref):
    s = _silu(s_ref[...])
    o_ref[0] = _dot(s, w_ref[0]) + b_ref[0]


def _ada_call(cond, ada_w, ada_b):
    depth, d, cols = ada_w.shape
    tn = 1536
    return pl.pallas_call(
        _ada_kernel,
        grid=(depth, cols // tn),
        in_specs=[pl.BlockSpec((SUBLANES, d), lambda l, j: (0, 0)),
                  pl.BlockSpec((1, d, tn), lambda l, j: (l, 0, j)),
                  pl.BlockSpec((1, 1, tn), lambda l, j: (l, 0, j))],
        out_specs=pl.BlockSpec((1, SUBLANES, tn), lambda l, j: (l, 0, j)),
        out_shape=jax.ShapeDtypeStruct((depth, SUBLANES, cols), F32),
        compiler_params=_cparams(("arbitrary", "arbitrary")),
        name="ada_mod",
    )(cond, ada_w, ada_b.reshape(depth, 1, cols))


PROJ_TILES = 2


def _proj_kernel(x_ref, g_ref, *refs):
    mods, w_ref, o_ref = refs[:2 * PROJ_TILES], refs[2 * PROJ_TILES], refs[2 * PROJ_TILES + 1]
    x = x_ref[...]
    y = x * lax.rsqrt(jnp.mean(x * x, axis=-1, keepdims=True) + EPS) * g_ref[...]
    h = jnp.concatenate(
        [(y[t * TILE:(t + 1) * TILE] * (1.0 + mods[2 * t + 1][0]) + mods[2 * t][0]).astype(BF16)
         for t in range(PROJ_TILES)], axis=0)
    o_ref[...] = jnp.dot(h, w_ref[...], preferred_element_type=F32)


def _proj_call(x, g, shift, scale, w, tpb, n_batch):
    n, d = x.shape
    cols = w.shape[1]
    tm, tn = PROJ_TILES * TILE, cols // 2

    def kind(t):
        def index(j, i):
            tile = i * PROJ_TILES + t
            return (jnp.where(tile % tpb == 0, n_batch, tile // tpb), 0, 0)
        return pl.BlockSpec((1, 1, d), index)

    mod_specs, mod_args = [], []
    for t in range(PROJ_TILES):
        mod_specs += [kind(t), kind(t)]
        mod_args += [shift, scale]
    return pl.pallas_call(
        _proj_kernel,
        grid=(cols // tn, n // tm),
        in_specs=[pl.BlockSpec((tm, d), lambda j, i: (i, 0)),
                  pl.BlockSpec((1, d), lambda j, i: (0, 0)), *mod_specs,
                  pl.BlockSpec((d, tn), lambda j, i: (0, j))],
        out_specs=pl.BlockSpec((tm, tn), lambda j, i: (i, j)),
        out_shape=jax.ShapeDtypeStruct((n, cols), F32),
        compiler_params=_cparams(("arbitrary", "arbitrary")),
        name="in_proj",
    )(x, g.reshape(1, d), *mod_args, w)


def _rope_kernel(u_ref, c_ref, s_ref, v_ref, o_ref, vt_ref):
    lanes = LANES
    n_batch, rows, w = u_ref.shape
    lane = lax.broadcasted_iota(jnp.int32, (rows, lanes), 1)
    hd = 2 * DA_HALF
    qscale = (DA_HALF ** -0.5) * math.log2(math.e)
    for b in range(n_batch):
        for k in range(w // lanes):
            cols = slice(k * lanes, (k + 1) * lanes)
            is_da = k * lanes < 2 * BW
            q = (DA_HALF if is_da else SW_DH) // 4
            tab = slice(0, lanes) if is_da else slice(lanes, 2 * lanes)
            x = u_ref[b, :, cols]
            partner = jnp.where((lane // q) % 2 == 0, pltpu.roll(x, lanes - q, 1),
                                pltpu.roll(x, q, 1))
            y = x * c_ref[:, tab] + partner * s_ref[:, tab]
            if k * lanes < BW:
                y = y * qscale
            o_ref[b, :, cols] = y.astype(BF16)
        vt = v_ref[b].T.astype(BF16)
        for h in range(DA_HEADS):
            vt_ref[b, h * DA_VROWS:h * DA_VROWS + hd, :] = vt[h * hd:(h + 1) * hd, :]
            vt_ref[b, h * DA_VROWS + hd:(h + 1) * DA_VROWS, :] = jnp.ones((DA_VROWS - hd, rows),
                                                                          BF16)


def _rope_call(u, cos_t, sin_t, n_batch, tpb):
    n, wu = u.shape
    rows_b = tpb * TILE
    w = 4 * BW
    vrows = DA_HEADS * DA_VROWS
    tab = pl.BlockSpec((TILE, cos_t.shape[1]), lambda i: (i, 0))
    u3 = u.reshape(n_batch, rows_b, wu)
    qk, vt = pl.pallas_call(
        _rope_kernel,
        grid=(tpb,),
        in_specs=[pl.BlockSpec((n_batch, TILE, w), lambda i: (0, i, C_ROPE // w)), tab, tab,
                  pl.BlockSpec((n_batch, TILE, BW), lambda i: (0, i, C_DAV // BW))],
        out_specs=[pl.BlockSpec((n_batch, TILE, w), lambda i: (0, i, 0)),
                   pl.BlockSpec((n_batch, vrows, TILE), lambda i: (0, 0, i))],
        out_shape=[jax.ShapeDtypeStruct((n_batch, rows_b, w), BF16),
                   jax.ShapeDtypeStruct((n_batch, vrows, rows_b), BF16)],
        compiler_params=_cparams(("arbitrary",)),
        name="rope",
    )(u3, cos_t, sin_t, u3)
    return qk.reshape(n, w), vt.reshape(n_batch * vrows, rows_b)


def _da_kernel(lam_ref, q_ref, k_ref, vt_ref, g_ref, o_ref,
               qs_ref, m_ref, acc_ref, *, nkt, lam_init):
    i = pl.program_id(1)
    tq = q_ref.shape[0]
    hd = 2 * DA_HALF
    q = q_ref[...]
    qmap = _head_of_lane((tq, BW), DA_HALF)
    for g in range(2 * DA_HEADS):
        qs_ref[g] = jnp.where(qmap == g, q, jnp.zeros_like(q))
    m_ref[...] = jnp.full(m_ref.shape, NEG, F32)
    acc_ref[...] = jnp.zeros(acc_ref.shape, F32)

    def tile(off, size):
        kt = k_ref[pl.ds(off, size), :]
        groups = range(2 * DA_HEADS)
        ss = [lax.dot_general(kt, qs_ref[g], _NT, preferred_element_type=F32) for g in groups]
        for g in groups:
            s = ss[g]
            m_old = m_ref[g]
            m_new = jnp.maximum(m_old, jnp.max(s, axis=0, keepdims=True))
            alpha = jnp.exp2(m_old - m_new)[0:1, :]
            p = jnp.exp2(s - m_new[0:1, :]).astype(BF16)
            m_ref[g] = m_new
            h, mm = g // 2, g % 2
            rows = slice(h * DA_VROWS, (h + 1) * DA_VROWS)
            pv = jnp.dot(vt_ref[rows, pl.ds(off, size)], p, preferred_element_type=F32)
            acc_ref[mm, rows, :] = acc_ref[mm, rows, :] * alpha + pv

    @pl.when(i == 0)
    def _():
        tile(0, TILE)

    @pl.when(i > 0)
    def _():
        def body(j, carry):
            tile(pl.multiple_of(j * DA_KT, DA_KT), DA_KT)
            return carry

        lax.fori_loop(0, nkt, body, 0)

    lam = lam_ref[0]
    parts = []
    for h in range(DA_HEADS):
        rows = slice(h * DA_VROWS, h * DA_VROWS + hd)
        den = slice(h * DA_VROWS + hd, h * DA_VROWS + hd + 1)
        o_h = (acc_ref[0, rows, :] / acc_ref[0, den, :]
               - lam * (acc_ref[1, rows, :] / acc_ref[1, den, :]))
        ms = jnp.mean(o_h * o_h, axis=0, keepdims=True)
        parts.append(o_h * lax.rsqrt(ms + EPS))
    y = jnp.concatenate(parts, axis=0).T * g_ref[...]
    o_ref[...] = y * (1.0 - lam_init)


def _da_call(lam, qk, vt, subln, n_batch, tpb, lam_init):
    n = qk.shape[0]
    rows_b = tpb * TILE
    assert rows_b % DA_KT == 0
    vrows = DA_HEADS * DA_VROWS
    kern = functools.partial(_da_kernel, nkt=rows_b // DA_KT, lam_init=lam_init)
    return pl.pallas_call(
        kern,
        grid=(n_batch, tpb),
        in_specs=[pl.BlockSpec(memory_space=pltpu.SMEM),
                  pl.BlockSpec((TILE, BW), lambda b, i: (b * tpb + i, 0)),
                  pl.BlockSpec((rows_b, BW), lambda b, i: (b, 1)),
                  pl.BlockSpec((vrows, rows_b), lambda b, i: (b, 0)),
                  pl.BlockSpec((1, BW), lambda b, i: (0, 0))],
        out_specs=pl.BlockSpec((TILE, BW), lambda b, i: (b * tpb + i, 0)),
        out_shape=jax.ShapeDtypeStruct((n, BW), F32),
        scratch_shapes=[pltpu.VMEM((2 * DA_HEADS, TILE, BW), BF16),
                        pltpu.VMEM((2 * DA_HEADS, SUBLANES, TILE), F32),
                        pltpu.VMEM((2, vrows, TILE), F32)],
        compiler_params=_cparams(("arbitrary", "arbitrary")),
        name="diff_attn",
    )(lam, qk, qk, vt, subln)


def _sw_kernel(sink_ref, bias_ref, q_ref, kp_ref, ko_ref, kn_ref, kc_ref,
               vp_ref, vo_ref, vn_ref, vc_ref, o_ref):
    tq = q_ref.shape[0]
    q = q_ref[...]
    kk = jnp.concatenate([kp_ref[...], ko_ref[...], kn_ref[...], kc_ref[...]], axis=0)
    vv = jnp.concatenate([vp_ref[...], vo_ref[...], vn_ref[...], vc_ref[...]], axis=0).astype(BF16)
    nk = kk.shape[0]
    bias = bias_ref[0]
    qhead = _head_of_lane((tq, BW), SW_DH)
    vhead = _head_of_lane((nk, BW), SW_DH)
    ps, vs = [], []
    for h in range(SW_HEADS):
        qm = jnp.where(qhead == h, q, jnp.zeros_like(q))
        s = lax.dot_general(qm, kk, _NT, preferred_element_type=F32) * (SW_DH ** -0.5) + bias
        sk = sink_ref[h]
        m = jnp.maximum(jnp.max(s, axis=1, keepdims=True), sk)
        p = jnp.exp(s - m)
        den = jnp.sum(p, axis=1, keepdims=True) + jnp.exp(sk - m)
        ps.append((p / den).astype(BF16))
        vs.append(jnp.where(vhead == h, vv, jnp.zeros_like(vv)))
    o_ref[...] = jnp.dot(jnp.concatenate(ps, axis=1), jnp.concatenate(vs, axis=0),
                         preferred_element_type=F32)


def _sw_bias():
    r = np.arange(TILE)[:, None]
    prev_ok = np.arange(CHUNK)[None, :] - CHUNK - r >= -WINDOW
    own_ok = np.abs(np.arange(TILE)[None, :] - r) <= WINDOW
    next_ok = np.arange(CHUNK)[None, :] + TILE - r <= WINDOW
    hide = lambda m: np.zeros_like(m)
    ctx = np.ones((TILE, TILE), bool)
    kinds = [np.concatenate([hide(prev_ok), hide(own_ok), hide(next_ok), ctx], axis=1)]
    for no_next in (False, True):
        for no_prev in (False, True):
            kinds.append(np.concatenate([hide(prev_ok) if no_prev else prev_ok, own_ok,
                                         hide(next_ok) if no_next else next_ok, ctx], axis=1))
    return jnp.asarray(np.where(np.stack(kinds), 0.0, NEG), F32)


def _sw_call(sink, qk, u, n_batch, tpb):
    n = qk.shape[0]
    per = TILE // CHUNK
    cpb = tpb * per
    bias = _sw_bias()

    def kind(b, j):
        lat = 1 + (j == 1).astype(jnp.int32) + 2 * (j == tpb - 1).astype(jnp.int32)
        return (jnp.where(j == 0, 0, lat), 0, 0)

    own = lambda b, j: b * tpb + j
    prv = lambda b, j: b * cpb + jnp.maximum(j * per - 1, 0)
    nxt = lambda b, j: b * cpb + jnp.minimum((j + 1) * per, cpb - 1)
    kcol, vcol = 3, C_SWV // BW
    edge = lambda f, col: pl.BlockSpec((CHUNK, BW), lambda b, j: (f(b, j), col))
    tile = lambda f, col: pl.BlockSpec((TILE, BW), lambda b, j: (f(b, j), col))
    ctx = lambda b, j: b * tpb
    return pl.pallas_call(
        _sw_kernel,
        grid=(n_batch, tpb),
        in_specs=[pl.BlockSpec(memory_space=pltpu.SMEM),
                  pl.BlockSpec((1,) + bias.shape[1:], kind),
                  tile(own, 2), edge(prv, kcol), tile(own, kcol), edge(nxt, kcol), tile(ctx, kcol),
                  edge(prv, vcol), tile(own, vcol), edge(nxt, vcol), tile(ctx, vcol)],
        out_specs=pl.BlockSpec((TILE, BW), lambda b, j: (own(b, j), 0)),
        out_shape=jax.ShapeDtypeStruct((n, BW), F32),
        compiler_params=_cparams(("arbitrary", "arbitrary")),
        name="window_attn",
    )(sink, bias, qk, qk, qk, qk, qk, u, u, u, u)


def _conv_kernel(prev_ref, cur_ref, next_ref, w_ref, b_ref, ln_ref, o_ref, z_ref, zs_ref, *, tpb):
    i = pl.program_id(0)
    pos = i % tpb
    has_prev = pos >= 2
    has_next = (pos >= 1) & (pos < tpb - 1)
    halo = prev_ref.shape[0]

    def glu(x):
        return x[:, :BW] * _sigmoid(x[:, BW:])

    zp = glu(prev_ref[...])
    zn = glu(next_ref[...])
    z_ref[0:halo, :] = jnp.where(has_prev, zp, 0.0)
    z_ref[halo:halo + TILE, :] = glu(cur_ref[...])
    z_ref[halo + TILE:2 * halo + TILE, :] = jnp.where(has_next, zn, 0.0)
    acc = jnp.zeros((TILE, BW), F32) + b_ref[...]
    pad = CONV_W // 2
    span = TILE + 2 * halo - SUBLANES
    for r in range(SUBLANES):
        zs_ref[r] = z_ref[r:r + span, :]
    for t in range(CONV_W):
        off = halo - pad + t
        shift, base = off % SUBLANES, off - off % SUBLANES
        acc = acc + zs_ref[shift, base:base + TILE, :] * w_ref[t:t + 1, :]
    mu = jnp.mean(acc, axis=-1, keepdims=True)
    xc = acc - mu
    y = xc * lax.rsqrt(jnp.mean(xc * xc, axis=-1, keepdims=True) + EPS)
    y = y * ln_ref[0:1, :] + ln_ref[1:2, :]
    o_ref[...] = _silu(y)


def _conv_call(u, w, b, ln, tpb):
    n = u.shape[0]
    halo = 16
    per = TILE // halo
    nh = n // halo
    wcv = 2 * BW
    col = C_CV // wcv
    return pl.pallas_call(
        functools.partial(_conv_kernel, tpb=tpb),
        grid=(n // TILE,),
        in_specs=[pl.BlockSpec((halo, wcv), lambda i: (jnp.maximum(i * per - 1, 0), col)),
                  pl.BlockSpec((TILE, wcv), lambda i: (i, col)),
                  pl.BlockSpec((halo, wcv), lambda i: (jnp.minimum((i + 1) * per, nh - 1), col)),
                  pl.BlockSpec((CONV_W, BW), lambda i: (0, 0)),
                  pl.BlockSpec((1, BW), lambda i: (0, 0)),
                  pl.BlockSpec((2, BW), lambda i: (0, 0))],
        out_specs=pl.BlockSpec((TILE, BW), lambda i: (i, 0)),
        out_shape=jax.ShapeDtypeStruct((n, BW), F32),
        scratch_shapes=[pltpu.VMEM((TILE + 2 * halo, BW), F32),
                        pltpu.VMEM((SUBLANES, TILE + 2 * halo - SUBLANES, BW), F32)],
        compiler_params=_cparams(("arbitrary",)),
        name="conformer_conv",
    )(u, u, u, w, b.reshape(1, BW), ln)


def _inv_unit_lower(ls, eye):
    ts = [eye + l for l in ls]
    lps = list(ls)
    step = 1
    while step < ls[0].shape[0] // 2:
        lps = [_dot(lp, lp) for lp in lps]
        ts = [t + _dot(t, lp) for t, lp in zip(ts, lps)]
        step *= 2
    return ts


def _rw_chunk_kernel(prev_ref, cur_ref, next_ref, mu_ref, w0_ref, a0_ref, kk_ref, ka_ref,
                     wup_ref, aup_ref, rk_ref, bd_ref,
                     m_ref, n_ref, q_ref, y0_ref, bonus_ref, z_ref, *, cpb):
    i = pl.program_id(0)
    jj = i % cpb
    ctx_chunks = TILE // CHUNK
    has_prev = (jj != 0) & (jj != ctx_chunks)
    has_next = (jj != ctx_chunks - 1) & (jj != cpb - 1)
    c = CHUNK
    zc = cur_ref[:, 0:RW_SHIFT_W]
    e = SUBLANES
    z_ref[0:e, :] = jnp.where(has_prev, prev_ref[:, 0:RW_SHIFT_W], 0.0)
    z_ref[e:e + c, :] = zc
    z_ref[e + c:2 * e + c, :] = jnp.where(has_next, next_ref[:, 0:RW_SHIFT_W], 0.0)
    bd = bd_ref[...]
    t_io = lax.broadcasted_iota(jnp.int32, (c, c), 0)
    s_io = lax.broadcasted_iota(jnp.int32, (c, c), 1)
    eye = jnp.where(t_io == s_io, 1.0, 0.0)
    lane_head = _head_of_lane((c, BW), RW_DH)
    row = lax.broadcasted_iota(jnp.int32, (BW, BW), 0)
    colm = lax.broadcasted_iota(jnp.int32, (BW, BW), 1)
    same_head = (row // RW_DH) == (colm // RW_DH)

    def stack(x):
        return jnp.concatenate(
            [jnp.where(lane_head == h, x, 0.0) for h in range(RW_HEADS)], axis=0)

    cat = lambda xs: jnp.concatenate(xs, axis=1)

    prep = []
    for d in range(2):
        zsh = z_ref[e - 1:e - 1 + c, :] if d == 0 else z_ref[e + 1:e + 1 + c, :]
        zs = zc + (zsh - zc) * mu_ref[d]
        r = zs[:, 0:BW]
        k = zs[:, BW:2 * BW]
        v = zs[:, 2 * BW:3 * BW]
        wa = zs[:, 3 * BW:RW_SHIFT_W]
        lw = -RW_DECAY_SCALE * _sigmoid(w0_ref[d] + _dot(jnp.tanh(wa), wup_ref[d]))
        a = _sigmoid(a0_ref[d] + _dot(wa, aup_ref[d]))
        kappa = k * kk_ref[d]
        kh = kappa * lax.rsqrt(jnp.maximum(_dot_sel(kappa * kappa, bd), 1e-12))
        kt = k * (1.0 + (a - 1.0) * ka_ref[d])
        akh = a * kh
        bonus_ref[d] = _dot_sel(r * kt * rk_ref[...], bd) * v

        ahead = (t_io - s_io) if d == 0 else (s_io - t_io)
        earlier = ahead > 0
        upto = ahead >= 0
        tri = jnp.where(upto, 1.0, 0.0).astype(BF16)
        cl = _sel_dot(tri, lw)
        tot = jnp.sum(lw, axis=0, keepdims=True)
        rho = 0.5 * tot
        cle = cl - lw
        a_true = -kh * jnp.exp(cle)
        r_true = r * jnp.exp(cl)
        a_c = -kh * jnp.exp(cle - rho)
        r_c = r * jnp.exp(cl - rho)
        b_c = akh * jnp.exp(rho - cl)
        k_c = kt * jnp.exp(rho - cl)
        b_end = akh * jnp.exp(tot - cl)
        k_end = kt * jnp.exp(tot - cl)

        pair = _dot(jnp.concatenate([stack(a_c), stack(r_c)], axis=0),
                    jnp.concatenate([b_c, k_c], axis=0), _NT)
        l_ab, l_ak, a_rb, a_rk = [], [], [], []
        for h in range(RW_HEADS):
            blk_a = pair[h * c:(h + 1) * c]
            blk_r = pair[(RW_HEADS + h) * c:(RW_HEADS + h + 1) * c]
            l_ab.append(jnp.where(earlier, blk_a[:, 0:c], 0.0))
            l_ak.append(jnp.where(earlier, blk_a[:, c:2 * c], 0.0))
            a_rb.append(jnp.where(upto, blk_r[:, 0:c], 0.0))
            a_rk.append(jnp.where(upto, blk_r[:, c:2 * c], 0.0))
        prep.append((l_ab, cat(l_ak), cat(a_rb), cat(a_rk), v, a_true, r_true, b_end, k_end, tot))

    t_all = _inv_unit_lower(prep[0][0] + prep[1][0], eye)

    for d in range(2):
        _, lak, arb, ark, v, a_true, r_true, b_end, k_end, tot = prep[d]
        t_inv = cat(t_all[d * RW_HEADS:(d + 1) * RW_HEADS])
        lv = _dot(lak, stack(v))
        w12 = _dot(t_inv, jnp.concatenate([stack(a_true), stack(lv)], axis=1))
        w1 = w12[:, 0:BW]
        w2 = w12[:, BW:2 * BW]
        q_ref[d, 0] = (r_true + _dot(arb, stack(w1))).astype(BF16)
        y0_ref[d, 0] = _dot(arb, stack(w2)) + _dot(ark, stack(v))
        b_end_t = b_end.T
        decay = jnp.where(row == colm, jnp.broadcast_to(jnp.exp(tot), (BW, BW)), 0.0)
        m_ref[d, 0] = (jnp.where(same_head, _dot(b_end_t, w1), 0.0) + decay).astype(BF16)
        n_ref[d, 0] = jnp.where(same_head, _dot(b_end_t, w2) + _dot(k_end.T, v), 0.0)


def _rw_chunk_call(u, lp, bd, tpb):
    n = u.shape[0]
    cpb = tpb * (TILE // CHUNK)
    nch = n // CHUNK
    edge = SUBLANES
    per = CHUNK // edge
    n8 = n // edge
    wrw = RW_SHIFT_W + RW_G_RANK
    col = C_RW // wrw
    pvec = lambda w: pl.BlockSpec((2, 1, w), lambda i: (0, 0, 0))
    pad = jnp.zeros((2, RW_DECAY_RANK, BW), F32)
    wup = jnp.concatenate([lp['rw_w_up'], pad], axis=1).astype(BF16)
    aup = jnp.concatenate([pad, lp['rw_a_up']], axis=1).astype(BF16)
    mat = lambda rows: pl.BlockSpec((2, 1, rows, BW), lambda i: (0, i, 0, 0))
    return pl.pallas_call(
        functools.partial(_rw_chunk_kernel, cpb=cpb),
        grid=(nch,),
        in_specs=[pl.BlockSpec((edge, wrw), lambda i: (jnp.maximum(i * per - 1, 0), col)),
                  pl.BlockSpec((CHUNK, wrw), lambda i: (i, col)),
                  pl.BlockSpec((edge, wrw), lambda i: (jnp.minimum((i + 1) * per, n8 - 1), col)),
                  pvec(RW_SHIFT_W), pvec(BW), pvec(BW), pvec(BW), pvec(BW),
                  pl.BlockSpec((2, 2 * RW_DECAY_RANK, BW), lambda i: (0, 0, 0)),
                  pl.BlockSpec((2, 2 * RW_A_RANK, BW), lambda i: (0, 0, 0)),
                  pl.BlockSpec((1, BW), lambda i: (0, 0)),
                  pl.BlockSpec((BW, BW), lambda i: (0, 0))],
        out_specs=[mat(BW), mat(BW), mat(CHUNK), mat(CHUNK),
                   pl.BlockSpec((2, CHUNK, BW), lambda i: (0, i, 0))],
        out_shape=[jax.ShapeDtypeStruct((2, nch, BW, BW), BF16),
                   jax.ShapeDtypeStruct((2, nch, BW, BW), F32),
                   jax.ShapeDtypeStruct((2, nch, CHUNK, BW), BF16),
                   jax.ShapeDtypeStruct((2, nch, CHUNK, BW), F32),
                   jax.ShapeDtypeStruct((2, n, BW), F32)],
        scratch_shapes=[pltpu.VMEM((CHUNK + 2 * edge, RW_SHIFT_W), F32)],
        compiler_params=_cparams(("arbitrary",)),
        name="rwkv_chunk",
    )(u, u, u, lp['rw_mu'].reshape(2, 1, RW_SHIFT_W), lp['rw_w0'].reshape(2, 1, BW),
      lp['rw_a0'].reshape(2, 1, BW), lp['rw_kk'].reshape(2, 1, BW), lp['rw_ka'].reshape(2, 1, BW),
      wup, aup, lp['rw_rk'].reshape(1, BW), bd)


def _rw_scan_kernel(*refs, n_batch):
    m_refs, n_refs, q_refs, y0_refs = refs[0:2], refs[2:4], refs[4:6], refs[6:8]
    y_refs, x_ref = refs[8:10], refs[10]

    @pl.when(pl.program_id(0) == 0)
    def _():
        x_ref[...] = jnp.zeros(x_ref.shape, F32)

    for d in range(2):
        for b in range(n_batch):
            x = x_ref[d, b]
            y_refs[d][b, 0] = _dot(q_refs[d][0, b, 0], x) + y0_refs[d][0, b, 0]
            x_ref[d, b] = _dot(m_refs[d][0, b, 0], x) + n_refs[d][0, b, 0]


def _rw_scan_call(m, nn, q, y0, n_batch, tpb):
    cpb = tpb * (TILE // CHUNK)
    ctx_chunks = TILE // CHUNK

    def chunk(d, i):
        rev = jnp.where(i < ctx_chunks, ctx_chunks - 1 - i, cpb + ctx_chunks - 1 - i)
        return i if d == 0 else rev

    def mat(rows, d):
        return pl.BlockSpec((1, n_batch, 1, rows, BW), lambda i: (d, 0, chunk(d, i), 0, 0))

    ins, specs = [], []
    for arr, rows in ((m, BW), (nn, BW), (q, CHUNK), (y0, CHUNK)):
        arr = arr.reshape(2, n_batch, cpb, rows, BW)
        for d in range(2):
            ins.append(arr)
            specs.append(mat(rows, d))
    yshape = jax.ShapeDtypeStruct((n_batch, cpb, CHUNK, BW), F32)
    ys = pl.pallas_call(
        functools.partial(_rw_scan_kernel, n_batch=n_batch),
        grid=(cpb,),
        in_specs=specs,
        out_specs=[pl.BlockSpec((n_batch, 1, CHUNK, BW), lambda i, d=d: (0, chunk(d, i), 0, 0))
                   for d in range(2)],
        out_shape=[yshape, yshape],
        scratch_shapes=[pltpu.VMEM((2, n_batch, BW, BW), F32)],
        compiler_params=_cparams(("arbitrary",)),
        name="rwkv_scan",
    )(*ins)
    return [y.reshape(n_batch * cpb * CHUNK, BW) for y in ys]


def _rw_readout(yf_ref, yb_ref, bonus_ref, gd_ref, gup_ref, gn_ref, bd_ref):
    bd = bd_ref[...]
    y = yf_ref[...] + yb_ref[...]
    mean = _dot_sel(y, bd) * (1.0 / RW_DH)
    yc = y - mean
    var = _dot_sel(yc * yc, bd) * (1.0 / RW_DH)
    yn = yc * lax.rsqrt(var + RW_GN_EPS) * gn_ref[0:1, :] + gn_ref[1:2, :]
    yn = yn + (bonus_ref[0] + bonus_ref[1])
    return yn * _dot(_sigmoid(gd_ref[...]), gup_ref[...])


MERGE_TILES = 2


def _merge_kernel(oa_ref, ob_ref, oc_ref, yf_ref, yb_ref, bonus_ref, gd_ref, gup_ref, gn_ref, bd_ref,
                  gl_ref, wb_ref, wo_ref, x_ref, g_ref, g2_ref, rw_ref, rb_ref, *refs):
    mods, (o_ref, tok_ref, gate_ref) = refs[:3 * MERGE_TILES], refs[3 * MERGE_TILES:]
    d = x_ref.shape[1]
    od = _rw_readout(yf_ref, yb_ref, bonus_ref, gd_ref, gup_ref, gn_ref, bd_ref)
    m = None
    for i, o in enumerate((oa_ref[...], ob_ref[...], oc_ref[...], od)):
        t = _sigmoid(gl_ref[:, i * d:(i + 1) * d]) * _dot(o, wb_ref[i])
        m = t if m is None else m + t
    y = _dot(m, wo_ref[...])
    y = y * lax.rsqrt(jnp.mean(y * y, axis=-1, keepdims=True) + EPS) * g_ref[...]
    toks = []
    for t in range(MERGE_TILES):
        rows = slice(t * TILE, (t + 1) * TILE)
        gate, shift, scale = (r[0] for r in mods[3 * t:3 * t + 3])
        x = x_ref[rows, :] + gate * y[rows]
        o_ref[rows, :] = x
        h = x * lax.rsqrt(jnp.mean(x * x, axis=-1, keepdims=True) + EPS) * g2_ref[...]
        tok = h * (1.0 + scale) + shift
        tok_ref[rows, :] = tok.astype(BF16)
        toks.append(tok)
    gate_ref[...] = _route(jnp.concatenate(toks, axis=0), rw_ref[...], rb_ref[...])


def _merge_call(outs, ys, bonus, gup, gn, bd, u, wb, wo, x, g, mod, g2, shift, scale, rw_t, rb,
                tpb, n_batch):
    n, d = x.shape
    tm = MERGE_TILES * TILE

    def kind(t):
        def index(i):
            tile = i * MERGE_TILES + t
            return (jnp.where(tile % tpb == 0, n_batch, tile // tpb), 0, 0)
        return pl.BlockSpec((1, 1, d), index)

    br = pl.BlockSpec((tm, BW), lambda i: (i, 0))
    gcol = (C_RW + RW_SHIFT_W) // RW_G_RANK
    readout = [br, br, pl.BlockSpec((2, tm, BW), lambda i: (0, i, 0)),
               pl.BlockSpec((tm, RW_G_RANK), lambda i: (i, gcol)),
               pl.BlockSpec((RW_G_RANK, BW), lambda i: (0, 0)),
               pl.BlockSpec((2, BW), lambda i: (0, 0)),
               pl.BlockSpec((BW, BW), lambda i: (0, 0))]
    vec = pl.BlockSpec((1, d), lambda i: (0, 0))
    row = pl.BlockSpec((tm, d), lambda i: (i, 0))
    wg = N_BRANCH * d
    mod_specs, mod_args = [], []
    for t in range(MERGE_TILES):
        mod_specs += [kind(t)] * 3
        mod_args += [mod, shift, scale]
    return pl.pallas_call(
        _merge_kernel,
        grid=(n // tm,),
        in_specs=[br, br, br, *readout,
                  pl.BlockSpec((tm, wg), lambda i: (i, C_GATE // wg)),
                  pl.BlockSpec((N_BRANCH, BW, d), lambda i: (0, 0, 0)),
                  pl.BlockSpec((d, d), lambda i: (0, 0)),
                  row, vec, vec,
                  pl.BlockSpec((N_EXPERTS, d), lambda i: (0, 0)),
                  pl.BlockSpec((N_EXPERTS, 1), lambda i: (0, 0)), *mod_specs],
        out_specs=[row, row, pl.BlockSpec((N_EXPERTS, tm), lambda i: (0, i))],
        out_shape=[jax.ShapeDtypeStruct((n, d), F32), jax.ShapeDtypeStruct((n, d), BF16),
                   jax.ShapeDtypeStruct((N_EXPERTS, n), F32)],
        compiler_params=_cparams(("arbitrary",), MOE_VMEM_LIMIT),
        name="merge",
    )(*outs, ys[0], ys[1], bonus, u, gup, gn, bd, u, wb, wo, x, g.reshape(1, d),
      g2.reshape(1, d), rw_t, rb.reshape(N_EXPERTS, 1), *mod_args)


def _route(tokens, w, b):
    tm = tokens.shape[0]
    gsz = N_EXPERTS // N_GROUPS
    logits = _dot3(w, tokens, _NT)
    sc = _sigmoid(logits).reshape(N_GROUPS, gsz, tm)
    bi = sc + b.reshape(N_GROUPS, gsz, 1)
    shape = (N_GROUPS, gsz, tm)
    g_io = lax.broadcasted_iota(jnp.int32, shape, 0)
    j_io = lax.broadcasted_iota(jnp.int32, shape, 1)
    e_io = g_io * gsz + j_io
    ninf = -jnp.inf
    m1 = jnp.max(bi, axis=1, keepdims=True)
    i1 = jnp.min(jnp.where(bi == m1, j_io, gsz), axis=1, keepdims=True)
    m2 = jnp.max(jnp.where(j_io == i1, ninf, bi), axis=1, keepdims=True)
    cur = jnp.broadcast_to(m1 + m2, shape)
    gsel = jnp.zeros(shape, F32)
    for _ in range(TOPK_GROUPS):
        mx = jnp.max(cur, axis=0, keepdims=True)
        ix = jnp.min(jnp.where(cur == mx, g_io, N_GROUPS), axis=0, keepdims=True)
        hit = g_io == ix
        gsel = jnp.where(hit, 1.0, gsel)
        cur = jnp.where(hit, ninf, cur)
    cur = jnp.where(gsel > 0.0, bi, ninf)
    esel = jnp.zeros(shape, F32)
    for _ in range(TOP_K):
        mx = jnp.max(jnp.max(cur, axis=0, keepdims=True), axis=1, keepdims=True)
        ix = jnp.min(jnp.min(jnp.where(cur == mx, e_io, N_EXPERTS), axis=0, keepdims=True),
                     axis=1, keepdims=True)
        hit = e_io == ix
        esel = jnp.where(hit, 1.0, esel)
        cur = jnp.where(hit, ninf, cur)
    wsel = sc * esel
    den = jnp.sum(jnp.sum(wsel, axis=0, keepdims=True), axis=1, keepdims=True)
    return (wsel / den * ROUTE_SCALE).reshape(N_EXPERTS, tm)


def _moe_kernel(x_ref, g_ref, wgu_ref, wd_ref, sgu_ref, sd_ref, o_ref):
    e = pl.program_id(1)

    def ffn(wgu, wd):
        hgu = _dot(x_ref[...], wgu)
        return _dot(_silu(hgu[:, :D_EXPERT]) * hgu[:, D_EXPERT:], wd)

    @pl.when(e == 0)
    def _():
        o_ref[...] = ffn(sgu_ref[...], sd_ref[...])

    gates = g_ref[...]
    lane = lax.broadcasted_iota(jnp.int32, gates.shape, 1)
    gcol = jnp.sum(jnp.where(lane == e, gates, 0.0), axis=1, keepdims=True)
    o_ref[...] += ffn(wgu_ref[0, 0], wd_ref[0, 0]) * gcol


def _moe_call(tok, gates, wgu, wd, layer, sgu, sd, tm):
    n, d = tok.shape
    return pl.pallas_call(
        _moe_kernel,
        grid=(n // tm, N_EXPERTS),
        in_specs=[pl.BlockSpec((tm, d), lambda i, e: (i, 0)),
                  pl.BlockSpec((tm, N_EXPERTS), lambda i, e: (i, 0)),
                  pl.BlockSpec((1, 1, d, 2 * D_EXPERT), lambda i, e: (layer, e, 0, 0)),
                  pl.BlockSpec((1, 1, D_EXPERT, d), lambda i, e: (layer, e, 0, 0)),
                  pl.BlockSpec((d, 2 * D_EXPERT), lambda i, e: (0, 0)),
                  pl.BlockSpec((D_EXPERT, d), lambda i, e: (0, 0))],
        out_specs=pl.BlockSpec((tm, d), lambda i, e: (i, 0)),
        out_shape=jax.ShapeDtypeStruct((n, d), F32),
        compiler_params=_cparams(("arbitrary", "arbitrary"), MOE_VMEM_LIMIT),
        name="moe_experts",
    )(tok, gates, wgu, wd, sgu, sd)


def _resid_kernel(x_ref, f_ref, g_ref, mod_ref, o_ref):
    f = f_ref[...]
    y = f * lax.rsqrt(jnp.mean(f * f, axis=-1, keepdims=True) + EPS) * g_ref[...]
    o_ref[...] = x_ref[...] + mod_ref[0] * y


def _resid_call(x, f, g, mod, tpb, n_batch, latent_only=False):
    n, d = x.shape
    vec = pl.BlockSpec((1, d), lambda *_: (0, 0))
    if latent_only:
        lpb = tpb - 1
        grid = (n_batch, lpb)
        row = pl.BlockSpec((TILE, d), lambda b, i: (b * tpb + 1 + i, 0))
        out = pl.BlockSpec((TILE, d), lambda b, i: (b * lpb + i, 0))
        kind = pl.BlockSpec((1, 1, d), lambda b, i: (b, 0, 0))
        n_out = n_batch * lpb * TILE
    else:
        grid = (n // TILE,)
        row = out = pl.BlockSpec((TILE, d), lambda i: (i, 0))
        kind = pl.BlockSpec((1, 1, d), lambda i: (jnp.where(i % tpb == 0, n_batch, i // tpb), 0, 0))
        n_out = n
    return pl.pallas_call(
        _resid_kernel,
        grid=grid,
        in_specs=[row, row, vec, kind],
        out_specs=out,
        out_shape=jax.ShapeDtypeStruct((n_out, d), F32),
        compiler_params=_cparams(("arbitrary",) * len(grid)),
        name="moe_residual",
    )(x, f, g.reshape(1, d), mod)


def _proj_weights(w):
    w = w.astype(BF16)
    k = w.shape[0]
    da, cv, sw = 0, 3 * BW, 5 * BW
    rw = sw + (SW_HEADS + 2 * SW_KV) * SW_DH
    gate = rw + RW_SHIFT_W + RW_G_RANK

    def per_q_head(x):
        return jnp.repeat(x.reshape(k, SW_KV, SW_DH), SW_HEADS // SW_KV, axis=1).reshape(k, -1)

    da_q, da_k, da_v = (w[:, da + i * BW:da + (i + 1) * BW] for i in range(3))
    sw_q = w[:, sw:sw + BW]
    sw_k = per_q_head(w[:, sw + BW:sw + BW + SW_KV * SW_DH])
    sw_v = per_q_head(w[:, sw + BW + SW_KV * SW_DH:rw])
    out = jnp.concatenate([
        w[:, gate:], da_q, da_k, sw_q, sw_k, da_v, sw_v, w[:, cv:sw], w[:, rw:gate]], axis=1)
    assert out.shape[1] == C_TOTAL
    return out


def _rope_tables(seq):
    rows = seq // GRID_W
    row = jnp.repeat(jnp.arange(rows, dtype=F32), GRID_W)
    colp = jnp.tile(jnp.arange(GRID_W, dtype=F32), rows)

    def tables(dim):
        q = dim // 4
        freqs = ROPE_BASE ** (-jnp.arange(q, dtype=F32) / q)
        ar, ac = row[:, None] * freqs, colp[:, None] * freqs
        cos = jnp.concatenate([jnp.cos(ar), jnp.cos(ar), jnp.cos(ac), jnp.cos(ac)], axis=1)
        sin = jnp.concatenate([-jnp.sin(ar), jnp.sin(ar), -jnp.sin(ac), jnp.sin(ac)], axis=1)
        return jnp.tile(cos, (1, LANES // dim)), jnp.tile(sin, (1, LANES // dim))

    c32, s32 = tables(DA_HALF)
    c64, s64 = tables(SW_DH)
    cos = jnp.concatenate([c32, c64], axis=1)
    sin = jnp.concatenate([s32, s64], axis=1)
    w = cos.shape[1]
    cos = jnp.concatenate([jnp.ones((TILE, w), F32), cos], axis=0)
    sin = jnp.concatenate([jnp.zeros((TILE, w), F32), sin], axis=0)
    return cos, sin


def _moe_tile(rows_b):
    best = TILE
    for t in range(TILE, MOE_MAX_TILE + 1, 16):
        if rows_b % t == 0:
            best = t
    return best


def kernel(x, c, ctx, c_ctx, ada_w, ada_b, norm_g, w_in, w_branch, w_out, da_lambda, da_subln,
           cv_w, cv_b, cv_ln, sw_sink, rw_mu, rw_w0, rw_w_up, rw_a0, rw_a_up, rw_kk, rw_ka,
           rw_g_up, rw_rk, rw_gn, router_w, router_b, ex_w_gu, ex_w_down, sh_w_gu, sh_w_down):
    n_batch, seq, d = x.shape
    ctx_len = ctx.shape[1]
    depth = w_in.shape[0]
    assert ctx_len == TILE and seq % TILE == 0 and seq % GRID_W == 0
    assert n_batch + 1 <= SUBLANES
    rows_b = ctx_len + seq
    tpb = rows_b // TILE
    n = n_batch * rows_b

    xs = jnp.concatenate([ctx, x], axis=1).reshape(n, d)
    cond = jnp.zeros((SUBLANES, d), F32).at[:n_batch].set(c).at[n_batch].set(c_ctx)
    mods = _ada_call(cond, ada_w, ada_b)[:, :n_batch + 1]
    cos_t, sin_t = _rope_tables(seq)
    hio = np.arange(BW) // RW_DH
    bd = jnp.asarray(hio[:, None] == hio[None, :], BF16)

    for l in range(depth):
        mod = [mods[l, :, i * d:(i + 1) * d].reshape(n_batch + 1, 1, d) for i in range(6)]
        ng = norm_g[l]
        lam_init = 0.8 - 0.6 * math.exp(-0.3 * l)
        lv = da_lambda[l]
        lam = (jnp.exp(jnp.sum(lv[0] * lv[1])) - jnp.exp(jnp.sum(lv[2] * lv[3])) + lam_init)
        lp = {'rw_mu': rw_mu[l], 'rw_w0': rw_w0[l], 'rw_w_up': rw_w_up[l], 'rw_a0': rw_a0[l],
              'rw_a_up': rw_a_up[l], 'rw_kk': rw_kk[l], 'rw_ka': rw_ka[l], 'rw_rk': rw_rk[l]}

        u = _proj_call(xs, ng[0], mod[0], mod[1], _proj_weights(w_in[l]), tpb, n_batch)
        qk, vt = _rope_call(u, cos_t, sin_t, n_batch, tpb)
        oa = _da_call(lam.reshape(1), qk, vt, jnp.tile(da_subln[l], DA_HEADS).reshape(1, BW),
                      n_batch, tpb, lam_init)
        ob = _conv_call(u, cv_w[l], cv_b[l], cv_ln[l], tpb)
        oc = _sw_call(sw_sink[l], qk, u, n_batch, tpb)
        cm, cn, cq, cy0, bonus = _rw_chunk_call(u, lp, bd, tpb)
        yscan = _rw_scan_call(cm, cn, cq, cy0, n_batch, tpb)
        xs, tok, gates_t = _merge_call(
            (oa, ob, oc), yscan, bonus, rw_g_up[l].astype(BF16), rw_gn[l], bd,
            u, w_branch[l].astype(BF16), w_out[l].astype(BF16), xs, ng[1],
            mod[2], ng[2], mod[3], mod[4], router_w[l].T, router_b[l], tpb, n_batch)
        f = _moe_call(tok, gates_t.T, ex_w_gu, ex_w_down, l, sh_w_gu[l].astype(BF16),
                      sh_w_down[l].astype(BF16), _moe_tile(rows_b))
        xs = _resid_call(xs, f, ng[3], mod[5], tpb, n_batch, latent_only=(l == depth - 1))

    return xs.reshape(n_batch, seq, d)
```

```python
import functools
import math

import numpy as np
import jax
import jax.numpy as jnp
from jax import lax
from jax.experimental import pallas as pl
from jax.experimental.pallas import tpu as pltpu

F32 = jnp.float32
BF16 = jnp.bfloat16
SUBLANES, LANES = 8, 128

GRID_W = 64
EPS = 1e-6
ROPE_BASE = 10000.0
N_BRANCH = 4
BW = 256
DA_HEADS = 4
DA_HALF = 32
SW_HEADS = 4
SW_KV = 2
SW_DH = 64
WINDOW = 128
CONV_W = 31
RW_HEADS = 4
RW_DH = 64
RW_DECAY_RANK = 64
RW_A_RANK = 64
RW_G_RANK = 128
RW_DECAY_SCALE = math.exp(-0.5)
RW_GN_EPS = 64e-5
RW_SHIFT_W = 3 * BW + RW_DECAY_RANK + RW_A_RANK
N_EXPERTS = 64
TOP_K = 6
N_GROUPS = 8
TOPK_GROUPS = 4
D_EXPERT = 256
ROUTE_SCALE = 2.5

TILE = 256
CHUNK = 128
DA_KT = 1408
MOE_MAX_TILE = 2112
MOE_EXPERTS_PER_STEP = 2
MOE_VMEM_LIMIT = 58 * 1024 * 1024
DA_VROWS = 80
NEG = -1e30

C_GATE = 0
C_ROPE = 4096
C_DAV = 5120
C_SWV = 5376
C_CV = 5632
C_RW = 6144
C_TOTAL = 7168

VMEM_LIMIT = 48 * 1024 * 1024

_NT = (((1,), (1,)), ((), ()))
_NN = (((1,), (0,)), ((), ()))


def _cparams(sem, vmem=VMEM_LIMIT):
    return pltpu.CompilerParams(dimension_semantics=sem, vmem_limit_bytes=vmem)


def _dot(a, b, dims=_NN):
    return lax.dot_general(a.astype(BF16), b.astype(BF16), dims, preferred_element_type=F32)


def _split2(x):
    hi = x.astype(BF16)
    lo = (x - hi.astype(F32)).astype(BF16)
    return hi, lo


def _dot3(a, b, dims=_NN):
    ah, al = _split2(a)
    bh, bl = _split2(b)
    dg = lambda x, y: lax.dot_general(x, y, dims, preferred_element_type=F32)
    return dg(ah, bh) + (dg(ah, bl) + dg(al, bh))


def _dot_sel(x, sel, dims=_NN):
    h0 = x.astype(BF16)
    r1 = x - h0.astype(F32)
    h1 = r1.astype(BF16)
    h2 = (r1 - h1.astype(F32)).astype(BF16)
    dg = lambda y: lax.dot_general(y, sel, dims, preferred_element_type=F32)
    return dg(h0) + (dg(h1) + dg(h2))


def _sel_dot(sel, x):
    h0 = x.astype(BF16)
    r1 = x - h0.astype(F32)
    h1 = r1.astype(BF16)
    h2 = (r1 - h1.astype(F32)).astype(BF16)
    dg = lambda y: lax.dot_general(sel, y, _NN, preferred_element_type=F32)
    return dg(h0) + (dg(h1) + dg(h2))


def _sigmoid(x):
    return jax.nn.sigmoid(x)


def _silu(x):
    return x * jax.nn.sigmoid(x)


def _head_of_lane(shape, width):
    return lax.broadcasted_iota(jnp.int32, shape, len(shape) - 1) // width


def _ada_kernel(s_ref, w_ref, b_ref, o_ref):
    s = _silu(s_ref[...])
    o_ref[0] = _dot(s, w_ref[0]) + b_ref[0]


def _ada_call(cond, ada_w, ada_b):
    depth, d, cols = ada_w.shape
    tn = 1536
    return pl.pallas_call(
        _ada_kernel,
        grid=(depth, cols // tn),
        in_specs=[pl.BlockSpec((SUBLANES, d), lambda l, j: (0, 0)),
                  pl.BlockSpec((1, d, tn), lambda l, j: (l, 0, j)),
                  pl.BlockSpec((1, 1, tn), lambda l, j: (l, 0, j))],
        out_specs=pl.BlockSpec((1, SUBLANES, tn), lambda l, j: (l, 0, j)),
        out_shape=jax.ShapeDtypeStruct((depth, SUBLANES, cols), F32),
        compiler_params=_cparams(("arbitrary", "arbitrary")),
        name="ada_mod",
    )(cond, ada_w, ada_b.reshape(depth, 1, cols))


PROJ_TILES = 2


def _proj_kernel(x_ref, g_ref, *refs):
    mods, w_ref, o_ref = refs[:2 * PROJ_TILES], refs[2 * PROJ_TILES], refs[2 * PROJ_TILES + 1]
    x = x_ref[...]
    y = x * lax.rsqrt(jnp.mean(x * x, axis=-1, keepdims=True) + EPS) * g_ref[...]
    h = jnp.concatenate(
        [(y[t * TILE:(t + 1) * TILE] * (1.0 + mods[2 * t + 1][0]) + mods[2 * t][0]).astype(BF16)
         for t in range(PROJ_TILES)], axis=0)
    o_ref[...] = jnp.dot(h, w_ref[...], preferred_element_type=F32)


def _proj_call(x, g, shift, scale, w, tpb, n_batch):
    n, d = x.shape
    cols = w.shape[1]
    tm, tn = PROJ_TILES * TILE, cols // 2

    def kind(t):
        def index(j, i):
            tile = i * PROJ_TILES + t
            return (jnp.where(tile % tpb == 0, n_batch, tile // tpb), 0, 0)
        return pl.BlockSpec((1, 1, d), index)

    mod_specs, mod_args = [], []
    for t in range(PROJ_TILES):
        mod_specs += [kind(t), kind(t)]
        mod_args += [shift, scale]
    return pl.pallas_call(
        _proj_kernel,
        grid=(cols // tn, n // tm),
        in_specs=[pl.BlockSpec((tm, d), lambda j, i: (i, 0)),
                  pl.BlockSpec((1, d), lambda j, i: (0, 0)), *mod_specs,
                  pl.BlockSpec((d, tn), lambda j, i: (0, j))],
        out_specs=pl.BlockSpec((tm, tn), lambda j, i: (i, j)),
        out_shape=jax.ShapeDtypeStruct((n, cols), F32),
        compiler_params=_cparams(("arbitrary", "arbitrary")),
        name="in_proj",
    )(x, g.reshape(1, d), *mod_args, w)


def _rope_kernel(u_ref, c_ref, s_ref, v_ref, o_ref, vt_ref):
    lanes = LANES
    n_batch, rows, w = u_ref.shape
    lane = lax.broadcasted_iota(jnp.int32, (rows, lanes), 1)
    hd = 2 * DA_HALF
    qscale = (DA_HALF ** -0.5) * math.log2(math.e)
    for b in range(n_batch):
        for k in range(w // lanes):
            cols = slice(k * lanes, (k + 1) * lanes)
            is_da = k * lanes < 2 * BW
            q = (DA_HALF if is_da else SW_DH) // 4
            tab = slice(0, lanes) if is_da else slice(lanes, 2 * lanes)
            x = u_ref[b, :, cols]
            partner = jnp.where((lane // q) % 2 == 0, pltpu.roll(x, lanes - q, 1),
                                pltpu.roll(x, q, 1))
            y = x * c_ref[:, tab] + partner * s_ref[:, tab]
            if k * lanes < BW:
                y = y * qscale
            o_ref[b, :, cols] = y.astype(BF16)
        vt = v_ref[b].T.astype(BF16)
        for h in range(DA_HEADS):
            vt_ref[b, h * DA_VROWS:h * DA_VROWS + hd, :] = vt[h * hd:(h + 1) * hd, :]
            vt_ref[b, h * DA_VROWS + hd:(h + 1) * DA_VROWS, :] = jnp.ones((DA_VROWS - hd, rows),
                                                                          BF16)


def _rope_call(u, cos_t, sin_t, n_batch, tpb):
    n, wu = u.shape
    rows_b = tpb * TILE
    w = 4 * BW
    vrows = DA_HEADS * DA_VROWS
    tab = pl.BlockSpec((TILE, cos_t.shape[1]), lambda i: (i, 0))
    u3 = u.reshape(n_batch, rows_b, wu)
    qk, vt = pl.pallas_call(
        _rope_kernel,
        grid=(tpb,),
        in_specs=[pl.BlockSpec((n_batch, TILE, w), lambda i: (0, i, C_ROPE // w)), tab, tab,
                  pl.BlockSpec((n_batch, TILE, BW), lambda i: (0, i, C_DAV // BW))],
        out_specs=[pl.BlockSpec((n_batch, TILE, w), lambda i: (0, i, 0)),
                   pl.BlockSpec((n_batch, vrows, TILE), lambda i: (0, 0, i))],
        out_shape=[jax.ShapeDtypeStruct((n_batch, rows_b, w), BF16),
                   jax.ShapeDtypeStruct((n_batch, vrows, rows_b), BF16)],
        compiler_params=_cparams(("arbitrary",)),
        name="rope",
    )(u3, cos_t, sin_t, u3)
    return qk.reshape(n, w), vt.reshape(n_batch * vrows, rows_b)


def _da_kernel(lam_ref, q_ref, k_ref, vt_ref, g_ref, o_ref,
               qs_ref, m_ref, acc_ref, *, nkt, lam_init):
    i = pl.program_id(1)
    tq = q_ref.shape[0]
    hd = 2 * DA_HALF
    q = q_ref[...]
    qmap = _head_of_lane((tq, BW), DA_HALF)
    for g in range(2 * DA_HEADS):
        qs_ref[g] = jnp.where(qmap == g, q, jnp.zeros_like(q))
    m_ref[...] = jnp.full(m_ref.shape, NEG, F32)
    acc_ref[...] = jnp.zeros(acc_ref.shape, F32)

    def tile(off, size):
        kt = k_ref[pl.ds(off, size), :]
        groups = range(2 * DA_HEADS)
        ss = [lax.dot_general(kt, qs_ref[g], _NT, preferred_element_type=F32) for g in groups]
        for g in groups:
            s = ss[g]
            m_old = m_ref[g]
            m_new = jnp.maximum(m_old, jnp.max(s, axis=0, keepdims=True))
            alpha = jnp.exp2(m_old - m_new)[0:1, :]
            p = jnp.exp2(s - m_new[0:1, :]).astype(BF16)
            m_ref[g] = m_new
            h, mm = g // 2, g % 2
            rows = slice(h * DA_VROWS, (h + 1) * DA_VROWS)
            pv = jnp.dot(vt_ref[rows, pl.ds(off, size)], p, preferred_element_type=F32)
            acc_ref[mm, rows, :] = acc_ref[mm, rows, :] * alpha + pv

    @pl.when(i == 0)
    def _():
        tile(0, TILE)

    @pl.when(i > 0)
    def _():
        def body(j, carry):
            tile(pl.multiple_of(j * DA_KT, DA_KT), DA_KT)
            return carry

        lax.fori_loop(0, nkt, body, 0)

    lam = lam_ref[0]
    parts = []
    for h in range(DA_HEADS):
        rows = slice(h * DA_VROWS, h * DA_VROWS + hd)
        den = slice(h * DA_VROWS + hd, h * DA_VROWS + hd + 1)
        o_h = (acc_ref[0, rows, :] / acc_ref[0, den, :]
               - lam * (acc_ref[1, rows, :] / acc_ref[1, den, :]))
        ms = jnp.mean(o_h * o_h, axis=0, keepdims=True)
        parts.append(o_h * lax.rsqrt(ms + EPS))
    y = jnp.concatenate(parts, axis=0).T * g_ref[...]
    o_ref[...] = y * (1.0 - lam_init)


def _da_call(lam, qk, vt, subln, n_batch, tpb, lam_init):
    n = qk.shape[0]
    rows_b = tpb * TILE
    assert rows_b % DA_KT == 0
    vrows = DA_HEADS * DA_VROWS
    kern = functools.partial(_da_kernel, nkt=rows_b // DA_KT, lam_init=lam_init)
    return pl.pallas_call(
        kern,
        grid=(n_batch, tpb),
        in_specs=[pl.BlockSpec(memory_space=pltpu.SMEM),
                  pl.BlockSpec((TILE, BW), lambda b, i: (b * tpb + i, 0)),
                  pl.BlockSpec((rows_b, BW), lambda b, i: (b, 1)),
                  pl.BlockSpec((vrows, rows_b), lambda b, i: (b, 0)),
                  pl.BlockSpec((1, BW), lambda b, i: (0, 0))],
        out_specs=pl.BlockSpec((TILE, BW), lambda b, i: (b * tpb + i, 0)),
        out_shape=jax.ShapeDtypeStruct((n, BW), F32),
        scratch_shapes=[pltpu.VMEM((2 * DA_HEADS, TILE, BW), BF16),
                        pltpu.VMEM((2 * DA_HEADS, SUBLANES, TILE), F32),
                        pltpu.VMEM((2, vrows, TILE), F32)],
        compiler_params=_cparams(("arbitrary", "arbitrary")),
        name="diff_attn",
    )(lam, qk, qk, vt, subln)


def _sw_kernel(sink_ref, bias_ref, q_ref, kp_ref, ko_ref, kn_ref, kc_ref,
               vp_ref, vo_ref, vn_ref, vc_ref, o_ref):
    tq = q_ref.shape[0]
    q = q_ref[...]
    kk = jnp.concatenate([kp_ref[...], ko_ref[...], kn_ref[...], kc_ref[...]], axis=0)
    vv = jnp.concatenate([vp_ref[...], vo_ref[...], vn_ref[...], vc_ref[...]], axis=0).astype(BF16)
    nk = kk.shape[0]
    bias = bias_ref[0]
    qhead = _head_of_lane((tq, BW), SW_DH)
    vhead = _head_of_lane((nk, BW), SW_DH)
    ps, vs = [], []
    for h in range(SW_HEADS):
        qm = jnp.where(qhead == h, q, jnp.zeros_like(q))
        s = lax.dot_general(qm, kk, _NT, preferred_element_type=F32) * (SW_DH ** -0.5) + bias
        sk = sink_ref[h]
        m = jnp.maximum(jnp.max(s, axis=1, keepdims=True), sk)
        p = jnp.exp(s - m)
        den = jnp.sum(p, axis=1, keepdims=True) + jnp.exp(sk - m)
        ps.append((p / den).astype(BF16))
        vs.append(jnp.where(vhead == h, vv, jnp.zeros_like(vv)))
    o_ref[...] = jnp.dot(jnp.concatenate(ps, axis=1), jnp.concatenate(vs, axis=0),
                         preferred_element_type=F32)


def _sw_bias():
    r = np.arange(TILE)[:, None]
    prev_ok = np.arange(CHUNK)[None, :] - CHUNK - r >= -WINDOW
    own_ok = np.abs(np.arange(TILE)[None, :] - r) <= WINDOW
    next_ok = np.arange(CHUNK)[None, :] + TILE - r <= WINDOW
    hide = lambda m: np.zeros_like(m)
    ctx = np.ones((TILE, TILE), bool)
    kinds = [np.concatenate([hide(prev_ok), hide(own_ok), hide(next_ok), ctx], axis=1)]
    for no_next in (False, True):
        for no_prev in (False, True):
            kinds.append(np.concatenate([hide(prev_ok) if no_prev else prev_ok, own_ok,
                                         hide(next_ok) if no_next else next_ok, ctx], axis=1))
    return jnp.asarray(np.where(np.stack(kinds), 0.0, NEG), F32)


def _sw_call(sink, qk, u, n_batch, tpb):
    n = qk.shape[0]
    per = TILE // CHUNK
    cpb = tpb * per
    bias = _sw_bias()

    def kind(b, j):
        lat = 1 + (j == 1).astype(jnp.int32) + 2 * (j == tpb - 1).astype(jnp.int32)
        return (jnp.where(j == 0, 0, lat), 0, 0)

    own = lambda b, j: b * tpb + j
    prv = lambda b, j: b * cpb + jnp.maximum(j * per - 1, 0)
    nxt = lambda b, j: b * cpb + jnp.minimum((j + 1) * per, cpb - 1)
    kcol, vcol = 3, C_SWV // BW
    edge = lambda f, col: pl.BlockSpec((CHUNK, BW), lambda b, j: (f(b, j), col))
    tile = lambda f, col: pl.BlockSpec((TILE, BW), lambda b, j: (f(b, j), col))
    ctx = lambda b, j: b * tpb
    return pl.pallas_call(
        _sw_kernel,
        grid=(n_batch, tpb),
        in_specs=[pl.BlockSpec(memory_space=pltpu.SMEM),
                  pl.BlockSpec((1,) + bias.shape[1:], kind),
                  tile(own, 2), edge(prv, kcol), tile(own, kcol), edge(nxt, kcol), tile(ctx, kcol),
                  edge(prv, vcol), tile(own, vcol), edge(nxt, vcol), tile(ctx, vcol)],
        out_specs=pl.BlockSpec((TILE, BW), lambda b, j: (own(b, j), 0)),
        out_shape=jax.ShapeDtypeStruct((n, BW), F32),
        compiler_params=_cparams(("arbitrary", "arbitrary")),
        name="window_attn",
    )(sink, bias, qk, qk, qk, qk, qk, u, u, u, u)


def _conv_kernel(prev_ref, cur_ref, next_ref, w_ref, b_ref, ln_ref, o_ref, z_ref, zs_ref, *, tpb):
    i = pl.program_id(0)
    pos = i % tpb
    has_prev = pos >= 2
    has_next = (pos >= 1) & (pos < tpb - 1)
    halo = prev_ref.shape[0]

    def glu(x):
        return x[:, :BW] * _sigmoid(x[:, BW:])

    zp = glu(prev_ref[...])
    zn = glu(next_ref[...])
    z_ref[0:halo, :] = jnp.where(has_prev, zp, 0.0)
    z_ref[halo:halo + TILE, :] = glu(cur_ref[...])
    z_ref[halo + TILE:2 * halo + TILE, :] = jnp.where(has_next, zn, 0.0)
    acc = jnp.zeros((TILE, BW), F32) + b_ref[...]
    pad = CONV_W // 2
    span = TILE + 2 * halo - SUBLANES
    for r in range(SUBLANES):
        zs_ref[r] = z_ref[r:r + span, :]
    for t in range(CONV_W):
        off = halo - pad + t
        shift, base = off % SUBLANES, off - off % SUBLANES
        acc = acc + zs_ref[shift, base:base + TILE, :] * w_ref[t:t + 1, :]
    mu = jnp.mean(acc, axis=-1, keepdims=True)
    xc = acc - mu
    y = xc * lax.rsqrt(jnp.mean(xc * xc, axis=-1, keepdims=True) + EPS)
    y = y * ln_ref[0:1, :] + ln_ref[1:2, :]
    o_ref[...] = _silu(y)


def _conv_call(u, w, b, ln, tpb):
    n = u.shape[0]
    halo = 16
    per = TILE // halo
    nh = n // halo
    wcv = 2 * BW
    col = C_CV // wcv
    return pl.pallas_call(
        functools.partial(_conv_kernel, tpb=tpb),
        grid=(n // TILE,),
        in_specs=[pl.BlockSpec((halo, wcv), lambda i: (jnp.maximum(i * per - 1, 0), col)),
                  pl.BlockSpec((TILE, wcv), lambda i: (i, col)),
                  pl.BlockSpec((halo, wcv), lambda i: (jnp.minimum((i + 1) * per, nh - 1), col)),
                  pl.BlockSpec((CONV_W, BW), lambda i: (0, 0)),
                  pl.BlockSpec((1, BW), lambda i: (0, 0)),
                  pl.BlockSpec((2, BW), lambda i: (0, 0))],
        out_specs=pl.BlockSpec((TILE, BW), lambda i: (i, 0)),
        out_shape=jax.ShapeDtypeStruct((n, BW), F32),
        scratch_shapes=[pltpu.VMEM((TILE + 2 * halo, BW), F32),
                        pltpu.VMEM((SUBLANES, TILE + 2 * halo - SUBLANES, BW), F32)],
        compiler_params=_cparams(("arbitrary",)),
        name="conformer_conv",
    )(u, u, u, w, b.reshape(1, BW), ln)


def _inv_unit_lower(ls, eye):
    ts = [eye + l for l in ls]
    lps = list(ls)
    step = 1
    while step < ls[0].shape[0] // 2:
        lps = [_dot(lp, lp) for lp in lps]
        ts = [t + _dot(t, lp) for t, lp in zip(ts, lps)]
        step *= 2
    return ts


def _rw_chunk_kernel(prev_ref, cur_ref, next_ref, mu_ref, w0_ref, a0_ref, kk_ref, ka_ref,
                     wup_ref, aup_ref, rk_ref, bd_ref,
                     m_ref, n_ref, q_ref, y0_ref, bonus_ref, z_ref, *, cpb):
    i = pl.program_id(0)
    jj = i % cpb
    ctx_chunks = TILE // CHUNK
    has_prev = (jj != 0) & (jj != ctx_chunks)
    has_next = (jj != ctx_chunks - 1) & (jj != cpb - 1)
    c = CHUNK
    zc = cur_ref[:, 0:RW_SHIFT_W]
    e = SUBLANES
    z_ref[0:e, :] = jnp.where(has_prev, prev_ref[:, 0:RW_SHIFT_W], 0.0)
    z_ref[e:e + c, :] = zc
    z_ref[e + c:2 * e + c, :] = jnp.where(has_next, next_ref[:, 0:RW_SHIFT_W], 0.0)
    bd = bd_ref[...]
    t_io = lax.broadcasted_iota(jnp.int32, (c, c), 0)
    s_io = lax.broadcasted_iota(jnp.int32, (c, c), 1)
    eye = jnp.where(t_io == s_io, 1.0, 0.0)
    lane_head = _head_of_lane((c, BW), RW_DH)
    row = lax.broadcasted_iota(jnp.int32, (BW, BW), 0)
    colm = lax.broadcasted_iota(jnp.int32, (BW, BW), 1)
    same_head = (row // RW_DH) == (colm // RW_DH)

    def stack(x):
        return jnp.concatenate(
            [jnp.where(lane_head == h, x, 0.0) for h in range(RW_HEADS)], axis=0)

    cat = lambda xs: jnp.concatenate(xs, axis=1)

    prep = []
    for d in range(2):
        zsh = z_ref[e - 1:e - 1 + c, :] if d == 0 else z_ref[e + 1:e + 1 + c, :]
        zs = zc + (zsh - zc) * mu_ref[d]
        r = zs[:, 0:BW]
        k = zs[:, BW:2 * BW]
        v = zs[:, 2 * BW:3 * BW]
        wa = zs[:, 3 * BW:RW_SHIFT_W]
        lw = -RW_DECAY_SCALE * _sigmoid(w0_ref[d] + _dot(jnp.tanh(wa), wup_ref[d]))
        a = _sigmoid(a0_ref[d] + _dot(wa, aup_ref[d]))
        kappa = k * kk_ref[d]
        kh = kappa * lax.rsqrt(jnp.maximum(_dot_sel(kappa * kappa, bd), 1e-12))
        kt = k * (1.0 + (a - 1.0) * ka_ref[d])
        akh = a * kh
        bonus_ref[d] = _dot_sel(r * kt * rk_ref[...], bd) * v

        ahead = (t_io - s_io) if d == 0 else (s_io - t_io)
        earlier = ahead > 0
        upto = ahead >= 0
        tri = jnp.where(upto, 1.0, 0.0).astype(BF16)
        cl = _sel_dot(tri, lw)
        tot = jnp.sum(lw, axis=0, keepdims=True)
        rho = 0.5 * tot
        cle = cl - lw
        a_true = -kh * jnp.exp(cle)
        r_true = r * jnp.exp(cl)
        a_c = -kh * jnp.exp(cle - rho)
        r_c = r * jnp.exp(cl - rho)
        b_c = akh * jnp.exp(rho - cl)
        k_c = kt * jnp.exp(rho - cl)
        b_end = akh * jnp.exp(tot - cl)
        k_end = kt * jnp.exp(tot - cl)

        pair = _dot(jnp.concatenate([stack(a_c), stack(r_c)], axis=0),
                    jnp.concatenate([b_c, k_c], axis=0), _NT)
        l_ab, l_ak, a_rb, a_rk = [], [], [], []
        for h in range(RW_HEADS):
            blk_a = pair[h * c:(h + 1) * c]
            blk_r = pair[(RW_HEADS + h) * c:(RW_HEADS + h + 1) * c]
            l_ab.append(jnp.where(earlier, blk_a[:, 0:c], 0.0))
            l_ak.append(jnp.where(earlier, blk_a[:, c:2 * c], 0.0))
            a_rb.append(jnp.where(upto, blk_r[:, 0:c], 0.0))
            a_rk.append(jnp.where(upto, blk_r[:, c:2 * c], 0.0))
        prep.append((l_ab, cat(l_ak), cat(a_rb), cat(a_rk), v, a_true, r_true, b_end, k_end, tot))

    t_all = _inv_unit_lower(prep[0][0] + prep[1][0], eye)

    for d in range(2):
        _, lak, arb, ark, v, a_true, r_true, b_end, k_end, tot = prep[d]
        t_inv = cat(t_all[d * RW_HEADS:(d + 1) * RW_HEADS])
        lv = _dot(lak, stack(v))
        w12 = _dot(t_inv, jnp.concatenate([stack(a_true), stack(lv)], axis=1))
        w1 = w12[:, 0:BW]
        w2 = w12[:, BW:2 * BW]
        q_ref[d, 0] = (r_true + _dot(arb, stack(w1))).astype(BF16)
        y0_ref[d, 0] = _dot(arb, stack(w2)) + _dot(ark, stack(v))
        b_end_t = b_end.T
        decay = jnp.where(row == colm, jnp.broadcast_to(jnp.exp(tot), (BW, BW)), 0.0)
        m_ref[d, 0] = (jnp.where(same_head, _dot(b_end_t, w1), 0.0) + decay).astype(BF16)
        n_ref[d, 0] = jnp.where(same_head, _dot(b_end_t, w2) + _dot(k_end.T, v), 0.0)


def _rw_chunk_call(u, lp, bd, tpb):
    n = u.shape[0]
    cpb = tpb * (TILE // CHUNK)
    nch = n // CHUNK
    edge = SUBLANES
    per = CHUNK // edge
    n8 = n // edge
    wrw = RW_SHIFT_W + RW_G_RANK
    col = C_RW // wrw
    pvec = lambda w: pl.BlockSpec((2, 1, w), lambda i: (0, 0, 0))
    pad = jnp.zeros((2, RW_DECAY_RANK, BW), F32)
    wup = jnp.concatenate([lp['rw_w_up'], pad], axis=1).astype(BF16)
    aup = jnp.concatenate([pad, lp['rw_a_up']], axis=1).astype(BF16)
    mat = lambda rows: pl.BlockSpec((2, 1, rows, BW), lambda i: (0, i, 0, 0))
    return pl.pallas_call(
        functools.partial(_rw_chunk_kernel, cpb=cpb),
        grid=(nch,),
        in_specs=[pl.BlockSpec((edge, wrw), lambda i: (jnp.maximum(i * per - 1, 0), col)),
                  pl.BlockSpec((CHUNK, wrw), lambda i: (i, col)),
                  pl.BlockSpec((edge, wrw), lambda i: (jnp.minimum((i + 1) * per, n8 - 1), col)),
                  pvec(RW_SHIFT_W), pvec(BW), pvec(BW), pvec(BW), pvec(BW),
                  pl.BlockSpec((2, 2 * RW_DECAY_RANK, BW), lambda i: (0, 0, 0)),
                  pl.BlockSpec((2, 2 * RW_A_RANK, BW), lambda i: (0, 0, 0)),
                  pl.BlockSpec((1, BW), lambda i: (0, 0)),
                  pl.BlockSpec((BW, BW), lambda i: (0, 0))],
        out_specs=[mat(BW), mat(BW), mat(CHUNK), mat(CHUNK),
                   pl.BlockSpec((2, CHUNK, BW), lambda i: (0, i, 0))],
        out_shape=[jax.ShapeDtypeStruct((2, nch, BW, BW), BF16),
                   jax.ShapeDtypeStruct((2, nch, BW, BW), F32),
                   jax.ShapeDtypeStruct((2, nch, CHUNK, BW), BF16),
                   jax.ShapeDtypeStruct((2, nch, CHUNK, BW), F32),
                   jax.ShapeDtypeStruct((2, n, BW), F32)],
        scratch_shapes=[pltpu.VMEM((CHUNK + 2 * edge, RW_SHIFT_W), F32)],
        compiler_params=_cparams(("arbitrary",)),
        name="rwkv_chunk",
    )(u, u, u, lp['rw_mu'].reshape(2, 1, RW_SHIFT_W), lp['rw_w0'].reshape(2, 1, BW),
      lp['rw_a0'].reshape(2, 1, BW), lp['rw_kk'].reshape(2, 1, BW), lp['rw_ka'].reshape(2, 1, BW),
      wup, aup, lp['rw_rk'].reshape(1, BW), bd)


def _rw_scan_kernel(*refs, n_batch):
    m_refs, n_refs, q_refs, y0_refs = refs[0:2], refs[2:4], refs[4:6], refs[6:8]
    y_refs, x_ref = refs[8:10], refs[10]

    @pl.when(pl.program_id(0) == 0)
    def _():
        x_ref[...] = jnp.zeros(x_ref.shape, F32)

    for d in range(2):
        for b in range(n_batch):
            x = x_ref[d, b]
            y_refs[d][b, 0] = _dot(q_refs[d][0, b, 0], x) + y0_refs[d][0, b, 0]
            x_ref[d, b] = _dot(m_refs[d][0, b, 0], x) + n_refs[d][0, b, 0]


def _rw_scan_call(m, nn, q, y0, n_batch, tpb):
    cpb = tpb * (TILE // CHUNK)
    ctx_chunks = TILE // CHUNK

    def chunk(d, i):
        rev = jnp.where(i < ctx_chunks, ctx_chunks - 1 - i, cpb + ctx_chunks - 1 - i)
        return i if d == 0 else rev

    def mat(rows, d):
        return pl.BlockSpec((1, n_batch, 1, rows, BW), lambda i: (d, 0, chunk(d, i), 0, 0))

    ins, specs = [], []
    for arr, rows in ((m, BW), (nn, BW), (q, CHUNK), (y0, CHUNK)):
        arr = arr.reshape(2, n_batch, cpb, rows, BW)
        for d in range(2):
            ins.append(arr)
            specs.append(mat(rows, d))
    yshape = jax.ShapeDtypeStruct((n_batch, cpb, CHUNK, BW), F32)
    ys = pl.pallas_call(
        functools.partial(_rw_scan_kernel, n_batch=n_batch),
        grid=(cpb,),
        in_specs=specs,
        out_specs=[pl.BlockSpec((n_batch, 1, CHUNK, BW), lambda i, d=d: (0, chunk(d, i), 0, 0))
                   for d in range(2)],
        out_shape=[yshape, yshape],
        scratch_shapes=[pltpu.VMEM((2, n_batch, BW, BW), F32)],
        compiler_params=_cparams(("arbitrary",)),
        name="rwkv_scan",
    )(*ins)
    return [y.reshape(n_batch * cpb * CHUNK, BW) for y in ys]


def _rw_readout(yf_ref, yb_ref, bonus_ref, gd_ref, gup_ref, gn_ref, bd_ref):
    bd = bd_ref[...]
    y = yf_ref[...] + yb_ref[...]
    mean = _dot_sel(y, bd) * (1.0 / RW_DH)
    yc = y - mean
    var = _dot_sel(yc * yc, bd) * (1.0 / RW_DH)
    yn = yc * lax.rsqrt(var + RW_GN_EPS) * gn_ref[0:1, :] + gn_ref[1:2, :]
    yn = yn + (bonus_ref[0] + bonus_ref[1])
    return yn * _dot(_sigmoid(gd_ref[...]), gup_ref[...])


MERGE_TILES = 2


def _merge_kernel(oa_ref, ob_ref, oc_ref, yf_ref, yb_ref, bonus_ref, gd_ref, gup_ref, gn_ref, bd_ref,
                  gl_ref, wb_ref, wo_ref, x_ref, g_ref, g2_ref, rw_ref, rb_ref, *refs):
    mods, (o_ref, tok_ref, gate_ref) = refs[:3 * MERGE_TILES], refs[3 * MERGE_TILES:]
    d = x_ref.shape[1]
    od = _rw_readout(yf_ref, yb_ref, bonus_ref, gd_ref, gup_ref, gn_ref, bd_ref)
    m = None
    for i, o in enumerate((oa_ref[...], ob_ref[...], oc_ref[...], od)):
        t = _sigmoid(gl_ref[:, i * d:(i + 1) * d]) * _dot(o, wb_ref[i])
        m = t if m is None else m + t
    y = _dot(m, wo_ref[...])
    y = y * lax.rsqrt(jnp.mean(y * y, axis=-1, keepdims=True) + EPS) * g_ref[...]
    toks = []
    for t in range(MERGE_TILES):
        rows = slice(t * TILE, (t + 1) * TILE)
        gate, shift, scale = (r[0] for r in mods[3 * t:3 * t + 3])
        x = x_ref[rows, :] + gate * y[rows]
        o_ref[rows, :] = x
        h = x * lax.rsqrt(jnp.mean(x * x, axis=-1, keepdims=True) + EPS) * g2_ref[...]
        tok = h * (1.0 + scale) + shift
        tok_ref[rows, :] = tok.astype(BF16)
        toks.append(tok)
    gate_ref[...] = _route(jnp.concatenate(toks, axis=0), rw_ref[...], rb_ref[...])


def _merge_call(outs, ys, bonus, gup, gn, bd, u, wb, wo, x, g, mod, g2, shift, scale, rw_t, rb,
                tpb, n_batch):
    n, d = x.shape
    tm = MERGE_TILES * TILE

    def kind(t):
        def index(i):
            tile = i * MERGE_TILES + t
            return (jnp.where(tile % tpb == 0, n_batch, tile // tpb), 0, 0)
        return pl.BlockSpec((1, 1, d), index)

    br = pl.BlockSpec((tm, BW), lambda i: (i, 0))
    gcol = (C_RW + RW_SHIFT_W) // RW_G_RANK
    readout = [br, br, pl.BlockSpec((2, tm, BW), lambda i: (0, i, 0)),
               pl.BlockSpec((tm, RW_G_RANK), lambda i: (i, gcol)),
               pl.BlockSpec((RW_G_RANK, BW), lambda i: (0, 0)),
               pl.BlockSpec((2, BW), lambda i: (0, 0)),
               pl.BlockSpec((BW, BW), lambda i: (0, 0))]
    vec = pl.BlockSpec((1, d), lambda i: (0, 0))
    row = pl.BlockSpec((tm, d), lambda i: (i, 0))
    wg = N_BRANCH * d
    mod_specs, mod_args = [], []
    for t in range(MERGE_TILES):
        mod_specs += [kind(t)] * 3
        mod_args += [mod, shift, scale]
    return pl.pallas_call(
        _merge_kernel,
        grid=(n // tm,),
        in_specs=[br, br, br, *readout,
                  pl.BlockSpec((tm, wg), lambda i: (i, C_GATE // wg)),
                  pl.BlockSpec((N_BRANCH, BW, d), lambda i: (0, 0, 0)),
                  pl.BlockSpec((d, d), lambda i: (0, 0)),
                  row, vec, vec,
                  pl.BlockSpec((N_EXPERTS, d), lambda i: (0, 0)),
                  pl.BlockSpec((N_EXPERTS, 1), lambda i: (0, 0)), *mod_specs],
        out_specs=[row, row, pl.BlockSpec((N_EXPERTS, tm), lambda i: (0, i))],
        out_shape=[jax.ShapeDtypeStruct((n, d), F32), jax.ShapeDtypeStruct((n, d), BF16),
                   jax.ShapeDtypeStruct((N_EXPERTS, n), F32)],
        compiler_params=_cparams(("arbitrary",), MOE_VMEM_LIMIT),
        name="merge",
    )(*outs, ys[0], ys[1], bonus, u, gup, gn, bd, u, wb, wo, x, g.reshape(1, d),
      g2.reshape(1, d), rw_t, rb.reshape(N_EXPERTS, 1), *mod_args)


def _route(tokens, w, b):
    tm = tokens.shape[0]
    gsz = N_EXPERTS // N_GROUPS
    logits = _dot3(w, tokens, _NT)
    sc = _sigmoid(logits).reshape(N_GROUPS, gsz, tm)
    bi = sc + b.reshape(N_GROUPS, gsz, 1)
    shape = (N_GROUPS, gsz, tm)
    g_io = lax.broadcasted_iota(jnp.int32, shape, 0)
    j_io = lax.broadcasted_iota(jnp.int32, shape, 1)
    e_io = g_io * gsz + j_io
    ninf = -jnp.inf
    m1 = jnp.max(bi, axis=1, keepdims=True)
    i1 = jnp.min(jnp.where(bi == m1, j_io, gsz), axis=1, keepdims=True)
    m2 = jnp.max(jnp.where(j_io == i1, ninf, bi), axis=1, keepdims=True)
    cur = jnp.broadcast_to(m1 + m2, shape)
    gsel = jnp.zeros(shape, F32)
    for _ in range(TOPK_GROUPS):
        mx = jnp.max(cur, axis=0, keepdims=True)
        ix = jnp.min(jnp.where(cur == mx, g_io, N_GROUPS), axis=0, keepdims=True)
        hit = g_io == ix
        gsel = jnp.where(hit, 1.0, gsel)
        cur = jnp.where(hit, ninf, cur)
    cur = jnp.where(gsel > 0.0, bi, ninf)
    esel = jnp.zeros(shape, F32)
    for _ in range(TOP_K):
        mx = jnp.max(jnp.max(cur, axis=0, keepdims=True), axis=1, keepdims=True)
        ix = jnp.min(jnp.min(jnp.where(cur == mx, e_io, N_EXPERTS), axis=0, keepdims=True),
                     axis=1, keepdims=True)
        hit = e_io == ix
        esel = jnp.where(hit, 1.0, esel)
        cur = jnp.where(hit, ninf, cur)
    wsel = sc * esel
    den = jnp.sum(jnp.sum(wsel, axis=0, keepdims=True), axis=1, keepdims=True)
    return (wsel / den * ROUTE_SCALE).reshape(N_EXPERTS, tm)


def _moe_kernel(x_ref, g_ref, wgu_ref, wd_ref, sgu_ref, sd_ref, o_ref):
    e = pl.program_id(1)

    def ffn(wgu, wd):
        hgu = _dot(x_ref[...], wgu)
        return _dot(_silu(hgu[:, :D_EXPERT]) * hgu[:, D_EXPERT:], wd)

    @pl.when(e == 0)
    def _():
        o_ref[...] = ffn(sgu_ref[...], sd_ref[...])

    gates = g_ref[...]
    lane = lax.broadcasted_iota(jnp.int32, gates.shape, 1)
    for k in range(MOE_EXPERTS_PER_STEP):
        gcol = jnp.sum(jnp.where(lane == e * MOE_EXPERTS_PER_STEP + k, gates, 0.0),
                       axis=1, keepdims=True)
        o_ref[...] += ffn(wgu_ref[0, k], wd_ref[0, k]) * gcol


def _moe_call(tok, gates, wgu, wd, layer, sgu, sd, tm):
    n, d = tok.shape
    eps = MOE_EXPERTS_PER_STEP
    return pl.pallas_call(
        _moe_kernel,
        grid=(n // tm, N_EXPERTS // eps),
        in_specs=[pl.BlockSpec((tm, d), lambda i, e: (i, 0)),
                  pl.BlockSpec((tm, N_EXPERTS), lambda i, e: (i, 0)),
                  pl.BlockSpec((1, eps, d, 2 * D_EXPERT), lambda i, e: (layer, e, 0, 0)),
                  pl.BlockSpec((1, eps, D_EXPERT, d), lambda i, e: (layer, e, 0, 0)),
                  pl.BlockSpec((d, 2 * D_EXPERT), lambda i, e: (0, 0)),
                  pl.BlockSpec((D_EXPERT, d), lambda i, e: (0, 0))],
        out_specs=pl.BlockSpec((tm, d), lambda i, e: (i, 0)),
        out_shape=jax.ShapeDtypeStruct((n, d), F32),
        compiler_params=_cparams(("arbitrary", "arbitrary"), MOE_VMEM_LIMIT),
        name="moe_experts",
    )(tok, gates, wgu, wd, sgu, sd)


def _resid_kernel(x_ref, f_ref, g_ref, mod_ref, o_ref):
    f = f_ref[...]
    y = f * lax.rsqrt(jnp.mean(f * f, axis=-1, keepdims=True) + EPS) * g_ref[...]
    o_ref[...] = x_ref[...] + mod_ref[0] * y


def _resid_call(x, f, g, mod, tpb, n_batch, latent_only=False):
    n, d = x.shape
    vec = pl.BlockSpec((1, d), lambda *_: (0, 0))
    if latent_only:
        lpb = tpb - 1
        grid = (n_batch, lpb)
        row = pl.BlockSpec((TILE, d), lambda b, i: (b * tpb + 1 + i, 0))
        out = pl.BlockSpec((TILE, d), lambda b, i: (b * lpb + i, 0))
        kind = pl.BlockSpec((1, 1, d), lambda b, i: (b, 0, 0))
        n_out = n_batch * lpb * TILE
    else:
        grid = (n // TILE,)
        row = out = pl.BlockSpec((TILE, d), lambda i: (i, 0))
        kind = pl.BlockSpec((1, 1, d), lambda i: (jnp.where(i % tpb == 0, n_batch, i // tpb), 0, 0))
        n_out = n
    return pl.pallas_call(
        _resid_kernel,
        grid=grid,
        in_specs=[row, row, vec, kind],
        out_specs=out,
        out_shape=jax.ShapeDtypeStruct((n_out, d), F32),
        compiler_params=_cparams(("arbitrary",) * len(grid)),
        name="moe_residual",
    )(x, f, g.reshape(1, d), mod)


def _proj_weights(w):
    w = w.astype(BF16)
    k = w.shape[0]
    da, cv, sw = 0, 3 * BW, 5 * BW
    rw = sw + (SW_HEADS + 2 * SW_KV) * SW_DH
    gate = rw + RW_SHIFT_W + RW_G_RANK

    def per_q_head(x):
        return jnp.repeat(x.reshape(k, SW_KV, SW_DH), SW_HEADS // SW_KV, axis=1).reshape(k, -1)

    da_q, da_k, da_v = (w[:, da + i * BW:da + (i + 1) * BW] for i in range(3))
    sw_q = w[:, sw:sw + BW]
    sw_k = per_q_head(w[:, sw + BW:sw + BW + SW_KV * SW_DH])
    sw_v = per_q_head(w[:, sw + BW + SW_KV * SW_DH:rw])
    out = jnp.concatenate([
        w[:, gate:], da_q, da_k, sw_q, sw_k, da_v, sw_v, w[:, cv:sw], w[:, rw:gate]], axis=1)
    assert out.shape[1] == C_TOTAL
    return out


def _rope_tables(seq):
    rows = seq // GRID_W
    row = jnp.repeat(jnp.arange(rows, dtype=F32), GRID_W)
    colp = jnp.tile(jnp.arange(GRID_W, dtype=F32), rows)

    def tables(dim):
        q = dim // 4
        freqs = ROPE_BASE ** (-jnp.arange(q, dtype=F32) / q)
        ar, ac = row[:, None] * freqs, colp[:, None] * freqs
        cos = jnp.concatenate([jnp.cos(ar), jnp.cos(ar), jnp.cos(ac), jnp.cos(ac)], axis=1)
        sin = jnp.concatenate([-jnp.sin(ar), jnp.sin(ar), -jnp.sin(ac), jnp.sin(ac)], axis=1)
        return jnp.tile(cos, (1, LANES // dim)), jnp.tile(sin, (1, LANES // dim))

    c32, s32 = tables(DA_HALF)
    c64, s64 = tables(SW_DH)
    cos = jnp.concatenate([c32, c64], axis=1)
    sin = jnp.concatenate([s32, s64], axis=1)
    w = cos.shape[1]
    cos = jnp.concatenate([jnp.ones((TILE, w), F32), cos], axis=0)
    sin = jnp.concatenate([jnp.zeros((TILE, w), F32), sin], axis=0)
    return cos, sin


def _moe_tile(rows_b):
    best = TILE
    for t in range(TILE, MOE_MAX_TILE + 1, 16):
        if rows_b % t == 0:
            best = t
    return best


def kernel(x, c, ctx, c_ctx, ada_w, ada_b, norm_g, w_in, w_branch, w_out, da_lambda, da_subln,
           cv_w, cv_b, cv_ln, sw_sink, rw_mu, rw_w0, rw_w_up, rw_a0, rw_a_up, rw_kk, rw_ka,
           rw_g_up, rw_rk, rw_gn, router_w, router_b, ex_w_gu, ex_w_down, sh_w_gu, sh_w_down):
    n_batch, seq, d = x.shape
    ctx_len = ctx.shape[1]
    depth = w_in.shape[0]
    assert ctx_len == TILE and seq % TILE == 0 and seq % GRID_W == 0
    assert n_batch + 1 <= SUBLANES
    rows_b = ctx_len + seq
    tpb = rows_b // TILE
    n = n_batch * rows_b

    xs = jnp.concatenate([ctx, x], axis=1).reshape(n, d)
    cond = jnp.zeros((SUBLANES, d), F32).at[:n_batch].set(c).at[n_batch].set(c_ctx)
    mods = _ada_call(cond, ada_w, ada_b)[:, :n_batch + 1]
    cos_t, sin_t = _rope_tables(seq)
    hio = np.arange(BW) // RW_DH
    bd = jnp.asarray(hio[:, None] == hio[None, :], BF16)

    for l in range(depth):
        mod = [mods[l, :, i * d:(i + 1) * d].reshape(n_batch + 1, 1, d) for i in range(6)]
        ng = norm_g[l]
        lam_init = 0.8 - 0.6 * math.exp(-0.3 * l)
        lv = da_lambda[l]
        lam = (jnp.exp(jnp.sum(lv[0] * lv[1])) - jnp.exp(jnp.sum(lv[2] * lv[3])) + lam_init)
        lp = {'rw_mu': rw_mu[l], 'rw_w0': rw_w0[l], 'rw_w_up': rw_w_up[l], 'rw_a0': rw_a0[l],
              'rw_a_up': rw_a_up[l], 'rw_kk': rw_kk[l], 'rw_ka': rw_ka[l], 'rw_rk': rw_rk[l]}

        u = _proj_call(xs, ng[0], mod[0], mod[1], _proj_weights(w_in[l]), tpb, n_batch)
        qk, vt = _rope_call(u, cos_t, sin_t, n_batch, tpb)
        oa = _da_call(lam.reshape(1), qk, vt, jnp.tile(da_subln[l], DA_HEADS).reshape(1, BW),
                      n_batch, tpb, lam_init)
        ob = _conv_call(u, cv_w[l], cv_b[l], cv_ln[l], tpb)
        oc = _sw_call(sw_sink[l], qk, u, n_batch, tpb)
        cm, cn, cq, cy0, bonus = _rw_chunk_call(u, lp, bd, tpb)
        yscan = _rw_scan_call(cm, cn, cq, cy0, n_batch, tpb)
        xs, tok, gates_t = _merge_call(
            (oa, ob, oc), yscan, bonus, rw_g_up[l].astype(BF16), rw_gn[l], bd,
            u, w_branch[l].astype(BF16), w_out[l].astype(BF16), xs, ng[1],
            mod[2], ng[2], mod[3], mod[4], router_w[l].T, router_b[l], tpb, n_batch)
        f = _moe_call(tok, gates_t.T, ex_w_gu, ex_w_down, l, sh_w_gu[l].astype(BF16),
                      sh_w_down[l].astype(BF16), _moe_tile(rows_b))
        xs = _resid_call(xs, f, ng[3], mod[5], tpb, n_batch, latent_only=(l == depth - 1))

    return xs.reshape(n_batch, seq, d)
```

```python
import functools
import math

import numpy as np
import jax
import jax.numpy as jnp
from jax import lax
from jax.experimental import pallas as pl
from jax.experimental.pallas import tpu as pltpu

F32 = jnp.float32
BF16 = jnp.bfloat16
SUBLANES, LANES = 8, 128

GRID_W = 64
EPS = 1e-6
ROPE_BASE = 10000.0
N_BRANCH = 4
BW = 256
DA_HEADS = 4
DA_HALF = 32
SW_HEADS = 4
SW_KV = 2
SW_DH = 64
WINDOW = 128
CONV_W = 31
RW_HEADS = 4
RW_DH = 64
RW_DECAY_RANK = 64
RW_A_RANK = 64
RW_G_RANK = 128
RW_DECAY_SCALE = math.exp(-0.5)
RW_GN_EPS = 64e-5
RW_SHIFT_W = 3 * BW + RW_DECAY_RANK + RW_A_RANK
N_EXPERTS = 64
TOP_K = 6
N_GROUPS = 8
TOPK_GROUPS = 4
D_EXPERT = 256
ROUTE_SCALE = 2.5

TILE = 256
CHUNK = 128
DA_KT = 1408
MOE_MAX_TILE = 2112
MOE_EXPERTS_PER_STEP = 2
MOE_VMEM_LIMIT = 58 * 1024 * 1024
DA_VROWS = 80
NEG = -1e30

C_GATE = 0
C_ROPE = 4096
C_DAV = 5120
C_SWV = 5376
C_CV = 5632
C_RW = 6144
C_TOTAL = 7168

VMEM_LIMIT = 48 * 1024 * 1024

_NT = (((1,), (1,)), ((), ()))
_NN = (((1,), (0,)), ((), ()))


def _cparams(sem, vmem=VMEM_LIMIT):
    return pltpu.CompilerParams(dimension_semantics=sem, vmem_limit_bytes=vmem)


def _dot(a, b, dims=_NN):
    return lax.dot_general(a.astype(BF16), b.astype(BF16), dims, preferred_element_type=F32)


def _split2(x):
    hi = x.astype(BF16)
    lo = (x - hi.astype(F32)).astype(BF16)
    return hi, lo


def _dot3(a, b, dims=_NN):
    ah, al = _split2(a)
    bh, bl = _split2(b)
    dg = lambda x, y: lax.dot_general(x, y, dims, preferred_element_type=F32)
    return dg(ah, bh) + (dg(ah, bl) + dg(al, bh))


def _dot_sel(x, sel, dims=_NN):
    h0 = x.astype(BF16)
    r1 = x - h0.astype(F32)
    h1 = r1.astype(BF16)
    h2 = (r1 - h1.astype(F32)).astype(BF16)
    dg = lambda y: lax.dot_general(y, sel, dims, preferred_element_type=F32)
    return dg(h0) + (dg(h1) + dg(h2))


def _sel_dot(sel, x):
    h0 = x.astype(BF16)
    r1 = x - h0.astype(F32)
    h1 = r1.astype(BF16)
    h2 = (r1 - h1.astype(F32)).astype(BF16)
    dg = lambda y: lax.dot_general(sel, y, _NN, preferred_element_type=F32)
    return dg(h0) + (dg(h1) + dg(h2))


def _sigmoid(x):
    return jax.nn.sigmoid(x)


def _silu(x):
    return x * jax.nn.sigmoid(x)


def _head_of_lane(shape, width):
    return lax.broadcasted_iota(jnp.int32, shape, len(shape) - 1) // width


def _ada_kernel(s_ref, w_ref, b_ref, o_ref):
    s = _silu(s_ref[...])
    o_ref[0] = _dot(s, w_ref[0]) + b_ref[0]


def _ada_call(cond, ada_w, ada_b):
    depth, d, cols = ada_w.shape
    tn = 1536
    return pl.pallas_call(
        _ada_kernel,
        grid=(depth, cols // tn),
        in_specs=[pl.BlockSpec((SUBLANES, d), lambda l, j: (0, 0)),
                  pl.BlockSpec((1, d, tn), lambda l, j: (l, 0, j)),
                  pl.BlockSpec((1, 1, tn), lambda l, j: (l, 0, j))],
        out_specs=pl.BlockSpec((1, SUBLANES, tn), lambda l, j: (l, 0, j)),
        out_shape=jax.ShapeDtypeStruct((depth, SUBLANES, cols), F32),
        compiler_params=_cparams(("arbitrary", "arbitrary")),
        name="ada_mod",
    )(cond, ada_w, ada_b.reshape(depth, 1, cols))


PROJ_TILES = 2


def _proj_kernel(x_ref, g_ref, *refs):
    mods, w_ref, o_ref = refs[:2 * PROJ_TILES], refs[2 * PROJ_TILES], refs[2 * PROJ_TILES + 1]
    x = x_ref[...]
    y = x * lax.rsqrt(jnp.mean(x * x, axis=-1, keepdims=True) + EPS) * g_ref[...]
    h = jnp.concatenate(
        [(y[t * TILE:(t + 1) * TILE] * (1.0 + mods[2 * t + 1][0]) + mods[2 * t][0]).astype(BF16)
         for t in range(PROJ_TILES)], axis=0)
    o_ref[...] = jnp.dot(h, w_ref[...], preferred_element_type=F32)


def _proj_call(x, g, shift, scale, w, tpb, n_batch):
    n, d = x.shape
    cols = w.shape[1]
    tm, tn = PROJ_TILES * TILE, cols // 2

    def kind(t):
        def index(j, i):
            tile = i * PROJ_TILES + t
            return (jnp.where(tile % tpb == 0, n_batch, tile // tpb), 0, 0)
        return pl.BlockSpec((1, 1, d), index)

    mod_specs, mod_args = [], []
    for t in range(PROJ_TILES):
        mod_specs += [kind(t), kind(t)]
        mod_args += [shift, scale]
    return pl.pallas_call(
        _proj_kernel,
        grid=(cols // tn, n // tm),
        in_specs=[pl.BlockSpec((tm, d), lambda j, i: (i, 0)),
                  pl.BlockSpec((1, d), lambda j, i: (0, 0)), *mod_specs,
                  pl.BlockSpec((d, tn), lambda j, i: (0, j))],
        out_specs=pl.BlockSpec((tm, tn), lambda j, i: (i, j)),
        out_shape=jax.ShapeDtypeStruct((n, cols), F32),
        compiler_params=_cparams(("arbitrary", "arbitrary")),
        name="in_proj",
    )(x, g.reshape(1, d), *mod_args, w)


def _rope_kernel(u_ref, c_ref, s_ref, v_ref, o_ref, vt_ref):
    lanes = LANES
    n_batch, rows, w = u_ref.shape
    lane = lax.broadcasted_iota(jnp.int32, (rows, lanes), 1)
    hd = 2 * DA_HALF
    qscale = (DA_HALF ** -0.5) * math.log2(math.e)
    for b in range(n_batch):
        for k in range(w // lanes):
            cols = slice(k * lanes, (k + 1) * lanes)
            is_da = k * lanes < 2 * BW
            q = (DA_HALF if is_da else SW_DH) // 4
            tab = slice(0, lanes) if is_da else slice(lanes, 2 * lanes)
            x = u_ref[b, :, cols]
            partner = jnp.where((lane // q) % 2 == 0, pltpu.roll(x, lanes - q, 1),
                                pltpu.roll(x, q, 1))
            y = x * c_ref[:, tab] + partner * s_ref[:, tab]
            if k * lanes < BW:
                y = y * qscale
            o_ref[b, :, cols] = y.astype(BF16)
        vt = v_ref[b].T.astype(BF16)
        for h in range(DA_HEADS):
            vt_ref[b, h * DA_VROWS:h * DA_VROWS + hd, :] = vt[h * hd:(h + 1) * hd, :]
            vt_ref[b, h * DA_VROWS + hd:(h + 1) * DA_VROWS, :] = jnp.ones((DA_VROWS - hd, rows),
                                                                          BF16)


def _rope_call(u, cos_t, sin_t, n_batch, tpb):
    n, wu = u.shape
    rows_b = tpb * TILE
    w = 4 * BW
    vrows = DA_HEADS * DA_VROWS
    tab = pl.BlockSpec((TILE, cos_t.shape[1]), lambda i: (i, 0))
    u3 = u.reshape(n_batch, rows_b, wu)
    qk, vt = pl.pallas_call(
        _rope_kernel,
        grid=(tpb,),
        in_specs=[pl.BlockSpec((n_batch, TILE, w), lambda i: (0, i, C_ROPE // w)), tab, tab,
                  pl.BlockSpec((n_batch, TILE, BW), lambda i: (0, i, C_DAV // BW))],
        out_specs=[pl.BlockSpec((n_batch, TILE, w), lambda i: (0, i, 0)),
                   pl.BlockSpec((n_batch, vrows, TILE), lambda i: (0, 0, i))],
        out_shape=[jax.ShapeDtypeStruct((n_batch, rows_b, w), BF16),
                   jax.ShapeDtypeStruct((n_batch, vrows, rows_b), BF16)],
        compiler_params=_cparams(("arbitrary",)),
        name="rope",
    )(u3, cos_t, sin_t, u3)
    return qk.reshape(n, w), vt.reshape(n_batch * vrows, rows_b)


def _da_kernel(lam_ref, q_ref, k_ref, vt_ref, g_ref, o_ref,
               qs_ref, m_ref, acc_ref, *, nkt, lam_init):
    i = pl.program_id(1)
    tq = q_ref.shape[0]
    hd = 2 * DA_HALF
    q = q_ref[...]
    qmap = _head_of_lane((tq, BW), DA_HALF)
    for g in range(2 * DA_HEADS):
        qs_ref[g] = jnp.where(qmap == g, q, jnp.zeros_like(q))
    m_ref[...] = jnp.full(m_ref.shape, NEG, F32)
    acc_ref[...] = jnp.zeros(acc_ref.shape, F32)

    def tile(off, size):
        kt = k_ref[pl.ds(off, size), :]
        groups = range(2 * DA_HEADS)
        ss = [lax.dot_general(kt, qs_ref[g], _NT, preferred_element_type=F32) for g in groups]
        for g in groups:
            s = ss[g]
            m_old = m_ref[g]
            m_new = jnp.maximum(m_old, jnp.max(s, axis=0, keepdims=True))
            alpha = jnp.exp2(m_old - m_new)[0:1, :]
            p = jnp.exp2(s - m_new[0:1, :]).astype(BF16)
            m_ref[g] = m_new
            h, mm = g // 2, g % 2
            rows = slice(h * DA_VROWS, (h + 1) * DA_VROWS)
            pv = jnp.dot(vt_ref[rows, pl.ds(off, size)], p, preferred_element_type=F32)
            acc_ref[mm, rows, :] = acc_ref[mm, rows, :] * alpha + pv

    @pl.when(i == 0)
    def _():
        tile(0, TILE)

    @pl.when(i > 0)
    def _():
        def body(j, carry):
            tile(pl.multiple_of(j * DA_KT, DA_KT), DA_KT)
            return carry

        lax.fori_loop(0, nkt, body, 0)

    lam = lam_ref[0]
    parts = []
    for h in range(DA_HEADS):
        rows = slice(h * DA_VROWS, h * DA_VROWS + hd)
        den = slice(h * DA_VROWS + hd, h * DA_VROWS + hd + 1)
        o_h = (acc_ref[0, rows, :] / acc_ref[0, den, :]
               - lam * (acc_ref[1, rows, :] / acc_ref[1, den, :]))
        ms = jnp.mean(o_h * o_h, axis=0, keepdims=True)
        parts.append(o_h * lax.rsqrt(ms + EPS))
    y = jnp.concatenate(parts, axis=0).T * g_ref[...]
    o_ref[...] = y * (1.0 - lam_init)


def _da_call(lam, qk, vt, subln, n_batch, tpb, lam_init):
    n = qk.shape[0]
    rows_b = tpb * TILE
    assert rows_b % DA_KT == 0
    vrows = DA_HEADS * DA_VROWS
    kern = functools.partial(_da_kernel, nkt=rows_b // DA_KT, lam_init=lam_init)
    return pl.pallas_call(
        kern,
        grid=(n_batch, tpb),
        in_specs=[pl.BlockSpec(memory_space=pltpu.SMEM),
                  pl.BlockSpec((TILE, BW), lambda b, i: (b * tpb + i, 0)),
                  pl.BlockSpec((rows_b, BW), lambda b, i: (b, 1)),
                  pl.BlockSpec((vrows, rows_b), lambda b, i: (b, 0)),
                  pl.BlockSpec((1, BW), lambda b, i: (0, 0))],
        out_specs=pl.BlockSpec((TILE, BW), lambda b, i: (b * tpb + i, 0)),
        out_shape=jax.ShapeDtypeStruct((n, BW), F32),
        scratch_shapes=[pltpu.VMEM((2 * DA_HEADS, TILE, BW), BF16),
                        pltpu.VMEM((2 * DA_HEADS, SUBLANES, TILE), F32),
                        pltpu.VMEM((2, vrows, TILE), F32)],
        compiler_params=_cparams(("arbitrary", "arbitrary")),
        name="diff_attn",
    )(lam, qk, qk, vt, subln)


def _sw_kernel(sink_ref, bias_ref, q_ref, kp_ref, ko_ref, kn_ref, kc_ref,
               vp_ref, vo_ref, vn_ref, vc_ref, o_ref):
    tq = q_ref.shape[0]
    q = q_ref[...]
    kk = jnp.concatenate([kp_ref[...], ko_ref[...], kn_ref[...], kc_ref[...]], axis=0)
    vv = jnp.concatenate([vp_ref[...], vo_ref[...], vn_ref[...], vc_ref[...]], axis=0).astype(BF16)
    nk = kk.shape[0]
    bias = bias_ref[0]
    qhead = _head_of_lane((tq, BW), SW_DH)
    vhead = _head_of_lane((nk, BW), SW_DH)
    ps, vs = [], []
    for h in range(SW_HEADS):
        qm = jnp.where(qhead == h, q, jnp.zeros_like(q))
        s = lax.dot_general(qm, kk, _NT, preferred_element_type=F32) * (SW_DH ** -0.5) + bias
        sk = sink_ref[h]
        m = jnp.maximum(jnp.max(s, axis=1, keepdims=True), sk)
        p = jnp.exp(s - m)
        den = jnp.sum(p, axis=1, keepdims=True) + jnp.exp(sk - m)
        ps.append((p / den).astype(BF16))
        vs.append(jnp.where(vhead == h, vv, jnp.zeros_like(vv)))
    o_ref[...] = jnp.dot(jnp.concatenate(ps, axis=1), jnp.concatenate(vs, axis=0),
                         preferred_element_type=F32)


def _sw_bias():
    r = np.arange(TILE)[:, None]
    prev_ok = np.arange(CHUNK)[None, :] - CHUNK - r >= -WINDOW
    own_ok = np.abs(np.arange(TILE)[None, :] - r) <= WINDOW
    next_ok = np.arange(CHUNK)[None, :] + TILE - r <= WINDOW
    hide = lambda m: np.zeros_like(m)
    ctx = np.ones((TILE, TILE), bool)
    kinds = [np.concatenate([hide(prev_ok), hide(own_ok), hide(next_ok), ctx], axis=1)]
    for no_next in (False, True):
        for no_prev in (False, True):
            kinds.append(np.concatenate([hide(prev_ok) if no_prev else prev_ok, own_ok,
                                         hide(next_ok) if no_next else next_ok, ctx], axis=1))
    return jnp.asarray(np.where(np.stack(kinds), 0.0, NEG), F32)


def _sw_call(sink, qk, u, n_batch, tpb):
    n = qk.shape[0]
    per = TILE // CHUNK
    cpb = tpb * per
    bias = _sw_bias()

    def kind(b, j):
        lat = 1 + (j == 1).astype(jnp.int32) + 2 * (j == tpb - 1).astype(jnp.int32)
        return (jnp.where(j == 0, 0, lat), 0, 0)

    own = lambda b, j: b * tpb + j
    prv = lambda b, j: b * cpb + jnp.maximum(j * per - 1, 0)
    nxt = lambda b, j: b * cpb + jnp.minimum((j + 1) * per, cpb - 1)
    kcol, vcol = 3, C_SWV // BW
    edge = lambda f, col: pl.BlockSpec((CHUNK, BW), lambda b, j: (f(b, j), col))
    tile = lambda f, col: pl.BlockSpec((TILE, BW), lambda b, j: (f(b, j), col))
    ctx = lambda b, j: b * tpb
    return pl.pallas_call(
        _sw_kernel,
        grid=(n_batch, tpb),
        in_specs=[pl.BlockSpec(memory_space=pltpu.SMEM),
                  pl.BlockSpec((1,) + bias.shape[1:], kind),
                  tile(own, 2), edge(prv, kcol), tile(own, kcol), edge(nxt, kcol), tile(ctx, kcol),
                  edge(prv, vcol), tile(own, vcol), edge(nxt, vcol), tile(ctx, vcol)],
        out_specs=pl.BlockSpec((TILE, BW), lambda b, j: (own(b, j), 0)),
        out_shape=jax.ShapeDtypeStruct((n, BW), F32),
        compiler_params=_cparams(("arbitrary", "arbitrary")),
        name="window_attn",
    )(sink, bias, qk, qk, qk, qk, qk, u, u, u, u)


def _conv_kernel(prev_ref, cur_ref, next_ref, w_ref, b_ref, ln_ref, o_ref, z_ref, zs_ref, *, tpb):
    i = pl.program_id(0)
    pos = i % tpb
    has_prev = pos >= 2
    has_next = (pos >= 1) & (pos < tpb - 1)
    halo = prev_ref.shape[0]

    def glu(x):
        return x[:, :BW] * _sigmoid(x[:, BW:])

    zp = glu(prev_ref[...])
    zn = glu(next_ref[...])
    z_ref[0:halo, :] = jnp.where(has_prev, zp, 0.0)
    z_ref[halo:halo + TILE, :] = glu(cur_ref[...])
    z_ref[halo + TILE:2 * halo + TILE, :] = jnp.where(has_next, zn, 0.0)
    acc = jnp.zeros((TILE, BW), F32) + b_ref[...]
    pad = CONV_W // 2
    span = TILE + 2 * halo - SUBLANES
    for r in range(SUBLANES):
        zs_ref[r] = z_ref[r:r + span, :]
    for t in range(CONV_W):
        off = halo - pad + t
        shift, base = off % SUBLANES, off - off % SUBLANES
        acc = acc + zs_ref[shift, base:base + TILE, :] * w_ref[t:t + 1, :]
    mu = jnp.mean(acc, axis=-1, keepdims=True)
    xc = acc - mu
    y = xc * lax.rsqrt(jnp.mean(xc * xc, axis=-1, keepdims=True) + EPS)
    y = y * ln_ref[0:1, :] + ln_ref[1:2, :]
    o_ref[...] = _silu(y)


def _conv_call(u, w, b, ln, tpb):
    n = u.shape[0]
    halo = 16
    per = TILE // halo
    nh = n // halo
    wcv = 2 * BW
    col = C_CV // wcv
    return pl.pallas_call(
        functools.partial(_conv_kernel, tpb=tpb),
        grid=(n // TILE,),
        in_specs=[pl.BlockSpec((halo, wcv), lambda i: (jnp.maximum(i * per - 1, 0), col)),
                  pl.BlockSpec((TILE, wcv), lambda i: (i, col)),
                  pl.BlockSpec((halo, wcv), lambda i: (jnp.minimum((i + 1) * per, nh - 1), col)),
                  pl.BlockSpec((CONV_W, BW), lambda i: (0, 0)),
                  pl.BlockSpec((1, BW), lambda i: (0, 0)),
                  pl.BlockSpec((2, BW), lambda i: (0, 0))],
        out_specs=pl.BlockSpec((TILE, BW), lambda i: (i, 0)),
        out_shape=jax.ShapeDtypeStruct((n, BW), F32),
        scratch_shapes=[pltpu.VMEM((TILE + 2 * halo, BW), F32),
                        pltpu.VMEM((SUBLANES, TILE + 2 * halo - SUBLANES, BW), F32)],
        compiler_params=_cparams(("arbitrary",)),
        name="conformer_conv",
    )(u, u, u, w, b.reshape(1, BW), ln)


def _inv_unit_lower(ls, eye):
    ts = [eye + l for l in ls]
    lps = list(ls)
    step = 1
    while step < ls[0].shape[0] // 2:
        lps = [_dot(lp, lp) for lp in lps]
        ts = [t + _dot(t, lp) for t, lp in zip(ts, lps)]
        step *= 2
    return ts


def _rw_chunk_kernel(prev_ref, cur_ref, next_ref, mu_ref, w0_ref, a0_ref, kk_ref, ka_ref,
                     wup_ref, aup_ref, rk_ref, bd_ref,
                     m_ref, n_ref, q_ref, y0_ref, bonus_ref, yf_ref, z_ref, xf_ref, *, cpb):
    i = pl.program_id(0)
    jj = i % cpb

    @pl.when(jj == 0)
    def _():
        xf_ref[...] = jnp.zeros(xf_ref.shape, F32)

    ctx_chunks = TILE // CHUNK
    has_prev = (jj != 0) & (jj != ctx_chunks)
    has_next = (jj != ctx_chunks - 1) & (jj != cpb - 1)
    c = CHUNK
    zc = cur_ref[:, 0:RW_SHIFT_W]
    e = SUBLANES
    z_ref[0:e, :] = jnp.where(has_prev, prev_ref[:, 0:RW_SHIFT_W], 0.0)
    z_ref[e:e + c, :] = zc
    z_ref[e + c:2 * e + c, :] = jnp.where(has_next, next_ref[:, 0:RW_SHIFT_W], 0.0)
    bd = bd_ref[...]
    t_io = lax.broadcasted_iota(jnp.int32, (c, c), 0)
    s_io = lax.broadcasted_iota(jnp.int32, (c, c), 1)
    eye = jnp.where(t_io == s_io, 1.0, 0.0)
    lane_head = _head_of_lane((c, BW), RW_DH)
    row = lax.broadcasted_iota(jnp.int32, (BW, BW), 0)
    colm = lax.broadcasted_iota(jnp.int32, (BW, BW), 1)
    same_head = (row // RW_DH) == (colm // RW_DH)

    def stack(x):
        return jnp.concatenate(
            [jnp.where(lane_head == h, x, 0.0) for h in range(RW_HEADS)], axis=0)

    cat = lambda xs: jnp.concatenate(xs, axis=1)

    prep = []
    for d in range(2):
        zsh = z_ref[e - 1:e - 1 + c, :] if d == 0 else z_ref[e + 1:e + 1 + c, :]
        zs = zc + (zsh - zc) * mu_ref[d]
        r = zs[:, 0:BW]
        k = zs[:, BW:2 * BW]
        v = zs[:, 2 * BW:3 * BW]
        wa = zs[:, 3 * BW:RW_SHIFT_W]
        lw = -RW_DECAY_SCALE * _sigmoid(w0_ref[d] + _dot(jnp.tanh(wa), wup_ref[d]))
        a = _sigmoid(a0_ref[d] + _dot(wa, aup_ref[d]))
        kappa = k * kk_ref[d]
        kh = kappa * lax.rsqrt(jnp.maximum(_dot_sel(kappa * kappa, bd), 1e-12))
        kt = k * (1.0 + (a - 1.0) * ka_ref[d])
        akh = a * kh
        bonus_ref[d] = _dot_sel(r * kt * rk_ref[...], bd) * v

        ahead = (t_io - s_io) if d == 0 else (s_io - t_io)
        earlier = ahead > 0
        upto = ahead >= 0
        tri = jnp.where(upto, 1.0, 0.0).astype(BF16)
        cl = _sel_dot(tri, lw)
        tot = jnp.sum(lw, axis=0, keepdims=True)
        rho = 0.5 * tot
        cle = cl - lw
        a_true = -kh * jnp.exp(cle)
        r_true = r * jnp.exp(cl)
        a_c = -kh * jnp.exp(cle - rho)
        r_c = r * jnp.exp(cl - rho)
        b_c = akh * jnp.exp(rho - cl)
        k_c = kt * jnp.exp(rho - cl)
        b_end = akh * jnp.exp(tot - cl)
        k_end = kt * jnp.exp(tot - cl)

        pair = _dot(jnp.concatenate([stack(a_c), stack(r_c)], axis=0),
                    jnp.concatenate([b_c, k_c], axis=0), _NT)
        l_ab, l_ak, a_rb, a_rk = [], [], [], []
        for h in range(RW_HEADS):
            blk_a = pair[h * c:(h + 1) * c]
            blk_r = pair[(RW_HEADS + h) * c:(RW_HEADS + h + 1) * c]
            l_ab.append(jnp.where(earlier, blk_a[:, 0:c], 0.0))
            l_ak.append(jnp.where(earlier, blk_a[:, c:2 * c], 0.0))
            a_rb.append(jnp.where(upto, blk_r[:, 0:c], 0.0))
            a_rk.append(jnp.where(upto, blk_r[:, c:2 * c], 0.0))
        prep.append((l_ab, cat(l_ak), cat(a_rb), cat(a_rk), v, a_true, r_true, b_end, k_end, tot))

    t_all = _inv_unit_lower(prep[0][0] + prep[1][0], eye)

    for d in range(2):
        _, lak, arb, ark, v, a_true, r_true, b_end, k_end, tot = prep[d]
        t_inv = cat(t_all[d * RW_HEADS:(d + 1) * RW_HEADS])
        lv = _dot(lak, stack(v))
        w12 = _dot(t_inv, jnp.concatenate([stack(a_true), stack(lv)], axis=1))
        w1 = w12[:, 0:BW]
        w2 = w12[:, BW:2 * BW]
        q_c = r_true + _dot(arb, stack(w1))
        y0_c = _dot(arb, stack(w2)) + _dot(ark, stack(v))
        b_end_t = b_end.T
        decay = jnp.where(row == colm, jnp.broadcast_to(jnp.exp(tot), (BW, BW)), 0.0)
        m_c = jnp.where(same_head, _dot(b_end_t, w1), 0.0) + decay
        n_c = jnp.where(same_head, _dot(b_end_t, w2) + _dot(k_end.T, v), 0.0)
        if d == 0:
            x = xf_ref[...]
            yf_ref[...] = _dot(q_c, x) + y0_c
            xf_ref[...] = _dot(m_c, x) + n_c
        else:
            q_ref[0, 0] = q_c.astype(BF16)
            y0_ref[0, 0] = y0_c
            m_ref[0, 0] = m_c.astype(BF16)
            n_ref[0, 0] = n_c


def _rw_chunk_call(u, lp, bd, tpb):
    n = u.shape[0]
    cpb = tpb * (TILE // CHUNK)
    nch = n // CHUNK
    edge = SUBLANES
    per = CHUNK // edge
    n8 = n // edge
    wrw = RW_SHIFT_W + RW_G_RANK
    col = C_RW // wrw
    pvec = lambda w: pl.BlockSpec((2, 1, w), lambda i: (0, 0, 0))
    pad = jnp.zeros((2, RW_DECAY_RANK, BW), F32)
    wup = jnp.concatenate([lp['rw_w_up'], pad], axis=1).astype(BF16)
    aup = jnp.concatenate([pad, lp['rw_a_up']], axis=1).astype(BF16)
    mat = lambda rows: pl.BlockSpec((1, 1, rows, BW), lambda i: (0, i, 0, 0))
    return pl.pallas_call(
        functools.partial(_rw_chunk_kernel, cpb=cpb),
        grid=(nch,),
        in_specs=[pl.BlockSpec((edge, wrw), lambda i: (jnp.maximum(i * per - 1, 0), col)),
                  pl.BlockSpec((CHUNK, wrw), lambda i: (i, col)),
                  pl.BlockSpec((edge, wrw), lambda i: (jnp.minimum((i + 1) * per, n8 - 1), col)),
                  pvec(RW_SHIFT_W), pvec(BW), pvec(BW), pvec(BW), pvec(BW),
                  pl.BlockSpec((2, 2 * RW_DECAY_RANK, BW), lambda i: (0, 0, 0)),
                  pl.BlockSpec((2, 2 * RW_A_RANK, BW), lambda i: (0, 0, 0)),
                  pl.BlockSpec((1, BW), lambda i: (0, 0)),
                  pl.BlockSpec((BW, BW), lambda i: (0, 0))],
        out_specs=[mat(BW), mat(BW), mat(CHUNK), mat(CHUNK),
                   pl.BlockSpec((2, CHUNK, BW), lambda i: (0, i, 0)),
                   pl.BlockSpec((CHUNK, BW), lambda i: (i, 0))],
        out_shape=[jax.ShapeDtypeStruct((1, nch, BW, BW), BF16),
                   jax.ShapeDtypeStruct((1, nch, BW, BW), F32),
                   jax.ShapeDtypeStruct((1, nch, CHUNK, BW), BF16),
                   jax.ShapeDtypeStruct((1, nch, CHUNK, BW), F32),
                   jax.ShapeDtypeStruct((2, n, BW), F32),
                   jax.ShapeDtypeStruct((n, BW), F32)],
        scratch_shapes=[pltpu.VMEM((CHUNK + 2 * edge, RW_SHIFT_W), F32),
                        pltpu.VMEM((BW, BW), F32)],
        compiler_params=_cparams(("arbitrary",)),
        name="rwkv_chunk",
    )(u, u, u, lp['rw_mu'].reshape(2, 1, RW_SHIFT_W), lp['rw_w0'].reshape(2, 1, BW),
      lp['rw_a0'].reshape(2, 1, BW), lp['rw_kk'].reshape(2, 1, BW), lp['rw_ka'].reshape(2, 1, BW),
      wup, aup, lp['rw_rk'].reshape(1, BW), bd)


def _rw_scan_kernel(*refs, n_batch):
    m_ref, n_ref, q_ref, y0_ref, y_ref, x_ref = refs

    @pl.when(pl.program_id(0) == 0)
    def _():
        x_ref[...] = jnp.zeros(x_ref.shape, F32)

    for b in range(n_batch):
        x = x_ref[b]
        y_ref[b, 0] = _dot(q_ref[0, b, 0], x) + y0_ref[0, b, 0]
        x_ref[b] = _dot(m_ref[0, b, 0], x) + n_ref[0, b, 0]


def _rw_scan_call(m, nn, q, y0, n_batch, tpb):
    cpb = tpb * (TILE // CHUNK)
    ctx_chunks = TILE // CHUNK
    chunk = lambda i: jnp.where(i < ctx_chunks, ctx_chunks - 1 - i, cpb + ctx_chunks - 1 - i)

    def mat(rows):
        return pl.BlockSpec((1, n_batch, 1, rows, BW), lambda i: (0, 0, chunk(i), 0, 0))

    ins, specs = [], []
    for arr, rows in ((m, BW), (nn, BW), (q, CHUNK), (y0, CHUNK)):
        ins.append(arr.reshape(1, n_batch, cpb, rows, BW))
        specs.append(mat(rows))
    y = pl.pallas_call(
        functools.partial(_rw_scan_kernel, n_batch=n_batch),
        grid=(cpb,),
        in_specs=specs,
        out_specs=pl.BlockSpec((n_batch, 1, CHUNK, BW), lambda i: (0, chunk(i), 0, 0)),
        out_shape=jax.ShapeDtypeStruct((n_batch, cpb, CHUNK, BW), F32),
        scratch_shapes=[pltpu.VMEM((n_batch, BW, BW), F32)],
        compiler_params=_cparams(("arbitrary",)),
        name="rwkv_scan",
    )(*ins)
    return y.reshape(n_batch * cpb * CHUNK, BW)


def _rw_readout(yf_ref, yb_ref, bonus_ref, gd_ref, gup_ref, gn_ref, bd_ref):
    bd = bd_ref[...]
    y = yf_ref[...] + yb_ref[...]
    mean = _dot_sel(y, bd) * (1.0 / RW_DH)
    yc = y - mean
    var = _dot_sel(yc * yc, bd) * (1.0 / RW_DH)
    yn = yc * lax.rsqrt(var + RW_GN_EPS) * gn_ref[0:1, :] + gn_ref[1:2, :]
    yn = yn + (bonus_ref[0] + bonus_ref[1])
    return yn * _dot(_sigmoid(gd_ref[...]), gup_ref[...])


MERGE_TILES = 2


def _merge_kernel(oa_ref, ob_ref, oc_ref, yf_ref, yb_ref, bonus_ref, gd_ref, gup_ref, gn_ref, bd_ref,
                  gl_ref, wb_ref, wo_ref, x_ref, g_ref, g2_ref, rw_ref, rb_ref, *refs):
    mods, (o_ref, tok_ref, gate_ref) = refs[:3 * MERGE_TILES], refs[3 * MERGE_TILES:]
    d = x_ref.shape[1]
    od = _rw_readout(yf_ref, yb_ref, bonus_ref, gd_ref, gup_ref, gn_ref, bd_ref)
    m = None
    for i, o in enumerate((oa_ref[...], ob_ref[...], oc_ref[...], od)):
        t = _sigmoid(gl_ref[:, i * d:(i + 1) * d]) * _dot(o, wb_ref[i])
        m = t if m is None else m + t
    y = _dot(m, wo_ref[...])
    y = y * lax.rsqrt(jnp.mean(y * y, axis=-1, keepdims=True) + EPS) * g_ref[...]
    toks = []
    for t in range(MERGE_TILES):
        rows = slice(t * TILE, (t + 1) * TILE)
        gate, shift, scale = (r[0] for r in mods[3 * t:3 * t + 3])
        x = x_ref[rows, :] + gate * y[rows]
        o_ref[rows, :] = x
        h = x * lax.rsqrt(jnp.mean(x * x, axis=-1, keepdims=True) + EPS) * g2_ref[...]
        tok = h * (1.0 + scale) + shift
        tok_ref[rows, :] = tok.astype(BF16)
        toks.append(tok)
    gate_ref[...] = _route(jnp.concatenate(toks, axis=0), rw_ref[...], rb_ref[...])


def _merge_call(outs, ys, bonus, gup, gn, bd, u, wb, wo, x, g, mod, g2, shift, scale, rw_t, rb,
                tpb, n_batch):
    n, d = x.shape
    tm = MERGE_TILES * TILE

    def kind(t):
        def index(i):
            tile = i * MERGE_TILES + t
            return (jnp.where(tile % tpb == 0, n_batch, tile // tpb), 0, 0)
        return pl.BlockSpec((1, 1, d), index)

    br = pl.BlockSpec((tm, BW), lambda i: (i, 0))
    gcol = (C_RW + RW_SHIFT_W) // RW_G_RANK
    readout = [br, br, pl.BlockSpec((2, tm, BW), lambda i: (0, i, 0)),
               pl.BlockSpec((tm, RW_G_RANK), lambda i: (i, gcol)),
               pl.BlockSpec((RW_G_RANK, BW), lambda i: (0, 0)),
               pl.BlockSpec((2, BW), lambda i: (0, 0)),
               pl.BlockSpec((BW, BW), lambda i: (0, 0))]
    vec = pl.BlockSpec((1, d), lambda i: (0, 0))
    row = pl.BlockSpec((tm, d), lambda i: (i, 0))
    wg = N_BRANCH * d
    mod_specs, mod_args = [], []
    for t in range(MERGE_TILES):
        mod_specs += [kind(t)] * 3
        mod_args += [mod, shift, scale]
    return pl.pallas_call(
        _merge_kernel,
        grid=(n // tm,),
        in_specs=[br, br, br, *readout,
                  pl.BlockSpec((tm, wg), lambda i: (i, C_GATE // wg)),
                  pl.BlockSpec((N_BRANCH, BW, d), lambda i: (0, 0, 0)),
                  pl.BlockSpec((d, d), lambda i: (0, 0)),
                  row, vec, vec,
                  pl.BlockSpec((N_EXPERTS, d), lambda i: (0, 0)),
                  pl.BlockSpec((N_EXPERTS, 1), lambda i: (0, 0)), *mod_specs],
        out_specs=[row, row, pl.BlockSpec((N_EXPERTS, tm), lambda i: (0, i))],
        out_shape=[jax.ShapeDtypeStruct((n, d), F32), jax.ShapeDtypeStruct((n, d), BF16),
                   jax.ShapeDtypeStruct((N_EXPERTS, n), F32)],
        compiler_params=_cparams(("arbitrary",), MOE_VMEM_LIMIT),
        name="merge",
    )(*outs, ys[0], ys[1], bonus, u, gup, gn, bd, u, wb, wo, x, g.reshape(1, d),
      g2.reshape(1, d), rw_t, rb.reshape(N_EXPERTS, 1), *mod_args)


def _route(tokens, w, b):
    tm = tokens.shape[0]
    gsz = N_EXPERTS // N_GROUPS
    logits = _dot3(w, tokens, _NT)
    sc = _sigmoid(logits).reshape(N_GROUPS, gsz, tm)
    bi = sc + b.reshape(N_GROUPS, gsz, 1)
    shape = (N_GROUPS, gsz, tm)
    g_io = lax.broadcasted_iota(jnp.int32, shape, 0)
    j_io = lax.broadcasted_iota(jnp.int32, shape, 1)
    e_io = g_io * gsz + j_io
    ninf = -jnp.inf
    m1 = jnp.max(bi, axis=1, keepdims=True)
    i1 = jnp.min(jnp.where(bi == m1, j_io, gsz), axis=1, keepdims=True)
    m2 = jnp.max(jnp.where(j_io == i1, ninf, bi), axis=1, keepdims=True)
    cur = jnp.broadcast_to(m1 + m2, shape)
    gsel = jnp.zeros(shape, F32)
    for _ in range(TOPK_GROUPS):
        mx = jnp.max(cur, axis=0, keepdims=True)
        ix = jnp.min(jnp.where(cur == mx, g_io, N_GROUPS), axis=0, keepdims=True)
        hit = g_io == ix
        gsel = jnp.where(hit, 1.0, gsel)
        cur = jnp.where(hit, ninf, cur)
    cur = jnp.where(gsel > 0.0, bi, ninf)
    esel = jnp.zeros(shape, F32)
    for _ in range(TOP_K):
        mx = jnp.max(jnp.max(cur, axis=0, keepdims=True), axis=1, keepdims=True)
        ix = jnp.min(jnp.min(jnp.where(cur == mx, e_io, N_EXPERTS), axis=0, keepdims=True),
                     axis=1, keepdims=True)
        hit = e_io == ix
        esel = jnp.where(hit, 1.0, esel)
        cur = jnp.where(hit, ninf, cur)
    wsel = sc * esel
    den = jnp.sum(jnp.sum(wsel, axis=0, keepdims=True), axis=1, keepdims=True)
    return (wsel / den * ROUTE_SCALE).reshape(N_EXPERTS, tm)


def _moe_kernel(x_ref, g_ref, wgu_ref, wd_ref, sgu_ref, sd_ref, o_ref):
    e = pl.program_id(1)

    def ffn(wgu, wd):
        hgu = _dot(x_ref[...], wgu)
        return _dot(_silu(hgu[:, :D_EXPERT]) * hgu[:, D_EXPERT:], wd)

    @pl.when(e == 0)
    def _():
        o_ref[...] = ffn(sgu_ref[...], sd_ref[...])

    gates = g_ref[...]
    lane = lax.broadcasted_iota(jnp.int32, gates.shape, 1)
    for k in range(MOE_EXPERTS_PER_STEP):
        gcol = jnp.sum(jnp.where(lane == e * MOE_EXPERTS_PER_STEP + k, gates, 0.0),
                       axis=1, keepdims=True)
        o_ref[...] += ffn(wgu_ref[0, k], wd_ref[0, k]) * gcol


def _moe_call(tok, gates, wgu, wd, layer, sgu, sd, tm):
    n, d = tok.shape
    eps = MOE_EXPERTS_PER_STEP
    return pl.pallas_call(
        _moe_kernel,
        grid=(n // tm, N_EXPERTS // eps),
        in_specs=[pl.BlockSpec((tm, d), lambda i, e: (i, 0)),
                  pl.BlockSpec((tm, N_EXPERTS), lambda i, e: (i, 0)),
                  pl.BlockSpec((1, eps, d, 2 * D_EXPERT), lambda i, e: (layer, e, 0, 0)),
                  pl.BlockSpec((1, eps, D_EXPERT, d), lambda i, e: (layer, e, 0, 0)),
                  pl.BlockSpec((d, 2 * D_EXPERT), lambda i, e: (0, 0)),
                  pl.BlockSpec((D_EXPERT, d), lambda i, e: (0, 0))],
        out_specs=pl.BlockSpec((tm, d), lambda i, e: (i, 0)),
        out_shape=jax.ShapeDtypeStruct((n, d), F32),
        compiler_params=_cparams(("arbitrary", "arbitrary"), MOE_VMEM_LIMIT),
        name="moe_experts",
    )(tok, gates, wgu, wd, sgu, sd)


def _resid_kernel(x_ref, f_ref, g_ref, mod_ref, o_ref):
    f = f_ref[...]
    y = f * lax.rsqrt(jnp.mean(f * f, axis=-1, keepdims=True) + EPS) * g_ref[...]
    o_ref[...] = x_ref[...] + mod_ref[0] * y


def _resid_call(x, f, g, mod, tpb, n_batch, latent_only=False):
    n, d = x.shape
    vec = pl.BlockSpec((1, d), lambda *_: (0, 0))
    if latent_only:
        lpb = tpb - 1
        grid = (n_batch, lpb)
        row = pl.BlockSpec((TILE, d), lambda b, i: (b * tpb + 1 + i, 0))
        out = pl.BlockSpec((TILE, d), lambda b, i: (b * lpb + i, 0))
        kind = pl.BlockSpec((1, 1, d), lambda b, i: (b, 0, 0))
        n_out = n_batch * lpb * TILE
    else:
        grid = (n // TILE,)
        row = out = pl.BlockSpec((TILE, d), lambda i: (i, 0))
        kind = pl.BlockSpec((1, 1, d), lambda i: (jnp.where(i % tpb == 0, n_batch, i // tpb), 0, 0))
        n_out = n
    return pl.pallas_call(
        _resid_kernel,
        grid=grid,
        in_specs=[row, row, vec, kind],
        out_specs=out,
        out_shape=jax.ShapeDtypeStruct((n_out, d), F32),
        compiler_params=_cparams(("arbitrary",) * len(grid)),
        name="moe_residual",
    )(x, f, g.reshape(1, d), mod)


def _proj_weights(w):
    w = w.astype(BF16)
    k = w.shape[0]
    da, cv, sw = 0, 3 * BW, 5 * BW
    rw = sw + (SW_HEADS + 2 * SW_KV) * SW_DH
    gate = rw + RW_SHIFT_W + RW_G_RANK

    def per_q_head(x):
        return jnp.repeat(x.reshape(k, SW_KV, SW_DH), SW_HEADS // SW_KV, axis=1).reshape(k, -1)

    da_q, da_k, da_v = (w[:, da + i * BW:da + (i + 1) * BW] for i in range(3))
    sw_q = w[:, sw:sw + BW]
    sw_k = per_q_head(w[:, sw + BW:sw + BW + SW_KV * SW_DH])
    sw_v = per_q_head(w[:, sw + BW + SW_KV * SW_DH:rw])
    out = jnp.concatenate([
        w[:, gate:], da_q, da_k, sw_q, sw_k, da_v, sw_v, w[:, cv:sw], w[:, rw:gate]], axis=1)
    assert out.shape[1] == C_TOTAL
    return out


def _rope_tables(seq):
    rows = seq // GRID_W
    row = jnp.repeat(jnp.arange(rows, dtype=F32), GRID_W)
    colp = jnp.tile(jnp.arange(GRID_W, dtype=F32), rows)

    def tables(dim):
        q = dim // 4
        freqs = ROPE_BASE ** (-jnp.arange(q, dtype=F32) / q)
        ar, ac = row[:, None] * freqs, colp[:, None] * freqs
        cos = jnp.concatenate([jnp.cos(ar), jnp.cos(ar), jnp.cos(ac), jnp.cos(ac)], axis=1)
        sin = jnp.concatenate([-jnp.sin(ar), jnp.sin(ar), -jnp.sin(ac), jnp.sin(ac)], axis=1)
        return jnp.tile(cos, (1, LANES // dim)), jnp.tile(sin, (1, LANES // dim))

    c32, s32 = tables(DA_HALF)
    c64, s64 = tables(SW_DH)
    cos = jnp.concatenate([c32, c64], axis=1)
    sin = jnp.concatenate([s32, s64], axis=1)
    w = cos.shape[1]
    cos = jnp.concatenate([jnp.ones((TILE, w), F32), cos], axis=0)
    sin = jnp.concatenate([jnp.zeros((TILE, w), F32), sin], axis=0)
    return cos, sin


def _moe_tile(rows_b):
    best = TILE
    for t in range(TILE, MOE_MAX_TILE + 1, 16):
        if rows_b % t == 0:
            best = t
    return best


def kernel(x, c, ctx, c_ctx, ada_w, ada_b, norm_g, w_in, w_branch, w_out, da_lambda, da_subln,
           cv_w, cv_b, cv_ln, sw_sink, rw_mu, rw_w0, rw_w_up, rw_a0, rw_a_up, rw_kk, rw_ka,
           rw_g_up, rw_rk, rw_gn, router_w, router_b, ex_w_gu, ex_w_down, sh_w_gu, sh_w_down):
    n_batch, seq, d = x.shape
    ctx_len = ctx.shape[1]
    depth = w_in.shape[0]
    assert ctx_len == TILE and seq % TILE == 0 and seq % GRID_W == 0
    assert n_batch + 1 <= SUBLANES
    rows_b = ctx_len + seq
    tpb = rows_b // TILE
    n = n_batch * rows_b

    xs = jnp.concatenate([ctx, x], axis=1).reshape(n, d)
    cond = jnp.zeros((SUBLANES, d), F32).at[:n_batch].set(c).at[n_batch].set(c_ctx)
    mods = _ada_call(cond, ada_w, ada_b)[:, :n_batch + 1]
    cos_t, sin_t = _rope_tables(seq)
    hio = np.arange(BW) // RW_DH
    bd = jnp.asarray(hio[:, None] == hio[None, :], BF16)

    for l in range(depth):
        mod = [mods[l, :, i * d:(i + 1) * d].reshape(n_batch + 1, 1, d) for i in range(6)]
        ng = norm_g[l]
        lam_init = 0.8 - 0.6 * math.exp(-0.3 * l)
        lv = da_lambda[l]
        lam = (jnp.exp(jnp.sum(lv[0] * lv[1])) - jnp.exp(jnp.sum(lv[2] * lv[3])) + lam_init)
        lp = {'rw_mu': rw_mu[l], 'rw_w0': rw_w0[l], 'rw_w_up': rw_w_up[l], 'rw_a0': rw_a0[l],
              'rw_a_up': rw_a_up[l], 'rw_kk': rw_kk[l], 'rw_ka': rw_ka[l], 'rw_rk': rw_rk[l]}

        u = _proj_call(xs, ng[0], mod[0], mod[1], _proj_weights(w_in[l]), tpb, n_batch)
        qk, vt = _rope_call(u, cos_t, sin_t, n_batch, tpb)
        oa = _da_call(lam.reshape(1), qk, vt, jnp.tile(da_subln[l], DA_HEADS).reshape(1, BW),
                      n_batch, tpb, lam_init)
        ob = _conv_call(u, cv_w[l], cv_b[l], cv_ln[l], tpb)
        oc = _sw_call(sw_sink[l], qk, u, n_batch, tpb)
        cm, cn, cq, cy0, bonus, y_fwd = _rw_chunk_call(u, lp, bd, tpb)
        yscan = (y_fwd, _rw_scan_call(cm, cn, cq, cy0, n_batch, tpb))
        xs, tok, gates_t = _merge_call(
            (oa, ob, oc), yscan, bonus, rw_g_up[l].astype(BF16), rw_gn[l], bd,
            u, w_branch[l].astype(BF16), w_out[l].astype(BF16), xs, ng[1],
            mod[2], ng[2], mod[3], mod[4], router_w[l].T, router_b[l], tpb, n_batch)
        f = _moe_call(tok, gates_t.T, ex_w_gu, ex_w_down, l, sh_w_gu[l].astype(BF16),
                      sh_w_down[l].astype(BF16), _moe_tile(rows_b))
        xs = _resid_call(xs, f, ng[3], mod[5], tpb, n_batch, latent_only=(l == depth - 1))

    return xs.reshape(n_batch, seq, d)
```
